```python
import jax, jax.numpy as jnp
from jax import lax
import numpy as np

D_MODEL = 1024
BATCH = 8
SEQ = 16384
DEPTH = 4

N_MIXERS = 2
N_POOL_LAYERS = (DEPTH + 1) // 2
N_MLA_LAYERS = DEPTH // 2
POOL_WINDOWS = (2, 4, 8, 16)
N_POOL_GROUPS = len(POOL_WINDOWS)
POOL_GROUP = D_MODEL // N_POOL_GROUPS
N_HEADS = D_MODEL // 128
QK_NOPE = 128
QK_ROPE = 64
QK_HEAD = QK_NOPE + QK_ROPE
V_HEAD = 128
Q_LORA = 3 * D_MODEL // 4
KV_LORA = D_MODEL // 4
ROPE_THETA = 10000.0
Q_BLOCK = 128
D_FF = 2816
FFN_HALF = 0.5
EPS = 1e-6

kernel_name = "hybrid_pool_mla_macaron_trunk"


def rmsnorm(x, gain):
    x32 = x.astype(jnp.float32)
    y = x32 * lax.rsqrt(jnp.mean(x32 * x32, axis=-1, keepdims=True) + EPS)
    return (y * gain.astype(jnp.float32)).astype(x.dtype)


def swiglu(h, w_gate, w_up, w_down):
    return (jax.nn.silu(h @ w_gate) * (h @ w_up)) @ w_down


def pool_mixer(h, w, scale):
    B, S, D = h.shape
    u = h.astype(jnp.float32).reshape(B, S, N_POOL_GROUPS, POOL_GROUP)
    cs = jnp.pad(jnp.cumsum(u, axis=1), ((0, 0), (1, 0), (0, 0), (0, 0)))
    sums = []
    for g, w_len in enumerate(POOL_WINDOWS):
        c = cs[:, :, g]
        lower = jnp.pad(c[:, :S + 1 - w_len], ((0, 0), (w_len - 1, 0), (0, 0)))
        sums.append(c[:, 1:] - lower)
    window_sum = jnp.stack(sums, axis=2)
    t = jnp.arange(S)
    count = jnp.minimum(t[:, None] + 1, jnp.array(POOL_WINDOWS, jnp.int32)[None, :])
    pooled = window_sum / count.astype(jnp.float32)[None, :, :, None] - u
    y = jnp.einsum('bsgc,gcd->bsgd', pooled.astype(h.dtype), w).reshape(B, S, D)
    return y * scale


def rope_tail(x, cos, sin):
    x_nope, x_pe = jnp.split(x, [QK_NOPE], axis=-1)
    x1, x2 = jnp.split(x_pe, 2, axis=-1)
    c = cos[:, :, None, :].astype(x.dtype)
    s = sin[:, :, None, :].astype(x.dtype)
    return jnp.concatenate([x_nope, x1 * c - x2 * s, x2 * c + x1 * s], axis=-1)


def causal_attention(q, k, v):
    B, S, H, Dq = q.shape
    nb = S // Q_BLOCK
    qb = q.reshape(B, nb, Q_BLOCK, H, Dq).transpose(1, 0, 3, 2, 4)
    kt = k.transpose(0, 2, 1, 3)
    vt = v.transpose(0, 2, 1, 3)
    kpos = jnp.arange(S)
    scale = QK_HEAD ** -0.5

    def one_block(args):
        q_blk, blk = args
        s = jnp.einsum('bhqd,bhkd->bhqk', q_blk, kt,
                       preferred_element_type=jnp.float32) * scale
        qpos = blk * Q_BLOCK + jnp.arange(Q_BLOCK)
        s = jnp.where(kpos[None, :] <= qpos[:, None], s, -jnp.inf)
        p = jax.nn.softmax(s, axis=-1).astype(vt.dtype)
        return jnp.einsum('bhqk,bhkv->bhqv', p, vt)

    out = lax.map(one_block, (qb, jnp.arange(nb)))
    return out.transpose(1, 0, 3, 2, 4).reshape(B, S, H, V_HEAD)


def mla_mixer(h, cos, sin, w_in, q_norm, w_q_up, kv_norm, w_kv_up,
              q_head_norm, k_head_norm, w_out):
    B, S, _ = h.shape
    lat = h @ w_in
    cq, ckv, k_pe = jnp.split(lat, [Q_LORA, Q_LORA + KV_LORA], axis=-1)
    q = (rmsnorm(cq, q_norm) @ w_q_up).reshape(B, S, N_HEADS, QK_HEAD)
    kv = (rmsnorm(ckv, kv_norm) @ w_kv_up).reshape(B, S, N_HEADS, QK_NOPE + V_HEAD)
    k_nope, v = jnp.split(kv, [QK_NOPE], axis=-1)
    k = jnp.concatenate(
        [k_nope, jnp.broadcast_to(k_pe[:, :, None, :], (B, S, N_HEADS, QK_ROPE))], axis=-1)
    q = rope_tail(rmsnorm(q, q_head_norm), cos, sin)
    k = rope_tail(rmsnorm(k, k_head_norm), cos, sin)
    o = causal_attention(q, k, v)
    return o.reshape(B, S, N_HEADS * V_HEAD) @ w_out


def _fwd_setup_inputs(seed: int = 0) -> dict:
    key = jax.random.key(seed)
    ks = jax.random.split(key, 24)
    f32 = jnp.float32

    def dense(k, shape, fan_in):
        return jax.random.normal(k, shape, f32) * fan_in ** -0.5

    def gain(k, shape):
        return 1.0 + 0.05 * jax.random.normal(k, shape, f32)

    x = jax.random.normal(ks[0], (BATCH, SEQ, D_MODEL), f32)
    offsets = jax.random.randint(ks[1], (BATCH, 1), 0, 4096, dtype=jnp.int32)
    positions = (jnp.arange(SEQ, dtype=jnp.int32)[None, :] + offsets).astype(jnp.int32)
    return {
        "x": x,
        "positions": positions,
        "ffn1_norm": gain(ks[2], (DEPTH, D_MODEL)),
        "ffn1_w_gate": dense(ks[3], (DEPTH, D_MODEL, D_FF), D_MODEL),
        "ffn1_w_up": dense(ks[4], (DEPTH, D_MODEL, D_FF), D_MODEL),
        "ffn1_w_down": dense(ks[5], (DEPTH, D_FF, D_MODEL), D_FF),
        "mix_norm": gain(ks[6], (DEPTH, D_MODEL)),
        "pool_w": dense(ks[7], (N_POOL_LAYERS, N_POOL_GROUPS, POOL_GROUP, POOL_GROUP), POOL_GROUP),
        "pool_scale": gain(ks[8], (N_POOL_LAYERS, D_MODEL)),
        "mla_w_in": dense(ks[9], (N_MLA_LAYERS, D_MODEL, Q_LORA + KV_LORA + QK_ROPE), D_MODEL),
        "mla_q_norm": gain(ks[10], (N_MLA_LAYERS, Q_LORA)),
        "mla_w_q_up": dense(ks[11], (N_MLA_LAYERS, Q_LORA, N_HEADS * QK_HEAD), Q_LORA),
        "mla_kv_norm": gain(ks[12], (N_MLA_LAYERS, KV_LORA)),
        "mla_w_kv_up": dense(ks[13], (N_MLA_LAYERS, KV_LORA, N_HEADS * (QK_NOPE + V_HEAD)), KV_LORA),
        "mla_q_head_norm": gain(ks[14], (N_MLA_LAYERS, QK_HEAD)),
        "mla_k_head_norm": gain(ks[15], (N_MLA_LAYERS, QK_HEAD)),
        "mla_w_out": dense(ks[16], (N_MLA_LAYERS, N_HEADS * V_HEAD, D_MODEL), N_HEADS * V_HEAD),
        "ffn2_norm": gain(ks[17], (DEPTH, D_MODEL)),
        "ffn2_w_gate": dense(ks[18], (DEPTH, D_MODEL, D_FF), D_MODEL),
        "ffn2_w_up": dense(ks[19], (DEPTH, D_MODEL, D_FF), D_MODEL),
        "ffn2_w_down": dense(ks[20], (DEPTH, D_FF, D_MODEL), D_FF),
    }


def _fwd_reference(x, positions, ffn1_norm, ffn1_w_gate, ffn1_w_up, ffn1_w_down, mix_norm,
              pool_w, pool_scale, mla_w_in, mla_q_norm, mla_w_q_up, mla_kv_norm,
              mla_w_kv_up, mla_q_head_norm, mla_k_head_norm, mla_w_out,
              ffn2_norm, ffn2_w_gate, ffn2_w_up, ffn2_w_down):
    inv_freq = 1.0 / (ROPE_THETA ** (jnp.arange(0, QK_ROPE, 2, dtype=jnp.float32) / QK_ROPE))
    ang = positions.astype(jnp.float32)[..., None] * inv_freq
    cos, sin = jnp.cos(ang), jnp.sin(ang)

    for i in range(DEPTH):
        h = rmsnorm(x, ffn1_norm[i])
        x = x + FFN_HALF * swiglu(h, ffn1_w_gate[i], ffn1_w_up[i], ffn1_w_down[i])
        h = rmsnorm(x, mix_norm[i])
        j = i // N_MIXERS
        if i % N_MIXERS == 0:
            x = x + pool_mixer(h, pool_w[j], pool_scale[j])
        else:
            x = x + mla_mixer(h, cos, sin, mla_w_in[j], mla_q_norm[j], mla_w_q_up[j],
                              mla_kv_norm[j], mla_w_kv_up[j], mla_q_head_norm[j],
                              mla_k_head_norm[j], mla_w_out[j])
        h = rmsnorm(x, ffn2_norm[i])
        x = x + FFN_HALF * swiglu(h, ffn2_w_gate[i], ffn2_w_up[i], ffn2_w_down[i])
    return x


import jax as _jax
import jax.numpy as _jnp

TWIN_FORMAT = 'train_step'
FWD_PARAMS = ['x', 'positions', 'ffn1_norm', 'ffn1_w_gate', 'ffn1_w_up', 'ffn1_w_down', 'mix_norm', 'pool_w', 'pool_scale', 'mla_w_in', 'mla_q_norm', 'mla_w_q_up', 'mla_kv_norm', 'mla_w_kv_up', 'mla_q_head_norm', 'mla_k_head_norm', 'mla_w_out', 'ffn2_norm', 'ffn2_w_gate', 'ffn2_w_up', 'ffn2_w_down']
TWIN_WEIGHTS = ['ffn1_norm', 'ffn1_w_gate', 'ffn1_w_up', 'ffn1_w_down', 'mix_norm', 'pool_w', 'pool_scale', 'mla_w_in', 'mla_q_norm', 'mla_w_q_up', 'mla_kv_norm', 'mla_w_kv_up', 'mla_q_head_norm', 'mla_k_head_norm', 'mla_w_out', 'ffn2_norm', 'ffn2_w_gate', 'ffn2_w_up', 'ffn2_w_down']
TWIN_DIFF_INPUT = 'x'
TWIN_INPUTS = ['x', 'positions', 'ffn1_norm', 'ffn1_w_gate', 'ffn1_w_up', 'ffn1_w_down', 'mix_norm', 'pool_w', 'pool_scale', 'mla_w_in', 'mla_q_norm', 'mla_w_q_up', 'mla_kv_norm', 'mla_w_kv_up', 'mla_q_head_norm', 'mla_k_head_norm', 'mla_w_out', 'ffn2_norm', 'ffn2_w_gate', 'ffn2_w_up', 'ffn2_w_down', 'loss_target', 'm_ffn1_norm', 'm_ffn1_w_gate', 'm_ffn1_w_up', 'm_ffn1_w_down', 'm_mix_norm', 'm_pool_w', 'm_pool_scale', 'm_mla_w_in', 'm_mla_q_norm', 'm_mla_w_q_up', 'm_mla_kv_norm', 'm_mla_w_kv_up', 'm_mla_q_head_norm', 'm_mla_k_head_norm', 'm_mla_w_out', 'm_ffn2_norm', 'm_ffn2_w_gate', 'm_ffn2_w_up', 'm_ffn2_w_down', 'v_ffn1_norm', 'v_ffn1_w_gate', 'v_ffn1_w_up', 'v_ffn1_w_down', 'v_mix_norm', 'v_pool_w', 'v_pool_scale', 'v_mla_w_in', 'v_mla_q_norm', 'v_mla_w_q_up', 'v_mla_kv_norm', 'v_mla_w_kv_up', 'v_mla_q_head_norm', 'v_mla_k_head_norm', 'v_mla_w_out', 'v_ffn2_norm', 'v_ffn2_w_gate', 'v_ffn2_w_up', 'v_ffn2_w_down']
TWIN_OUTPUTS = ['loss', 'grad_x', 'grad_ffn1_norm', 'grad_ffn1_w_gate', 'grad_ffn1_w_up', 'grad_ffn1_w_down', 'grad_mix_norm', 'grad_pool_w', 'grad_pool_scale', 'grad_mla_w_in', 'grad_mla_q_norm', 'grad_mla_w_q_up', 'grad_mla_kv_norm', 'grad_mla_w_kv_up', 'grad_mla_q_head_norm', 'grad_mla_k_head_norm', 'grad_mla_w_out', 'grad_ffn2_norm', 'grad_ffn2_w_gate', 'grad_ffn2_w_up', 'grad_ffn2_w_down', 'delta_ffn1_norm', 'delta_ffn1_w_gate', 'delta_ffn1_w_up', 'delta_ffn1_w_down', 'delta_mix_norm', 'delta_pool_w', 'delta_pool_scale', 'delta_mla_w_in', 'delta_mla_q_norm', 'delta_mla_w_q_up', 'delta_mla_kv_norm', 'delta_mla_w_kv_up', 'delta_mla_q_head_norm', 'delta_mla_k_head_norm', 'delta_mla_w_out', 'delta_ffn2_norm', 'delta_ffn2_w_gate', 'delta_ffn2_w_up', 'delta_ffn2_w_down', 'new_m_ffn1_norm', 'new_m_ffn1_w_gate', 'new_m_ffn1_w_up', 'new_m_ffn1_w_down', 'new_m_mix_norm', 'new_m_pool_w', 'new_m_pool_scale', 'new_m_mla_w_in', 'new_m_mla_q_norm', 'new_m_mla_w_q_up', 'new_m_mla_kv_norm', 'new_m_mla_w_kv_up', 'new_m_mla_q_head_norm', 'new_m_mla_k_head_norm', 'new_m_mla_w_out', 'new_m_ffn2_norm', 'new_m_ffn2_w_gate', 'new_m_ffn2_w_up', 'new_m_ffn2_w_down', 'new_v_ffn1_norm', 'new_v_ffn1_w_gate', 'new_v_ffn1_w_up', 'new_v_ffn1_w_down', 'new_v_mix_norm', 'new_v_pool_w', 'new_v_pool_scale', 'new_v_mla_w_in', 'new_v_mla_q_norm', 'new_v_mla_w_q_up', 'new_v_mla_kv_norm', 'new_v_mla_w_kv_up', 'new_v_mla_q_head_norm', 'new_v_mla_k_head_norm', 'new_v_mla_w_out', 'new_v_ffn2_norm', 'new_v_ffn2_w_gate', 'new_v_ffn2_w_up', 'new_v_ffn2_w_down']
TWIN_LEAF_KINDS = {'loss': 'loss', 'grad_x': 'grad_x', 'grad_ffn1_norm': 'grad_w', 'grad_ffn1_w_gate': 'grad_w', 'grad_ffn1_w_up': 'grad_w', 'grad_ffn1_w_down': 'grad_w', 'grad_mix_norm': 'grad_w', 'grad_pool_w': 'grad_w', 'grad_pool_scale': 'grad_w', 'grad_mla_w_in': 'grad_w', 'grad_mla_q_norm': 'grad_w', 'grad_mla_w_q_up': 'grad_w', 'grad_mla_kv_norm': 'grad_w', 'grad_mla_w_kv_up': 'grad_w', 'grad_mla_q_head_norm': 'grad_w', 'grad_mla_k_head_norm': 'grad_w', 'grad_mla_w_out': 'grad_w', 'grad_ffn2_norm': 'grad_w', 'grad_ffn2_w_gate': 'grad_w', 'grad_ffn2_w_up': 'grad_w', 'grad_ffn2_w_down': 'grad_w', 'delta_ffn1_norm': 'delta_w', 'delta_ffn1_w_gate': 'delta_w', 'delta_ffn1_w_up': 'delta_w', 'delta_ffn1_w_down': 'delta_w', 'delta_mix_norm': 'delta_w', 'delta_pool_w': 'delta_w', 'delta_pool_scale': 'delta_w', 'delta_mla_w_in': 'delta_w', 'delta_mla_q_norm': 'delta_w', 'delta_mla_w_q_up': 'delta_w', 'delta_mla_kv_norm': 'delta_w', 'delta_mla_w_kv_up': 'delta_w', 'delta_mla_q_head_norm': 'delta_w', 'delta_mla_k_head_norm': 'delta_w', 'delta_mla_w_out': 'delta_w', 'delta_ffn2_norm': 'delta_w', 'delta_ffn2_w_gate': 'delta_w', 'delta_ffn2_w_up': 'delta_w', 'delta_ffn2_w_down': 'delta_w', 'new_m_ffn1_norm': 'new_m', 'new_m_ffn1_w_gate': 'new_m', 'new_m_ffn1_w_up': 'new_m', 'new_m_ffn1_w_down': 'new_m', 'new_m_mix_norm': 'new_m', 'new_m_pool_w': 'new_m', 'new_m_pool_scale': 'new_m', 'new_m_mla_w_in': 'new_m', 'new_m_mla_q_norm': 'new_m', 'new_m_mla_w_q_up': 'new_m', 'new_m_mla_kv_norm': 'new_m', 'new_m_mla_w_kv_up': 'new_m', 'new_m_mla_q_head_norm': 'new_m', 'new_m_mla_k_head_norm': 'new_m', 'new_m_mla_w_out': 'new_m', 'new_m_ffn2_norm': 'new_m', 'new_m_ffn2_w_gate': 'new_m', 'new_m_ffn2_w_up': 'new_m', 'new_m_ffn2_w_down': 'new_m', 'new_v_ffn1_norm': 'new_v', 'new_v_ffn1_w_gate': 'new_v', 'new_v_ffn1_w_up': 'new_v', 'new_v_ffn1_w_down': 'new_v', 'new_v_mix_norm': 'new_v', 'new_v_pool_w': 'new_v', 'new_v_pool_scale': 'new_v', 'new_v_mla_w_in': 'new_v', 'new_v_mla_q_norm': 'new_v', 'new_v_mla_w_q_up': 'new_v', 'new_v_mla_kv_norm': 'new_v', 'new_v_mla_w_kv_up': 'new_v', 'new_v_mla_q_head_norm': 'new_v', 'new_v_mla_k_head_norm': 'new_v', 'new_v_mla_w_out': 'new_v', 'new_v_ffn2_norm': 'new_v', 'new_v_ffn2_w_gate': 'new_v', 'new_v_ffn2_w_up': 'new_v', 'new_v_ffn2_w_down': 'new_v'}


def _forward(args):
    return _fwd_reference(*[args[k] for k in FWD_PARAMS])


def _output_shape():
    def fwd():
        inp = _fwd_setup_inputs(0)
        return _fwd_reference(*[inp[k] for k in FWD_PARAMS])
    out = _jax.eval_shape(fwd)
    return out.shape, out.dtype

N_MICROBATCH = 1
ADAM_LR = 0.001
ADAM_B1 = 0.9
ADAM_B2 = 0.999
ADAM_EPS = 1e-08
ADAM_WD = 0.01
ADAM_STEP = 10
PER_EXAMPLE_BATCH_AXIS = {'x': 0, 'positions': 0, 'loss_target': 0}
SHARED_INPUTS = []
_WEIGHT_DTYPES = {'ffn1_norm': _jnp.float32, 'ffn1_w_gate': _jnp.float32, 'ffn1_w_up': _jnp.float32, 'ffn1_w_down': _jnp.float32, 'mix_norm': _jnp.float32, 'pool_w': _jnp.float32, 'pool_scale': _jnp.float32, 'mla_w_in': _jnp.float32, 'mla_q_norm': _jnp.float32, 'mla_w_q_up': _jnp.float32, 'mla_kv_norm': _jnp.float32, 'mla_w_kv_up': _jnp.float32, 'mla_q_head_norm': _jnp.float32, 'mla_k_head_norm': _jnp.float32, 'mla_w_out': _jnp.float32, 'ffn2_norm': _jnp.float32, 'ffn2_w_gate': _jnp.float32, 'ffn2_w_up': _jnp.float32, 'ffn2_w_down': _jnp.float32}
MOMENT_SCALE = {'ffn1_norm': 2.408514e+01, 'ffn1_w_gate': 2.231471e-01, 'ffn1_w_up': 3.411454e-01, 'ffn1_w_down': 5.757564e-01, 'mix_norm': 7.677915e+01, 'pool_w': 1.234665e+01, 'pool_scale': 1.084718e+02, 'mla_w_in': 4.252459e-01, 'mla_q_norm': 2.463520e-01, 'mla_w_q_up': 1.736659e-01, 'mla_kv_norm': 3.444706e+00, 'mla_w_kv_up': 2.567411e-01, 'mla_q_head_norm': 2.532067e+00, 'mla_k_head_norm': 2.535808e+00, 'mla_w_out': 2.912188e-01, 'ffn2_norm': 2.460000e+01, 'ffn2_w_gate': 1.890778e-01, 'ffn2_w_up': 3.473199e-01, 'ffn2_w_down': 5.728716e-01}


def _to_microbatches(a, axis):
    t = _jnp.moveaxis(a, axis, 0)
    t = t.reshape((N_MICROBATCH, t.shape[0] // N_MICROBATCH) + t.shape[1:])
    return _jnp.moveaxis(t, 1, axis + 1)


def setup_inputs(seed: int = 0) -> dict:
    inp = _fwd_setup_inputs(seed)
    key = _jax.random.fold_in(_jax.random.key(seed), 7919)
    shape, _ = _output_shape()
    out = dict(inp)
    out["loss_target"] = _jax.random.normal(_jax.random.fold_in(key, 0), shape, _jnp.float32)
    for i, name in enumerate(TWIN_WEIGHTS):
        w = inp[name].astype(_jnp.float32)
        if MOMENT_SCALE is None:
            s = _jnp.sqrt(_jnp.mean(_jnp.square(w)) + 1e-30)
        else:
            s = MOMENT_SCALE[name]
        km, kv = _jax.random.split(_jax.random.fold_in(key, i + 1))
        out[name] = w
        out["m_" + name] = s * _jax.random.normal(km, w.shape, _jnp.float32)
        out["v_" + name] = (s * s) * _jax.random.uniform(kv, w.shape, _jnp.float32, 0.5, 1.5)
    if N_MICROBATCH > 1:
        for name, axis in PER_EXAMPLE_BATCH_AXIS.items():
            out[name] = _to_microbatches(out[name], axis)
    return {'x': out['x'], 'positions': out['positions'], 'ffn1_norm': out['ffn1_norm'], 'ffn1_w_gate': out['ffn1_w_gate'], 'ffn1_w_up': out['ffn1_w_up'], 'ffn1_w_down': out['ffn1_w_down'], 'mix_norm': out['mix_norm'], 'pool_w': out['pool_w'], 'pool_scale': out['pool_scale'], 'mla_w_in': out['mla_w_in'], 'mla_q_norm': out['mla_q_norm'], 'mla_w_q_up': out['mla_w_q_up'], 'mla_kv_norm': out['mla_kv_norm'], 'mla_w_kv_up': out['mla_w_kv_up'], 'mla_q_head_norm': out['mla_q_head_norm'], 'mla_k_head_norm': out['mla_k_head_norm'], 'mla_w_out': out['mla_w_out'], 'ffn2_norm': out['ffn2_norm'], 'ffn2_w_gate': out['ffn2_w_gate'], 'ffn2_w_up': out['ffn2_w_up'], 'ffn2_w_down': out['ffn2_w_down'], 'loss_target': out['loss_target'], 'm_ffn1_norm': out['m_ffn1_norm'], 'm_ffn1_w_gate': out['m_ffn1_w_gate'], 'm_ffn1_w_up': out['m_ffn1_w_up'], 'm_ffn1_w_down': out['m_ffn1_w_down'], 'm_mix_norm': out['m_mix_norm'], 'm_pool_w': out['m_pool_w'], 'm_pool_scale': out['m_pool_scale'], 'm_mla_w_in': out['m_mla_w_in'], 'm_mla_q_norm': out['m_mla_q_norm'], 'm_mla_w_q_up': out['m_mla_w_q_up'], 'm_mla_kv_norm': out['m_mla_kv_norm'], 'm_mla_w_kv_up': out['m_mla_w_kv_up'], 'm_mla_q_head_norm': out['m_mla_q_head_norm'], 'm_mla_k_head_norm': out['m_mla_k_head_norm'], 'm_mla_w_out': out['m_mla_w_out'], 'm_ffn2_norm': out['m_ffn2_norm'], 'm_ffn2_w_gate': out['m_ffn2_w_gate'], 'm_ffn2_w_up': out['m_ffn2_w_up'], 'm_ffn2_w_down': out['m_ffn2_w_down'], 'v_ffn1_norm': out['v_ffn1_norm'], 'v_ffn1_w_gate': out['v_ffn1_w_gate'], 'v_ffn1_w_up': out['v_ffn1_w_up'], 'v_ffn1_w_down': out['v_ffn1_w_down'], 'v_mix_norm': out['v_mix_norm'], 'v_pool_w': out['v_pool_w'], 'v_pool_scale': out['v_pool_scale'], 'v_mla_w_in': out['v_mla_w_in'], 'v_mla_q_norm': out['v_mla_q_norm'], 'v_mla_w_q_up': out['v_mla_w_q_up'], 'v_mla_kv_norm': out['v_mla_kv_norm'], 'v_mla_w_kv_up': out['v_mla_w_kv_up'], 'v_mla_q_head_norm': out['v_mla_q_head_norm'], 'v_mla_k_head_norm': out['v_mla_k_head_norm'], 'v_mla_w_out': out['v_mla_w_out'], 'v_ffn2_norm': out['v_ffn2_norm'], 'v_ffn2_w_gate': out['v_ffn2_w_gate'], 'v_ffn2_w_up': out['v_ffn2_w_up'], 'v_ffn2_w_down': out['v_ffn2_w_down']}


def _loss(weights, diff, rest, loss_target):
    with _jax.named_scope("forward"):
        args = {**rest, TWIN_DIFF_INPUT: diff, **{k: w.astype(_WEIGHT_DTYPES[k]) for k, w in weights.items()}}
        y = _forward(args)
    with _jax.named_scope("loss_head"):
        err = _jnp.square(y.astype(_jnp.float32) - loss_target)
        return 0.5 * _jnp.sum(_jnp.mean(err, axis=-1)) if err.ndim else 0.5 * err


def _adamw(w, g, m, v):
    m = ADAM_B1 * m + (1.0 - ADAM_B1) * g
    v = ADAM_B2 * v + (1.0 - ADAM_B2) * _jnp.square(g)
    m_hat = m / (1.0 - ADAM_B1 ** ADAM_STEP)
    v_hat = v / (1.0 - ADAM_B2 ** ADAM_STEP)
    delta = -ADAM_LR * (m_hat / (_jnp.sqrt(v_hat) + ADAM_EPS) + ADAM_WD * w)
    return delta, m, v


def reference(x, positions, ffn1_norm, ffn1_w_gate, ffn1_w_up, ffn1_w_down, mix_norm, pool_w, pool_scale, mla_w_in, mla_q_norm, mla_w_q_up, mla_kv_norm, mla_w_kv_up, mla_q_head_norm, mla_k_head_norm, mla_w_out, ffn2_norm, ffn2_w_gate, ffn2_w_up, ffn2_w_down, loss_target, m_ffn1_norm, m_ffn1_w_gate, m_ffn1_w_up, m_ffn1_w_down, m_mix_norm, m_pool_w, m_pool_scale, m_mla_w_in, m_mla_q_norm, m_mla_w_q_up, m_mla_kv_norm, m_mla_w_kv_up, m_mla_q_head_norm, m_mla_k_head_norm, m_mla_w_out, m_ffn2_norm, m_ffn2_w_gate, m_ffn2_w_up, m_ffn2_w_down, v_ffn1_norm, v_ffn1_w_gate, v_ffn1_w_up, v_ffn1_w_down, v_mix_norm, v_pool_w, v_pool_scale, v_mla_w_in, v_mla_q_norm, v_mla_w_q_up, v_mla_kv_norm, v_mla_w_kv_up, v_mla_q_head_norm, v_mla_k_head_norm, v_mla_w_out, v_ffn2_norm, v_ffn2_w_gate, v_ffn2_w_up, v_ffn2_w_down):
    given = dict(x=x, positions=positions, ffn1_norm=ffn1_norm, ffn1_w_gate=ffn1_w_gate, ffn1_w_up=ffn1_w_up, ffn1_w_down=ffn1_w_down, mix_norm=mix_norm, pool_w=pool_w, pool_scale=pool_scale, mla_w_in=mla_w_in, mla_q_norm=mla_q_norm, mla_w_q_up=mla_w_q_up, mla_kv_norm=mla_kv_norm, mla_w_kv_up=mla_w_kv_up, mla_q_head_norm=mla_q_head_norm, mla_k_head_norm=mla_k_head_norm, mla_w_out=mla_w_out, ffn2_norm=ffn2_norm, ffn2_w_gate=ffn2_w_gate, ffn2_w_up=ffn2_w_up, ffn2_w_down=ffn2_w_down, loss_target=loss_target, m_ffn1_norm=m_ffn1_norm, m_ffn1_w_gate=m_ffn1_w_gate, m_ffn1_w_up=m_ffn1_w_up, m_ffn1_w_down=m_ffn1_w_down, m_mix_norm=m_mix_norm, m_pool_w=m_pool_w, m_pool_scale=m_pool_scale, m_mla_w_in=m_mla_w_in, m_mla_q_norm=m_mla_q_norm, m_mla_w_q_up=m_mla_w_q_up, m_mla_kv_norm=m_mla_kv_norm, m_mla_w_kv_up=m_mla_w_kv_up, m_mla_q_head_norm=m_mla_q_head_norm, m_mla_k_head_norm=m_mla_k_head_norm, m_mla_w_out=m_mla_w_out, m_ffn2_norm=m_ffn2_norm, m_ffn2_w_gate=m_ffn2_w_gate, m_ffn2_w_up=m_ffn2_w_up, m_ffn2_w_down=m_ffn2_w_down, v_ffn1_norm=v_ffn1_norm, v_ffn1_w_gate=v_ffn1_w_gate, v_ffn1_w_up=v_ffn1_w_up, v_ffn1_w_down=v_ffn1_w_down, v_mix_norm=v_mix_norm, v_pool_w=v_pool_w, v_pool_scale=v_pool_scale, v_mla_w_in=v_mla_w_in, v_mla_q_norm=v_mla_q_norm, v_mla_w_q_up=v_mla_w_q_up, v_mla_kv_norm=v_mla_kv_norm, v_mla_w_kv_up=v_mla_w_kv_up, v_mla_q_head_norm=v_mla_q_head_norm, v_mla_k_head_norm=v_mla_k_head_norm, v_mla_w_out=v_mla_w_out, v_ffn2_norm=v_ffn2_norm, v_ffn2_w_gate=v_ffn2_w_gate, v_ffn2_w_up=v_ffn2_w_up, v_ffn2_w_down=v_ffn2_w_down)
    weights = {n: given[n] for n in TWIN_WEIGHTS}
    shared = {n: given[n] for n in SHARED_INPUTS}
    per_example = {n: given[n] for n in ['x', 'positions']}
    grad_fn = _jax.value_and_grad(_loss, argnums=(0, 1))

    def one_microbatch(ex, loss_target):
        ex = dict(ex)
        diff = ex.pop(TWIN_DIFF_INPUT)
        return grad_fn(weights, diff, {**shared, **ex}, loss_target)

    if N_MICROBATCH == 1:
        loss, (grad_w, grad_x) = one_microbatch(per_example, given["loss_target"])
    else:
        def body(carry, xs):
            loss_sum, grad_sum = carry
            l_k, (gw_k, gx_k) = one_microbatch(xs[0], xs[1])
            with _jax.named_scope("update"):
                return (loss_sum + l_k, _jax.tree.map(_jnp.add, grad_sum, gw_k)), gx_k

        init = (_jnp.zeros((), _jnp.float32), _jax.tree.map(_jnp.zeros_like, weights))
        (loss, grad_w), grad_x = _jax.lax.scan(body, init, (per_example, given["loss_target"]))
    with _jax.named_scope("update"):
        delta_w, new_m, new_v = {}, {}, {}
        for n in TWIN_WEIGHTS:
            delta_w[n], new_m[n], new_v[n] = _adamw(weights[n], grad_w[n], given["m_" + n], given["v_" + n])
    return (loss, grad_x, *[grad_w[n] for n in TWIN_WEIGHTS], *[delta_w[n] for n in TWIN_WEIGHTS],
            *[new_m[n] for n in TWIN_WEIGHTS], *[new_v[n] for n in TWIN_WEIGHTS])
```

```python
import functools

import numpy as np

import jax
import jax.numpy as jnp
from jax import lax
from jax.experimental import pallas as pl
from jax.experimental.pallas import tpu as pltpu

F32 = jnp.float32
MXU_DTYPE = jnp.bfloat16
MESH = pl.DeviceIdType.MESH
N_CHIPS = 4
N_DEVICES = 8
LANES = 128
VMEM_LIMIT_BYTES = 56 * 2**20
NORM_EPS = 1e-6
QK_NOPE, QK_ROPE, V_HEAD = 128, 64, 128
QK_HEAD = QK_NOPE + QK_ROPE
ROPE_THETA = 10000.0
POOL_WINDOWS = (2, 4, 8, 16)
POOL_HALO = 16
FFN_HALF = 0.5
ADAM_LR, ADAM_B1, ADAM_B2, ADAM_EPS, ADAM_WD, ADAM_STEP = 0.001, 0.9, 0.999, 1e-08, 0.01, 10
FFN_TILE = 512
WGRAD_TILE = 512
MLA_TILE = 256
POOL_TILE = 512
ATTN_TILE = 512
ROW_TILE = 1024


def _cast(v):
    return v.astype(MXU_DTYPE)


def _mm(a, b):
    return jnp.dot(a, b, preferred_element_type=F32)


def _mm_nt(a, b):
    return lax.dot_general(a, b, (((1,), (1,)), ((), ())), preferred_element_type=F32)


def _mm_tn(a, b):
    return lax.dot_general(a, b, (((0,), (0,)), ((), ())), preferred_element_type=F32)


def _rms_fwd(v, gain):
    r = lax.rsqrt(jnp.mean(v * v, axis=-1, keepdims=True) + NORM_EPS)
    return v * r * gain, r


def _rms_bwd(v, r, gain, dy):
    vr = v * r
    gy = dy * gain
    dv = r * (gy - vr * jnp.mean(gy * vr, axis=-1, keepdims=True))
    return dv, jnp.sum(dy * vr, axis=0, keepdims=True)


def _params(semantics=None):
    return pltpu.CompilerParams(dimension_semantics=semantics, vmem_limit_bytes=VMEM_LIMIT_BYTES)


def _tile(n, want):
    t = min(n, want)
    assert n % t == 0, (n, want)
    return t


def _full(shape):
    nd = len(shape)
    return pl.BlockSpec(shape, lambda *_: (0,) * nd)


def _sds(shape, dtype=F32):
    return jax.ShapeDtypeStruct(shape, dtype)


def _ffn_fwd(x, gain, w_gu, w_dn, i_gate, i_up, i_down):
    s, d = x.shape
    fs = w_gu.shape[-1]
    tm = _tile(s, FFN_TILE)

    def body(x_ref, g_ref, wg_ref, wu_ref, wd_ref, y_ref, h_sc, acc_sc):
        j = pl.program_id(1)

        @pl.when(j == 0)
        def _():
            h, _ = _rms_fwd(x_ref[...], g_ref[...])
            h_sc[...] = _cast(h)
            acc_sc[...] = jnp.zeros_like(acc_sc)

        h = h_sc[...]
        g = _mm(h, wg_ref[...])
        u = _mm(h, wu_ref[...])
        act = (g * jax.nn.sigmoid(g)) * u
        acc_sc[...] += _mm(_cast(act), wd_ref[...])

        @pl.when(j == N_CHIPS - 1)
        def _():
            y_ref[...] = x_ref[...] + FFN_HALF * acc_sc[...]

    return pl.pallas_call(
        body, name="ffn_fwd", grid=(s // tm, N_CHIPS),
        in_specs=[
            pl.BlockSpec((tm, d), lambda i, j: (i, 0)),
            _full((1, d)),
            pl.BlockSpec((None, None, d, fs), lambda i, j: (j, i_gate, 0, 0)),
            pl.BlockSpec((None, None, d, fs), lambda i, j: (j, i_up, 0, 0)),
            pl.BlockSpec((None, None, fs, d), lambda i, j: (j, i_down, 0, 0)),
        ],
        out_specs=pl.BlockSpec((tm, d), lambda i, j: (i, 0)),
        out_shape=_sds((s, d)),
        scratch_shapes=[pltpu.VMEM((tm, d), MXU_DTYPE), pltpu.VMEM((tm, d), F32)],
        compiler_params=_params(("arbitrary", "arbitrary")),
    )(x, gain, w_gu, w_gu, w_dn)


def _ffn_bwd_dgrad(x, gain, dy, w_gu, w_dn, i_gate, i_up, i_down):
    s, d = x.shape
    fs = w_gu.shape[-1]
    tm = _tile(s, FFN_TILE)
    n_tiles = s // tm

    def body(x_ref, g_ref, dy_ref, wg_ref, wu_ref, wd_ref,
             dx_ref, dgain_ref, hb_ref, dyb_ref, dg_ref, du_ref, act_ref, h_sc, r_sc, dh_sc):
        i, j = pl.program_id(0), pl.program_id(1)

        @pl.when(j == 0)
        def _():
            h, r = _rms_fwd(x_ref[...], g_ref[...])
            h_sc[...] = _cast(h)
            r_sc[...] = r
            dh_sc[...] = jnp.zeros_like(dh_sc)
            hb_ref[...] = _cast(h)
            dyb_ref[...] = _cast(dy_ref[...])

        h = h_sc[...]
        dyb = _cast(dy_ref[...])
        g = _mm(h, wg_ref[...])
        u = _mm(h, wu_ref[...])
        sg = jax.nn.sigmoid(g)
        silu = g * sg
        dact = FFN_HALF * _mm_nt(dyb, wd_ref[...])
        dgb = _cast(dact * u * (sg * (1.0 + g * (1.0 - sg))))
        dub = _cast(dact * silu)
        dg_ref[...] = dgb
        du_ref[...] = dub
        act_ref[...] = _cast(silu * u)
        dh_sc[...] += _mm_nt(dgb, wg_ref[...]) + _mm_nt(dub, wu_ref[...])

        @pl.when(j == N_CHIPS - 1)
        def _():
            dxn, dgn = _rms_bwd(x_ref[...], r_sc[...], g_ref[...], dh_sc[...])
            dx_ref[...] = dy_ref[...] + dxn

            @pl.when(i == 0)
            def _():
                dgain_ref[...] = dgn

            @pl.when(i > 0)
            def _():
                dgain_ref[...] += dgn

    tok = pl.BlockSpec((tm, d), lambda i, j: (i, 0))
    chunk = pl.BlockSpec((None, tm, fs), lambda i, j: (j, i, 0))
    return pl.pallas_call(
        body, name="ffn_bwd_dgrad", grid=(n_tiles, N_CHIPS),
        in_specs=[
            tok, _full((1, d)), tok,
            pl.BlockSpec((None, None, d, fs), lambda i, j: (j, i_gate, 0, 0)),
            pl.BlockSpec((None, None, d, fs), lambda i, j: (j, i_up, 0, 0)),
            pl.BlockSpec((None, None, fs, d), lambda i, j: (j, i_down, 0, 0)),
        ],
        out_specs=[tok, _full((1, d)), tok, tok, chunk, chunk, chunk],
        out_shape=[_sds((s, d)), _sds((1, d)), _sds((s, d), MXU_DTYPE), _sds((s, d), MXU_DTYPE),
                   _sds((N_CHIPS, s, fs), MXU_DTYPE), _sds((N_CHIPS, s, fs), MXU_DTYPE),
                   _sds((N_CHIPS, s, fs), MXU_DTYPE)],
        scratch_shapes=[pltpu.VMEM((tm, d), MXU_DTYPE), pltpu.VMEM((tm, 1), F32), pltpu.VMEM((tm, d), F32)],
        compiler_params=_params(("arbitrary", "arbitrary")),
    )(x, gain, dy, w_gu, w_gu, w_dn)


def _ffn_wgrad(hb, dyb, dg, du, act):
    s, d = hb.shape
    fs = dg.shape[-1]
    tk = _tile(s, WGRAD_TILE)
    n_k = s // tk

    def body(h_ref, dy_ref, dg_ref, du_ref, act_ref, wg_ref, wu_ref, wd_ref):
        k = pl.program_id(1)

        @pl.when(k == 0)
        def _():
            wg_ref[...] = jnp.zeros_like(wg_ref)
            wu_ref[...] = jnp.zeros_like(wu_ref)
            wd_ref[...] = jnp.zeros_like(wd_ref)

        h = h_ref[...]
        wg_ref[...] += _mm_tn(h, dg_ref[...])
        wu_ref[...] += _mm_tn(h, du_ref[...])
        wd_ref[...] += FFN_HALF * _mm_tn(act_ref[...], dy_ref[...])

    tok = pl.BlockSpec((tk, d), lambda j, k: (k, 0))
    chunk = pl.BlockSpec((None, tk, fs), lambda j, k: (j, k, 0))
    return pl.pallas_call(
        body, name="ffn_wgrad", grid=(N_CHIPS, n_k),
        in_specs=[tok, tok, chunk, chunk, chunk],
        out_specs=[pl.BlockSpec((None, d, fs), lambda j, k: (j, 0, 0)),
                   pl.BlockSpec((None, d, fs), lambda j, k: (j, 0, 0)),
                   pl.BlockSpec((None, fs, d), lambda j, k: (j, 0, 0))],
        out_shape=[_sds((N_CHIPS, d, fs)), _sds((N_CHIPS, d, fs)), _sds((N_CHIPS, fs, d))],
        compiler_params=_params(("arbitrary", "arbitrary")),
    )(hb, dyb, dg, du, act)


def _inv_count(first_row, n_rows, window):
    t = first_row + lax.broadcasted_iota(jnp.int32, (n_rows, 1), 0)
    return 1.0 / jnp.minimum(t + 1, window).astype(F32)


def _trailing_sum(v, window):
    k = 1
    while k < window:
        v = v + pltpu.roll(v, k, 0)
        k *= 2
    return v


def _leading_sum(v, window):
    n = v.shape[0]
    k = 1
    while k < window:
        v = v + pltpu.roll(v, n - k, 0)
        k *= 2
    return v


def _pool_normed_rows(x_ref, prev_ref, g_ref, i):
    h, r = _rms_fwd(x_ref[...], g_ref[...])
    hp, _ = _rms_fwd(prev_ref[...], g_ref[...])
    hp = jnp.where(i > 0, hp, 0.0)
    return jnp.concatenate([hp, h], axis=0), r


def _pooled_group(he, g, pg, first_row, tm):
    ue = he[:, g * pg:(g + 1) * pg]
    win = _trailing_sum(ue, POOL_WINDOWS[g])[POOL_HALO:]
    return win * _inv_count(first_row, tm, POOL_WINDOWS[g]) - ue[POOL_HALO:]


def _pool_specs(s, d, tm):
    per = tm // POOL_HALO
    last = s // POOL_HALO - 1
    tok = pl.BlockSpec((tm, d), lambda i: (i, 0))
    prev = pl.BlockSpec((POOL_HALO, d), lambda i: (jnp.maximum(i * per - 1, 0), 0))
    nxt = pl.BlockSpec((POOL_HALO, d), lambda i: (jnp.minimum((i + 1) * per, last), 0))
    return tok, prev, nxt


def _pool_fwd(x, gain, w, scale):
    s, d = x.shape
    n_g, pg = w.shape[0], w.shape[-1]
    tm = _tile(s, POOL_TILE)
    tok, prev, _ = _pool_specs(s, d, tm)

    def body(x_ref, prev_ref, g_ref, w_ref, sc_ref, y_ref):
        i = pl.program_id(0)
        he, _ = _pool_normed_rows(x_ref, prev_ref, g_ref, i)
        z = [_mm(_cast(_pooled_group(he, g, pg, i * tm, tm)), w_ref[g]) for g in range(n_g)]
        y_ref[...] = x_ref[...] + jnp.concatenate(z, axis=-1) * sc_ref[...]

    return pl.pallas_call(
        body, name="pool_fwd", grid=(s // tm,),
        in_specs=[tok, prev, _full((1, d)), _full(w.shape), _full((1, d))],
        out_specs=tok, out_shape=_sds((s, d)),
        compiler_params=_params(("arbitrary",)),
    )(x, x, gain, w, scale)


def _pool_bwd(x, gain, w, scale, dy):
    s, d = x.shape
    n_g, pg = w.shape[0], w.shape[-1]
    tm = _tile(s, POOL_TILE)
    n_tiles = s // tm
    tok, prev, nxt = _pool_specs(s, d, tm)

    def body(x_ref, prev_ref, dy_ref, next_ref, g_ref, w_ref, sc_ref, dx_ref, dgain_ref, dw_ref, dsc_ref):
        i = pl.program_id(0)

        @pl.when(i == 0)
        def _():
            dgain_ref[...] = jnp.zeros_like(dgain_ref)
            dw_ref[...] = jnp.zeros_like(dw_ref)
            dsc_ref[...] = jnp.zeros_like(dsc_ref)

        he, r = _pool_normed_rows(x_ref, prev_ref, g_ref, i)
        dy = dy_ref[...]
        dyn = jnp.where(i < n_tiles - 1, next_ref[...], 0.0)
        dze = jnp.concatenate([dy, dyn], axis=0) * sc_ref[...]
        dh, dsc = [], []
        for g in range(n_g):
            cols = slice(g * pg, (g + 1) * pg)
            pooled = _cast(_pooled_group(he, g, pg, i * tm, tm))
            dsc.append(jnp.sum(dy[:, cols] * _mm(pooled, w_ref[g]), axis=0, keepdims=True))
            dzb = _cast(dze[:, cols])
            dw_ref[g] += _mm_tn(pooled, dzb[:tm])
            dpool = _mm_nt(dzb, w_ref[g])
            spread = _leading_sum(dpool * _inv_count(i * tm, tm + POOL_HALO, POOL_WINDOWS[g]), POOL_WINDOWS[g])
            dh.append(spread[:tm] - dpool[:tm])
        dsc_ref[...] += jnp.concatenate(dsc, axis=-1)
        dxn, dgn = _rms_bwd(x_ref[...], r, g_ref[...], jnp.concatenate(dh, axis=-1))
        dgain_ref[...] += dgn
        dx_ref[...] = dy + dxn

    return pl.pallas_call(
        body, name="pool_bwd", grid=(n_tiles,),
        in_specs=[tok, prev, tok, nxt, _full((1, d)), _full(w.shape), _full((1, d))],
        out_specs=[tok, _full((1, d)), _full(w.shape), _full((1, d))],
        out_shape=[_sds((s, d)), _sds((1, d)), _sds(w.shape), _sds((1, d))],
        compiler_params=_params(("arbitrary",)),
    )(x, x, dy, dy, gain, w, scale)


def _rope_tables(pos_col, inv_freq):
    s = pos_col.shape[0]
    tm = _tile(s, ROW_TILE)
    half = QK_ROPE // 2

    def body(p_ref, f_ref, c_ref, s_ref):
        ang = p_ref[...].astype(F32) * f_ref[...]
        cos, sin = jnp.cos(ang), jnp.sin(ang)
        c_ref[...] = jnp.concatenate([jnp.ones((tm, QK_NOPE), F32), cos, cos], axis=-1)
        s_ref[...] = jnp.concatenate([jnp.zeros((tm, QK_NOPE), F32), -sin, sin], axis=-1)

    tab = pl.BlockSpec((tm, QK_HEAD), lambda i: (i, 0))
    return pl.pallas_call(
        body, name="rope_tables", grid=(s // tm,),
        in_specs=[pl.BlockSpec((tm, 1), lambda i: (i, 0)), _full((1, half))],
        out_specs=[tab, tab], out_shape=[_sds((s, QK_HEAD)), _sds((s, QK_HEAD))],
        compiler_params=_params(("arbitrary",)),
    )(pos_col, inv_freq)


def _swap_rope_halves(v):
    half = QK_ROPE // 2
    return jnp.concatenate([v[:, :QK_NOPE], v[:, QK_NOPE + half:], v[:, QK_NOPE:QK_NOPE + half]], axis=-1)


def _rope(v, cos, sin):
    return v * cos + _swap_rope_halves(v) * sin


def _rope_transposed(dv, cos, sin):
    return dv * cos + _swap_rope_halves(dv * sin)


def _mla_qkv_fwd(x, gain, w_in, q_norm, kv_norm, w_q, w_kn, w_v, q_head_norm, k_head_norm, cos, sin):
    s, d = x.shape
    n_h, ql = w_q.shape[0], w_q.shape[1]
    kvl, lat_w = w_kn.shape[1], w_in.shape[1]
    tm = _tile(s, MLA_TILE)

    def body(x_ref, g_ref, win_ref, qn_ref, kvn_ref, wq_ref, wkn_ref, wv_ref, qhn_ref, khn_ref, c_ref, s_ref,
             lat_ref, q_ref, k_ref, v_ref):
        h, _ = _rms_fwd(x_ref[...], g_ref[...])
        lat = _mm(_cast(h), win_ref[...])
        lat_ref[...] = lat
        cqn, _ = _rms_fwd(lat[:, :ql], qn_ref[...])
        ckvn, _ = _rms_fwd(lat[:, ql:ql + kvl], kvn_ref[...])
        kpe = lat[:, ql + kvl:]
        cqb, ckb = _cast(cqn), _cast(ckvn)
        cos_t, sin_t = c_ref[...], s_ref[...]
        v_ref[...] = _cast(_mm(ckb, wv_ref[...]))
        for hh in range(n_h):
            qn, _ = _rms_fwd(_mm(cqb, wq_ref[hh]), qhn_ref[...])
            q_ref[hh] = _cast(_rope(qn, cos_t, sin_t))
            kn, _ = _rms_fwd(jnp.concatenate([_mm(ckb, wkn_ref[hh]), kpe], axis=-1), khn_ref[...])
            k_ref[hh] = _cast(_rope(kn, cos_t, sin_t))

    tok = lambda w: pl.BlockSpec((tm, w), lambda i: (i, 0))
    heads = pl.BlockSpec((n_h, tm, QK_HEAD), lambda i: (0, i, 0))
    return pl.pallas_call(
        body, name="mla_qkv_fwd", grid=(s // tm,),
        in_specs=[tok(d), _full((1, d)), _full(w_in.shape), _full((1, ql)), _full((1, kvl)), _full(w_q.shape),
                  _full(w_kn.shape), _full(w_v.shape), _full((1, QK_HEAD)), _full((1, QK_HEAD)),
                  tok(QK_HEAD), tok(QK_HEAD)],
        out_specs=[tok(lat_w), heads, heads, tok(n_h * V_HEAD)],
        out_shape=[_sds((s, lat_w)), _sds((n_h, s, QK_HEAD), MXU_DTYPE), _sds((n_h, s, QK_HEAD), MXU_DTYPE),
                   _sds((s, n_h * V_HEAD), MXU_DTYPE)],
        compiler_params=_params(("arbitrary",)),
    )(x, gain, w_in, q_norm, kv_norm, w_q, w_kn, w_v, q_head_norm, k_head_norm, cos, sin)


def _mla_qkv_bwd(x, lat, dy, dq, dk, dv, gain, w_in, q_norm, kv_norm, w_q, w_kn, w_v, q_head_norm, k_head_norm,
                 cos, sin):
    s, d = x.shape
    n_h, ql = w_q.shape[0], w_q.shape[1]
    kvl, lat_w = w_kn.shape[1], w_in.shape[1]
    tm = _tile(s, MLA_TILE)

    def body(x_ref, lat_ref, dy_ref, dq_ref, dk_ref, dv_ref, g_ref, win_ref, qn_ref, kvn_ref, wq_ref, wkn_ref,
             wv_ref, qhn_ref, khn_ref, c_ref, s_ref,
             dx_ref, dg_ref, dwin_ref, dqn_ref, dkvn_ref, dwq_ref, dwkn_ref, dwv_ref, dqhn_ref, dkhn_ref):
        @pl.when(pl.program_id(0) == 0)
        def _():
            for ref in (dg_ref, dwin_ref, dqn_ref, dkvn_ref, dwq_ref, dwkn_ref, dwv_ref, dqhn_ref, dkhn_ref):
                ref[...] = jnp.zeros_like(ref)

        x_t = x_ref[...]
        h, r = _rms_fwd(x_t, g_ref[...])
        hb = _cast(h)
        lat = lat_ref[...]
        cq, ckv, kpe = lat[:, :ql], lat[:, ql:ql + kvl], lat[:, ql + kvl:]
        cqn, rq = _rms_fwd(cq, qn_ref[...])
        ckvn, rkv = _rms_fwd(ckv, kvn_ref[...])
        cqb, ckb = _cast(cqn), _cast(ckvn)
        cos_t, sin_t = c_ref[...], s_ref[...]

        dvb = _cast(dv_ref[...])
        dwv_ref[...] += _mm_tn(ckb, dvb)
        dckvn = _mm_nt(dvb, wv_ref[...])
        dcqn = jnp.zeros((tm, ql), F32)
        dkpe = jnp.zeros((tm, QK_ROPE), F32)
        dqhn = jnp.zeros((1, QK_HEAD), F32)
        dkhn = jnp.zeros((1, QK_HEAD), F32)
        for hh in range(n_h):
            qp = _mm(cqb, wq_ref[hh])
            _, rqp = _rms_fwd(qp, qhn_ref[...])
            dqp, dgq = _rms_bwd(qp, rqp, qhn_ref[...], _rope_transposed(dq_ref[hh], cos_t, sin_t))
            dqhn += dgq
            dqpb = _cast(dqp)
            dwq_ref[hh] += _mm_tn(cqb, dqpb)
            dcqn += _mm_nt(dqpb, wq_ref[hh])

            kp = jnp.concatenate([_mm(ckb, wkn_ref[hh]), kpe], axis=-1)
            _, rkp = _rms_fwd(kp, khn_ref[...])
            dkp, dgk = _rms_bwd(kp, rkp, khn_ref[...], _rope_transposed(dk_ref[hh], cos_t, sin_t))
            dkhn += dgk
            dknb = _cast(dkp[:, :QK_NOPE])
            dkpe += dkp[:, QK_NOPE:]
            dwkn_ref[hh] += _mm_tn(ckb, dknb)
            dckvn += _mm_nt(dknb, wkn_ref[hh])
        dqhn_ref[...] += dqhn
        dkhn_ref[...] += dkhn

        dcq, dgn = _rms_bwd(cq, rq, qn_ref[...], dcqn)
        dqn_ref[...] += dgn
        dckv, dgn = _rms_bwd(ckv, rkv, kvn_ref[...], dckvn)
        dkvn_ref[...] += dgn
        dlb = _cast(jnp.concatenate([dcq, dckv, dkpe], axis=-1))
        dwin_ref[...] += _mm_tn(hb, dlb)
        dxn, dgn = _rms_bwd(x_t, r, g_ref[...], _mm_nt(dlb, win_ref[...]))
        dg_ref[...] += dgn
        dx_ref[...] = dy_ref[...] + dxn

    tok = lambda w: pl.BlockSpec((tm, w), lambda i: (i, 0))
    heads = pl.BlockSpec((n_h, tm, QK_HEAD), lambda i: (0, i, 0))
    return pl.pallas_call(
        body, name="mla_qkv_bwd", grid=(s // tm,),
        in_specs=[tok(d), tok(lat_w), tok(d), heads, heads, tok(n_h * V_HEAD), _full((1, d)), _full(w_in.shape),
                  _full((1, ql)), _full((1, kvl)), _full(w_q.shape), _full(w_kn.shape), _full(w_v.shape),
                  _full((1, QK_HEAD)), _full((1, QK_HEAD)), tok(QK_HEAD), tok(QK_HEAD)],
        out_specs=[tok(d), _full((1, d)), _full(w_in.shape), _full((1, ql)), _full((1, kvl)), _full(w_q.shape),
                   _full(w_kn.shape), _full(w_v.shape), _full((1, QK_HEAD)), _full((1, QK_HEAD))],
        out_shape=[_sds((s, d)), _sds((1, d)), _sds(w_in.shape), _sds((1, ql)), _sds((1, kvl)), _sds(w_q.shape),
                   _sds(w_kn.shape), _sds(w_v.shape), _sds((1, QK_HEAD)), _sds((1, QK_HEAD))],
        compiler_params=_params(("arbitrary",)),
    )(x, lat, dy, dq, dk, dv, gain, w_in, q_norm, kv_norm, w_q, w_kn, w_v, q_head_norm, k_head_norm, cos, sin)


def _causal_pairs(n, kv_major):
    if kv_major:
        pairs = [(i, j) for j in range(n) for i in range(j, n)]
        first = [i == j for i, j in pairs]
        last = [i == n - 1 for i, j in pairs]
    else:
        pairs = [(i, j) for i in range(n) for j in range(i + 1)]
        first = [j == 0 for i, j in pairs]
        last = [j == i for i, j in pairs]
    as_i32 = lambda v: jnp.asarray(np.asarray(v, dtype=np.int32))
    return as_i32([p[0] for p in pairs]), as_i32([p[1] for p in pairs]), as_i32(first), as_i32(last)


def _causal_scores(q, k, qi, kj, t):
    sc = _mm_nt(q, k) * (QK_HEAD ** -0.5)
    row = qi * t + lax.broadcasted_iota(jnp.int32, (t, t), 0)
    col = kj * t + lax.broadcasted_iota(jnp.int32, (t, t), 1)
    return jnp.where(col <= row, sc, -jnp.inf)


def _flash_fwd(q, k, v):
    n_h, s, _ = q.shape
    t = _tile(s, ATTN_TILE)
    qi_t, kj_t, first_t, last_t = _causal_pairs(s // t, kv_major=False)

    def body(qi_ref, kj_ref, first_ref, last_ref, q_ref, k_ref, v_ref, o_ref, lse_ref, m_sc, l_sc, acc_sc):
        p = pl.program_id(1)

        @pl.when(first_ref[p] == 1)
        def _():
            m_sc[...] = jnp.full_like(m_sc, -jnp.inf)
            l_sc[...] = jnp.zeros_like(l_sc)
            acc_sc[...] = jnp.zeros_like(acc_sc)

        sc = _causal_scores(q_ref[...], k_ref[...], qi_ref[p], kj_ref[p], t)
        m_old = m_sc[...]
        m_new = jnp.maximum(m_old, jnp.max(sc, axis=-1, keepdims=True))
        alpha = jnp.exp(m_old - m_new)
        pr = jnp.exp(sc - m_new)
        l_sc[...] = alpha * l_sc[...] + jnp.sum(pr, axis=-1, keepdims=True)
        acc_sc[...] = alpha * acc_sc[...] + _mm(_cast(pr), v_ref[...])
        m_sc[...] = m_new

        @pl.when(last_ref[p] == 1)
        def _():
            o_ref[...] = acc_sc[...] / l_sc[...]
            lse_ref[...] = jnp.broadcast_to(m_sc[...] + jnp.log(l_sc[...]), (t, LANES))

    grid_spec = pltpu.PrefetchScalarGridSpec(
        num_scalar_prefetch=4, grid=(n_h, qi_t.shape[0]),
        in_specs=[pl.BlockSpec((None, t, QK_HEAD), lambda h, p, qi, kj, f, l: (h, qi[p], 0)),
                  pl.BlockSpec((None, t, QK_HEAD), lambda h, p, qi, kj, f, l: (h, kj[p], 0)),
                  pl.BlockSpec((t, V_HEAD), lambda h, p, qi, kj, f, l: (kj[p], h))],
        out_specs=[pl.BlockSpec((t, V_HEAD), lambda h, p, qi, kj, f, l: (qi[p], h)),
                   pl.BlockSpec((None, t, LANES), lambda h, p, qi, kj, f, l: (h, qi[p], 0))],
        scratch_shapes=[pltpu.VMEM((t, 1), F32), pltpu.VMEM((t, 1), F32), pltpu.VMEM((t, V_HEAD), F32)])
    return pl.pallas_call(
        body, name="flash_fwd", grid_spec=grid_spec,
        out_shape=[_sds((s, n_h * V_HEAD)), _sds((n_h, s, LANES))],
        compiler_params=_params(("arbitrary", "arbitrary")),
    )(qi_t, kj_t, first_t, last_t, q, k, v)


def _flash_bwd_dq(q, k, v, do, lse, delta):
    n_h, s, _ = q.shape
    t = _tile(s, ATTN_TILE)
    qi_t, kj_t, first_t, last_t = _causal_pairs(s // t, kv_major=False)

    def body(qi_ref, kj_ref, first_ref, last_ref, q_ref, k_ref, v_ref, do_ref, lse_ref, dl_ref, dq_ref, acc_sc):
        p = pl.program_id(1)

        @pl.when(first_ref[p] == 1)
        def _():
            acc_sc[...] = jnp.zeros_like(acc_sc)

        sc = _causal_scores(q_ref[...], k_ref[...], qi_ref[p], kj_ref[p], t)
        pr = jnp.exp(sc - lse_ref[:, :1])
        dp = _mm_nt(do_ref[...], v_ref[...])
        ds = pr * (dp - dl_ref[:, :1]) * (QK_HEAD ** -0.5)
        acc_sc[...] += _mm(_cast(ds), k_ref[...])

        @pl.when(last_ref[p] == 1)
        def _():
            dq_ref[...] = acc_sc[...]

    q_spec = pl.BlockSpec((None, t, QK_HEAD), lambda h, p, qi, kj, f, l: (h, qi[p], 0))
    stat = pl.BlockSpec((None, t, LANES), lambda h, p, qi, kj, f, l: (h, qi[p], 0))
    grid_spec = pltpu.PrefetchScalarGridSpec(
        num_scalar_prefetch=4, grid=(n_h, qi_t.shape[0]),
        in_specs=[q_spec,
                  pl.BlockSpec((None, t, QK_HEAD), lambda h, p, qi, kj, f, l: (h, kj[p], 0)),
                  pl.BlockSpec((t, V_HEAD), lambda h, p, qi, kj, f, l: (kj[p], h)),
                  pl.BlockSpec((t, V_HEAD), lambda h, p, qi, kj, f, l: (qi[p], h)),
                  stat, stat],
        out_specs=q_spec,
        scratch_shapes=[pltpu.VMEM((t, QK_HEAD), F32)])
    return pl.pallas_call(
        body, name="flash_bwd_dq", grid_spec=grid_spec, out_shape=_sds((n_h, s, QK_HEAD)),
        compiler_params=_params(("arbitrary", "arbitrary")),
    )(qi_t, kj_t, first_t, last_t, q, k, v, do, lse, delta)


def _flash_bwd_dkv(q, k, v, do, lse, delta):
    n_h, s, _ = q.shape
    t = _tile(s, ATTN_TILE)
    qi_t, kj_t, first_t, last_t = _causal_pairs(s // t, kv_major=True)

    def body(qi_ref, kj_ref, first_ref, last_ref, q_ref, k_ref, v_ref, do_ref, lse_ref, dl_ref,
             dk_ref, dv_ref, dk_sc, dv_sc):
        p = pl.program_id(1)

        @pl.when(first_ref[p] == 1)
        def _():
            dk_sc[...] = jnp.zeros_like(dk_sc)
            dv_sc[...] = jnp.zeros_like(dv_sc)

        q_t, do_t = q_ref[...], do_ref[...]
        sc = _causal_scores(q_t, k_ref[...], qi_ref[p], kj_ref[p], t)
        pr = jnp.exp(sc - lse_ref[:, :1])
        dv_sc[...] += _mm_tn(_cast(pr), do_t)
        dp = _mm_nt(do_t, v_ref[...])
        ds = pr * (dp - dl_ref[:, :1]) * (QK_HEAD ** -0.5)
        dk_sc[...] += _mm_tn(_cast(ds), q_t)

        @pl.when(last_ref[p] == 1)
        def _():
            dk_ref[...] = dk_sc[...]
            dv_ref[...] = dv_sc[...]

    k_spec = pl.BlockSpec((None, t, QK_HEAD), lambda h, p, qi, kj, f, l: (h, kj[p], 0))
    v_spec = pl.BlockSpec((t, V_HEAD), lambda h, p, qi, kj, f, l: (kj[p], h))
    stat = pl.BlockSpec((None, t, LANES), lambda h, p, qi, kj, f, l: (h, qi[p], 0))
    grid_spec = pltpu.PrefetchScalarGridSpec(
        num_scalar_prefetch=4, grid=(n_h, qi_t.shape[0]),
        in_specs=[pl.BlockSpec((None, t, QK_HEAD), lambda h, p, qi, kj, f, l: (h, qi[p], 0)),
                  k_spec, v_spec,
                  pl.BlockSpec((t, V_HEAD), lambda h, p, qi, kj, f, l: (qi[p], h)),
                  stat, stat],
        out_specs=[k_spec, v_spec],
        scratch_shapes=[pltpu.VMEM((t, QK_HEAD), F32), pltpu.VMEM((t, V_HEAD), F32)])
    return pl.pallas_call(
        body, name="flash_bwd_dkv", grid_spec=grid_spec,
        out_shape=[_sds((n_h, s, QK_HEAD)), _sds((s, n_h * V_HEAD))],
        compiler_params=_params(("arbitrary", "arbitrary")),
    )(qi_t, kj_t, first_t, last_t, q, k, v, do, lse, delta)


def _mla_out_fwd(x, o, w_out):
    s, d = x.shape
    tm = _tile(s, FFN_TILE)

    def body(x_ref, o_ref, w_ref, y_ref):
        y_ref[...] = x_ref[...] + _mm(_cast(o_ref[...]), w_ref[...])

    tok = lambda w: pl.BlockSpec((tm, w), lambda i: (i, 0))
    return pl.pallas_call(
        body, name="mla_out_fwd", grid=(s // tm,),
        in_specs=[tok(d), tok(o.shape[1]), _full(w_out.shape)],
        out_specs=tok(d), out_shape=_sds((s, d)),
        compiler_params=_params(("arbitrary",)),
    )(x, o, w_out)


def _mla_out_bwd(dy, o, w_out):
    s, d = dy.shape
    hv = o.shape[1]
    n_h = hv // V_HEAD
    tm = _tile(s, FFN_TILE)

    def body(dy_ref, o_ref, w_ref, do_ref, dl_ref, dw_ref):
        @pl.when(pl.program_id(0) == 0)
        def _():
            dw_ref[...] = jnp.zeros_like(dw_ref)

        dyb = _cast(dy_ref[...])
        o_t = o_ref[...]
        do = _mm_nt(dyb, w_ref[...])
        do_ref[...] = _cast(do)
        prod = do * o_t
        for hh in range(n_h):
            dl = jnp.sum(prod[:, hh * V_HEAD:(hh + 1) * V_HEAD], axis=-1, keepdims=True)
            dl_ref[hh] = jnp.broadcast_to(dl, (tm, LANES))
        dw_ref[...] += _mm_tn(_cast(o_t), dyb)

    tok = lambda w: pl.BlockSpec((tm, w), lambda i: (i, 0))
    return pl.pallas_call(
        body, name="mla_out_bwd", grid=(s // tm,),
        in_specs=[tok(d), tok(hv), _full(w_out.shape)],
        out_specs=[tok(hv), pl.BlockSpec((n_h, tm, LANES), lambda i: (0, i, 0)), _full(w_out.shape)],
        out_shape=[_sds((s, hv), MXU_DTYPE), _sds((n_h, s, LANES)), _sds(w_out.shape)],
        compiler_params=_params(("arbitrary",)),
    )(dy, o, w_out)


def _loss_and_grad(y, target):
    s, d = y.shape
    tm = _tile(s, ROW_TILE)

    def body(y_ref, t_ref, loss_ref, dy_ref):
        @pl.when(pl.program_id(0) == 0)
        def _():
            loss_ref[...] = jnp.zeros_like(loss_ref)

        err = y_ref[...] - t_ref[...]
        dy_ref[...] = err * (1.0 / d)
        loss_ref[...] += 0.5 * jnp.sum(jnp.mean(err * err, axis=-1, keepdims=True), axis=0, keepdims=True)

    tok = pl.BlockSpec((tm, d), lambda i: (i, 0))
    return pl.pallas_call(
        body, name="loss_and_grad", grid=(s // tm,),
        in_specs=[tok, tok], out_specs=[_full((1, 1)), tok],
        out_shape=[_sds((1, 1)), _sds((s, d))],
        compiler_params=_params(("arbitrary",)),
    )(y, target)


def _mesh_position():
    return lax.axis_index("x"), lax.axis_index("y"), lax.axis_index("c")


def _other_chips(x, y):
    return [(1 - x, y), (x, 1 - y), (1 - x, 1 - y)]


ANY = pl.BlockSpec(memory_space=pl.ANY)


def _gather_over_chips(arrs):
    n = len(arrs)
    halves = [a.shape[0] // 2 for a in arrs]
    assert all(a.shape[0] % 2 == 0 for a in arrs)

    def body(*refs):
        srcs, outs = refs[:n], refs[n:2 * n]
        send_sems, recv_sems, local_sems = refs[2 * n:]
        x, y, c = _mesh_position()
        me, sibling = (x, y, c), (x, y, 1 - c)
        chips = _other_chips(x, y)
        my_chip = 2 * x + y

        def rows(t, chip, half):
            return outs[t].at[chip, pl.ds(half * halves[t], halves[t])]

        def copy(t, k, src, dst, to):
            return pltpu.make_async_remote_copy(src_ref=src, dst_ref=dst, send_sem=send_sems.at[t, k],
                                                recv_sem=recv_sems.at[t, k], device_id=to, device_id_type=MESH)

        started, local = [], []
        for t in range(n):
            cp = pltpu.make_async_copy(srcs[t], outs[t].at[my_chip], local_sems.at[t])
            cp.start()
            local.append(cp)
            for k, (px, py) in enumerate(chips):
                cp = copy(t, k, srcs[t].at[pl.ds(c * halves[t], halves[t])], rows(t, my_chip, c), (px, py, c))
                cp.start()
                started.append(cp)
        for t in range(n):
            for k, (px, py) in enumerate(chips):
                landed = rows(t, 2 * px + py, c)
                copy(t, k, landed, landed, me).wait_recv()
                cp = copy(t, 3 + k, landed, landed, sibling)
                cp.start()
                started.append(cp)
        for t in range(n):
            for k, (px, py) in enumerate(chips):
                passed = rows(t, 2 * px + py, 1 - c)
                copy(t, 3 + k, passed, passed, me).wait_recv()
        for cp in started:
            cp.wait_send()
        for cp in local:
            cp.wait()

    return pl.pallas_call(
        body, name="gather_over_chips",
        in_specs=[ANY] * n, out_specs=[ANY] * n,
        out_shape=[_sds((N_CHIPS,) + a.shape, a.dtype) for a in arrs],
        scratch_shapes=[pltpu.SemaphoreType.DMA((n, 6)), pltpu.SemaphoreType.DMA((n, 6)),
                        pltpu.SemaphoreType.DMA((n,))],
    )(*arrs)


def _send_other_half_to_sibling(grads):
    n = len(grads)
    halves = [g.shape[1] // 2 for g in grads]

    def body(*refs):
        srcs, outs = refs[:n], refs[n:2 * n]
        send_sems, recv_sems = refs[2 * n:]
        x, y, c = _mesh_position()
        copies = []
        for t in range(n):
            cp = pltpu.make_async_remote_copy(
                src_ref=srcs[t].at[pl.ds(0, N_CHIPS), pl.ds((1 - c) * halves[t], halves[t])], dst_ref=outs[t],
                send_sem=send_sems.at[t], recv_sem=recv_sems.at[t], device_id=(x, y, 1 - c), device_id_type=MESH)
            cp.start()
            copies.append(cp)
        for cp in copies:
            cp.wait_recv()
        for cp in copies:
            cp.wait_send()

    return pl.pallas_call(
        body, name="send_other_half_to_sibling",
        in_specs=[ANY] * n, out_specs=[ANY] * n,
        out_shape=[_sds((N_CHIPS, h) + g.shape[2:]) for g, h in zip(grads, halves)],
        scratch_shapes=[pltpu.SemaphoreType.DMA((n,)), pltpu.SemaphoreType.DMA((n,))],
    )(*grads)


def _send_blocks_to_chips(parts):
    n = len(parts)

    def body(*refs):
        srcs, outs = refs[:n], refs[n:2 * n]
        send_sems, recv_sems = refs[2 * n:]
        x, y, c = _mesh_position()
        copies = []
        for t in range(n):
            for k, (px, py) in enumerate(_other_chips(x, y)):
                cp = pltpu.make_async_remote_copy(
                    src_ref=srcs[t].at[2 * px + py], dst_ref=outs[t].at[k], send_sem=send_sems.at[t, k],
                    recv_sem=recv_sems.at[t, k], device_id=(px, py, c), device_id_type=MESH)
                cp.start()
                copies.append(cp)
        for cp in copies:
            cp.wait_recv()
        for cp in copies:
            cp.wait_send()

    return pl.pallas_call(
        body, name="send_blocks_to_chips",
        in_specs=[ANY] * n, out_specs=[ANY] * n,
        out_shape=[_sds((N_CHIPS - 1,) + p.shape[1:]) for p in parts],
        scratch_shapes=[pltpu.SemaphoreType.DMA((n, N_CHIPS - 1)), pltpu.SemaphoreType.DMA((n, N_CHIPS - 1))],
    )(*parts)


def _join_halves_with_sibling(halves_in):
    n = len(halves_in)

    def body(*refs):
        srcs, outs = refs[:n], refs[n:2 * n]
        send_sems, recv_sems, local_sems = refs[2 * n:]
        x, y, c = _mesh_position()
        copies, local = [], []
        for t in range(n):
            h = srcs[t].shape[0]
            cp = pltpu.make_async_copy(srcs[t], outs[t].at[pl.ds(c * h, h)], local_sems.at[t])
            cp.start()
            local.append(cp)
            cp = pltpu.make_async_remote_copy(
                src_ref=srcs[t], dst_ref=outs[t].at[pl.ds(c * h, h)], send_sem=send_sems.at[t],
                recv_sem=recv_sems.at[t], device_id=(x, y, 1 - c), device_id_type=MESH)
            cp.start()
            copies.append(cp)
        for cp in copies:
            cp.wait_recv()
        for cp in copies:
            cp.wait_send()
        for cp in local:
            cp.wait()

    return pl.pallas_call(
        body, name="join_halves_with_sibling",
        in_specs=[ANY] * n, out_specs=[ANY] * n,
        out_shape=[_sds((2 * h.shape[0],) + h.shape[1:]) for h in halves_in],
        scratch_shapes=[pltpu.SemaphoreType.DMA((n,)), pltpu.SemaphoreType.DMA((n,)), pltpu.SemaphoreType.DMA((n,))],
    )(*halves_in)


def _gather_over_devices(rows):
    r = rows.shape[0]

    def body(in_ref, out_ref, send_sems, recv_sems, local_sem):
        x, y, c = _mesh_position()
        mine = pltpu.make_async_copy(in_ref, out_ref.at[4 * x + 2 * y + c], local_sem)
        mine.start()
        copies = []
        for mask in range(1, N_DEVICES):
            fx, fy, fc = (mask >> 2) & 1, (mask >> 1) & 1, mask & 1
            px, py, pc = (1 - x if fx else x), (1 - y if fy else y), (1 - c if fc else c)
            send = pltpu.make_async_remote_copy(
                src_ref=in_ref, dst_ref=out_ref.at[4 * x + 2 * y + c], send_sem=send_sems.at[mask - 1],
                recv_sem=recv_sems.at[mask - 1], device_id=(px, py, pc), device_id_type=MESH)
            send.start()
            recv = pltpu.make_async_remote_copy(
                src_ref=in_ref, dst_ref=out_ref.at[4 * px + 2 * py + pc], send_sem=send_sems.at[mask - 1],
                recv_sem=recv_sems.at[mask - 1], device_id=(px, py, pc), device_id_type=MESH)
            copies.append((send, recv))
        for _, recv in copies:
            recv.wait_recv()
        for send, _ in copies:
            send.wait_send()
        mine.wait()

    vm = pl.BlockSpec(memory_space=pltpu.VMEM)
    return pl.pallas_call(
        body, name="gather_over_devices", in_specs=[vm], out_specs=vm,
        out_shape=_sds((N_DEVICES, r, LANES)),
        scratch_shapes=[pltpu.SemaphoreType.DMA((N_DEVICES - 1,)), pltpu.SemaphoreType.DMA((N_DEVICES - 1,)),
                        pltpu.SemaphoreType.DMA],
    )(rows)


def _add_sibling_half(grad, received, core):
    _, l, r, c = grad.shape
    half = l // 2

    def body(core_ref, g_ref, r_ref, o_ref):
        o_ref[...] = g_ref[...] + r_ref[...]

    blk = lambda f: pl.BlockSpec((None, None, r, c), f)
    grid_spec = pltpu.PrefetchScalarGridSpec(
        num_scalar_prefetch=1, grid=(N_CHIPS, half),
        in_specs=[blk(lambda j, i, core: (j, core[0] * half + i, 0, 0)), blk(lambda j, i, core: (j, i, 0, 0))],
        out_specs=blk(lambda j, i, core: (j, i, 0, 0)))
    return pl.pallas_call(
        body, name="add_sibling_half", grid_spec=grid_spec, out_shape=_sds((N_CHIPS, half, r, c)),
        compiler_params=_params(("arbitrary", "arbitrary")),
    )(core, grad, received)


def _add_chip_blocks(part, received, chip):
    _, half, r, c = part.shape

    def body(chip_ref, p_ref, r0_ref, r1_ref, r2_ref, o_ref):
        o_ref[...] = ((p_ref[...] + r0_ref[...]) + r1_ref[...]) + r2_ref[...]

    blk = lambda f: pl.BlockSpec((None, None, r, c), f)
    grid_spec = pltpu.PrefetchScalarGridSpec(
        num_scalar_prefetch=1, grid=(half,),
        in_specs=[blk(lambda i, chip: (chip[0], i, 0, 0))] + [
            blk(functools.partial(lambda i, chip, k: (k, i, 0, 0), k=k)) for k in range(N_CHIPS - 1)],
        out_specs=pl.BlockSpec((None, r, c), lambda i, chip: (i, 0, 0)))
    return pl.pallas_call(
        body, name="add_chip_blocks", grid_spec=grid_spec, out_shape=_sds((half, r, c)),
        compiler_params=_params(("arbitrary",)),
    )(chip, part, received, received, received)


def _sum_over_devices(parts):
    _, r, _ = parts.shape

    def body(p_ref, o_ref):
        acc = p_ref[0]
        for k in range(1, N_DEVICES):
            acc = acc + p_ref[k]
        o_ref[...] = acc

    return pl.pallas_call(body, name="sum_over_devices", out_shape=_sds((r, LANES)))(parts)


def _adamw_math(w, g, m, v):
    m = ADAM_B1 * m + (1.0 - ADAM_B1) * g
    v = ADAM_B2 * v + (1.0 - ADAM_B2) * (g * g)
    m_hat = m / (1.0 - ADAM_B1 ** ADAM_STEP)
    v_hat = v / (1.0 - ADAM_B2 ** ADAM_STEP)
    delta = -ADAM_LR * (m_hat / (jnp.sqrt(v_hat) + ADAM_EPS) + ADAM_WD * w)
    return delta, m, v


def _adamw_stacked(w, m, v, grads, offset):
    l, r, c = w.shape
    tr = r
    while tr * c * 4 > 2**20 and tr % 16 == 0:
        tr //= 2

    def body(w_ref, m_ref, v_ref, g_ref, go_ref, d_ref, mo_ref, vo_ref):
        g = g_ref[...]
        go_ref[...] = g
        d_ref[...], mo_ref[...], vo_ref[...] = _adamw_math(w_ref[...], g, m_ref[...], v_ref[...])

    blk = pl.BlockSpec((None, tr, c), lambda i, j: (i, j, 0))
    return pl.pallas_call(
        body, name="adamw_stacked", grid=(l, r // tr),
        in_specs=[blk, blk, blk, pl.BlockSpec((None, tr, c), lambda i, j: (offset + i, j, 0))],
        out_specs=[blk] * 4, out_shape=[_sds((l, r, c))] * 4,
        compiler_params=_params(("arbitrary", "arbitrary")),
    )(w, m, v, grads)


def _adamw_small(w, m, v, g):
    def body(w_ref, m_ref, v_ref, g_ref, d_ref, mo_ref, vo_ref):
        d_ref[...], mo_ref[...], vo_ref[...] = _adamw_math(w_ref[...], g_ref[...], m_ref[...], v_ref[...])

    return pl.pallas_call(body, name="adamw_small", out_shape=[_sds(w.shape)] * 3)(w, m, v, g)


def _pack_rows(arrs):
    flat = jnp.concatenate([a.reshape(-1) for a in arrs])
    pad = (-flat.shape[0]) % (8 * LANES)
    return jnp.pad(flat, (0, pad)).reshape(-1, LANES)


def _unpack_rows(rows, shapes, lead=()):
    flat = rows.reshape(lead + (-1,))
    out, at = [], 0
    for shp in shapes:
        size = int(np.prod(shp))
        out.append(flat[..., at:at + size].reshape(lead + tuple(shp)))
        at += size
    return out


WEIGHT_NAMES = ('ffn1_norm', 'ffn1_w_gate', 'ffn1_w_up', 'ffn1_w_down', 'mix_norm', 'pool_w', 'pool_scale',
                'mla_w_in', 'mla_q_norm', 'mla_w_q_up', 'mla_kv_norm', 'mla_w_kv_up', 'mla_q_head_norm',
                'mla_k_head_norm', 'mla_w_out', 'ffn2_norm', 'ffn2_w_gate', 'ffn2_w_up', 'ffn2_w_down')


def _chips_to_columns(g):
    return jnp.transpose(g, (1, 2, 0, 3)).reshape(g.shape[1], g.shape[2], -1)


def _columns_to_chips(full):
    n, r, c4 = full.shape
    return jnp.transpose(full.reshape(n, r, N_CHIPS, c4 // N_CHIPS), (2, 0, 1, 3))


def kernel(x, positions, ffn1_norm, ffn1_w_gate, ffn1_w_up, ffn1_w_down, mix_norm, pool_w, pool_scale, mla_w_in, mla_q_norm, mla_w_q_up, mla_kv_norm, mla_w_kv_up, mla_q_head_norm, mla_k_head_norm, mla_w_out, ffn2_norm, ffn2_w_gate, ffn2_w_up, ffn2_w_down, loss_target, m_ffn1_norm, m_ffn1_w_gate, m_ffn1_w_up, m_ffn1_w_down, m_mix_norm, m_pool_w, m_pool_scale, m_mla_w_in, m_mla_q_norm, m_mla_w_q_up, m_mla_kv_norm, m_mla_w_kv_up, m_mla_q_head_norm, m_mla_k_head_norm, m_mla_w_out, m_ffn2_norm, m_ffn2_w_gate, m_ffn2_w_up, m_ffn2_w_down, v_ffn1_norm, v_ffn1_w_gate, v_ffn1_w_up, v_ffn1_w_down, v_mix_norm, v_pool_w, v_pool_scale, v_mla_w_in, v_mla_q_norm, v_mla_w_q_up, v_mla_kv_norm, v_mla_w_kv_up, v_mla_q_head_norm, v_mla_k_head_norm, v_mla_w_out, v_ffn2_norm, v_ffn2_w_gate, v_ffn2_w_up, v_ffn2_w_down):
    env = dict(locals())
    w = {n: env[n] for n in WEIGHT_NAMES}
    mom = {n: env["m_" + n] for n in WEIGHT_NAMES}
    var = {n: env["v_" + n] for n in WEIGHT_NAMES}

    s, d = x.shape[1], x.shape[2]
    depth = ffn1_norm.shape[0]
    n_mla, n_pool, n_groups = mla_w_in.shape[0], pool_w.shape[0], pool_w.shape[1]
    pool_c = pool_w.shape[3]
    q_lora = N_CHIPS * mla_q_norm.shape[1]
    kv_lora = N_CHIPS * mla_kv_norm.shape[1]
    n_heads = N_CHIPS * mla_w_q_up.shape[2] // QK_HEAD
    cx, cy, cc = _mesh_position()
    chip = 2 * cx + cy
    chip_arr = jnp.reshape(chip, (1,)).astype(jnp.int32)
    core_arr = jnp.reshape(cc, (1,)).astype(jnp.int32)

    shard_gu = _cast(jnp.concatenate([ffn1_w_gate, ffn1_w_up, ffn2_w_gate, ffn2_w_up], axis=0))
    shard_dn = _cast(jnp.concatenate([ffn1_w_down, ffn2_w_down], axis=0))
    shard_pool = _cast(pool_w.reshape((n_pool * n_groups,) + pool_w.shape[2:]))
    w_gu, w_dn, g_in, g_qup, g_kvup, g_out, g_pool = _gather_over_chips(
        [shard_gu, shard_dn, _cast(mla_w_in), _cast(mla_w_q_up), _cast(mla_w_kv_up), _cast(mla_w_out), shard_pool])
    small_shapes = [mla_q_norm.shape, mla_kv_norm.shape]
    small = _gather_over_devices(_pack_rows([mla_q_norm, mla_kv_norm]))[::2]
    qn_chips, kvn_chips = _unpack_rows(small, small_shapes, lead=(N_CHIPS,))
    q_norm_full = jnp.transpose(qn_chips, (1, 0, 2)).reshape(n_mla, 1, q_lora)
    kv_norm_full = jnp.transpose(kvn_chips, (1, 0, 2)).reshape(n_mla, 1, kv_lora)

    w_in_full = _chips_to_columns(g_in)
    w_q_heads = jnp.transpose(_chips_to_columns(g_qup).reshape(n_mla, q_lora, n_heads, QK_HEAD), (0, 2, 1, 3))
    w_kv = _chips_to_columns(g_kvup).reshape(n_mla, kv_lora, n_heads, QK_NOPE + V_HEAD)
    w_kn_heads = jnp.transpose(w_kv[..., :QK_NOPE], (0, 2, 1, 3))
    w_v_full = w_kv[..., QK_NOPE:].reshape(n_mla, kv_lora, n_heads * V_HEAD)
    w_out_full = jnp.transpose(g_out, (1, 0, 2, 3)).reshape(n_mla, n_heads * V_HEAD, d)
    pool_full = jnp.transpose(g_pool.reshape(N_CHIPS, n_pool, n_groups, pool_c // N_CHIPS, pool_c),
                              (1, 2, 0, 3, 4)).reshape(n_pool, n_groups, pool_c, pool_c)

    inv_freq = (1.0 / (ROPE_THETA ** (jnp.arange(0, QK_ROPE, 2, dtype=F32) / QK_ROPE))).reshape(1, -1)
    cos_t, sin_t = _rope_tables(positions.reshape(s, 1), inv_freq)

    row = lambda a, i: a[i].reshape(1, -1)
    i_gate1, i_up1, i_gate2, i_up2 = (lambda i: i), (lambda i: depth + i), (lambda i: 2 * depth + i), (lambda i: 3 * depth + i)
    i_dn1, i_dn2 = (lambda i: i), (lambda i: depth + i)

    h = x.reshape(s, d)
    saved = []
    for i in range(depth):
        rec = {"x_ffn1": h}
        h = _ffn_fwd(h, row(ffn1_norm, i), w_gu, w_dn, i_gate1(i), i_up1(i), i_dn1(i))
        rec["x_mix"] = h
        j = i // 2
        if i % 2 == 0:
            h = _pool_fwd(h, row(mix_norm, i), pool_full[j], row(pool_scale, j))
        else:
            lat, q, k, v = _mla_qkv_fwd(h, row(mix_norm, i), w_in_full[j], q_norm_full[j], kv_norm_full[j],
                                        w_q_heads[j], w_kn_heads[j], w_v_full[j], row(mla_q_head_norm, j),
                                        row(mla_k_head_norm, j), cos_t, sin_t)
            o, lse = _flash_fwd(q, k, v)
            rec.update(lat=lat, q=q, k=k, v=v, o=o, lse=lse)
            h = _mla_out_fwd(h, o, w_out_full[j])
        rec["x_ffn2"] = h
        h = _ffn_fwd(h, row(ffn2_norm, i), w_gu, w_dn, i_gate2(i), i_up2(i), i_dn2(i))
        saved.append(rec)

    loss_part, dy = _loss_and_grad(h, loss_target.reshape(s, d))
    loss = lax.psum(loss_part[0, 0], ("x", "y", "c"))

    g_gu = [None] * (4 * depth)
    g_dn = [None] * (2 * depth)
    g_norm = {n: [None] * depth for n in ("ffn1_norm", "mix_norm", "ffn2_norm")}
    g_pool_w, g_pool_scale = [None] * n_pool, [None] * n_pool
    g_mla = {n: [None] * n_mla for n in ("w_in", "q_norm", "kv_norm", "w_q", "w_kv", "qhn", "khn", "w_out")}
    for i in reversed(range(depth)):
        rec = saved[i]
        dy, g_norm["ffn2_norm"][i], hb, dyb, dgt, dup, act = _ffn_bwd_dgrad(
            rec["x_ffn2"], row(ffn2_norm, i), dy, w_gu, w_dn, i_gate2(i), i_up2(i), i_dn2(i))
        g_gu[i_gate2(i)], g_gu[i_up2(i)], g_dn[i_dn2(i)] = _ffn_wgrad(hb, dyb, dgt, dup, act)
        j = i // 2
        if i % 2 == 0:
            dy, g_norm["mix_norm"][i], g_pool_w[j], g_pool_scale[j] = _pool_bwd(
                rec["x_mix"], row(mix_norm, i), pool_full[j], row(pool_scale, j), dy)
        else:
            do, delta, g_mla["w_out"][j] = _mla_out_bwd(dy, rec["o"], w_out_full[j])
            dq = _flash_bwd_dq(rec["q"], rec["k"], rec["v"], do, rec["lse"], delta)
            dk, dv = _flash_bwd_dkv(rec["q"], rec["k"], rec["v"], do, rec["lse"], delta)
            (dy, g_norm["mix_norm"][i], g_mla["w_in"][j], g_mla["q_norm"][j], g_mla["kv_norm"][j], dwq, dwkn, dwv,
             g_mla["qhn"][j], g_mla["khn"][j]) = _mla_qkv_bwd(
                rec["x_mix"], rec["lat"], dy, dq, dk, dv, row(mix_norm, i), w_in_full[j], q_norm_full[j],
                kv_norm_full[j], w_q_heads[j], w_kn_heads[j], w_v_full[j], row(mla_q_head_norm, j),
                row(mla_k_head_norm, j), cos_t, sin_t)
            g_mla["w_q"][j] = jnp.transpose(dwq, (1, 0, 2)).reshape(q_lora, n_heads * QK_HEAD)
            g_mla["w_kv"][j] = jnp.concatenate(
                [jnp.transpose(dwkn, (1, 0, 2)), dwv.reshape(kv_lora, n_heads, V_HEAD)], axis=-1
            ).reshape(kv_lora, n_heads * (QK_NOPE + V_HEAD))
        dy, g_norm["ffn1_norm"][i], hb, dyb, dgt, dup, act = _ffn_bwd_dgrad(
            rec["x_ffn1"], row(ffn1_norm, i), dy, w_gu, w_dn, i_gate1(i), i_up1(i), i_dn1(i))
        g_gu[i_gate1(i)], g_gu[i_up1(i)], g_dn[i_dn1(i)] = _ffn_wgrad(hb, dyb, dgt, dup, act)
    grad_x = dy.reshape(x.shape)

    full_grads = [
        jnp.stack(g_gu, axis=1),
        jnp.stack(g_dn, axis=1),
        _columns_to_chips(jnp.stack(g_mla["w_in"])),
        _columns_to_chips(jnp.stack(g_mla["w_q"])),
        _columns_to_chips(jnp.stack(g_mla["w_kv"])),
        jnp.transpose(jnp.stack(g_mla["w_out"]).reshape(n_mla, N_CHIPS, -1, d), (1, 0, 2, 3)),
        jnp.transpose(jnp.stack(g_pool_w).reshape(n_pool, n_groups, N_CHIPS, pool_c // N_CHIPS, pool_c),
                      (2, 0, 1, 3, 4)).reshape(N_CHIPS, n_pool * n_groups, pool_c // N_CHIPS, pool_c),
    ]
    from_sibling = _send_other_half_to_sibling(full_grads)
    chip_sums = [_add_sibling_half(g, r, core_arr) for g, r in zip(full_grads, from_sibling)]
    from_chips = _send_blocks_to_chips(chip_sums)
    half_sums = [_add_chip_blocks(p, r, chip_arr) for p, r in zip(chip_sums, from_chips)]
    r_gu, r_dn, r_in, r_qup, r_kvup, r_out, r_pool = _join_halves_with_sibling(half_sums)

    small_grads = [jnp.concatenate(g_norm["ffn1_norm"]), jnp.concatenate(g_norm["mix_norm"]),
                   jnp.concatenate(g_norm["ffn2_norm"]), jnp.concatenate(g_pool_scale),
                   jnp.concatenate(g_mla["qhn"]), jnp.concatenate(g_mla["khn"]),
                   jnp.concatenate(g_mla["q_norm"]), jnp.concatenate(g_mla["kv_norm"])]
    small_sum = _sum_over_devices(_gather_over_devices(_pack_rows(small_grads)))
    (s_ffn1, s_mix, s_ffn2, s_pscale, s_qhn, s_khn, s_qn, s_kvn) = _unpack_rows(small_sum, [g.shape for g in small_grads])
    qn_w, kvn_w = mla_q_norm.shape[1], mla_kv_norm.shape[1]
    s_qn = lax.dynamic_slice_in_dim(s_qn, chip * qn_w, qn_w, axis=1)
    s_kvn = lax.dynamic_slice_in_dim(s_kvn, chip * kvn_w, kvn_w, axis=1)

    grads, deltas, new_m, new_v = {}, {}, {}, {}

    def stacked(name, reduced, offset):
        shape = w[name].shape
        as3 = lambda a: a.reshape((-1,) + shape[-2:])
        out = _adamw_stacked(as3(w[name]), as3(mom[name]), as3(var[name]), reduced, offset)
        grads[name], deltas[name], new_m[name], new_v[name] = [o.reshape(shape) for o in out]

    def small_update(name, g):
        grads[name] = g
        deltas[name], new_m[name], new_v[name] = _adamw_small(w[name], mom[name], var[name], g)

    stacked("ffn1_w_gate", r_gu, 0)
    stacked("ffn1_w_up", r_gu, depth)
    stacked("ffn2_w_gate", r_gu, 2 * depth)
    stacked("ffn2_w_up", r_gu, 3 * depth)
    stacked("ffn1_w_down", r_dn, 0)
    stacked("ffn2_w_down", r_dn, depth)
    stacked("mla_w_in", r_in, 0)
    stacked("mla_w_q_up", r_qup, 0)
    stacked("mla_w_kv_up", r_kvup, 0)
    stacked("mla_w_out", r_out, 0)
    stacked("pool_w", r_pool, 0)
    small_update("ffn1_norm", s_ffn1)
    small_update("mix_norm", s_mix)
    small_update("ffn2_norm", s_ffn2)
    small_update("pool_scale", s_pscale)
    small_update("mla_q_head_norm", s_qhn)
    small_update("mla_k_head_norm", s_khn)
    small_update("mla_q_norm", s_qn)
    small_update("mla_kv_norm", s_kvn)

    return (loss, grad_x, *[grads[n] for n in WEIGHT_NAMES], *[deltas[n] for n in WEIGHT_NAMES],
            *[new_m[n] for n in WEIGHT_NAMES], *[new_v[n] for n in WEIGHT_NAMES])
```

```python
import functools

import numpy as np

import jax
import jax.numpy as jnp
from jax import lax
from jax.experimental import pallas as pl
from jax.experimental.pallas import tpu as pltpu

F32 = jnp.float32
MXU_DTYPE = jnp.bfloat16
MESH = pl.DeviceIdType.MESH
N_CHIPS = 4
N_DEVICES = 8
LANES = 128
VMEM_LIMIT_BYTES = 56 * 2**20
NORM_EPS = 1e-6
QK_NOPE, QK_ROPE, V_HEAD = 128, 64, 128
QK_HEAD = QK_NOPE + QK_ROPE
ROPE_THETA = 10000.0
POOL_WINDOWS = (2, 4, 8, 16)
POOL_HALO = 16
FFN_HALF = 0.5
ADAM_LR, ADAM_B1, ADAM_B2, ADAM_EPS, ADAM_WD, ADAM_STEP = 0.001, 0.9, 0.999, 1e-08, 0.01, 10
FFN_TILE = 512
WGRAD_TILE = 512
MLA_TILE = 256
POOL_TILE = 512
ATTN_TILE = 512
ROW_TILE = 1024


def _cast(v):
    return v.astype(MXU_DTYPE)


def _mm(a, b):
    return jnp.dot(a, b, preferred_element_type=F32)


def _mm_nt(a, b):
    return lax.dot_general(a, b, (((1,), (1,)), ((), ())), preferred_element_type=F32)


def _mm_tn(a, b):
    return lax.dot_general(a, b, (((0,), (0,)), ((), ())), preferred_element_type=F32)


def _rms_fwd(v, gain):
    r = lax.rsqrt(jnp.mean(v * v, axis=-1, keepdims=True) + NORM_EPS)
    return v * r * gain, r


def _rms_bwd(v, r, gain, dy):
    vr = v * r
    gy = dy * gain
    dv = r * (gy - vr * jnp.mean(gy * vr, axis=-1, keepdims=True))
    return dv, jnp.sum(dy * vr, axis=0, keepdims=True)


def _params(semantics=None):
    return pltpu.CompilerParams(dimension_semantics=semantics, vmem_limit_bytes=VMEM_LIMIT_BYTES)


def _tile(n, want):
    t = min(n, want)
    assert n % t == 0, (n, want)
    return t


def _full(shape):
    nd = len(shape)
    return pl.BlockSpec(shape, lambda *_: (0,) * nd)


def _sds(shape, dtype=F32):
    return jax.ShapeDtypeStruct(shape, dtype)


def _ffn_fwd(x, gain, w_gu, w_dn, i_gate, i_up, i_down):
    s, d = x.shape
    fs = w_gu.shape[-1]
    tm = _tile(s, FFN_TILE)

    def body(x_ref, g_ref, wg_ref, wu_ref, wd_ref, y_ref, h_sc, acc_sc):
        j = pl.program_id(1)

        @pl.when(j == 0)
        def _():
            h, _ = _rms_fwd(x_ref[...], g_ref[...])
            h_sc[...] = _cast(h)
            acc_sc[...] = jnp.zeros_like(acc_sc)

        h = h_sc[...]
        g = _mm(h, wg_ref[...])
        u = _mm(h, wu_ref[...])
        act = (g * jax.nn.sigmoid(g)) * u
        acc_sc[...] += _mm(_cast(act), wd_ref[...])

        @pl.when(j == N_CHIPS - 1)
        def _():
            y_ref[...] = x_ref[...] + FFN_HALF * acc_sc[...]

    return pl.pallas_call(
        body, name="ffn_fwd", grid=(s // tm, N_CHIPS),
        in_specs=[
            pl.BlockSpec((tm, d), lambda i, j: (i, 0)),
            _full((1, d)),
            pl.BlockSpec((None, None, d, fs), lambda i, j: (j, i_gate, 0, 0)),
            pl.BlockSpec((None, None, d, fs), lambda i, j: (j, i_up, 0, 0)),
            pl.BlockSpec((None, None, fs, d), lambda i, j: (j, i_down, 0, 0)),
        ],
        out_specs=pl.BlockSpec((tm, d), lambda i, j: (i, 0)),
        out_shape=_sds((s, d)),
        scratch_shapes=[pltpu.VMEM((tm, d), MXU_DTYPE), pltpu.VMEM((tm, d), F32)],
        compiler_params=_params(("arbitrary", "arbitrary")),
    )(x, gain, w_gu, w_gu, w_dn)


def _ffn_bwd_dgrad(x, gain, dy, w_gu, w_dn, i_gate, i_up, i_down):
    s, d = x.shape
    fs = w_gu.shape[-1]
    tm = _tile(s, FFN_TILE)
    n_tiles = s // tm

    def body(x_ref, g_ref, dy_ref, wg_ref, wu_ref, wd_ref,
             dx_ref, dgain_ref, hb_ref, dyb_ref, dg_ref, du_ref, act_ref, h_sc, r_sc, dh_sc):
        i, j = pl.program_id(0), pl.program_id(1)

        @pl.when(j == 0)
        def _():
            h, r = _rms_fwd(x_ref[...], g_ref[...])
            h_sc[...] = _cast(h)
            r_sc[...] = r
            dh_sc[...] = jnp.zeros_like(dh_sc)
            hb_ref[...] = _cast(h)
            dyb_ref[...] = _cast(dy_ref[...])

        h = h_sc[...]
        dyb = _cast(dy_ref[...])
        g = _mm(h, wg_ref[...])
        u = _mm(h, wu_ref[...])
        sg = jax.nn.sigmoid(g)
        silu = g * sg
        dact = FFN_HALF * _mm_nt(dyb, wd_ref[...])
        dgb = _cast(dact * u * (sg * (1.0 + g * (1.0 - sg))))
        dub = _cast(dact * silu)
        dg_ref[...] = dgb
        du_ref[...] = dub
        act_ref[...] = _cast(silu * u)
        dh_sc[...] += _mm_nt(dgb, wg_ref[...]) + _mm_nt(dub, wu_ref[...])

        @pl.when(j == N_CHIPS - 1)
        def _():
            dxn, dgn = _rms_bwd(x_ref[...], r_sc[...], g_ref[...], dh_sc[...])
            dx_ref[...] = dy_ref[...] + dxn

            @pl.when(i == 0)
            def _():
                dgain_ref[...] = dgn

            @pl.when(i > 0)
            def _():
                dgain_ref[...] += dgn

    tok = pl.BlockSpec((tm, d), lambda i, j: (i, 0))
    chunk = pl.BlockSpec((None, tm, fs), lambda i, j: (j, i, 0))
    return pl.pallas_call(
        body, name="ffn_bwd_dgrad", grid=(n_tiles, N_CHIPS),
        in_specs=[
            tok, _full((1, d)), tok,
            pl.BlockSpec((None, None, d, fs), lambda i, j: (j, i_gate, 0, 0)),
            pl.BlockSpec((None, None, d, fs), lambda i, j: (j, i_up, 0, 0)),
            pl.BlockSpec((None, None, fs, d), lambda i, j: (j, i_down, 0, 0)),
        ],
        out_specs=[tok, _full((1, d)), tok, tok, chunk, chunk, chunk],
        out_shape=[_sds((s, d)), _sds((1, d)), _sds((s, d), MXU_DTYPE), _sds((s, d), MXU_DTYPE),
                   _sds((N_CHIPS, s, fs), MXU_DTYPE), _sds((N_CHIPS, s, fs), MXU_DTYPE),
                   _sds((N_CHIPS, s, fs), MXU_DTYPE)],
        scratch_shapes=[pltpu.VMEM((tm, d), MXU_DTYPE), pltpu.VMEM((tm, 1), F32), pltpu.VMEM((tm, d), F32)],
        compiler_params=_params(("arbitrary", "arbitrary")),
    )(x, gain, dy, w_gu, w_gu, w_dn)


def _ffn_wgrad(hb, dyb, dg, du, act):
    s, d = hb.shape
    fs = dg.shape[-1]
    tk = _tile(s, WGRAD_TILE)
    n_k = s // tk

    def body(h_ref, dy_ref, dg_ref, du_ref, act_ref, wg_ref, wu_ref, wd_ref):
        k = pl.program_id(1)

        @pl.when(k == 0)
        def _():
            wg_ref[...] = jnp.zeros_like(wg_ref)
            wu_ref[...] = jnp.zeros_like(wu_ref)
            wd_ref[...] = jnp.zeros_like(wd_ref)

        h = h_ref[...]
        wg_ref[...] += _mm_tn(h, dg_ref[...])
        wu_ref[...] += _mm_tn(h, du_ref[...])
        wd_ref[...] += FFN_HALF * _mm_tn(act_ref[...], dy_ref[...])

    tok = pl.BlockSpec((tk, d), lambda j, k: (k, 0))
    chunk = pl.BlockSpec((None, tk, fs), lambda j, k: (j, k, 0))
    return pl.pallas_call(
        body, name="ffn_wgrad", grid=(N_CHIPS, n_k),
        in_specs=[tok, tok, chunk, chunk, chunk],
        out_specs=[pl.BlockSpec((None, d, fs), lambda j, k: (j, 0, 0)),
                   pl.BlockSpec((None, d, fs), lambda j, k: (j, 0, 0)),
                   pl.BlockSpec((None, fs, d), lambda j, k: (j, 0, 0))],
        out_shape=[_sds((N_CHIPS, d, fs)), _sds((N_CHIPS, d, fs)), _sds((N_CHIPS, fs, d))],
        compiler_params=_params(("arbitrary", "arbitrary")),
    )(hb, dyb, dg, du, act)


def _inv_count(first_row, n_rows, window):
    t = first_row + lax.broadcasted_iota(jnp.int32, (n_rows, 1), 0)
    return 1.0 / jnp.minimum(t + 1, window).astype(F32)


def _trailing_sum(v, window):
    k = 1
    while k < window:
        v = v + pltpu.roll(v, k, 0)
        k *= 2
    return v


def _leading_sum(v, window):
    n = v.shape[0]
    k = 1
    while k < window:
        v = v + pltpu.roll(v, n - k, 0)
        k *= 2
    return v


def _pool_normed_rows(x_ref, prev_ref, g_ref, i):
    h, r = _rms_fwd(x_ref[...], g_ref[...])
    hp, _ = _rms_fwd(prev_ref[...], g_ref[...])
    hp = jnp.where(i > 0, hp, 0.0)
    return jnp.concatenate([hp, h], axis=0), r


def _pooled_group(he, g, pg, first_row, tm):
    ue = he[:, g * pg:(g + 1) * pg]
    win = _trailing_sum(ue, POOL_WINDOWS[g])[POOL_HALO:]
    return win * _inv_count(first_row, tm, POOL_WINDOWS[g]) - ue[POOL_HALO:]


def _pool_specs(s, d, tm):
    per = tm // POOL_HALO
    last = s // POOL_HALO - 1
    tok = pl.BlockSpec((tm, d), lambda i: (i, 0))
    prev = pl.BlockSpec((POOL_HALO, d), lambda i: (jnp.maximum(i * per - 1, 0), 0))
    nxt = pl.BlockSpec((POOL_HALO, d), lambda i: (jnp.minimum((i + 1) * per, last), 0))
    return tok, prev, nxt


def _pool_fwd(x, gain, w, scale):
    s, d = x.shape
    n_g, pg = w.shape[0], w.shape[-1]
    tm = _tile(s, POOL_TILE)
    tok, prev, _ = _pool_specs(s, d, tm)

    def body(x_ref, prev_ref, g_ref, w_ref, sc_ref, y_ref):
        i = pl.program_id(0)
        he, _ = _pool_normed_rows(x_ref, prev_ref, g_ref, i)
        z = [_mm(_cast(_pooled_group(he, g, pg, i * tm, tm)), w_ref[g]) for g in range(n_g)]
        y_ref[...] = x_ref[...] + jnp.concatenate(z, axis=-1) * sc_ref[...]

    return pl.pallas_call(
        body, name="pool_fwd", grid=(s // tm,),
        in_specs=[tok, prev, _full((1, d)), _full(w.shape), _full((1, d))],
        out_specs=tok, out_shape=_sds((s, d)),
        compiler_params=_params(("arbitrary",)),
    )(x, x, gain, w, scale)


def _pool_bwd(x, gain, w, scale, dy):
    s, d = x.shape
    n_g, pg = w.shape[0], w.shape[-1]
    tm = _tile(s, POOL_TILE)
    n_tiles = s // tm
    tok, prev, nxt = _pool_specs(s, d, tm)

    def body(x_ref, prev_ref, dy_ref, next_ref, g_ref, w_ref, sc_ref, dx_ref, dgain_ref, dw_ref, dsc_ref):
        i = pl.program_id(0)

        @pl.when(i == 0)
        def _():
            dgain_ref[...] = jnp.zeros_like(dgain_ref)
            dw_ref[...] = jnp.zeros_like(dw_ref)
            dsc_ref[...] = jnp.zeros_like(dsc_ref)

        he, r = _pool_normed_rows(x_ref, prev_ref, g_ref, i)
        dy = dy_ref[...]
        dyn = jnp.where(i < n_tiles - 1, next_ref[...], 0.0)
        dze = jnp.concatenate([dy, dyn], axis=0) * sc_ref[...]
        dh, dsc = [], []
        for g in range(n_g):
            cols = slice(g * pg, (g + 1) * pg)
            pooled = _cast(_pooled_group(he, g, pg, i * tm, tm))
            dsc.append(jnp.sum(dy[:, cols] * _mm(pooled, w_ref[g]), axis=0, keepdims=True))
            dzb = _cast(dze[:, cols])
            dw_ref[g] += _mm_tn(pooled, dzb[:tm])
            dpool = _mm_nt(dzb, w_ref[g])
            spread = _leading_sum(dpool * _inv_count(i * tm, tm + POOL_HALO, POOL_WINDOWS[g]), POOL_WINDOWS[g])
            dh.append(spread[:tm] - dpool[:tm])
        dsc_ref[...] += jnp.concatenate(dsc, axis=-1)
        dxn, dgn = _rms_bwd(x_ref[...], r, g_ref[...], jnp.concatenate(dh, axis=-1))
        dgain_ref[...] += dgn
        dx_ref[...] = dy + dxn

    return pl.pallas_call(
        body, name="pool_bwd", grid=(n_tiles,),
        in_specs=[tok, prev, tok, nxt, _full((1, d)), _full(w.shape), _full((1, d))],
        out_specs=[tok, _full((1, d)), _full(w.shape), _full((1, d))],
        out_shape=[_sds((s, d)), _sds((1, d)), _sds(w.shape), _sds((1, d))],
        compiler_params=_params(("arbitrary",)),
    )(x, x, dy, dy, gain, w, scale)


def _rope_tables(pos_col, inv_freq):
    s = pos_col.shape[0]
    tm = _tile(s, ROW_TILE)
    half = QK_ROPE // 2

    def body(p_ref, f_ref, c_ref, s_ref):
        ang = p_ref[...].astype(F32) * f_ref[...]
        cos, sin = jnp.cos(ang), jnp.sin(ang)
        c_ref[...] = jnp.concatenate([jnp.ones((tm, QK_NOPE), F32), cos, cos], axis=-1)
        s_ref[...] = jnp.concatenate([jnp.zeros((tm, QK_NOPE), F32), -sin, sin], axis=-1)

    tab = pl.BlockSpec((tm, QK_HEAD), lambda i: (i, 0))
    return pl.pallas_call(
        body, name="rope_tables", grid=(s // tm,),
        in_specs=[pl.BlockSpec((tm, 1), lambda i: (i, 0)), _full((1, half))],
        out_specs=[tab, tab], out_shape=[_sds((s, QK_HEAD)), _sds((s, QK_HEAD))],
        compiler_params=_params(("arbitrary",)),
    )(pos_col, inv_freq)


def _swap_rope_halves(v):
    half = QK_ROPE // 2
    return jnp.concatenate([v[:, :QK_NOPE], v[:, QK_NOPE + half:], v[:, QK_NOPE:QK_NOPE + half]], axis=-1)


def _rope(v, cos, sin):
    return v * cos + _swap_rope_halves(v) * sin


def _rope_transposed(dv, cos, sin):
    return dv * cos + _swap_rope_halves(dv * sin)


def _mla_qkv_fwd(x, gain, w_in, q_norm, kv_norm, w_q, w_kn, w_v, q_head_norm, k_head_norm, cos, sin):
    s, d = x.shape
    n_h, ql = w_q.shape[0], w_q.shape[1]
    kvl, lat_w = w_kn.shape[1], w_in.shape[1]
    tm = _tile(s, MLA_TILE)

    def body(x_ref, g_ref, win_ref, qn_ref, kvn_ref, wq_ref, wkn_ref, wv_ref, qhn_ref, khn_ref, c_ref, s_ref,
             lat_ref, q_ref, k_ref, v_ref):
        h, _ = _rms_fwd(x_ref[...], g_ref[...])
        lat = _mm(_cast(h), win_ref[...])
        lat_ref[...] = lat
        cqn, _ = _rms_fwd(lat[:, :ql], qn_ref[...])
        ckvn, _ = _rms_fwd(lat[:, ql:ql + kvl], kvn_ref[...])
        kpe = lat[:, ql + kvl:]
        cqb, ckb = _cast(cqn), _cast(ckvn)
        cos_t, sin_t = c_ref[...], s_ref[...]
        v_ref[...] = _cast(_mm(ckb, wv_ref[...]))
        for hh in range(n_h):
            qn, _ = _rms_fwd(_mm(cqb, wq_ref[hh]), qhn_ref[...])
            q_ref[hh] = _cast(_rope(qn, cos_t, sin_t))
            kn, _ = _rms_fwd(jnp.concatenate([_mm(ckb, wkn_ref[hh]), kpe], axis=-1), khn_ref[...])
            k_ref[hh] = _cast(_rope(kn, cos_t, sin_t))

    tok = lambda w: pl.BlockSpec((tm, w), lambda i: (i, 0))
    heads = pl.BlockSpec((n_h, tm, QK_HEAD), lambda i: (0, i, 0))
    return pl.pallas_call(
        body, name="mla_qkv_fwd", grid=(s // tm,),
        in_specs=[tok(d), _full((1, d)), _full(w_in.shape), _full((1, ql)), _full((1, kvl)), _full(w_q.shape),
                  _full(w_kn.shape), _full(w_v.shape), _full((1, QK_HEAD)), _full((1, QK_HEAD)),
                  tok(QK_HEAD), tok(QK_HEAD)],
        out_specs=[tok(lat_w), heads, heads, tok(n_h * V_HEAD)],
        out_shape=[_sds((s, lat_w)), _sds((n_h, s, QK_HEAD), MXU_DTYPE), _sds((n_h, s, QK_HEAD), MXU_DTYPE),
                   _sds((s, n_h * V_HEAD), MXU_DTYPE)],
        compiler_params=_params(("arbitrary",)),
    )(x, gain, w_in, q_norm, kv_norm, w_q, w_kn, w_v, q_head_norm, k_head_norm, cos, sin)


def _mla_qkv_bwd(x, lat, dy, dq, dk, dv, gain, w_in, q_norm, kv_norm, w_q, w_kn, w_v, q_head_norm, k_head_norm,
                 cos, sin):
    s, d = x.shape
    n_h, ql = w_q.shape[0], w_q.shape[1]
    kvl, lat_w = w_kn.shape[1], w_in.shape[1]
    tm = _tile(s, MLA_TILE)

    def body(x_ref, lat_ref, dy_ref, dq_ref, dk_ref, dv_ref, g_ref, win_ref, qn_ref, kvn_ref, wq_ref, wkn_ref,
             wv_ref, qhn_ref, khn_ref, c_ref, s_ref,
             dx_ref, dg_ref, dwin_ref, dqn_ref, dkvn_ref, dwq_ref, dwkn_ref, dwv_ref, dqhn_ref, dkhn_ref):
        @pl.when(pl.program_id(0) == 0)
        def _():
            for ref in (dg_ref, dwin_ref, dqn_ref, dkvn_ref, dwq_ref, dwkn_ref, dwv_ref, dqhn_ref, dkhn_ref):
                ref[...] = jnp.zeros_like(ref)

        x_t = x_ref[...]
        h, r = _rms_fwd(x_t, g_ref[...])
        hb = _cast(h)
        lat = lat_ref[...]
        cq, ckv, kpe = lat[:, :ql], lat[:, ql:ql + kvl], lat[:, ql + kvl:]
        cqn, rq = _rms_fwd(cq, qn_ref[...])
        ckvn, rkv = _rms_fwd(ckv, kvn_ref[...])
        cqb, ckb = _cast(cqn), _cast(ckvn)
        cos_t, sin_t = c_ref[...], s_ref[...]

        dvb = _cast(dv_ref[...])
        dwv_ref[...] += _mm_tn(ckb, dvb)
        dckvn = _mm_nt(dvb, wv_ref[...])
        dcqn = jnp.zeros((tm, ql), F32)
        dkpe = jnp.zeros((tm, QK_ROPE), F32)
        dqhn = jnp.zeros((1, QK_HEAD), F32)
        dkhn = jnp.zeros((1, QK_HEAD), F32)
        for hh in range(n_h):
            qp = _mm(cqb, wq_ref[hh])
            _, rqp = _rms_fwd(qp, qhn_ref[...])
            dqp, dgq = _rms_bwd(qp, rqp, qhn_ref[...], _rope_transposed(dq_ref[hh], cos_t, sin_t))
            dqhn += dgq
            dqpb = _cast(dqp)
            dwq_ref[hh] += _mm_tn(cqb, dqpb)
            dcqn += _mm_nt(dqpb, wq_ref[hh])

            kp = jnp.concatenate([_mm(ckb, wkn_ref[hh]), kpe], axis=-1)
            _, rkp = _rms_fwd(kp, khn_ref[...])
            dkp, dgk = _rms_bwd(kp, rkp, khn_ref[...], _rope_transposed(dk_ref[hh], cos_t, sin_t))
            dkhn += dgk
            dknb = _cast(dkp[:, :QK_NOPE])
            dkpe += dkp[:, QK_NOPE:]
            dwkn_ref[hh] += _mm_tn(ckb, dknb)
            dckvn += _mm_nt(dknb, wkn_ref[hh])
        dqhn_ref[...] += dqhn
        dkhn_ref[...] += dkhn

        dcq, dgn = _rms_bwd(cq, rq, qn_ref[...], dcqn)
        dqn_ref[...] += dgn
        dckv, dgn = _rms_bwd(ckv, rkv, kvn_ref[...], dckvn)
        dkvn_ref[...] += dgn
        dlb = _cast(jnp.concatenate([dcq, dckv, dkpe], axis=-1))
        dwin_ref[...] += _mm_tn(hb, dlb)
        dxn, dgn = _rms_bwd(x_t, r, g_ref[...], _mm_nt(dlb, win_ref[...]))
        dg_ref[...] += dgn
        dx_ref[...] = dy_ref[...] + dxn

    tok = lambda w: pl.BlockSpec((tm, w), lambda i: (i, 0))
    heads = pl.BlockSpec((n_h, tm, QK_HEAD), lambda i: (0, i, 0))
    return pl.pallas_call(
        body, name="mla_qkv_bwd", grid=(s // tm,),
        in_specs=[tok(d), tok(lat_w), tok(d), heads, heads, tok(n_h * V_HEAD), _full((1, d)), _full(w_in.shape),
                  _full((1, ql)), _full((1, kvl)), _full(w_q.shape), _full(w_kn.shape), _full(w_v.shape),
                  _full((1, QK_HEAD)), _full((1, QK_HEAD)), tok(QK_HEAD), tok(QK_HEAD)],
        out_specs=[tok(d), _full((1, d)), _full(w_in.shape), _full((1, ql)), _full((1, kvl)), _full(w_q.shape),
                   _full(w_kn.shape), _full(w_v.shape), _full((1, QK_HEAD)), _full((1, QK_HEAD))],
        out_shape=[_sds((s, d)), _sds((1, d)), _sds(w_in.shape), _sds((1, ql)), _sds((1, kvl)), _sds(w_q.shape),
                   _sds(w_kn.shape), _sds(w_v.shape), _sds((1, QK_HEAD)), _sds((1, QK_HEAD))],
        compiler_params=_params(("arbitrary",)),
    )(x, lat, dy, dq, dk, dv, gain, w_in, q_norm, kv_norm, w_q, w_kn, w_v, q_head_norm, k_head_norm, cos, sin)


def _scores_t(k_t, q_t, t, masked):
    z = _mm_nt(k_t, q_t) * (QK_HEAD ** -0.5)
    if masked:
        key = lax.broadcasted_iota(jnp.int32, (t, t), 0)
        query = lax.broadcasted_iota(jnp.int32, (t, t), 1)
        z = jnp.where(key <= query, z, -jnp.inf)
    return z


def _flash_fwd(q, k, vt):
    n_h, s, _ = q.shape
    t = _tile(s, ATTN_TILE)
    n = s // t

    def body(q_ref, k_ref, vt_ref, ot_ref, lse_ref, m_sc, l_sc, acc_sc):
        i = pl.program_id(1)
        m_sc[...] = jnp.full_like(m_sc, -jnp.inf)
        l_sc[...] = jnp.zeros_like(l_sc)
        acc_sc[...] = jnp.zeros_like(acc_sc)
        q_t = q_ref[...]

        def step(j, masked):
            z = _scores_t(k_ref[pl.ds(pl.multiple_of(j * t, t), t), :], q_t, t, masked)
            m_old = m_sc[...]
            m_new = jnp.maximum(m_old, jnp.max(z, axis=0, keepdims=True))
            alpha = jnp.exp(m_old - m_new)
            pr = jnp.exp(z - m_new)
            l_sc[...] = alpha * l_sc[...] + jnp.sum(pr, axis=0, keepdims=True)
            acc_sc[...] = alpha * acc_sc[...] + _mm(vt_ref[j], _cast(pr))
            m_sc[...] = m_new

        def below_diagonal(j, carry):
            step(j, False)
            return carry

        lax.fori_loop(0, i, below_diagonal, 0)
        step(i, True)
        ot_ref[...] = acc_sc[...] / l_sc[...]
        lse_ref[...] = m_sc[...] + jnp.log(l_sc[...])

    return pl.pallas_call(
        body, name="flash_fwd", grid=(n_h, n),
        in_specs=[pl.BlockSpec((None, t, QK_HEAD), lambda h, i: (h, i, 0)),
                  pl.BlockSpec((None, s, QK_HEAD), lambda h, i: (h, 0, 0)),
                  pl.BlockSpec((None, n, V_HEAD, t), lambda h, i: (h, 0, 0, 0))],
        out_specs=[pl.BlockSpec((V_HEAD, t), lambda h, i: (h, i)),
                   pl.BlockSpec((None, 1, t), lambda h, i: (h, 0, i))],
        out_shape=[_sds((n_h * V_HEAD, s)), _sds((n_h, 1, s))],
        scratch_shapes=[pltpu.VMEM((1, t), F32), pltpu.VMEM((1, t), F32), pltpu.VMEM((V_HEAD, t), F32)],
        compiler_params=_params(("arbitrary", "arbitrary")),
    )(q, k, vt)


def _flash_bwd(q, k, kt, v, do, lse, delta):
    n_h, s, _ = q.shape
    t = _tile(s, ATTN_TILE)
    n = s // t

    def body(q_ref, do_ref, lse_ref, dl_ref, k_ref, kt_ref, v_ref, dqt_ref, dk_ref, dv_ref, dk_sc, dv_sc):
        j = pl.program_id(1)

        @pl.when(j == 0)
        def _():
            dqt_ref[...] = jnp.zeros_like(dqt_ref)

        dk_sc[...] = jnp.zeros_like(dk_sc)
        dv_sc[...] = jnp.zeros_like(dv_sc)
        k_t, kt_t, v_t = k_ref[...], kt_ref[...], v_ref[...]

        def step(i, masked):
            rows = pl.ds(pl.multiple_of(i * t, t), t)
            q_t, do_t = q_ref[rows, :], do_ref[rows, :]
            pr = jnp.exp(_scores_t(k_t, q_t, t, masked) - lse_ref[i])
            dv_sc[...] += _mm(_cast(pr), do_t)
            dsb = _cast(pr * (_mm_nt(v_t, do_t) - dl_ref[i]) * (QK_HEAD ** -0.5))
            dk_sc[...] += _mm(dsb, q_t)
            dqt_ref[i] += _mm(kt_t, dsb)

        step(j, True)

        def below_diagonal(i, carry):
            step(i, False)
            return carry

        lax.fori_loop(j + 1, n, below_diagonal, 0)
        dk_ref[...] = dk_sc[...]
        dv_ref[...] = dv_sc[...]

    stat = pl.BlockSpec((None, n, 1, t), lambda h, j: (h, 0, 0, 0))
    return pl.pallas_call(
        body, name="flash_bwd", grid=(n_h, n),
        in_specs=[pl.BlockSpec((None, s, QK_HEAD), lambda h, j: (h, 0, 0)),
                  pl.BlockSpec((s, V_HEAD), lambda h, j: (0, h)),
                  stat, stat,
                  pl.BlockSpec((None, t, QK_HEAD), lambda h, j: (h, j, 0)),
                  pl.BlockSpec((None, QK_HEAD, t), lambda h, j: (h, 0, j)),
                  pl.BlockSpec((t, V_HEAD), lambda h, j: (j, h))],
        out_specs=[pl.BlockSpec((None, n, QK_HEAD, t), lambda h, j: (h, 0, 0, 0)),
                   pl.BlockSpec((None, t, QK_HEAD), lambda h, j: (h, j, 0)),
                   pl.BlockSpec((t, V_HEAD), lambda h, j: (j, h))],
        out_shape=[_sds((n_h, n, QK_HEAD, t)), _sds((n_h, s, QK_HEAD)), _sds((s, n_h * V_HEAD))],
        scratch_shapes=[pltpu.VMEM((t, QK_HEAD), F32), pltpu.VMEM((t, V_HEAD), F32)],
        compiler_params=_params(("arbitrary", "arbitrary")),
    )(q, do, lse, delta, k, kt, v)


def _mla_out_fwd(x, ot, w_out):
    s, d = x.shape
    hv = ot.shape[0]
    tm = _tile(s, FFN_TILE)

    def body(x_ref, ot_ref, w_ref, y_ref):
        y_ref[...] = x_ref[...] + _mm_tn(_cast(ot_ref[...]), w_ref[...])

    tok = pl.BlockSpec((tm, d), lambda i: (i, 0))
    return pl.pallas_call(
        body, name="mla_out_fwd", grid=(s // tm,),
        in_specs=[tok, pl.BlockSpec((hv, tm), lambda i: (0, i)), _full(w_out.shape)],
        out_specs=tok, out_shape=_sds((s, d)),
        compiler_params=_params(("arbitrary",)),
    )(x, ot, w_out)


def _mla_out_bwd(dy, ot, w_out):
    s, d = dy.shape
    hv = ot.shape[0]
    n_h = hv // V_HEAD
    tm = _tile(s, FFN_TILE)

    def body(dy_ref, ot_ref, w_ref, do_ref, dl_ref, dw_ref):
        @pl.when(pl.program_id(0) == 0)
        def _():
            dw_ref[...] = jnp.zeros_like(dw_ref)

        dyb = _cast(dy_ref[...])
        o_t = ot_ref[...]
        do_ref[...] = _cast(_mm_nt(dyb, w_ref[...]))
        prod = _mm_nt(w_ref[...], dyb) * o_t
        for hh in range(n_h):
            dl_ref[hh] = jnp.sum(prod[hh * V_HEAD:(hh + 1) * V_HEAD], axis=0, keepdims=True)
        dw_ref[...] += _mm(_cast(o_t), dyb)

    return pl.pallas_call(
        body, name="mla_out_bwd", grid=(s // tm,),
        in_specs=[pl.BlockSpec((tm, d), lambda i: (i, 0)), pl.BlockSpec((hv, tm), lambda i: (0, i)),
                  _full(w_out.shape)],
        out_specs=[pl.BlockSpec((tm, hv), lambda i: (i, 0)), pl.BlockSpec((n_h, 1, tm), lambda i: (0, 0, i)),
                   _full(w_out.shape)],
        out_shape=[_sds((s, hv), MXU_DTYPE), _sds((n_h, 1, s)), _sds(w_out.shape)],
        compiler_params=_params(("arbitrary",)),
    )(dy, ot, w_out)


def _loss_and_grad(y, target):
    s, d = y.shape
    tm = _tile(s, ROW_TILE)

    def body(y_ref, t_ref, loss_ref, dy_ref):
        @pl.when(pl.program_id(0) == 0)
        def _():
            loss_ref[...] = jnp.zeros_like(loss_ref)

        err = y_ref[...] - t_ref[...]
        dy_ref[...] = err * (1.0 / d)
        loss_ref[...] += 0.5 * jnp.sum(jnp.mean(err * err, axis=-1, keepdims=True), axis=0, keepdims=True)

    tok = pl.BlockSpec((tm, d), lambda i: (i, 0))
    return pl.pallas_call(
        body, name="loss_and_grad", grid=(s // tm,),
        in_specs=[tok, tok], out_specs=[_full((1, 1)), tok],
        out_shape=[_sds((1, 1)), _sds((s, d))],
        compiler_params=_params(("arbitrary",)),
    )(y, target)


def _mesh_position():
    return lax.axis_index("x"), lax.axis_index("y"), lax.axis_index("c")


def _other_chips(x, y):
    return [(1 - x, y), (x, 1 - y), (1 - x, 1 - y)]


ANY = pl.BlockSpec(memory_space=pl.ANY)


def _gather_over_chips(arrs):
    n = len(arrs)
    halves = [a.shape[0] // 2 for a in arrs]
    assert all(a.shape[0] % 2 == 0 for a in arrs)
    own = 2 * (N_CHIPS - 1)

    def body(*refs):
        srcs, outs = refs[:n], refs[n:2 * n]
        send_sems, recv_sems = refs[2 * n:]
        x, y, c = _mesh_position()
        me, sibling = (x, y, c), (x, y, 1 - c)
        chips = _other_chips(x, y)
        my_chip = 2 * x + y

        def rows(t, chip, half):
            return outs[t].at[chip, pl.ds(half * halves[t], halves[t])]

        def copy(t, k, src, dst, to):
            return pltpu.make_async_remote_copy(src_ref=src, dst_ref=dst, send_sem=send_sems.at[t, k],
                                                recv_sem=recv_sems.at[t, k], device_id=to, device_id_type=MESH)

        started = []
        for t in range(n):
            for k, (px, py) in enumerate(chips):
                cp = copy(t, k, srcs[t].at[pl.ds(c * halves[t], halves[t])], rows(t, my_chip, c), (px, py, c))
                cp.start()
                started.append(cp)
            cp = copy(t, own, srcs[t], outs[t].at[my_chip], sibling)
            cp.start()
            started.append(cp)
        for t in range(n):
            for k, (px, py) in enumerate(chips):
                landed = rows(t, 2 * px + py, c)
                copy(t, k, landed, landed, me).wait_recv()
                cp = copy(t, N_CHIPS - 1 + k, landed, landed, sibling)
                cp.start()
                started.append(cp)
        for t in range(n):
            for k, (px, py) in enumerate(chips):
                passed = rows(t, 2 * px + py, 1 - c)
                copy(t, N_CHIPS - 1 + k, passed, passed, me).wait_recv()
            copy(t, own, srcs[t], outs[t].at[my_chip], me).wait_recv()
        for cp in started:
            cp.wait_send()

    return pl.pallas_call(
        body, name="gather_over_chips",
        in_specs=[ANY] * n, out_specs=[ANY] * n,
        out_shape=[_sds((N_CHIPS,) + a.shape, a.dtype) for a in arrs],
        scratch_shapes=[pltpu.SemaphoreType.DMA((n, own + 1)), pltpu.SemaphoreType.DMA((n, own + 1))],
    )(*arrs)


def _send_other_half_to_sibling(grads):
    n = len(grads)
    halves = [g.shape[1] // 2 for g in grads]

    def body(*refs):
        srcs, outs = refs[:n], refs[n:2 * n]
        send_sems, recv_sems = refs[2 * n:]
        x, y, c = _mesh_position()
        copies = []
        for t in range(n):
            cp = pltpu.make_async_remote_copy(
                src_ref=srcs[t].at[pl.ds(0, N_CHIPS), pl.ds((1 - c) * halves[t], halves[t])], dst_ref=outs[t],
                send_sem=send_sems.at[t], recv_sem=recv_sems.at[t], device_id=(x, y, 1 - c), device_id_type=MESH)
            cp.start()
            copies.append(cp)
        for cp in copies:
            cp.wait_recv()
        for cp in copies:
            cp.wait_send()

    return pl.pallas_call(
        body, name="send_other_half_to_sibling",
        in_specs=[ANY] * n, out_specs=[ANY] * n,
        out_shape=[_sds((N_CHIPS, h) + g.shape[2:]) for g, h in zip(grads, halves)],
        scratch_shapes=[pltpu.SemaphoreType.DMA((n,)), pltpu.SemaphoreType.DMA((n,))],
    )(*grads)


def _send_blocks_to_chips(parts):
    n = len(parts)

    def body(*refs):
        srcs, outs = refs[:n], refs[n:2 * n]
        send_sems, recv_sems = refs[2 * n:]
        x, y, c = _mesh_position()
        copies = []
        for t in range(n):
            for k, (px, py) in enumerate(_other_chips(x, y)):
                cp = pltpu.make_async_remote_copy(
                    src_ref=srcs[t].at[2 * px + py], dst_ref=outs[t].at[k], send_sem=send_sems.at[t, k],
                    recv_sem=recv_sems.at[t, k], device_id=(px, py, c), device_id_type=MESH)
                cp.start()
                copies.append(cp)
        for cp in copies:
            cp.wait_recv()
        for cp in copies:
            cp.wait_send()

    return pl.pallas_call(
        body, name="send_blocks_to_chips",
        in_specs=[ANY] * n, out_specs=[ANY] * n,
        out_shape=[_sds((N_CHIPS - 1,) + p.shape[1:]) for p in parts],
        scratch_shapes=[pltpu.SemaphoreType.DMA((n, N_CHIPS - 1)), pltpu.SemaphoreType.DMA((n, N_CHIPS - 1))],
    )(*parts)


def _join_halves_with_sibling(sums):
    n = len(sums)

    def body(*refs):
        srcs, outs = refs[:n], refs[n:2 * n]
        send_sems, recv_sems = refs[2 * n:]
        x, y, c = _mesh_position()
        copies = []
        for t in range(n):
            h = srcs[t].shape[0] // 2
            mine = pl.ds(c * h, h)
            cp = pltpu.make_async_remote_copy(
                src_ref=srcs[t].at[mine], dst_ref=outs[t].at[mine], send_sem=send_sems.at[t],
                recv_sem=recv_sems.at[t], device_id=(x, y, 1 - c), device_id_type=MESH)
            cp.start()
            copies.append(cp)
        for t in range(n):
            h = srcs[t].shape[0] // 2
            theirs = pl.ds((1 - c) * h, h)
            pltpu.make_async_remote_copy(
                src_ref=srcs[t].at[theirs], dst_ref=outs[t].at[theirs], send_sem=send_sems.at[t],
                recv_sem=recv_sems.at[t], device_id=(x, y, 1 - c), device_id_type=MESH).wait_recv()
        for cp in copies:
            cp.wait_send()

    return pl.pallas_call(
        body, name="join_halves_with_sibling",
        in_specs=[ANY] * n, out_specs=[ANY] * n,
        out_shape=[_sds(a.shape) for a in sums],
        input_output_aliases={t: t for t in range(n)},
        scratch_shapes=[pltpu.SemaphoreType.DMA((n,)), pltpu.SemaphoreType.DMA((n,))],
    )(*sums)


def _gather_over_devices(rows):
    r = rows.shape[0]

    def body(in_ref, out_ref, send_sems, recv_sems, local_sem):
        x, y, c = _mesh_position()
        mine = pltpu.make_async_copy(in_ref, out_ref.at[4 * x + 2 * y + c], local_sem)
        mine.start()
        copies = []
        for mask in range(1, N_DEVICES):
            fx, fy, fc = (mask >> 2) & 1, (mask >> 1) & 1, mask & 1
            px, py, pc = (1 - x if fx else x), (1 - y if fy else y), (1 - c if fc else c)
            send = pltpu.make_async_remote_copy(
                src_ref=in_ref, dst_ref=out_ref.at[4 * x + 2 * y + c], send_sem=send_sems.at[mask - 1],
                recv_sem=recv_sems.at[mask - 1], device_id=(px, py, pc), device_id_type=MESH)
            send.start()
            recv = pltpu.make_async_remote_copy(
                src_ref=in_ref, dst_ref=out_ref.at[4 * px + 2 * py + pc], send_sem=send_sems.at[mask - 1],
                recv_sem=recv_sems.at[mask - 1], device_id=(px, py, pc), device_id_type=MESH)
            copies.append((send, recv))
        for _, recv in copies:
            recv.wait_recv()
        for send, _ in copies:
            send.wait_send()
        mine.wait()

    vm = pl.BlockSpec(memory_space=pltpu.VMEM)
    return pl.pallas_call(
        body, name="gather_over_devices", in_specs=[vm], out_specs=vm,
        out_shape=_sds((N_DEVICES, r, LANES)),
        scratch_shapes=[pltpu.SemaphoreType.DMA((N_DEVICES - 1,)), pltpu.SemaphoreType.DMA((N_DEVICES - 1,)),
                        pltpu.SemaphoreType.DMA],
    )(rows)


def _add_sibling_half(grad, received, core):
    _, l, r, c = grad.shape
    half = l // 2

    def body(core_ref, g_ref, r_ref, o_ref):
        o_ref[...] = g_ref[...] + r_ref[...]

    blk = lambda f: pl.BlockSpec((None, None, r, c), f)
    grid_spec = pltpu.PrefetchScalarGridSpec(
        num_scalar_prefetch=1, grid=(N_CHIPS, half),
        in_specs=[blk(lambda j, i, core: (j, core[0] * half + i, 0, 0)), blk(lambda j, i, core: (j, i, 0, 0))],
        out_specs=blk(lambda j, i, core: (j, i, 0, 0)))
    return pl.pallas_call(
        body, name="add_sibling_half", grid_spec=grid_spec, out_shape=_sds((N_CHIPS, half, r, c)),
        compiler_params=_params(("arbitrary", "arbitrary")),
    )(core, grad, received)


def _add_chip_blocks(part, received, chip, core):
    _, half, r, c = part.shape

    def body(chip_ref, core_ref, p_ref, r0_ref, r1_ref, r2_ref, o_ref):
        o_ref[...] = ((p_ref[...] + r0_ref[...]) + r1_ref[...]) + r2_ref[...]

    blk = lambda f: pl.BlockSpec((None, None, r, c), f)
    grid_spec = pltpu.PrefetchScalarGridSpec(
        num_scalar_prefetch=2, grid=(half,),
        in_specs=[blk(lambda i, chip, core: (chip[0], i, 0, 0))] + [
            blk(functools.partial(lambda i, chip, core, k: (k, i, 0, 0), k=k)) for k in range(N_CHIPS - 1)],
        out_specs=pl.BlockSpec((None, r, c), lambda i, chip, core: (core[0] * half + i, 0, 0)))
    return pl.pallas_call(
        body, name="add_chip_blocks", grid_spec=grid_spec, out_shape=_sds((2 * half, r, c)),
        compiler_params=_params(("arbitrary",)),
    )(chip, core, part, received, received, received)


def _sum_over_devices(parts):
    _, r, _ = parts.shape

    def body(p_ref, o_ref):
        acc = p_ref[0]
        for k in range(1, N_DEVICES):
            acc = acc + p_ref[k]
        o_ref[...] = acc

    return pl.pallas_call(body, name="sum_over_devices", out_shape=_sds((r, LANES)))(parts)


def _adamw_math(w, g, m, v):
    m = ADAM_B1 * m + (1.0 - ADAM_B1) * g
    v = ADAM_B2 * v + (1.0 - ADAM_B2) * (g * g)
    m_hat = m / (1.0 - ADAM_B1 ** ADAM_STEP)
    v_hat = v / (1.0 - ADAM_B2 ** ADAM_STEP)
    delta = -ADAM_LR * (m_hat / (jnp.sqrt(v_hat) + ADAM_EPS) + ADAM_WD * w)
    return delta, m, v


def _adamw_stacked(w, m, v, grads, offset):
    l, r, c = w.shape
    tr = r
    while tr * c * 4 > 2**20 and tr % 16 == 0:
        tr //= 2

    def body(w_ref, m_ref, v_ref, g_ref, go_ref, d_ref, mo_ref, vo_ref):
        g = g_ref[...]
        go_ref[...] = g
        d_ref[...], mo_ref[...], vo_ref[...] = _adamw_math(w_ref[...], g, m_ref[...], v_ref[...])

    blk = pl.BlockSpec((None, tr, c), lambda i, j: (i, j, 0))
    return pl.pallas_call(
        body, name="adamw_stacked", grid=(l, r // tr),
        in_specs=[blk, blk, blk, pl.BlockSpec((None, tr, c), lambda i, j: (offset + i, j, 0))],
        out_specs=[blk] * 4, out_shape=[_sds((l, r, c))] * 4,
        compiler_params=_params(("arbitrary", "arbitrary")),
    )(w, m, v, grads)


def _adamw_small(w, m, v, g):
    def body(w_ref, m_ref, v_ref, g_ref, d_ref, mo_ref, vo_ref):
        d_ref[...], mo_ref[...], vo_ref[...] = _adamw_math(w_ref[...], g_ref[...], m_ref[...], v_ref[...])

    return pl.pallas_call(body, name="adamw_small", out_shape=[_sds(w.shape)] * 3)(w, m, v, g)


def _pack_rows(arrs):
    flat = jnp.concatenate([a.reshape(-1) for a in arrs])
    pad = (-flat.shape[0]) % (8 * LANES)
    return jnp.pad(flat, (0, pad)).reshape(-1, LANES)


def _unpack_rows(rows, shapes, lead=()):
    flat = rows.reshape(lead + (-1,))
    out, at = [], 0
    for shp in shapes:
        size = int(np.prod(shp))
        out.append(flat[..., at:at + size].reshape(lead + tuple(shp)))
        at += size
    return out


WEIGHT_NAMES = ('ffn1_norm', 'ffn1_w_gate', 'ffn1_w_up', 'ffn1_w_down', 'mix_norm', 'pool_w', 'pool_scale',
                'mla_w_in', 'mla_q_norm', 'mla_w_q_up', 'mla_kv_norm', 'mla_w_kv_up', 'mla_q_head_norm',
                'mla_k_head_norm', 'mla_w_out', 'ffn2_norm', 'ffn2_w_gate', 'ffn2_w_up', 'ffn2_w_down')


def _chips_to_columns(g):
    return jnp.transpose(g, (1, 2, 0, 3)).reshape(g.shape[1], g.shape[2], -1)


def _columns_to_chips(full):
    n, r, c4 = full.shape
    return jnp.transpose(full.reshape(n, r, N_CHIPS, c4 // N_CHIPS), (2, 0, 1, 3))


def kernel(x, positions, ffn1_norm, ffn1_w_gate, ffn1_w_up, ffn1_w_down, mix_norm, pool_w, pool_scale, mla_w_in, mla_q_norm, mla_w_q_up, mla_kv_norm, mla_w_kv_up, mla_q_head_norm, mla_k_head_norm, mla_w_out, ffn2_norm, ffn2_w_gate, ffn2_w_up, ffn2_w_down, loss_target, m_ffn1_norm, m_ffn1_w_gate, m_ffn1_w_up, m_ffn1_w_down, m_mix_norm, m_pool_w, m_pool_scale, m_mla_w_in, m_mla_q_norm, m_mla_w_q_up, m_mla_kv_norm, m_mla_w_kv_up, m_mla_q_head_norm, m_mla_k_head_norm, m_mla_w_out, m_ffn2_norm, m_ffn2_w_gate, m_ffn2_w_up, m_ffn2_w_down, v_ffn1_norm, v_ffn1_w_gate, v_ffn1_w_up, v_ffn1_w_down, v_mix_norm, v_pool_w, v_pool_scale, v_mla_w_in, v_mla_q_norm, v_mla_w_q_up, v_mla_kv_norm, v_mla_w_kv_up, v_mla_q_head_norm, v_mla_k_head_norm, v_mla_w_out, v_ffn2_norm, v_ffn2_w_gate, v_ffn2_w_up, v_ffn2_w_down):
    env = dict(locals())
    w = {n: env[n] for n in WEIGHT_NAMES}
    mom = {n: env["m_" + n] for n in WEIGHT_NAMES}
    var = {n: env["v_" + n] for n in WEIGHT_NAMES}

    s, d = x.shape[1], x.shape[2]
    depth = ffn1_norm.shape[0]
    n_mla, n_pool, n_groups = mla_w_in.shape[0], pool_w.shape[0], pool_w.shape[1]
    pool_c = pool_w.shape[3]
    q_lora = N_CHIPS * mla_q_norm.shape[1]
    kv_lora = N_CHIPS * mla_kv_norm.shape[1]
    n_heads = N_CHIPS * mla_w_q_up.shape[2] // QK_HEAD
    t_attn = _tile(s, ATTN_TILE)
    n_attn = s // t_attn
    cx, cy, cc = _mesh_position()
    chip = 2 * cx + cy
    chip_arr = jnp.reshape(chip, (1,)).astype(jnp.int32)
    core_arr = jnp.reshape(cc, (1,)).astype(jnp.int32)

    shard_gu = _cast(jnp.concatenate([ffn1_w_gate, ffn1_w_up, ffn2_w_gate, ffn2_w_up], axis=0))
    shard_dn = _cast(jnp.concatenate([ffn1_w_down, ffn2_w_down], axis=0))
    shard_pool = _cast(pool_w.reshape((n_pool * n_groups,) + pool_w.shape[2:]))
    w_gu, w_dn, g_in, g_qup, g_kvup, g_out, g_pool = _gather_over_chips(
        [shard_gu, shard_dn, _cast(mla_w_in), _cast(mla_w_q_up), _cast(mla_w_kv_up), _cast(mla_w_out), shard_pool])
    small_shapes = [mla_q_norm.shape, mla_kv_norm.shape]
    small = _gather_over_devices(_pack_rows([mla_q_norm, mla_kv_norm]))[::2]
    qn_chips, kvn_chips = _unpack_rows(small, small_shapes, lead=(N_CHIPS,))
    q_norm_full = jnp.transpose(qn_chips, (1, 0, 2)).reshape(n_mla, 1, q_lora)
    kv_norm_full = jnp.transpose(kvn_chips, (1, 0, 2)).reshape(n_mla, 1, kv_lora)

    w_in_full = _chips_to_columns(g_in)
    w_q_heads = jnp.transpose(_chips_to_columns(g_qup).reshape(n_mla, q_lora, n_heads, QK_HEAD), (0, 2, 1, 3))
    w_kv = _chips_to_columns(g_kvup).reshape(n_mla, kv_lora, n_heads, QK_NOPE + V_HEAD)
    w_kn_heads = jnp.transpose(w_kv[..., :QK_NOPE], (0, 2, 1, 3))
    w_v_full = w_kv[..., QK_NOPE:].reshape(n_mla, kv_lora, n_heads * V_HEAD)
    w_out_full = jnp.transpose(g_out, (1, 0, 2, 3)).reshape(n_mla, n_heads * V_HEAD, d)
    pool_full = jnp.transpose(g_pool.reshape(N_CHIPS, n_pool, n_groups, pool_c // N_CHIPS, pool_c),
                              (1, 2, 0, 3, 4)).reshape(n_pool, n_groups, pool_c, pool_c)

    inv_freq = (1.0 / (ROPE_THETA ** (jnp.arange(0, QK_ROPE, 2, dtype=F32) / QK_ROPE))).reshape(1, -1)
    cos_t, sin_t = _rope_tables(positions.reshape(s, 1), inv_freq)

    row = lambda a, i: a[i].reshape(1, -1)
    i_gate1, i_up1, i_gate2, i_up2 = (lambda i: i), (lambda i: depth + i), (lambda i: 2 * depth + i), (lambda i: 3 * depth + i)
    i_dn1, i_dn2 = (lambda i: i), (lambda i: depth + i)

    h = x.reshape(s, d)
    saved = []
    for i in range(depth):
        rec = {"x_ffn1": h}
        h = _ffn_fwd(h, row(ffn1_norm, i), w_gu, w_dn, i_gate1(i), i_up1(i), i_dn1(i))
        rec["x_mix"] = h
        j = i // 2
        if i % 2 == 0:
            h = _pool_fwd(h, row(mix_norm, i), pool_full[j], row(pool_scale, j))
        else:
            lat, q, k, v = _mla_qkv_fwd(h, row(mix_norm, i), w_in_full[j], q_norm_full[j], kv_norm_full[j],
                                        w_q_heads[j], w_kn_heads[j], w_v_full[j], row(mla_q_head_norm, j),
                                        row(mla_k_head_norm, j), cos_t, sin_t)
            vt = jnp.transpose(v.reshape(n_attn, t_attn, n_heads, V_HEAD), (2, 0, 3, 1))
            ot, lse = _flash_fwd(q, k, vt)
            rec.update(lat=lat, q=q, k=k, v=v, ot=ot, lse=lse)
            h = _mla_out_fwd(h, ot, w_out_full[j])
        rec["x_ffn2"] = h
        h = _ffn_fwd(h, row(ffn2_norm, i), w_gu, w_dn, i_gate2(i), i_up2(i), i_dn2(i))
        saved.append(rec)

    loss_part, dy = _loss_and_grad(h, loss_target.reshape(s, d))
    loss = lax.psum(loss_part[0, 0], ("x", "y", "c"))

    g_gu = [None] * (4 * depth)
    g_dn = [None] * (2 * depth)
    g_norm = {n: [None] * depth for n in ("ffn1_norm", "mix_norm", "ffn2_norm")}
    g_pool_w, g_pool_scale = [None] * n_pool, [None] * n_pool
    g_mla = {n: [None] * n_mla for n in ("w_in", "q_norm", "kv_norm", "w_q", "w_kv", "qhn", "khn", "w_out")}
    for i in reversed(range(depth)):
        rec = saved[i]
        dy, g_norm["ffn2_norm"][i], hb, dyb, dgt, dup, act = _ffn_bwd_dgrad(
            rec["x_ffn2"], row(ffn2_norm, i), dy, w_gu, w_dn, i_gate2(i), i_up2(i), i_dn2(i))
        g_gu[i_gate2(i)], g_gu[i_up2(i)], g_dn[i_dn2(i)] = _ffn_wgrad(hb, dyb, dgt, dup, act)
        j = i // 2
        if i % 2 == 0:
            dy, g_norm["mix_norm"][i], g_pool_w[j], g_pool_scale[j] = _pool_bwd(
                rec["x_mix"], row(mix_norm, i), pool_full[j], row(pool_scale, j), dy)
        else:
            do, delta, g_mla["w_out"][j] = _mla_out_bwd(dy, rec["ot"], w_out_full[j])
            by_tile = lambda a: a.reshape(n_heads, n_attn, 1, t_attn)
            dqt, dk, dv = _flash_bwd(rec["q"], rec["k"], jnp.transpose(rec["k"], (0, 2, 1)), rec["v"], do,
                                     by_tile(rec["lse"]), by_tile(delta))
            dq = jnp.transpose(dqt, (0, 1, 3, 2)).reshape(n_heads, s, QK_HEAD)
            (dy, g_norm["mix_norm"][i], g_mla["w_in"][j], g_mla["q_norm"][j], g_mla["kv_norm"][j], dwq, dwkn, dwv,
             g_mla["qhn"][j], g_mla["khn"][j]) = _mla_qkv_bwd(
                rec["x_mix"], rec["lat"], dy, dq, dk, dv, row(mix_norm, i), w_in_full[j], q_norm_full[j],
                kv_norm_full[j], w_q_heads[j], w_kn_heads[j], w_v_full[j], row(mla_q_head_norm, j),
                row(mla_k_head_norm, j), cos_t, sin_t)
            g_mla["w_q"][j] = jnp.transpose(dwq, (1, 0, 2)).reshape(q_lora, n_heads * QK_HEAD)
            g_mla["w_kv"][j] = jnp.concatenate(
                [jnp.transpose(dwkn, (1, 0, 2)), dwv.reshape(kv_lora, n_heads, V_HEAD)], axis=-1
            ).reshape(kv_lora, n_heads * (QK_NOPE + V_HEAD))
        dy, g_norm["ffn1_norm"][i], hb, dyb, dgt, dup, act = _ffn_bwd_dgrad(
            rec["x_ffn1"], row(ffn1_norm, i), dy, w_gu, w_dn, i_gate1(i), i_up1(i), i_dn1(i))
        g_gu[i_gate1(i)], g_gu[i_up1(i)], g_dn[i_dn1(i)] = _ffn_wgrad(hb, dyb, dgt, dup, act)
    grad_x = dy.reshape(x.shape)

    full_grads = [
        jnp.stack(g_gu, axis=1),
        jnp.stack(g_dn, axis=1),
        _columns_to_chips(jnp.stack(g_mla["w_in"])),
        _columns_to_chips(jnp.stack(g_mla["w_q"])),
        _columns_to_chips(jnp.stack(g_mla["w_kv"])),
        jnp.transpose(jnp.stack(g_mla["w_out"]).reshape(n_mla, N_CHIPS, -1, d), (1, 0, 2, 3)),
        jnp.transpose(jnp.stack(g_pool_w).reshape(n_pool, n_groups, N_CHIPS, pool_c // N_CHIPS, pool_c),
                      (2, 0, 1, 3, 4)).reshape(N_CHIPS, n_pool * n_groups, pool_c // N_CHIPS, pool_c),
    ]
    from_sibling = _send_other_half_to_sibling(full_grads)
    chip_sums = [_add_sibling_half(g, r, core_arr) for g, r in zip(full_grads, from_sibling)]
    from_chips = _send_blocks_to_chips(chip_sums)
    half_sums = [_add_chip_blocks(p, r, chip_arr, core_arr) for p, r in zip(chip_sums, from_chips)]
    r_gu, r_dn, r_in, r_qup, r_kvup, r_out, r_pool = _join_halves_with_sibling(half_sums)

    small_grads = [jnp.concatenate(g_norm["ffn1_norm"]), jnp.concatenate(g_norm["mix_norm"]),
                   jnp.concatenate(g_norm["ffn2_norm"]), jnp.concatenate(g_pool_scale),
                   jnp.concatenate(g_mla["qhn"]), jnp.concatenate(g_mla["khn"]),
                   jnp.concatenate(g_mla["q_norm"]), jnp.concatenate(g_mla["kv_norm"])]
    small_sum = _sum_over_devices(_gather_over_devices(_pack_rows(small_grads)))
    (s_ffn1, s_mix, s_ffn2, s_pscale, s_qhn, s_khn, s_qn, s_kvn) = _unpack_rows(small_sum, [g.shape for g in small_grads])
    qn_w, kvn_w = mla_q_norm.shape[1], mla_kv_norm.shape[1]
    s_qn = lax.dynamic_slice_in_dim(s_qn, chip * qn_w, qn_w, axis=1)
    s_kvn = lax.dynamic_slice_in_dim(s_kvn, chip * kvn_w, kvn_w, axis=1)

    grads, deltas, new_m, new_v = {}, {}, {}, {}

    def stacked(name, reduced, offset):
        shape = w[name].shape
        as3 = lambda a: a.reshape((-1,) + shape[-2:])
        out = _adamw_stacked(as3(w[name]), as3(mom[name]), as3(var[name]), reduced, offset)
        grads[name], deltas[name], new_m[name], new_v[name] = [o.reshape(shape) for o in out]

    def small_update(name, g):
        grads[name] = g
        deltas[name], new_m[name], new_v[name] = _adamw_small(w[name], mom[name], var[name], g)

    stacked("ffn1_w_gate", r_gu, 0)
    stacked("ffn1_w_up", r_gu, depth)
    stacked("ffn2_w_gate", r_gu, 2 * depth)
    stacked("ffn2_w_up", r_gu, 3 * depth)
    stacked("ffn1_w_down", r_dn, 0)
    stacked("ffn2_w_down", r_dn, depth)
    stacked("mla_w_in", r_in, 0)
    stacked("mla_w_q_up", r_qup, 0)
    stacked("mla_w_kv_up", r_kvup, 0)
    stacked("mla_w_out", r_out, 0)
    stacked("pool_w", r_pool, 0)
    small_update("ffn1_norm", s_ffn1)
    small_update("mix_norm", s_mix)
    small_update("ffn2_norm", s_ffn2)
    small_update("pool_scale", s_pscale)
    small_update("mla_q_head_norm", s_qhn)
    small_update("mla_k_head_norm", s_khn)
    small_update("mla_q_norm", s_qn)
    small_update("mla_kv_norm", s_kvn)

    return (loss, grad_x, *[grads[n] for n in WEIGHT_NAMES], *[deltas[n] for n in WEIGHT_NAMES],
            *[new_m[n] for n in WEIGHT_NAMES], *[new_v[n] for n in WEIGHT_NAMES])
```

```python
import functools

import numpy as np

import jax
import jax.numpy as jnp
from jax import lax
from jax.experimental import pallas as pl
from jax.experimental.pallas import tpu as pltpu

F32 = jnp.float32
MXU_DTYPE = jnp.bfloat16
WIRE_DTYPE = jnp.bfloat16
MESH = pl.DeviceIdType.MESH
N_CHIPS = 4
N_DEVICES = 8
LANES = 128
VMEM_LIMIT_BYTES = 56 * 2**20
NORM_EPS = 1e-6
QK_NOPE, QK_ROPE, V_HEAD = 128, 64, 128
QK_HEAD = QK_NOPE + QK_ROPE
ROPE_THETA = 10000.0
POOL_WINDOWS = (2, 4, 8, 16)
POOL_HALO = 16
FFN_HALF = 0.5
ADAM_LR, ADAM_B1, ADAM_B2, ADAM_EPS, ADAM_WD, ADAM_STEP = 0.001, 0.9, 0.999, 1e-08, 0.01, 10
FFN_TILE = 512
WGRAD_TILE = 512
MLA_TILE = 256
POOL_TILE = 512
ATTN_TILE = 512
ROW_TILE = 1024


def _cast(v):
    return v.astype(MXU_DTYPE)


def _mm(a, b):
    return jnp.dot(a, b, preferred_element_type=F32)


def _mm_nt(a, b):
    return lax.dot_general(a, b, (((1,), (1,)), ((), ())), preferred_element_type=F32)


def _mm_tn(a, b):
    return lax.dot_general(a, b, (((0,), (0,)), ((), ())), preferred_element_type=F32)


def _rms_fwd(v, gain):
    r = lax.rsqrt(jnp.mean(v * v, axis=-1, keepdims=True) + NORM_EPS)
    return v * r * gain, r


def _rms_bwd(v, r, gain, dy):
    vr = v * r
    gy = dy * gain
    dv = r * (gy - vr * jnp.mean(gy * vr, axis=-1, keepdims=True))
    return dv, jnp.sum(dy * vr, axis=0, keepdims=True)


def _params(semantics=None):
    return pltpu.CompilerParams(dimension_semantics=semantics, vmem_limit_bytes=VMEM_LIMIT_BYTES)


def _tile(n, want):
    t = min(n, want)
    assert n % t == 0, (n, want)
    return t


def _full(shape):
    nd = len(shape)
    return pl.BlockSpec(shape, lambda *_: (0,) * nd)


def _sds(shape, dtype=F32):
    return jax.ShapeDtypeStruct(shape, dtype)


def _ffn_fwd(x, gain, w_gu, w_dn, i_gate, i_up, i_down):
    s, d = x.shape
    fs = w_gu.shape[-1]
    tm = _tile(s, FFN_TILE)

    def body(x_ref, g_ref, wg_ref, wu_ref, wd_ref, y_ref, hb_ref, gate_ref, up_ref, acc_sc):
        j = pl.program_id(1)

        @pl.when(j == 0)
        def _():
            h, _ = _rms_fwd(x_ref[...], g_ref[...])
            hb_ref[...] = _cast(h)
            acc_sc[...] = jnp.zeros_like(acc_sc)

        h = hb_ref[...]
        g = _mm(h, wg_ref[...])
        u = _mm(h, wu_ref[...])
        gate_ref[...] = _cast(g)
        up_ref[...] = _cast(u)
        act = (g * jax.nn.sigmoid(g)) * u
        acc_sc[...] += _mm(_cast(act), wd_ref[...])

        @pl.when(j == N_CHIPS - 1)
        def _():
            y_ref[...] = x_ref[...] + FFN_HALF * acc_sc[...]

    tok = pl.BlockSpec((tm, d), lambda i, j: (i, 0))
    chunk = pl.BlockSpec((None, tm, fs), lambda i, j: (j, i, 0))
    return pl.pallas_call(
        body, name="ffn_fwd", grid=(s // tm, N_CHIPS),
        in_specs=[
            tok, _full((1, d)),
            pl.BlockSpec((None, None, d, fs), lambda i, j: (j, i_gate, 0, 0)),
            pl.BlockSpec((None, None, d, fs), lambda i, j: (j, i_up, 0, 0)),
            pl.BlockSpec((None, None, fs, d), lambda i, j: (j, i_down, 0, 0)),
        ],
        out_specs=[tok, tok, chunk, chunk],
        out_shape=[_sds((s, d)), _sds((s, d), MXU_DTYPE), _sds((N_CHIPS, s, fs), MXU_DTYPE),
                   _sds((N_CHIPS, s, fs), MXU_DTYPE)],
        scratch_shapes=[pltpu.VMEM((tm, d), F32)],
        compiler_params=_params(("arbitrary", "arbitrary")),
    )(x, gain, w_gu, w_gu, w_dn)


def _ffn_bwd_dgrad(x, gain, dy, gate, up, w_gu, w_dn, i_gate, i_up, i_down):
    s, d = x.shape
    fs = w_gu.shape[-1]
    tm = _tile(s, FFN_TILE)
    n_tiles = s // tm

    def body(x_ref, g_ref, dy_ref, gate_ref, up_ref, wg_ref, wu_ref, wd_ref,
             dx_ref, dgain_ref, dyb_ref, dg_ref, du_ref, act_ref, dh_sc):
        i, j = pl.program_id(0), pl.program_id(1)

        @pl.when(j == 0)
        def _():
            dh_sc[...] = jnp.zeros_like(dh_sc)
            dyb_ref[...] = _cast(dy_ref[...])

        g = gate_ref[...].astype(F32)
        u = up_ref[...].astype(F32)
        sg = jax.nn.sigmoid(g)
        silu = g * sg
        dact = FFN_HALF * _mm_nt(dyb_ref[...], wd_ref[...])
        dgb = _cast(dact * u * (sg * (1.0 + g * (1.0 - sg))))
        dub = _cast(dact * silu)
        dg_ref[...] = dgb
        du_ref[...] = dub
        act_ref[...] = _cast(silu * u)
        dh_sc[...] += _mm_nt(dgb, wg_ref[...]) + _mm_nt(dub, wu_ref[...])

        @pl.when(j == N_CHIPS - 1)
        def _():
            _, r = _rms_fwd(x_ref[...], g_ref[...])
            dxn, dgn = _rms_bwd(x_ref[...], r, g_ref[...], dh_sc[...])
            dx_ref[...] = dy_ref[...] + dxn

            @pl.when(i == 0)
            def _():
                dgain_ref[...] = dgn

            @pl.when(i > 0)
            def _():
                dgain_ref[...] += dgn

    tok = pl.BlockSpec((tm, d), lambda i, j: (i, 0))
    chunk = pl.BlockSpec((None, tm, fs), lambda i, j: (j, i, 0))
    return pl.pallas_call(
        body, name="ffn_bwd_dgrad", grid=(n_tiles, N_CHIPS),
        in_specs=[
            tok, _full((1, d)), tok, chunk, chunk,
            pl.BlockSpec((None, None, d, fs), lambda i, j: (j, i_gate, 0, 0)),
            pl.BlockSpec((None, None, d, fs), lambda i, j: (j, i_up, 0, 0)),
            pl.BlockSpec((None, None, fs, d), lambda i, j: (j, i_down, 0, 0)),
        ],
        out_specs=[tok, _full((1, d)), tok, chunk, chunk, chunk],
        out_shape=[_sds((s, d)), _sds((1, d)), _sds((s, d), MXU_DTYPE),
                   _sds((N_CHIPS, s, fs), MXU_DTYPE), _sds((N_CHIPS, s, fs), MXU_DTYPE),
                   _sds((N_CHIPS, s, fs), MXU_DTYPE)],
        scratch_shapes=[pltpu.VMEM((tm, d), F32)],
        compiler_params=_params(("arbitrary", "arbitrary")),
    )(x, gain, dy, gate, up, w_gu, w_gu, w_dn)


def _ffn_wgrad(hb, dyb, dg, du, act):
    s, d = hb.shape
    fs = dg.shape[-1]
    tk = _tile(s, WGRAD_TILE)
    n_k = s // tk

    def body(h_ref, dy_ref, dg_ref, du_ref, act_ref, wg_ref, wu_ref, wd_ref):
        k = pl.program_id(1)

        @pl.when(k == 0)
        def _():
            wg_ref[...] = jnp.zeros_like(wg_ref)
            wu_ref[...] = jnp.zeros_like(wu_ref)
            wd_ref[...] = jnp.zeros_like(wd_ref)

        h = h_ref[...]
        wg_ref[...] += _mm_tn(h, dg_ref[...])
        wu_ref[...] += _mm_tn(h, du_ref[...])
        wd_ref[...] += FFN_HALF * _mm_tn(act_ref[...], dy_ref[...])

    tok = pl.BlockSpec((tk, d), lambda j, k: (k, 0))
    chunk = pl.BlockSpec((None, tk, fs), lambda j, k: (j, k, 0))
    return pl.pallas_call(
        body, name="ffn_wgrad", grid=(N_CHIPS, n_k),
        in_specs=[tok, tok, chunk, chunk, chunk],
        out_specs=[pl.BlockSpec((None, d, fs), lambda j, k: (j, 0, 0)),
                   pl.BlockSpec((None, d, fs), lambda j, k: (j, 0, 0)),
                   pl.BlockSpec((None, fs, d), lambda j, k: (j, 0, 0))],
        out_shape=[_sds((N_CHIPS, d, fs)), _sds((N_CHIPS, d, fs)), _sds((N_CHIPS, fs, d))],
        compiler_params=_params(("arbitrary", "arbitrary")),
    )(hb, dyb, dg, du, act)


def _inv_count(first_row, n_rows, window):
    t = first_row + lax.broadcasted_iota(jnp.int32, (n_rows, 1), 0)
    return 1.0 / jnp.minimum(t + 1, window).astype(F32)


def _trailing_sum(v, window):
    k = 1
    while k < window:
        v = v + pltpu.roll(v, k, 0)
        k *= 2
    return v


def _leading_sum(v, window):
    n = v.shape[0]
    k = 1
    while k < window:
        v = v + pltpu.roll(v, n - k, 0)
        k *= 2
    return v


def _pool_normed_rows(x_ref, prev_ref, g_ref, i):
    h, r = _rms_fwd(x_ref[...], g_ref[...])
    hp, _ = _rms_fwd(prev_ref[...], g_ref[...])
    hp = jnp.where(i > 0, hp, 0.0)
    return jnp.concatenate([hp, h], axis=0), r


def _pooled_group(he, g, pg, first_row, tm):
    ue = he[:, g * pg:(g + 1) * pg]
    win = _trailing_sum(ue, POOL_WINDOWS[g])[POOL_HALO:]
    return win * _inv_count(first_row, tm, POOL_WINDOWS[g]) - ue[POOL_HALO:]


def _pool_specs(s, d, tm):
    per = tm // POOL_HALO
    last = s // POOL_HALO - 1
    tok = pl.BlockSpec((tm, d), lambda i: (i, 0))
    prev = pl.BlockSpec((POOL_HALO, d), lambda i: (jnp.maximum(i * per - 1, 0), 0))
    nxt = pl.BlockSpec((POOL_HALO, d), lambda i: (jnp.minimum((i + 1) * per, last), 0))
    return tok, prev, nxt


def _pool_fwd(x, gain, w, scale):
    s, d = x.shape
    n_g, pg = w.shape[0], w.shape[-1]
    tm = _tile(s, POOL_TILE)
    tok, prev, _ = _pool_specs(s, d, tm)

    def body(x_ref, prev_ref, g_ref, w_ref, sc_ref, y_ref):
        i = pl.program_id(0)
        he, _ = _pool_normed_rows(x_ref, prev_ref, g_ref, i)
        z = [_mm(_cast(_pooled_group(he, g, pg, i * tm, tm)), w_ref[g]) for g in range(n_g)]
        y_ref[...] = x_ref[...] + jnp.concatenate(z, axis=-1) * sc_ref[...]

    return pl.pallas_call(
        body, name="pool_fwd", grid=(s // tm,),
        in_specs=[tok, prev, _full((1, d)), _full(w.shape), _full((1, d))],
        out_specs=tok, out_shape=_sds((s, d)),
        compiler_params=_params(("arbitrary",)),
    )(x, x, gain, w, scale)


def _pool_bwd(x, gain, w, scale, dy):
    s, d = x.shape
    n_g, pg = w.shape[0], w.shape[-1]
    tm = _tile(s, POOL_TILE)
    n_tiles = s // tm
    tok, prev, nxt = _pool_specs(s, d, tm)

    def body(x_ref, prev_ref, dy_ref, next_ref, g_ref, w_ref, sc_ref, dx_ref, dgain_ref, dw_ref, dsc_ref):
        i = pl.program_id(0)

        @pl.when(i == 0)
        def _():
            dgain_ref[...] = jnp.zeros_like(dgain_ref)
            dw_ref[...] = jnp.zeros_like(dw_ref)
            dsc_ref[...] = jnp.zeros_like(dsc_ref)

        he, r = _pool_normed_rows(x_ref, prev_ref, g_ref, i)
        dy = dy_ref[...]
        dyn = jnp.where(i < n_tiles - 1, next_ref[...], 0.0)
        dze = jnp.concatenate([dy, dyn], axis=0) * sc_ref[...]
        dh, dsc = [], []
        for g in range(n_g):
            cols = slice(g * pg, (g + 1) * pg)
            pooled = _cast(_pooled_group(he, g, pg, i * tm, tm))
            dsc.append(jnp.sum(dy[:, cols] * _mm(pooled, w_ref[g]), axis=0, keepdims=True))
            dzb = _cast(dze[:, cols])
            dw_ref[g] += _mm_tn(pooled, dzb[:tm])
            dpool = _mm_nt(dzb, w_ref[g])
            spread = _leading_sum(dpool * _inv_count(i * tm, tm + POOL_HALO, POOL_WINDOWS[g]), POOL_WINDOWS[g])
            dh.append(spread[:tm] - dpool[:tm])
        dsc_ref[...] += jnp.concatenate(dsc, axis=-1)
        dxn, dgn = _rms_bwd(x_ref[...], r, g_ref[...], jnp.concatenate(dh, axis=-1))
        dgain_ref[...] += dgn
        dx_ref[...] = dy + dxn

    return pl.pallas_call(
        body, name="pool_bwd", grid=(n_tiles,),
        in_specs=[tok, prev, tok, nxt, _full((1, d)), _full(w.shape), _full((1, d))],
        out_specs=[tok, _full((1, d)), _full(w.shape), _full((1, d))],
        out_shape=[_sds((s, d)), _sds((1, d)), _sds(w.shape), _sds((1, d))],
        compiler_params=_params(("arbitrary",)),
    )(x, x, dy, dy, gain, w, scale)


def _rope_tables(pos_col, inv_freq):
    s = pos_col.shape[0]
    tm = _tile(s, ROW_TILE)
    half = QK_ROPE // 2

    def body(p_ref, f_ref, c_ref, s_ref):
        ang = p_ref[...].astype(F32) * f_ref[...]
        cos, sin = jnp.cos(ang), jnp.sin(ang)
        c_ref[...] = jnp.concatenate([jnp.ones((tm, QK_NOPE), F32), cos, cos], axis=-1)
        s_ref[...] = jnp.concatenate([jnp.zeros((tm, QK_NOPE), F32), -sin, sin], axis=-1)

    tab = pl.BlockSpec((tm, QK_HEAD), lambda i: (i, 0))
    return pl.pallas_call(
        body, name="rope_tables", grid=(s // tm,),
        in_specs=[pl.BlockSpec((tm, 1), lambda i: (i, 0)), _full((1, half))],
        out_specs=[tab, tab], out_shape=[_sds((s, QK_HEAD)), _sds((s, QK_HEAD))],
        compiler_params=_params(("arbitrary",)),
    )(pos_col, inv_freq)


def _swap_rope_halves(v):
    half = QK_ROPE // 2
    return jnp.concatenate([v[:, :QK_NOPE], v[:, QK_NOPE + half:], v[:, QK_NOPE:QK_NOPE + half]], axis=-1)


def _rope(v, cos, sin):
    return v * cos + _swap_rope_halves(v) * sin


def _rope_transposed(dv, cos, sin):
    return dv * cos + _swap_rope_halves(dv * sin)


def _mla_qkv_fwd(x, gain, w_in, q_norm, kv_norm, w_q, w_kn, w_v, q_head_norm, k_head_norm, cos, sin):
    s, d = x.shape
    n_h, ql = w_q.shape[0], w_q.shape[1]
    kvl, lat_w = w_kn.shape[1], w_in.shape[1]
    tm = _tile(s, MLA_TILE)

    def body(x_ref, g_ref, win_ref, qn_ref, kvn_ref, wq_ref, wkn_ref, wv_ref, qhn_ref, khn_ref, c_ref, s_ref,
             lat_ref, q_ref, k_ref, v_ref):
        h, _ = _rms_fwd(x_ref[...], g_ref[...])
        lat = _mm(_cast(h), win_ref[...])
        lat_ref[...] = lat
        cqn, _ = _rms_fwd(lat[:, :ql], qn_ref[...])
        ckvn, _ = _rms_fwd(lat[:, ql:ql + kvl], kvn_ref[...])
        kpe = lat[:, ql + kvl:]
        cqb, ckb = _cast(cqn), _cast(ckvn)
        cos_t, sin_t = c_ref[...], s_ref[...]
        v_ref[...] = _cast(_mm(ckb, wv_ref[...]))
        for hh in range(n_h):
            qn, _ = _rms_fwd(_mm(cqb, wq_ref[hh]), qhn_ref[...])
            q_ref[hh] = _cast(_rope(qn, cos_t, sin_t))
            kn, _ = _rms_fwd(jnp.concatenate([_mm(ckb, wkn_ref[hh]), kpe], axis=-1), khn_ref[...])
            k_ref[hh] = _cast(_rope(kn, cos_t, sin_t))

    tok = lambda w: pl.BlockSpec((tm, w), lambda i: (i, 0))
    heads = pl.BlockSpec((n_h, tm, QK_HEAD), lambda i: (0, i, 0))
    return pl.pallas_call(
        body, name="mla_qkv_fwd", grid=(s // tm,),
        in_specs=[tok(d), _full((1, d)), _full(w_in.shape), _full((1, ql)), _full((1, kvl)), _full(w_q.shape),
                  _full(w_kn.shape), _full(w_v.shape), _full((1, QK_HEAD)), _full((1, QK_HEAD)),
                  tok(QK_HEAD), tok(QK_HEAD)],
        out_specs=[tok(lat_w), heads, heads, tok(n_h * V_HEAD)],
        out_shape=[_sds((s, lat_w)), _sds((n_h, s, QK_HEAD), MXU_DTYPE), _sds((n_h, s, QK_HEAD), MXU_DTYPE),
                   _sds((s, n_h * V_HEAD), MXU_DTYPE)],
        compiler_params=_params(("arbitrary",)),
    )(x, gain, w_in, q_norm, kv_norm, w_q, w_kn, w_v, q_head_norm, k_head_norm, cos, sin)


def _mla_qkv_bwd(x, lat, dy, dq, dk, dv, gain, w_in, q_norm, kv_norm, w_q, w_kn, w_v, q_head_norm, k_head_norm,
                 cos, sin):
    s, d = x.shape
    n_h, ql = w_q.shape[0], w_q.shape[1]
    kvl, lat_w = w_kn.shape[1], w_in.shape[1]
    tm = _tile(s, MLA_TILE)

    def body(x_ref, lat_ref, dy_ref, dq_ref, dk_ref, dv_ref, g_ref, win_ref, qn_ref, kvn_ref, wq_ref, wkn_ref,
             wv_ref, qhn_ref, khn_ref, c_ref, s_ref,
             dx_ref, dg_ref, dwin_ref, dqn_ref, dkvn_ref, dwq_ref, dwkn_ref, dwv_ref, dqhn_ref, dkhn_ref):
        @pl.when(pl.program_id(0) == 0)
        def _():
            for ref in (dg_ref, dwin_ref, dqn_ref, dkvn_ref, dwq_ref, dwkn_ref, dwv_ref, dqhn_ref, dkhn_ref):
                ref[...] = jnp.zeros_like(ref)

        x_t = x_ref[...]
        h, r = _rms_fwd(x_t, g_ref[...])
        hb = _cast(h)
        lat = lat_ref[...]
        cq, ckv, kpe = lat[:, :ql], lat[:, ql:ql + kvl], lat[:, ql + kvl:]
        cqn, rq = _rms_fwd(cq, qn_ref[...])
        ckvn, rkv = _rms_fwd(ckv, kvn_ref[...])
        cqb, ckb = _cast(cqn), _cast(ckvn)
        cos_t, sin_t = c_ref[...], s_ref[...]

        dvb = _cast(dv_ref[...])
        dwv_ref[...] += _mm_tn(ckb, dvb)
        dckvn = _mm_nt(dvb, wv_ref[...])
        dcqn = jnp.zeros((tm, ql), F32)
        dkpe = jnp.zeros((tm, QK_ROPE), F32)
        dqhn = jnp.zeros((1, QK_HEAD), F32)
        dkhn = jnp.zeros((1, QK_HEAD), F32)
        for hh in range(n_h):
            qp = _mm(cqb, wq_ref[hh])
            _, rqp = _rms_fwd(qp, qhn_ref[...])
            dqp, dgq = _rms_bwd(qp, rqp, qhn_ref[...], _rope_transposed(dq_ref[hh], cos_t, sin_t))
            dqhn += dgq
            dqpb = _cast(dqp)
            dwq_ref[hh] += _mm_tn(cqb, dqpb)
            dcqn += _mm_nt(dqpb, wq_ref[hh])

            kp = jnp.concatenate([_mm(ckb, wkn_ref[hh]), kpe], axis=-1)
            _, rkp = _rms_fwd(kp, khn_ref[...])
            dkp, dgk = _rms_bwd(kp, rkp, khn_ref[...], _rope_transposed(dk_ref[hh], cos_t, sin_t))
            dkhn += dgk
            dknb = _cast(dkp[:, :QK_NOPE])
            dkpe += dkp[:, QK_NOPE:]
            dwkn_ref[hh] += _mm_tn(ckb, dknb)
            dckvn += _mm_nt(dknb, wkn_ref[hh])
        dqhn_ref[...] += dqhn
        dkhn_ref[...] += dkhn

        dcq, dgn = _rms_bwd(cq, rq, qn_ref[...], dcqn)
        dqn_ref[...] += dgn
        dckv, dgn = _rms_bwd(ckv, rkv, kvn_ref[...], dckvn)
        dkvn_ref[...] += dgn
        dlb = _cast(jnp.concatenate([dcq, dckv, dkpe], axis=-1))
        dwin_ref[...] += _mm_tn(hb, dlb)
        dxn, dgn = _rms_bwd(x_t, r, g_ref[...], _mm_nt(dlb, win_ref[...]))
        dg_ref[...] += dgn
        dx_ref[...] = dy_ref[...] + dxn

    tok = lambda w: pl.BlockSpec((tm, w), lambda i: (i, 0))
    heads = pl.BlockSpec((n_h, tm, QK_HEAD), lambda i: (0, i, 0))
    return pl.pallas_call(
        body, name="mla_qkv_bwd", grid=(s // tm,),
        in_specs=[tok(d), tok(lat_w), tok(d), heads, heads, tok(n_h * V_HEAD), _full((1, d)), _full(w_in.shape),
                  _full((1, ql)), _full((1, kvl)), _full(w_q.shape), _full(w_kn.shape), _full(w_v.shape),
                  _full((1, QK_HEAD)), _full((1, QK_HEAD)), tok(QK_HEAD), tok(QK_HEAD)],
        out_specs=[tok(d), _full((1, d)), _full(w_in.shape), _full((1, ql)), _full((1, kvl)), _full(w_q.shape),
                   _full(w_kn.shape), _full(w_v.shape), _full((1, QK_HEAD)), _full((1, QK_HEAD))],
        out_shape=[_sds((s, d)), _sds((1, d)), _sds(w_in.shape), _sds((1, ql)), _sds((1, kvl)), _sds(w_q.shape),
                   _sds(w_kn.shape), _sds(w_v.shape), _sds((1, QK_HEAD)), _sds((1, QK_HEAD))],
        compiler_params=_params(("arbitrary",)),
    )(x, lat, dy, dq, dk, dv, gain, w_in, q_norm, kv_norm, w_q, w_kn, w_v, q_head_norm, k_head_norm, cos, sin)


def _scores_t(k_t, q_t, t, masked):
    z = _mm_nt(k_t, q_t) * (QK_HEAD ** -0.5)
    if masked:
        key = lax.broadcasted_iota(jnp.int32, (t, t), 0)
        query = lax.broadcasted_iota(jnp.int32, (t, t), 1)
        z = jnp.where(key <= query, z, -jnp.inf)
    return z


def _flash_fwd(q, k, vt):
    n_h, s, _ = q.shape
    t = _tile(s, ATTN_TILE)
    n = s // t

    def body(q_ref, k_ref, vt_ref, ot_ref, lse_ref, m_sc, l_sc, acc_sc):
        i = pl.program_id(1)
        m_sc[...] = jnp.full_like(m_sc, -jnp.inf)
        l_sc[...] = jnp.zeros_like(l_sc)
        acc_sc[...] = jnp.zeros_like(acc_sc)
        q_t = q_ref[...]

        def step(j, masked):
            z = _scores_t(k_ref[pl.ds(pl.multiple_of(j * t, t), t), :], q_t, t, masked)
            m_old = m_sc[...]
            m_new = jnp.maximum(m_old, jnp.max(z, axis=0, keepdims=True))
            alpha = jnp.exp(m_old - m_new)
            pr = jnp.exp(z - m_new)
            l_sc[...] = alpha * l_sc[...] + jnp.sum(pr, axis=0, keepdims=True)
            acc_sc[...] = alpha * acc_sc[...] + _mm(vt_ref[j], _cast(pr))
            m_sc[...] = m_new

        def below_diagonal(j, carry):
            step(j, False)
            return carry

        lax.fori_loop(0, i, below_diagonal, 0)
        step(i, True)
        ot_ref[...] = acc_sc[...] / l_sc[...]
        lse_ref[...] = m_sc[...] + jnp.log(l_sc[...])

    return pl.pallas_call(
        body, name="flash_fwd", grid=(n_h, n),
        in_specs=[pl.BlockSpec((None, t, QK_HEAD), lambda h, i: (h, i, 0)),
                  pl.BlockSpec((None, s, QK_HEAD), lambda h, i: (h, 0, 0)),
                  pl.BlockSpec((None, n, V_HEAD, t), lambda h, i: (h, 0, 0, 0))],
        out_specs=[pl.BlockSpec((V_HEAD, t), lambda h, i: (h, i)),
                   pl.BlockSpec((None, 1, t), lambda h, i: (h, 0, i))],
        out_shape=[_sds((n_h * V_HEAD, s)), _sds((n_h, 1, s))],
        scratch_shapes=[pltpu.VMEM((1, t), F32), pltpu.VMEM((1, t), F32), pltpu.VMEM((V_HEAD, t), F32)],
        compiler_params=_params(("arbitrary", "arbitrary")),
    )(q, k, vt)


def _flash_bwd(q, k, kt, v, do, lse, delta):
    n_h, s, _ = q.shape
    t = _tile(s, ATTN_TILE)
    n = s // t

    def body(q_ref, do_ref, lse_ref, dl_ref, k_ref, kt_ref, v_ref, dqt_ref, dk_ref, dv_ref, dk_sc, dv_sc):
        j = pl.program_id(1)

        @pl.when(j == 0)
        def _():
            dqt_ref[...] = jnp.zeros_like(dqt_ref)

        dk_sc[...] = jnp.zeros_like(dk_sc)
        dv_sc[...] = jnp.zeros_like(dv_sc)
        k_t, kt_t, v_t = k_ref[...], kt_ref[...], v_ref[...]

        def step(i, masked):
            rows = pl.ds(pl.multiple_of(i * t, t), t)
            q_t, do_t = q_ref[rows, :], do_ref[rows, :]
            pr = jnp.exp(_scores_t(k_t, q_t, t, masked) - lse_ref[i])
            dv_sc[...] += _mm(_cast(pr), do_t)
            dsb = _cast(pr * (_mm_nt(v_t, do_t) - dl_ref[i]) * (QK_HEAD ** -0.5))
            dk_sc[...] += _mm(dsb, q_t)
            dqt_ref[i] += _mm(kt_t, dsb)

        step(j, True)

        def below_diagonal(i, carry):
            step(i, False)
            return carry

        lax.fori_loop(j + 1, n, below_diagonal, 0)
        dk_ref[...] = dk_sc[...]
        dv_ref[...] = dv_sc[...]

    stat = pl.BlockSpec((None, n, 1, t), lambda h, j: (h, 0, 0, 0))
    return pl.pallas_call(
        body, name="flash_bwd", grid=(n_h, n),
        in_specs=[pl.BlockSpec((None, s, QK_HEAD), lambda h, j: (h, 0, 0)),
                  pl.BlockSpec((s, V_HEAD), lambda h, j: (0, h)),
                  stat, stat,
                  pl.BlockSpec((None, t, QK_HEAD), lambda h, j: (h, j, 0)),
                  pl.BlockSpec((None, QK_HEAD, t), lambda h, j: (h, 0, j)),
                  pl.BlockSpec((t, V_HEAD), lambda h, j: (j, h))],
        out_specs=[pl.BlockSpec((None, n, QK_HEAD, t), lambda h, j: (h, 0, 0, 0)),
                   pl.BlockSpec((None, t, QK_HEAD), lambda h, j: (h, j, 0)),
                   pl.BlockSpec((t, V_HEAD), lambda h, j: (j, h))],
        out_shape=[_sds((n_h, n, QK_HEAD, t)), _sds((n_h, s, QK_HEAD)), _sds((s, n_h * V_HEAD))],
        scratch_shapes=[pltpu.VMEM((t, QK_HEAD), F32), pltpu.VMEM((t, V_HEAD), F32)],
        compiler_params=_params(("arbitrary", "arbitrary")),
    )(q, do, lse, delta, k, kt, v)


def _mla_out_fwd(x, ot, w_out):
    s, d = x.shape
    hv = ot.shape[0]
    tm = _tile(s, FFN_TILE)

    def body(x_ref, ot_ref, w_ref, y_ref):
        y_ref[...] = x_ref[...] + _mm_tn(_cast(ot_ref[...]), w_ref[...])

    tok = pl.BlockSpec((tm, d), lambda i: (i, 0))
    return pl.pallas_call(
        body, name="mla_out_fwd", grid=(s // tm,),
        in_specs=[tok, pl.BlockSpec((hv, tm), lambda i: (0, i)), _full(w_out.shape)],
        out_specs=tok, out_shape=_sds((s, d)),
        compiler_params=_params(("arbitrary",)),
    )(x, ot, w_out)


def _mla_out_bwd(dy, ot, w_out):
    s, d = dy.shape
    hv = ot.shape[0]
    n_h = hv // V_HEAD
    tm = _tile(s, FFN_TILE)

    def body(dy_ref, ot_ref, w_ref, do_ref, dl_ref, dw_ref):
        @pl.when(pl.program_id(0) == 0)
        def _():
            dw_ref[...] = jnp.zeros_like(dw_ref)

        dyb = _cast(dy_ref[...])
        o_t = ot_ref[...]
        do_ref[...] = _cast(_mm_nt(dyb, w_ref[...]))
        prod = _mm_nt(w_ref[...], dyb) * o_t
        for hh in range(n_h):
            dl_ref[hh] = jnp.sum(prod[hh * V_HEAD:(hh + 1) * V_HEAD], axis=0, keepdims=True)
        dw_ref[...] += _mm(_cast(o_t), dyb)

    return pl.pallas_call(
        body, name="mla_out_bwd", grid=(s // tm,),
        in_specs=[pl.BlockSpec((tm, d), lambda i: (i, 0)), pl.BlockSpec((hv, tm), lambda i: (0, i)),
                  _full(w_out.shape)],
        out_specs=[pl.BlockSpec((tm, hv), lambda i: (i, 0)), pl.BlockSpec((n_h, 1, tm), lambda i: (0, 0, i)),
                   _full(w_out.shape)],
        out_shape=[_sds((s, hv), MXU_DTYPE), _sds((n_h, 1, s)), _sds(w_out.shape)],
        compiler_params=_params(("arbitrary",)),
    )(dy, ot, w_out)


def _loss_and_grad(y, target):
    s, d = y.shape
    tm = _tile(s, ROW_TILE)

    def body(y_ref, t_ref, loss_ref, dy_ref):
        @pl.when(pl.program_id(0) == 0)
        def _():
            loss_ref[...] = jnp.zeros_like(loss_ref)

        err = y_ref[...] - t_ref[...]
        dy_ref[...] = err * (1.0 / d)
        loss_ref[...] += 0.5 * jnp.sum(jnp.mean(err * err, axis=-1, keepdims=True), axis=0, keepdims=True)

    tok = pl.BlockSpec((tm, d), lambda i: (i, 0))
    return pl.pallas_call(
        body, name="loss_and_grad", grid=(s // tm,),
        in_specs=[tok, tok], out_specs=[_full((1, 1)), tok],
        out_shape=[_sds((1, 1)), _sds((s, d))],
        compiler_params=_params(("arbitrary",)),
    )(y, target)


def _mesh_position():
    return lax.axis_index("x"), lax.axis_index("y"), lax.axis_index("c")


def _other_chips(x, y):
    return [(1 - x, y), (x, 1 - y), (1 - x, 1 - y)]


ANY = pl.BlockSpec(memory_space=pl.ANY)


def _gather_over_chips(arrs):
    n = len(arrs)
    halves = [a.shape[0] // 2 for a in arrs]
    assert all(a.shape[0] % 2 == 0 for a in arrs)
    own = 2 * (N_CHIPS - 1)

    def body(*refs):
        srcs, outs = refs[:n], refs[n:2 * n]
        send_sems, recv_sems = refs[2 * n:]
        x, y, c = _mesh_position()
        me, sibling = (x, y, c), (x, y, 1 - c)
        chips = _other_chips(x, y)
        my_chip = 2 * x + y

        def rows(t, chip, half):
            return outs[t].at[chip, pl.ds(half * halves[t], halves[t])]

        def copy(t, k, src, dst, to):
            return pltpu.make_async_remote_copy(src_ref=src, dst_ref=dst, send_sem=send_sems.at[t, k],
                                                recv_sem=recv_sems.at[t, k], device_id=to, device_id_type=MESH)

        started = []
        for t in range(n):
            for k, (px, py) in enumerate(chips):
                cp = copy(t, k, srcs[t].at[pl.ds(c * halves[t], halves[t])], rows(t, my_chip, c), (px, py, c))
                cp.start()
                started.append(cp)
            cp = copy(t, own, srcs[t], outs[t].at[my_chip], sibling)
            cp.start()
            started.append(cp)
        for t in range(n):
            for k, (px, py) in enumerate(chips):
                landed = rows(t, 2 * px + py, c)
                copy(t, k, landed, landed, me).wait_recv()
                cp = copy(t, N_CHIPS - 1 + k, landed, landed, sibling)
                cp.start()
                started.append(cp)
        for t in range(n):
            for k, (px, py) in enumerate(chips):
                passed = rows(t, 2 * px + py, 1 - c)
                copy(t, N_CHIPS - 1 + k, passed, passed, me).wait_recv()
            copy(t, own, srcs[t], outs[t].at[my_chip], me).wait_recv()
        for cp in started:
            cp.wait_send()

    return pl.pallas_call(
        body, name="gather_over_chips",
        in_specs=[ANY] * n, out_specs=[ANY] * n,
        out_shape=[_sds((N_CHIPS,) + a.shape, a.dtype) for a in arrs],
        scratch_shapes=[pltpu.SemaphoreType.DMA((n, own + 1)), pltpu.SemaphoreType.DMA((n, own + 1))],
    )(*arrs)


def _send_other_half_to_sibling(grads):
    n = len(grads)
    halves = [g.shape[1] // 2 for g in grads]

    def body(*refs):
        srcs, outs = refs[:n], refs[n:2 * n]
        send_sems, recv_sems = refs[2 * n:]
        x, y, c = _mesh_position()
        copies = []
        for t in range(n):
            cp = pltpu.make_async_remote_copy(
                src_ref=srcs[t].at[pl.ds(0, N_CHIPS), pl.ds((1 - c) * halves[t], halves[t])], dst_ref=outs[t],
                send_sem=send_sems.at[t], recv_sem=recv_sems.at[t], device_id=(x, y, 1 - c), device_id_type=MESH)
            cp.start()
            copies.append(cp)
        for cp in copies:
            cp.wait_recv()
        for cp in copies:
            cp.wait_send()

    return pl.pallas_call(
        body, name="send_other_half_to_sibling",
        in_specs=[ANY] * n, out_specs=[ANY] * n,
        out_shape=[_sds((N_CHIPS, h) + g.shape[2:]) for g, h in zip(grads, halves)],
        scratch_shapes=[pltpu.SemaphoreType.DMA((n,)), pltpu.SemaphoreType.DMA((n,))],
    )(*grads)


def _send_blocks_to_chips(parts):
    n = len(parts)

    def body(*refs):
        srcs, outs = refs[:n], refs[n:2 * n]
        send_sems, recv_sems = refs[2 * n:]
        x, y, c = _mesh_position()
        copies = []
        for t in range(n):
            for k, (px, py) in enumerate(_other_chips(x, y)):
                cp = pltpu.make_async_remote_copy(
                    src_ref=srcs[t].at[2 * px + py], dst_ref=outs[t].at[k], send_sem=send_sems.at[t, k],
                    recv_sem=recv_sems.at[t, k], device_id=(px, py, c), device_id_type=MESH)
                cp.start()
                copies.append(cp)
        for cp in copies:
            cp.wait_recv()
        for cp in copies:
            cp.wait_send()

    return pl.pallas_call(
        body, name="send_blocks_to_chips",
        in_specs=[ANY] * n, out_specs=[ANY] * n,
        out_shape=[_sds((N_CHIPS - 1,) + p.shape[1:], p.dtype) for p in parts],
        scratch_shapes=[pltpu.SemaphoreType.DMA((n, N_CHIPS - 1)), pltpu.SemaphoreType.DMA((n, N_CHIPS - 1))],
    )(*parts)


def _join_halves_with_sibling(sums):
    n = len(sums)

    def body(*refs):
        srcs, outs = refs[:n], refs[n:2 * n]
        send_sems, recv_sems = refs[2 * n:]
        x, y, c = _mesh_position()
        copies = []
        for t in range(n):
            h = srcs[t].shape[0] // 2
            mine = pl.ds(c * h, h)
            cp = pltpu.make_async_remote_copy(
                src_ref=srcs[t].at[mine], dst_ref=outs[t].at[mine], send_sem=send_sems.at[t],
                recv_sem=recv_sems.at[t], device_id=(x, y, 1 - c), device_id_type=MESH)
            cp.start()
            copies.append(cp)
        for t in range(n):
            h = srcs[t].shape[0] // 2
            theirs = pl.ds((1 - c) * h, h)
            pltpu.make_async_remote_copy(
                src_ref=srcs[t].at[theirs], dst_ref=outs[t].at[theirs], send_sem=send_sems.at[t],
                recv_sem=recv_sems.at[t], device_id=(x, y, 1 - c), device_id_type=MESH).wait_recv()
        for cp in copies:
            cp.wait_send()

    return pl.pallas_call(
        body, name="join_halves_with_sibling",
        in_specs=[ANY] * n, out_specs=[ANY] * n,
        out_shape=[_sds(a.shape) for a in sums],
        input_output_aliases={t: t for t in range(n)},
        scratch_shapes=[pltpu.SemaphoreType.DMA((n,)), pltpu.SemaphoreType.DMA((n,))],
    )(*sums)


def _gather_over_devices(rows):
    r = rows.shape[0]

    def body(in_ref, out_ref, send_sems, recv_sems, local_sem):
        x, y, c = _mesh_position()
        mine = pltpu.make_async_copy(in_ref, out_ref.at[4 * x + 2 * y + c], local_sem)
        mine.start()
        copies = []
        for mask in range(1, N_DEVICES):
            fx, fy, fc = (mask >> 2) & 1, (mask >> 1) & 1, mask & 1
            px, py, pc = (1 - x if fx else x), (1 - y if fy else y), (1 - c if fc else c)
            send = pltpu.make_async_remote_copy(
                src_ref=in_ref, dst_ref=out_ref.at[4 * x + 2 * y + c], send_sem=send_sems.at[mask - 1],
                recv_sem=recv_sems.at[mask - 1], device_id=(px, py, pc), device_id_type=MESH)
            send.start()
            recv = pltpu.make_async_remote_copy(
                src_ref=in_ref, dst_ref=out_ref.at[4 * px + 2 * py + pc], send_sem=send_sems.at[mask - 1],
                recv_sem=recv_sems.at[mask - 1], device_id=(px, py, pc), device_id_type=MESH)
            copies.append((send, recv))
        for _, recv in copies:
            recv.wait_recv()
        for send, _ in copies:
            send.wait_send()
        mine.wait()

    vm = pl.BlockSpec(memory_space=pltpu.VMEM)
    return pl.pallas_call(
        body, name="gather_over_devices", in_specs=[vm], out_specs=vm,
        out_shape=_sds((N_DEVICES, r, LANES)),
        scratch_shapes=[pltpu.SemaphoreType.DMA((N_DEVICES - 1,)), pltpu.SemaphoreType.DMA((N_DEVICES - 1,)),
                        pltpu.SemaphoreType.DMA],
    )(rows)


def _add_sibling_half(grad, received, chip, core):
    _, l, r, c = grad.shape
    half = l // 2

    def body(chip_ref, core_ref, g_ref, r_ref, wire_ref, own_ref):
        total = g_ref[...] + r_ref[...]
        wire_ref[...] = total.astype(WIRE_DTYPE)

        @pl.when(pl.program_id(1) == chip_ref[0])
        def _():
            own_ref[...] = total

    blk = lambda f: pl.BlockSpec((None, None, r, c), f)
    grid_spec = pltpu.PrefetchScalarGridSpec(
        num_scalar_prefetch=2, grid=(half, N_CHIPS),
        in_specs=[blk(lambda i, j, chip, core: (j, core[0] * half + i, 0, 0)),
                  blk(lambda i, j, chip, core: (j, i, 0, 0))],
        out_specs=[blk(lambda i, j, chip, core: (j, i, 0, 0)),
                   pl.BlockSpec((None, r, c), lambda i, j, chip, core: (i, 0, 0))])
    return pl.pallas_call(
        body, name="add_sibling_half", grid_spec=grid_spec,
        out_shape=[_sds((N_CHIPS, half, r, c), WIRE_DTYPE), _sds((half, r, c))],
        compiler_params=_params(("arbitrary", "arbitrary")),
    )(chip, core, grad, received)


def _add_chip_blocks(own, received, core):
    half, r, c = own.shape

    def body(core_ref, p_ref, r0_ref, r1_ref, r2_ref, o_ref):
        o_ref[...] = ((p_ref[...] + r0_ref[...].astype(F32)) + r1_ref[...].astype(F32)) + r2_ref[...].astype(F32)

    grid_spec = pltpu.PrefetchScalarGridSpec(
        num_scalar_prefetch=1, grid=(half,),
        in_specs=[pl.BlockSpec((None, r, c), lambda i, core: (i, 0, 0))] + [
            pl.BlockSpec((None, None, r, c), functools.partial(lambda i, core, k: (k, i, 0, 0), k=k))
            for k in range(N_CHIPS - 1)],
        out_specs=pl.BlockSpec((None, r, c), lambda i, core: (core[0] * half + i, 0, 0)))
    return pl.pallas_call(
        body, name="add_chip_blocks", grid_spec=grid_spec, out_shape=_sds((2 * half, r, c)),
        compiler_params=_params(("arbitrary",)),
    )(core, own, received, received, received)


def _sum_over_devices(parts):
    _, r, _ = parts.shape

    def body(p_ref, o_ref):
        acc = p_ref[0]
        for k in range(1, N_DEVICES):
            acc = acc + p_ref[k]
        o_ref[...] = acc

    return pl.pallas_call(body, name="sum_over_devices", out_shape=_sds((r, LANES)))(parts)


def _adamw_math(w, g, m, v):
    m = ADAM_B1 * m + (1.0 - ADAM_B1) * g
    v = ADAM_B2 * v + (1.0 - ADAM_B2) * (g * g)
    m_hat = m / (1.0 - ADAM_B1 ** ADAM_STEP)
    v_hat = v / (1.0 - ADAM_B2 ** ADAM_STEP)
    delta = -ADAM_LR * (m_hat / (jnp.sqrt(v_hat) + ADAM_EPS) + ADAM_WD * w)
    return delta, m, v


def _adamw_stacked(w, m, v, grads, offset):
    l, r, c = w.shape
    tr = r
    while tr * c * 4 > 2**20 and tr % 16 == 0:
        tr //= 2

    def body(w_ref, m_ref, v_ref, g_ref, go_ref, d_ref, mo_ref, vo_ref):
        g = g_ref[...]
        go_ref[...] = g
        d_ref[...], mo_ref[...], vo_ref[...] = _adamw_math(w_ref[...], g, m_ref[...], v_ref[...])

    blk = pl.BlockSpec((None, tr, c), lambda i, j: (i, j, 0))
    return pl.pallas_call(
        body, name="adamw_stacked", grid=(l, r // tr),
        in_specs=[blk, blk, blk, pl.BlockSpec((None, tr, c), lambda i, j: (offset + i, j, 0))],
        out_specs=[blk] * 4, out_shape=[_sds((l, r, c))] * 4,
        compiler_params=_params(("arbitrary", "arbitrary")),
    )(w, m, v, grads)


def _adamw_small(w, m, v, g):
    def body(w_ref, m_ref, v_ref, g_ref, d_ref, mo_ref, vo_ref):
        d_ref[...], mo_ref[...], vo_ref[...] = _adamw_math(w_ref[...], g_ref[...], m_ref[...], v_ref[...])

    return pl.pallas_call(body, name="adamw_small", out_shape=[_sds(w.shape)] * 3)(w, m, v, g)


def _pack_rows(arrs):
    flat = jnp.concatenate([a.reshape(-1) for a in arrs])
    pad = (-flat.shape[0]) % (8 * LANES)
    return jnp.pad(flat, (0, pad)).reshape(-1, LANES)


def _unpack_rows(rows, shapes, lead=()):
    flat = rows.reshape(lead + (-1,))
    out, at = [], 0
    for shp in shapes:
        size = int(np.prod(shp))
        out.append(flat[..., at:at + size].reshape(lead + tuple(shp)))
        at += size
    return out


WEIGHT_NAMES = ('ffn1_norm', 'ffn1_w_gate', 'ffn1_w_up', 'ffn1_w_down', 'mix_norm', 'pool_w', 'pool_scale',
                'mla_w_in', 'mla_q_norm', 'mla_w_q_up', 'mla_kv_norm', 'mla_w_kv_up', 'mla_q_head_norm',
                'mla_k_head_norm', 'mla_w_out', 'ffn2_norm', 'ffn2_w_gate', 'ffn2_w_up', 'ffn2_w_down')


def _chips_to_columns(g):
    return jnp.transpose(g, (1, 2, 0, 3)).reshape(g.shape[1], g.shape[2], -1)


def _columns_to_chips(full):
    n, r, c4 = full.shape
    return jnp.transpose(full.reshape(n, r, N_CHIPS, c4 // N_CHIPS), (2, 0, 1, 3))


def kernel(x, positions, ffn1_norm, ffn1_w_gate, ffn1_w_up, ffn1_w_down, mix_norm, pool_w, pool_scale, mla_w_in, mla_q_norm, mla_w_q_up, mla_kv_norm, mla_w_kv_up, mla_q_head_norm, mla_k_head_norm, mla_w_out, ffn2_norm, ffn2_w_gate, ffn2_w_up, ffn2_w_down, loss_target, m_ffn1_norm, m_ffn1_w_gate, m_ffn1_w_up, m_ffn1_w_down, m_mix_norm, m_pool_w, m_pool_scale, m_mla_w_in, m_mla_q_norm, m_mla_w_q_up, m_mla_kv_norm, m_mla_w_kv_up, m_mla_q_head_norm, m_mla_k_head_norm, m_mla_w_out, m_ffn2_norm, m_ffn2_w_gate, m_ffn2_w_up, m_ffn2_w_down, v_ffn1_norm, v_ffn1_w_gate, v_ffn1_w_up, v_ffn1_w_down, v_mix_norm, v_pool_w, v_pool_scale, v_mla_w_in, v_mla_q_norm, v_mla_w_q_up, v_mla_kv_norm, v_mla_w_kv_up, v_mla_q_head_norm, v_mla_k_head_norm, v_mla_w_out, v_ffn2_norm, v_ffn2_w_gate, v_ffn2_w_up, v_ffn2_w_down):
    env = dict(locals())
    w = {n: env[n] for n in WEIGHT_NAMES}
    mom = {n: env["m_" + n] for n in WEIGHT_NAMES}
    var = {n: env["v_" + n] for n in WEIGHT_NAMES}

    s, d = x.shape[1], x.shape[2]
    depth = ffn1_norm.shape[0]
    n_mla, n_pool, n_groups = mla_w_in.shape[0], pool_w.shape[0], pool_w.shape[1]
    pool_c = pool_w.shape[3]
    q_lora = N_CHIPS * mla_q_norm.shape[1]
    kv_lora = N_CHIPS * mla_kv_norm.shape[1]
    n_heads = N_CHIPS * mla_w_q_up.shape[2] // QK_HEAD
    t_attn = _tile(s, ATTN_TILE)
    n_attn = s // t_attn
    cx, cy, cc = _mesh_position()
    chip = 2 * cx + cy
    chip_arr = jnp.reshape(chip, (1,)).astype(jnp.int32)
    core_arr = jnp.reshape(cc, (1,)).astype(jnp.int32)

    shard_gu = _cast(jnp.concatenate([ffn1_w_gate, ffn1_w_up, ffn2_w_gate, ffn2_w_up], axis=0))
    shard_dn = _cast(jnp.concatenate([ffn1_w_down, ffn2_w_down], axis=0))
    shard_pool = _cast(pool_w.reshape((n_pool * n_groups,) + pool_w.shape[2:]))
    w_gu, w_dn, g_in, g_qup, g_kvup, g_out, g_pool = _gather_over_chips(
        [shard_gu, shard_dn, _cast(mla_w_in), _cast(mla_w_q_up), _cast(mla_w_kv_up), _cast(mla_w_out), shard_pool])
    small_shapes = [mla_q_norm.shape, mla_kv_norm.shape]
    small = _gather_over_devices(_pack_rows([mla_q_norm, mla_kv_norm]))[::2]
    qn_chips, kvn_chips = _unpack_rows(small, small_shapes, lead=(N_CHIPS,))
    q_norm_full = jnp.transpose(qn_chips, (1, 0, 2)).reshape(n_mla, 1, q_lora)
    kv_norm_full = jnp.transpose(kvn_chips, (1, 0, 2)).reshape(n_mla, 1, kv_lora)

    w_in_full = _chips_to_columns(g_in)
    w_q_heads = jnp.transpose(_chips_to_columns(g_qup).reshape(n_mla, q_lora, n_heads, QK_HEAD), (0, 2, 1, 3))
    w_kv = _chips_to_columns(g_kvup).reshape(n_mla, kv_lora, n_heads, QK_NOPE + V_HEAD)
    w_kn_heads = jnp.transpose(w_kv[..., :QK_NOPE], (0, 2, 1, 3))
    w_v_full = w_kv[..., QK_NOPE:].reshape(n_mla, kv_lora, n_heads * V_HEAD)
    w_out_full = jnp.transpose(g_out, (1, 0, 2, 3)).reshape(n_mla, n_heads * V_HEAD, d)
    pool_full = jnp.transpose(g_pool.reshape(N_CHIPS, n_pool, n_groups, pool_c // N_CHIPS, pool_c),
                              (1, 2, 0, 3, 4)).reshape(n_pool, n_groups, pool_c, pool_c)

    inv_freq = (1.0 / (ROPE_THETA ** (jnp.arange(0, QK_ROPE, 2, dtype=F32) / QK_ROPE))).reshape(1, -1)
    cos_t, sin_t = _rope_tables(positions.reshape(s, 1), inv_freq)

    row = lambda a, i: a[i].reshape(1, -1)
    i_gate1, i_up1, i_gate2, i_up2 = (lambda i: i), (lambda i: depth + i), (lambda i: 2 * depth + i), (lambda i: 3 * depth + i)
    i_dn1, i_dn2 = (lambda i: i), (lambda i: depth + i)

    h = x.reshape(s, d)
    saved = []
    for i in range(depth):
        rec = {"x_ffn1": h}
        h, *rec["ffn1"] = _ffn_fwd(h, row(ffn1_norm, i), w_gu, w_dn, i_gate1(i), i_up1(i), i_dn1(i))
        rec["x_mix"] = h
        j = i // 2
        if i % 2 == 0:
            h = _pool_fwd(h, row(mix_norm, i), pool_full[j], row(pool_scale, j))
        else:
            lat, q, k, v = _mla_qkv_fwd(h, row(mix_norm, i), w_in_full[j], q_norm_full[j], kv_norm_full[j],
                                        w_q_heads[j], w_kn_heads[j], w_v_full[j], row(mla_q_head_norm, j),
                                        row(mla_k_head_norm, j), cos_t, sin_t)
            vt = jnp.transpose(v.reshape(n_attn, t_attn, n_heads, V_HEAD), (2, 0, 3, 1))
            ot, lse = _flash_fwd(q, k, vt)
            rec.update(lat=lat, q=q, k=k, v=v, ot=ot, lse=lse)
            h = _mla_out_fwd(h, ot, w_out_full[j])
        rec["x_ffn2"] = h
        h, *rec["ffn2"] = _ffn_fwd(h, row(ffn2_norm, i), w_gu, w_dn, i_gate2(i), i_up2(i), i_dn2(i))
        saved.append(rec)

    loss_part, dy = _loss_and_grad(h, loss_target.reshape(s, d))
    loss = lax.psum(loss_part[0, 0], ("x", "y", "c"))

    g_gu = [None] * (4 * depth)
    g_dn = [None] * (2 * depth)
    g_norm = {n: [None] * depth for n in ("ffn1_norm", "mix_norm", "ffn2_norm")}
    g_pool_w, g_pool_scale = [None] * n_pool, [None] * n_pool
    g_mla = {n: [None] * n_mla for n in ("w_in", "q_norm", "kv_norm", "w_q", "w_kv", "qhn", "khn", "w_out")}
    for i in reversed(range(depth)):
        rec = saved[i]
        hb, gate, up = rec["ffn2"]
        dy, g_norm["ffn2_norm"][i], dyb, dgt, dup, act = _ffn_bwd_dgrad(
            rec["x_ffn2"], row(ffn2_norm, i), dy, gate, up, w_gu, w_dn, i_gate2(i), i_up2(i), i_dn2(i))
        g_gu[i_gate2(i)], g_gu[i_up2(i)], g_dn[i_dn2(i)] = _ffn_wgrad(hb, dyb, dgt, dup, act)
        j = i // 2
        if i % 2 == 0:
            dy, g_norm["mix_norm"][i], g_pool_w[j], g_pool_scale[j] = _pool_bwd(
                rec["x_mix"], row(mix_norm, i), pool_full[j], row(pool_scale, j), dy)
        else:
            do, delta, g_mla["w_out"][j] = _mla_out_bwd(dy, rec["ot"], w_out_full[j])
            by_tile = lambda a: a.reshape(n_heads, n_attn, 1, t_attn)
            dqt, dk, dv = _flash_bwd(rec["q"], rec["k"], jnp.transpose(rec["k"], (0, 2, 1)), rec["v"], do,
                                     by_tile(rec["lse"]), by_tile(delta))
            dq = jnp.transpose(dqt, (0, 1, 3, 2)).reshape(n_heads, s, QK_HEAD)
            (dy, g_norm["mix_norm"][i], g_mla["w_in"][j], g_mla["q_norm"][j], g_mla["kv_norm"][j], dwq, dwkn, dwv,
             g_mla["qhn"][j], g_mla["khn"][j]) = _mla_qkv_bwd(
                rec["x_mix"], rec["lat"], dy, dq, dk, dv, row(mix_norm, i), w_in_full[j], q_norm_full[j],
                kv_norm_full[j], w_q_heads[j], w_kn_heads[j], w_v_full[j], row(mla_q_head_norm, j),
                row(mla_k_head_norm, j), cos_t, sin_t)
            g_mla["w_q"][j] = jnp.transpose(dwq, (1, 0, 2)).reshape(q_lora, n_heads * QK_HEAD)
            g_mla["w_kv"][j] = jnp.concatenate(
                [jnp.transpose(dwkn, (1, 0, 2)), dwv.reshape(kv_lora, n_heads, V_HEAD)], axis=-1
            ).reshape(kv_lora, n_heads * (QK_NOPE + V_HEAD))
        hb, gate, up = rec["ffn1"]
        dy, g_norm["ffn1_norm"][i], dyb, dgt, dup, act = _ffn_bwd_dgrad(
            rec["x_ffn1"], row(ffn1_norm, i), dy, gate, up, w_gu, w_dn, i_gate1(i), i_up1(i), i_dn1(i))
        g_gu[i_gate1(i)], g_gu[i_up1(i)], g_dn[i_dn1(i)] = _ffn_wgrad(hb, dyb, dgt, dup, act)
    grad_x = dy.reshape(x.shape)

    full_grads = [
        jnp.stack(g_gu, axis=1),
        jnp.stack(g_dn, axis=1),
        _columns_to_chips(jnp.stack(g_mla["w_in"])),
        _columns_to_chips(jnp.stack(g_mla["w_q"])),
        _columns_to_chips(jnp.stack(g_mla["w_kv"])),
        jnp.transpose(jnp.stack(g_mla["w_out"]).reshape(n_mla, N_CHIPS, -1, d), (1, 0, 2, 3)),
        jnp.transpose(jnp.stack(g_pool_w).reshape(n_pool, n_groups, N_CHIPS, pool_c // N_CHIPS, pool_c),
                      (2, 0, 1, 3, 4)).reshape(N_CHIPS, n_pool * n_groups, pool_c // N_CHIPS, pool_c),
    ]
    from_sibling = _send_other_half_to_sibling(full_grads)
    chip_sums = [_add_sibling_half(g, r, chip_arr, core_arr) for g, r in zip(full_grads, from_sibling)]
    from_chips = _send_blocks_to_chips([wire for wire, _ in chip_sums])
    half_sums = [_add_chip_blocks(own, r, core_arr) for (_, own), r in zip(chip_sums, from_chips)]
    r_gu, r_dn, r_in, r_qup, r_kvup, r_out, r_pool = _join_halves_with_sibling(half_sums)

    small_grads = [jnp.concatenate(g_norm["ffn1_norm"]), jnp.concatenate(g_norm["mix_norm"]),
                   jnp.concatenate(g_norm["ffn2_norm"]), jnp.concatenate(g_pool_scale),
                   jnp.concatenate(g_mla["qhn"]), jnp.concatenate(g_mla["khn"]),
                   jnp.concatenate(g_mla["q_norm"]), jnp.concatenate(g_mla["kv_norm"])]
    small_sum = _sum_over_devices(_gather_over_devices(_pack_rows(small_grads)))
    (s_ffn1, s_mix, s_ffn2, s_pscale, s_qhn, s_khn, s_qn, s_kvn) = _unpack_rows(small_sum, [g.shape for g in small_grads])
    qn_w, kvn_w = mla_q_norm.shape[1], mla_kv_norm.shape[1]
    s_qn = lax.dynamic_slice_in_dim(s_qn, chip * qn_w, qn_w, axis=1)
    s_kvn = lax.dynamic_slice_in_dim(s_kvn, chip * kvn_w, kvn_w, axis=1)

    grads, deltas, new_m, new_v = {}, {}, {}, {}

    def stacked(name, reduced, offset):
        shape = w[name].shape
        as3 = lambda a: a.reshape((-1,) + shape[-2:])
        out = _adamw_stacked(as3(w[name]), as3(mom[name]), as3(var[name]), reduced, offset)
        grads[name], deltas[name], new_m[name], new_v[name] = [o.reshape(shape) for o in out]

    def small_update(name, g):
        grads[name] = g
        deltas[name], new_m[name], new_v[name] = _adamw_small(w[name], mom[name], var[name], g)

    stacked("ffn1_w_gate", r_gu, 0)
    stacked("ffn1_w_up", r_gu, depth)
    stacked("ffn2_w_gate", r_gu, 2 * depth)
    stacked("ffn2_w_up", r_gu, 3 * depth)
    stacked("ffn1_w_down", r_dn, 0)
    stacked("ffn2_w_down", r_dn, depth)
    stacked("mla_w_in", r_in, 0)
    stacked("mla_w_q_up", r_qup, 0)
    stacked("mla_w_kv_up", r_kvup, 0)
    stacked("mla_w_out", r_out, 0)
    stacked("pool_w", r_pool, 0)
    small_update("ffn1_norm", s_ffn1)
    small_update("mix_norm", s_mix)
    small_update("ffn2_norm", s_ffn2)
    small_update("pool_scale", s_pscale)
    small_update("mla_q_head_norm", s_qhn)
    small_update("mla_k_head_norm", s_khn)
    small_update("mla_q_norm", s_qn)
    small_update("mla_kv_norm", s_kvn)

    return (loss, grad_x, *[grads[n] for n in WEIGHT_NAMES], *[deltas[n] for n in WEIGHT_NAMES],
            *[new_m[n] for n in WEIGHT_NAMES], *[new_v[n] for n in WEIGHT_NAMES])
```

```python
import functools

import numpy as np

import jax
import jax.numpy as jnp
from jax import lax
from jax.experimental import pallas as pl
from jax.experimental.pallas import tpu as pltpu

F32 = jnp.float32
MXU_DTYPE = jnp.bfloat16
WIRE_DTYPE = jnp.bfloat16
MESH = pl.DeviceIdType.MESH
N_CHIPS = 4
N_DEVICES = 8
LANES = 128
VMEM_LIMIT_BYTES = 56 * 2**20
NORM_EPS = 1e-6
QK_NOPE, QK_ROPE, V_HEAD = 128, 64, 128
QK_HEAD = QK_NOPE + QK_ROPE
ROPE_THETA = 10000.0
POOL_WINDOWS = (2, 4, 8, 16)
POOL_HALO = 16
FFN_HALF = 0.5
ADAM_LR, ADAM_B1, ADAM_B2, ADAM_EPS, ADAM_WD, ADAM_STEP = 0.001, 0.9, 0.999, 1e-08, 0.01, 10
FFN_TILE = 512
WGRAD_TILE = 512
MLA_TILE = 256
POOL_TILE = 512
ATTN_TILE = 512
ROW_TILE = 1024


def _cast(v):
    return v.astype(MXU_DTYPE)


def _mm(a, b):
    return jnp.dot(a, b, preferred_element_type=F32)


def _mm_nt(a, b):
    return lax.dot_general(a, b, (((1,), (1,)), ((), ())), preferred_element_type=F32)


def _mm_tn(a, b):
    return lax.dot_general(a, b, (((0,), (0,)), ((), ())), preferred_element_type=F32)


def _rms_fwd(v, gain):
    r = lax.rsqrt(jnp.mean(v * v, axis=-1, keepdims=True) + NORM_EPS)
    return v * r * gain, r


def _rms_bwd(v, r, gain, dy):
    vr = v * r
    gy = dy * gain
    dv = r * (gy - vr * jnp.mean(gy * vr, axis=-1, keepdims=True))
    return dv, jnp.sum(dy * vr, axis=0, keepdims=True)


def _params(semantics=None):
    return pltpu.CompilerParams(dimension_semantics=semantics, vmem_limit_bytes=VMEM_LIMIT_BYTES)


def _tile(n, want):
    t = min(n, want)
    assert n % t == 0, (n, want)
    return t


def _full(shape):
    nd = len(shape)
    return pl.BlockSpec(shape, lambda *_: (0,) * nd)


def _sds(shape, dtype=F32):
    return jax.ShapeDtypeStruct(shape, dtype)


def _ffn_fwd(x, gain, w_gu, w_dn, i_gate, i_up, i_down):
    s, d = x.shape
    fs = w_gu.shape[-1]
    tm = _tile(s, FFN_TILE)

    def body(x_ref, g_ref, wg_ref, wu_ref, wd_ref, y_ref, hb_ref, gate_ref, up_ref, acc_sc):
        j = pl.program_id(1)

        @pl.when(j == 0)
        def _():
            h, _ = _rms_fwd(x_ref[...], g_ref[...])
            hb_ref[...] = _cast(h)
            acc_sc[...] = jnp.zeros_like(acc_sc)

        h = hb_ref[...]
        g = _mm(h, wg_ref[...])
        u = _mm(h, wu_ref[...])
        gate_ref[...] = _cast(g)
        up_ref[...] = _cast(u)
        act = (g * jax.nn.sigmoid(g)) * u
        acc_sc[...] += _mm(_cast(act), wd_ref[...])

        @pl.when(j == N_CHIPS - 1)
        def _():
            y_ref[...] = x_ref[...] + FFN_HALF * acc_sc[...]

    tok = pl.BlockSpec((tm, d), lambda i, j: (i, 0))
    chunk = pl.BlockSpec((None, tm, fs), lambda i, j: (j, i, 0))
    return pl.pallas_call(
        body, name="ffn_fwd", grid=(s // tm, N_CHIPS),
        in_specs=[
            tok, _full((1, d)),
            pl.BlockSpec((None, None, d, fs), lambda i, j: (j, i_gate, 0, 0)),
            pl.BlockSpec((None, None, d, fs), lambda i, j: (j, i_up, 0, 0)),
            pl.BlockSpec((None, None, fs, d), lambda i, j: (j, i_down, 0, 0)),
        ],
        out_specs=[tok, tok, chunk, chunk],
        out_shape=[_sds((s, d)), _sds((s, d), MXU_DTYPE), _sds((N_CHIPS, s, fs), MXU_DTYPE),
                   _sds((N_CHIPS, s, fs), MXU_DTYPE)],
        scratch_shapes=[pltpu.VMEM((tm, d), F32)],
        compiler_params=_params(("arbitrary", "arbitrary")),
    )(x, gain, w_gu, w_gu, w_dn)


def _ffn_bwd_dgrad(x, gain, dy, gate, up, w_gu, w_dn, i_gate, i_up, i_down):
    s, d = x.shape
    fs = w_gu.shape[-1]
    tm = _tile(s, FFN_TILE)
    n_tiles = s // tm

    def body(x_ref, g_ref, dy_ref, gate_ref, up_ref, wg_ref, wu_ref, wd_ref,
             dx_ref, dgain_ref, dyb_ref, dg_ref, du_ref, act_ref, dh_sc):
        i, j = pl.program_id(0), pl.program_id(1)

        @pl.when(j == 0)
        def _():
            dh_sc[...] = jnp.zeros_like(dh_sc)
            dyb_ref[...] = _cast(dy_ref[...])

        g = gate_ref[...].astype(F32)
        u = up_ref[...].astype(F32)
        sg = jax.nn.sigmoid(g)
        silu = g * sg
        dact = FFN_HALF * _mm_nt(dyb_ref[...], wd_ref[...])
        dgb = _cast(dact * u * (sg * (1.0 + g * (1.0 - sg))))
        dub = _cast(dact * silu)
        dg_ref[...] = dgb
        du_ref[...] = dub
        act_ref[...] = _cast(silu * u)
        dh_sc[...] += _mm_nt(dgb, wg_ref[...]) + _mm_nt(dub, wu_ref[...])

        @pl.when(j == N_CHIPS - 1)
        def _():
            _, r = _rms_fwd(x_ref[...], g_ref[...])
            dxn, dgn = _rms_bwd(x_ref[...], r, g_ref[...], dh_sc[...])
            dx_ref[...] = dy_ref[...] + dxn

            @pl.when(i == 0)
            def _():
                dgain_ref[...] = dgn

            @pl.when(i > 0)
            def _():
                dgain_ref[...] += dgn

    tok = pl.BlockSpec((tm, d), lambda i, j: (i, 0))
    chunk = pl.BlockSpec((None, tm, fs), lambda i, j: (j, i, 0))
    return pl.pallas_call(
        body, name="ffn_bwd_dgrad", grid=(n_tiles, N_CHIPS),
        in_specs=[
            tok, _full((1, d)), tok, chunk, chunk,
            pl.BlockSpec((None, None, d, fs), lambda i, j: (j, i_gate, 0, 0)),
            pl.BlockSpec((None, None, d, fs), lambda i, j: (j, i_up, 0, 0)),
            pl.BlockSpec((None, None, fs, d), lambda i, j: (j, i_down, 0, 0)),
        ],
        out_specs=[tok, _full((1, d)), tok, chunk, chunk, chunk],
        out_shape=[_sds((s, d)), _sds((1, d)), _sds((s, d), MXU_DTYPE),
                   _sds((N_CHIPS, s, fs), MXU_DTYPE), _sds((N_CHIPS, s, fs), MXU_DTYPE),
                   _sds((N_CHIPS, s, fs), MXU_DTYPE)],
        scratch_shapes=[pltpu.VMEM((tm, d), F32)],
        compiler_params=_params(("arbitrary", "arbitrary")),
    )(x, gain, dy, gate, up, w_gu, w_gu, w_dn)


def _ffn_wgrad(hb, dyb, dg, du, act):
    s, d = hb.shape
    fs = dg.shape[-1]
    tk = _tile(s, WGRAD_TILE)
    n_k = s // tk

    def body(h_ref, dy_ref, dg_ref, du_ref, act_ref, wg_ref, wu_ref, wd_ref):
        k = pl.program_id(1)

        @pl.when(k == 0)
        def _():
            wg_ref[...] = jnp.zeros_like(wg_ref)
            wu_ref[...] = jnp.zeros_like(wu_ref)
            wd_ref[...] = jnp.zeros_like(wd_ref)

        h = h_ref[...]
        wg_ref[...] += _mm_tn(h, dg_ref[...])
        wu_ref[...] += _mm_tn(h, du_ref[...])
        wd_ref[...] += FFN_HALF * _mm_tn(act_ref[...], dy_ref[...])

    tok = pl.BlockSpec((tk, d), lambda j, k: (k, 0))
    chunk = pl.BlockSpec((None, tk, fs), lambda j, k: (j, k, 0))
    return pl.pallas_call(
        body, name="ffn_wgrad", grid=(N_CHIPS, n_k),
        in_specs=[tok, tok, chunk, chunk, chunk],
        out_specs=[pl.BlockSpec((None, d, fs), lambda j, k: (j, 0, 0)),
                   pl.BlockSpec((None, d, fs), lambda j, k: (j, 0, 0)),
                   pl.BlockSpec((None, fs, d), lambda j, k: (j, 0, 0))],
        out_shape=[_sds((N_CHIPS, d, fs)), _sds((N_CHIPS, d, fs)), _sds((N_CHIPS, fs, d))],
        compiler_params=_params(("arbitrary", "arbitrary")),
    )(hb, dyb, dg, du, act)


def _inv_count(first_row, n_rows, window):
    t = first_row + lax.broadcasted_iota(jnp.int32, (n_rows, 1), 0)
    return 1.0 / jnp.minimum(t + 1, window).astype(F32)


def _trailing_sum(v, window):
    k = 1
    while k < window:
        v = v + pltpu.roll(v, k, 0)
        k *= 2
    return v


def _leading_sum(v, window):
    n = v.shape[0]
    k = 1
    while k < window:
        v = v + pltpu.roll(v, n - k, 0)
        k *= 2
    return v


def _pool_normed_rows(x_ref, prev_ref, g_ref, i):
    h, r = _rms_fwd(x_ref[...], g_ref[...])
    hp, _ = _rms_fwd(prev_ref[...], g_ref[...])
    hp = jnp.where(i > 0, hp, 0.0)
    return jnp.concatenate([hp, h], axis=0), r


def _pooled_group(he, g, pg, first_row, tm):
    ue = he[:, g * pg:(g + 1) * pg]
    win = _trailing_sum(ue, POOL_WINDOWS[g])[POOL_HALO:]
    return win * _inv_count(first_row, tm, POOL_WINDOWS[g]) - ue[POOL_HALO:]


def _pool_specs(s, d, tm):
    per = tm // POOL_HALO
    last = s // POOL_HALO - 1
    tok = pl.BlockSpec((tm, d), lambda i: (i, 0))
    prev = pl.BlockSpec((POOL_HALO, d), lambda i: (jnp.maximum(i * per - 1, 0), 0))
    nxt = pl.BlockSpec((POOL_HALO, d), lambda i: (jnp.minimum((i + 1) * per, last), 0))
    return tok, prev, nxt


def _pool_fwd(x, gain, w, scale):
    s, d = x.shape
    n_g, pg = w.shape[0], w.shape[-1]
    tm = _tile(s, POOL_TILE)
    tok, prev, _ = _pool_specs(s, d, tm)

    def body(x_ref, prev_ref, g_ref, w_ref, sc_ref, y_ref):
        i = pl.program_id(0)
        he, _ = _pool_normed_rows(x_ref, prev_ref, g_ref, i)
        z = [_mm(_cast(_pooled_group(he, g, pg, i * tm, tm)), w_ref[g]) for g in range(n_g)]
        y_ref[...] = x_ref[...] + jnp.concatenate(z, axis=-1) * sc_ref[...]

    return pl.pallas_call(
        body, name="pool_fwd", grid=(s // tm,),
        in_specs=[tok, prev, _full((1, d)), _full(w.shape), _full((1, d))],
        out_specs=tok, out_shape=_sds((s, d)),
        compiler_params=_params(("arbitrary",)),
    )(x, x, gain, w, scale)


def _pool_bwd(x, gain, w, scale, dy):
    s, d = x.shape
    n_g, pg = w.shape[0], w.shape[-1]
    tm = _tile(s, POOL_TILE)
    n_tiles = s // tm
    tok, prev, nxt = _pool_specs(s, d, tm)

    def body(x_ref, prev_ref, dy_ref, next_ref, g_ref, w_ref, sc_ref, dx_ref, dgain_ref, dw_ref, dsc_ref):
        i = pl.program_id(0)

        @pl.when(i == 0)
        def _():
            dgain_ref[...] = jnp.zeros_like(dgain_ref)
            dw_ref[...] = jnp.zeros_like(dw_ref)
            dsc_ref[...] = jnp.zeros_like(dsc_ref)

        he, r = _pool_normed_rows(x_ref, prev_ref, g_ref, i)
        dy = dy_ref[...]
        dyn = jnp.where(i < n_tiles - 1, next_ref[...], 0.0)
        dze = jnp.concatenate([dy, dyn], axis=0) * sc_ref[...]
        dh, dsc = [], []
        for g in range(n_g):
            cols = slice(g * pg, (g + 1) * pg)
            pooled = _cast(_pooled_group(he, g, pg, i * tm, tm))
            dsc.append(jnp.sum(dy[:, cols] * _mm(pooled, w_ref[g]), axis=0, keepdims=True))
            dzb = _cast(dze[:, cols])
            dw_ref[g] += _mm_tn(pooled, dzb[:tm])
            dpool = _mm_nt(dzb, w_ref[g])
            spread = _leading_sum(dpool * _inv_count(i * tm, tm + POOL_HALO, POOL_WINDOWS[g]), POOL_WINDOWS[g])
            dh.append(spread[:tm] - dpool[:tm])
        dsc_ref[...] += jnp.concatenate(dsc, axis=-1)
        dxn, dgn = _rms_bwd(x_ref[...], r, g_ref[...], jnp.concatenate(dh, axis=-1))
        dgain_ref[...] += dgn
        dx_ref[...] = dy + dxn

    return pl.pallas_call(
        body, name="pool_bwd", grid=(n_tiles,),
        in_specs=[tok, prev, tok, nxt, _full((1, d)), _full(w.shape), _full((1, d))],
        out_specs=[tok, _full((1, d)), _full(w.shape), _full((1, d))],
        out_shape=[_sds((s, d)), _sds((1, d)), _sds(w.shape), _sds((1, d))],
        compiler_params=_params(("arbitrary",)),
    )(x, x, dy, dy, gain, w, scale)


def _rope_tables(pos_col, inv_freq):
    s = pos_col.shape[0]
    tm = _tile(s, ROW_TILE)
    half = QK_ROPE // 2

    def body(p_ref, f_ref, c_ref, s_ref):
        ang = p_ref[...].astype(F32) * f_ref[...]
        cos, sin = jnp.cos(ang), jnp.sin(ang)
        c_ref[...] = jnp.concatenate([jnp.ones((tm, QK_NOPE), F32), cos, cos], axis=-1)
        s_ref[...] = jnp.concatenate([jnp.zeros((tm, QK_NOPE), F32), -sin, sin], axis=-1)

    tab = pl.BlockSpec((tm, QK_HEAD), lambda i: (i, 0))
    return pl.pallas_call(
        body, name="rope_tables", grid=(s // tm,),
        in_specs=[pl.BlockSpec((tm, 1), lambda i: (i, 0)), _full((1, half))],
        out_specs=[tab, tab], out_shape=[_sds((s, QK_HEAD)), _sds((s, QK_HEAD))],
        compiler_params=_params(("arbitrary",)),
    )(pos_col, inv_freq)


def _swap_rope_halves(v):
    half = QK_ROPE // 2
    return jnp.concatenate([v[:, :QK_NOPE], v[:, QK_NOPE + half:], v[:, QK_NOPE:QK_NOPE + half]], axis=-1)


def _rope(v, cos, sin):
    return v * cos + _swap_rope_halves(v) * sin


def _rope_transposed(dv, cos, sin):
    return dv * cos + _swap_rope_halves(dv * sin)


def _mla_qkv_fwd(x, gain, w_in, q_norm, kv_norm, w_q, w_kn, w_v, q_head_norm, k_head_norm, cos, sin):
    s, d = x.shape
    n_h, ql = w_q.shape[0], w_q.shape[1]
    kvl, lat_w = w_kn.shape[1], w_in.shape[1]
    tm = _tile(s, MLA_TILE)

    def body(x_ref, g_ref, win_ref, qn_ref, kvn_ref, wq_ref, wkn_ref, wv_ref, qhn_ref, khn_ref, c_ref, s_ref,
             lat_ref, q_ref, k_ref, v_ref):
        h, _ = _rms_fwd(x_ref[...], g_ref[...])
        lat = _mm(_cast(h), win_ref[...])
        lat_ref[...] = lat
        cqn, _ = _rms_fwd(lat[:, :ql], qn_ref[...])
        ckvn, _ = _rms_fwd(lat[:, ql:ql + kvl], kvn_ref[...])
        kpe = lat[:, ql + kvl:]
        cqb, ckb = _cast(cqn), _cast(ckvn)
        cos_t, sin_t = c_ref[...], s_ref[...]
        v_ref[...] = _cast(_mm(ckb, wv_ref[...]))
        for hh in range(n_h):
            qn, _ = _rms_fwd(_mm(cqb, wq_ref[hh]), qhn_ref[...])
            q_ref[hh] = _cast(_rope(qn, cos_t, sin_t))
            kn, _ = _rms_fwd(jnp.concatenate([_mm(ckb, wkn_ref[hh]), kpe], axis=-1), khn_ref[...])
            k_ref[hh] = _cast(_rope(kn, cos_t, sin_t))

    tok = lambda w: pl.BlockSpec((tm, w), lambda i: (i, 0))
    heads = pl.BlockSpec((n_h, tm, QK_HEAD), lambda i: (0, i, 0))
    return pl.pallas_call(
        body, name="mla_qkv_fwd", grid=(s // tm,),
        in_specs=[tok(d), _full((1, d)), _full(w_in.shape), _full((1, ql)), _full((1, kvl)), _full(w_q.shape),
                  _full(w_kn.shape), _full(w_v.shape), _full((1, QK_HEAD)), _full((1, QK_HEAD)),
                  tok(QK_HEAD), tok(QK_HEAD)],
        out_specs=[tok(lat_w), heads, heads, tok(n_h * V_HEAD)],
        out_shape=[_sds((s, lat_w)), _sds((n_h, s, QK_HEAD), MXU_DTYPE), _sds((n_h, s, QK_HEAD), MXU_DTYPE),
                   _sds((s, n_h * V_HEAD), MXU_DTYPE)],
        compiler_params=_params(("arbitrary",)),
    )(x, gain, w_in, q_norm, kv_norm, w_q, w_kn, w_v, q_head_norm, k_head_norm, cos, sin)


def _mla_qkv_bwd(x, lat, dy, dq, dk, dv, gain, w_in, q_norm, kv_norm, w_q, w_kn, w_v, q_head_norm, k_head_norm,
                 cos, sin):
    s, d = x.shape
    n_h, ql = w_q.shape[0], w_q.shape[1]
    kvl, lat_w = w_kn.shape[1], w_in.shape[1]
    tm = _tile(s, MLA_TILE)

    def body(x_ref, lat_ref, dy_ref, dq_ref, dk_ref, dv_ref, g_ref, win_ref, qn_ref, kvn_ref, wq_ref, wkn_ref,
             wv_ref, qhn_ref, khn_ref, c_ref, s_ref,
             dx_ref, dg_ref, dwin_ref, dqn_ref, dkvn_ref, dwq_ref, dwkn_ref, dwv_ref, dqhn_ref, dkhn_ref):
        @pl.when(pl.program_id(0) == 0)
        def _():
            for ref in (dg_ref, dwin_ref, dqn_ref, dkvn_ref, dwq_ref, dwkn_ref, dwv_ref, dqhn_ref, dkhn_ref):
                ref[...] = jnp.zeros_like(ref)

        x_t = x_ref[...]
        h, r = _rms_fwd(x_t, g_ref[...])
        hb = _cast(h)
        lat = lat_ref[...]
        cq, ckv, kpe = lat[:, :ql], lat[:, ql:ql + kvl], lat[:, ql + kvl:]
        cqn, rq = _rms_fwd(cq, qn_ref[...])
        ckvn, rkv = _rms_fwd(ckv, kvn_ref[...])
        cqb, ckb = _cast(cqn), _cast(ckvn)
        cos_t, sin_t = c_ref[...], s_ref[...]

        dvb = _cast(dv_ref[...])
        dwv_ref[...] += _mm_tn(ckb, dvb)
        dckvn = _mm_nt(dvb, wv_ref[...])
        dcqn = jnp.zeros((tm, ql), F32)
        dkpe = jnp.zeros((tm, QK_ROPE), F32)
        dqhn = jnp.zeros((1, QK_HEAD), F32)
        dkhn = jnp.zeros((1, QK_HEAD), F32)
        for hh in range(n_h):
            qp = _mm(cqb, wq_ref[hh])
            _, rqp = _rms_fwd(qp, qhn_ref[...])
            dqp, dgq = _rms_bwd(qp, rqp, qhn_ref[...], _rope_transposed(dq_ref[hh], cos_t, sin_t))
            dqhn += dgq
            dqpb = _cast(dqp)
            dwq_ref[hh] += _mm_tn(cqb, dqpb)
            dcqn += _mm_nt(dqpb, wq_ref[hh])

            kp = jnp.concatenate([_mm(ckb, wkn_ref[hh]), kpe], axis=-1)
            _, rkp = _rms_fwd(kp, khn_ref[...])
            dkp, dgk = _rms_bwd(kp, rkp, khn_ref[...], _rope_transposed(dk_ref[hh], cos_t, sin_t))
            dkhn += dgk
            dknb = _cast(dkp[:, :QK_NOPE])
            dkpe += dkp[:, QK_NOPE:]
            dwkn_ref[hh] += _mm_tn(ckb, dknb)
            dckvn += _mm_nt(dknb, wkn_ref[hh])
        dqhn_ref[...] += dqhn
        dkhn_ref[...] += dkhn

        dcq, dgn = _rms_bwd(cq, rq, qn_ref[...], dcqn)
        dqn_ref[...] += dgn
        dckv, dgn = _rms_bwd(ckv, rkv, kvn_ref[...], dckvn)
        dkvn_ref[...] += dgn
        dlb = _cast(jnp.concatenate([dcq, dckv, dkpe], axis=-1))
        dwin_ref[...] += _mm_tn(hb, dlb)
        dxn, dgn = _rms_bwd(x_t, r, g_ref[...], _mm_nt(dlb, win_ref[...]))
        dg_ref[...] += dgn
        dx_ref[...] = dy_ref[...] + dxn

    tok = lambda w: pl.BlockSpec((tm, w), lambda i: (i, 0))
    heads = pl.BlockSpec((n_h, tm, QK_HEAD), lambda i: (0, i, 0))
    return pl.pallas_call(
        body, name="mla_qkv_bwd", grid=(s // tm,),
        in_specs=[tok(d), tok(lat_w), tok(d), heads, heads, tok(n_h * V_HEAD), _full((1, d)), _full(w_in.shape),
                  _full((1, ql)), _full((1, kvl)), _full(w_q.shape), _full(w_kn.shape), _full(w_v.shape),
                  _full((1, QK_HEAD)), _full((1, QK_HEAD)), tok(QK_HEAD), tok(QK_HEAD)],
        out_specs=[tok(d), _full((1, d)), _full(w_in.shape), _full((1, ql)), _full((1, kvl)), _full(w_q.shape),
                   _full(w_kn.shape), _full(w_v.shape), _full((1, QK_HEAD)), _full((1, QK_HEAD))],
        out_shape=[_sds((s, d)), _sds((1, d)), _sds(w_in.shape), _sds((1, ql)), _sds((1, kvl)), _sds(w_q.shape),
                   _sds(w_kn.shape), _sds(w_v.shape), _sds((1, QK_HEAD)), _sds((1, QK_HEAD))],
        compiler_params=_params(("arbitrary",)),
    )(x, lat, dy, dq, dk, dv, gain, w_in, q_norm, kv_norm, w_q, w_kn, w_v, q_head_norm, k_head_norm, cos, sin)


def _scores_t(k_t, q_t):
    return _mm_nt(k_t, q_t) * (QK_HEAD ** -0.5)


def _mask_above_diagonal(z, t):
    key = lax.broadcasted_iota(jnp.int32, (t, t), 0)
    query = lax.broadcasted_iota(jnp.int32, (t, t), 1)
    return jnp.where(key <= query, z, -jnp.inf)


def _flash_fwd(q, k, vt):
    n_h, s, _ = q.shape
    t = _tile(s, ATTN_TILE)
    n = s // t

    def body(q_ref, k_ref, vt_ref, ot_ref, lse_ref, m_sc, l_sc, acc_sc, z_sc):
        i = pl.program_id(1)
        m_sc[...] = jnp.full_like(m_sc, -jnp.inf)
        l_sc[...] = jnp.zeros_like(l_sc)
        acc_sc[...] = jnp.zeros_like(acc_sc)
        q_t = q_ref[...]

        def fetch(j, slot):
            z_sc[slot] = _scores_t(k_ref[pl.ds(pl.multiple_of(j * t, t), t), :], q_t)

        def stage(j, slot, masked, prefetch=True):
            if prefetch:
                fetch(j + 1, 1 - slot)
            z = _mask_above_diagonal(z_sc[slot], t) if masked else z_sc[slot]
            m_old = m_sc[...]
            m_new = jnp.maximum(m_old, jnp.max(z, axis=0, keepdims=True))
            alpha = jnp.exp(m_old - m_new)
            pr = jnp.exp(z - m_new)
            l_sc[...] = alpha * l_sc[...] + jnp.sum(pr, axis=0, keepdims=True)
            acc_sc[...] = alpha * acc_sc[...] + _mm(vt_ref[j], _cast(pr))
            m_sc[...] = m_new

        def pair_below_diagonal(pair, carry):
            stage(2 * pair, 0, False)
            stage(2 * pair + 1, 1, False)
            return carry

        fetch(0, 0)
        lax.fori_loop(0, i >> 1, pair_below_diagonal, 0)

        @pl.when((i & 1) == 1)
        def _():
            stage(i - 1, 0, False)
            stage(i, 1, True, prefetch=False)

        @pl.when((i & 1) == 0)
        def _():
            stage(i, 0, True, prefetch=False)

        ot_ref[...] = acc_sc[...] / l_sc[...]
        lse_ref[...] = m_sc[...] + jnp.log(l_sc[...])

    whole_head = dict(pipeline_mode=pl.Buffered(1))
    return pl.pallas_call(
        body, name="flash_fwd", grid=(n_h, n),
        in_specs=[pl.BlockSpec((None, t, QK_HEAD), lambda h, i: (h, i, 0)),
                  pl.BlockSpec((None, s, QK_HEAD), lambda h, i: (h, 0, 0), **whole_head),
                  pl.BlockSpec((None, n, V_HEAD, t), lambda h, i: (h, 0, 0, 0), **whole_head)],
        out_specs=[pl.BlockSpec((V_HEAD, t), lambda h, i: (h, i)),
                   pl.BlockSpec((None, 1, t), lambda h, i: (h, 0, i))],
        out_shape=[_sds((n_h * V_HEAD, s)), _sds((n_h, 1, s))],
        scratch_shapes=[pltpu.VMEM((1, t), F32), pltpu.VMEM((1, t), F32), pltpu.VMEM((V_HEAD, t), F32),
                        pltpu.VMEM((2, t, t), F32)],
        compiler_params=_params(("arbitrary", "arbitrary")),
    )(q, k, vt)


def _flash_bwd(q, k, kt, v, do, lse, delta):
    n_h, s, _ = q.shape
    t = _tile(s, ATTN_TILE)
    n = s // t

    def body(q_ref, do_ref, lse_ref, dl_ref, k_ref, kt_ref, v_ref, dqt_ref, dk_ref, dv_ref, dk_sc, dv_sc, z_sc, dp_sc):
        j = pl.program_id(1)

        @pl.when(j == 0)
        def _():
            dqt_ref[...] = jnp.zeros_like(dqt_ref)

        dk_sc[...] = jnp.zeros_like(dk_sc)
        dv_sc[...] = jnp.zeros_like(dv_sc)
        k_t, kt_t, v_t = k_ref[...], kt_ref[...], v_ref[...]

        def rows(i):
            return pl.ds(pl.multiple_of(i * t, t), t)

        def fetch(i, slot):
            i = jnp.minimum(i, n - 1)
            z_sc[slot] = _scores_t(k_t, q_ref[rows(i), :])
            dp_sc[slot] = _mm_nt(v_t, do_ref[rows(i), :])

        def stage(i, slot, masked, prefetch=True):
            if prefetch:
                fetch(i + 1, 1 - slot)
            z = _mask_above_diagonal(z_sc[slot], t) if masked else z_sc[slot]
            pr = jnp.exp(z - lse_ref[i])
            dsb = _cast(pr * (dp_sc[slot] - dl_ref[i]) * (QK_HEAD ** -0.5))
            dv_sc[...] += _mm(_cast(pr), do_ref[rows(i), :])
            dk_sc[...] += _mm(dsb, q_ref[rows(i), :])
            dqt_ref[i] += _mm(kt_t, dsb)

        def pair_below_diagonal(pair, carry):
            stage(j + 1 + 2 * pair, 1, False)
            stage(j + 2 + 2 * pair, 0, False)
            return carry

        below = n - 1 - j
        fetch(j, 0)
        stage(j, 0, True)
        lax.fori_loop(0, below >> 1, pair_below_diagonal, 0)

        @pl.when((below & 1) == 1)
        def _():
            stage(n - 1, 1, False, prefetch=False)

        dk_ref[...] = dk_sc[...]
        dv_ref[...] = dv_sc[...]

    whole_head = dict(pipeline_mode=pl.Buffered(1))
    stat = pl.BlockSpec((None, n, 1, t), lambda h, j: (h, 0, 0, 0))
    return pl.pallas_call(
        body, name="flash_bwd", grid=(n_h, n),
        in_specs=[pl.BlockSpec((None, s, QK_HEAD), lambda h, j: (h, 0, 0), **whole_head),
                  pl.BlockSpec((s, V_HEAD), lambda h, j: (0, h), **whole_head),
                  stat, stat,
                  pl.BlockSpec((None, t, QK_HEAD), lambda h, j: (h, j, 0)),
                  pl.BlockSpec((None, QK_HEAD, t), lambda h, j: (h, 0, j)),
                  pl.BlockSpec((t, V_HEAD), lambda h, j: (j, h))],
        out_specs=[pl.BlockSpec((None, n, QK_HEAD, t), lambda h, j: (h, 0, 0, 0)),
                   pl.BlockSpec((None, t, QK_HEAD), lambda h, j: (h, j, 0)),
                   pl.BlockSpec((t, V_HEAD), lambda h, j: (j, h))],
        out_shape=[_sds((n_h, n, QK_HEAD, t)), _sds((n_h, s, QK_HEAD)), _sds((s, n_h * V_HEAD))],
        scratch_shapes=[pltpu.VMEM((t, QK_HEAD), F32), pltpu.VMEM((t, V_HEAD), F32),
                        pltpu.VMEM((2, t, t), F32), pltpu.VMEM((2, t, t), F32)],
        compiler_params=_params(("arbitrary", "arbitrary")),
    )(q, do, lse, delta, k, kt, v)


def _mla_out_fwd(x, ot, w_out):
    s, d = x.shape
    hv = ot.shape[0]
    tm = _tile(s, FFN_TILE)

    def body(x_ref, ot_ref, w_ref, y_ref):
        y_ref[...] = x_ref[...] + _mm_tn(_cast(ot_ref[...]), w_ref[...])

    tok = pl.BlockSpec((tm, d), lambda i: (i, 0))
    return pl.pallas_call(
        body, name="mla_out_fwd", grid=(s // tm,),
        in_specs=[tok, pl.BlockSpec((hv, tm), lambda i: (0, i)), _full(w_out.shape)],
        out_specs=tok, out_shape=_sds((s, d)),
        compiler_params=_params(("arbitrary",)),
    )(x, ot, w_out)


def _mla_out_bwd(dy, ot, w_out):
    s, d = dy.shape
    hv = ot.shape[0]
    n_h = hv // V_HEAD
    tm = _tile(s, FFN_TILE)

    def body(dy_ref, ot_ref, w_ref, do_ref, dl_ref, dw_ref):
        @pl.when(pl.program_id(0) == 0)
        def _():
            dw_ref[...] = jnp.zeros_like(dw_ref)

        dyb = _cast(dy_ref[...])
        o_t = ot_ref[...]
        do_ref[...] = _cast(_mm_nt(dyb, w_ref[...]))
        prod = _mm_nt(w_ref[...], dyb) * o_t
        for hh in range(n_h):
            dl_ref[hh] = jnp.sum(prod[hh * V_HEAD:(hh + 1) * V_HEAD], axis=0, keepdims=True)
        dw_ref[...] += _mm(_cast(o_t), dyb)

    return pl.pallas_call(
        body, name="mla_out_bwd", grid=(s // tm,),
        in_specs=[pl.BlockSpec((tm, d), lambda i: (i, 0)), pl.BlockSpec((hv, tm), lambda i: (0, i)),
                  _full(w_out.shape)],
        out_specs=[pl.BlockSpec((tm, hv), lambda i: (i, 0)), pl.BlockSpec((n_h, 1, tm), lambda i: (0, 0, i)),
                   _full(w_out.shape)],
        out_shape=[_sds((s, hv), MXU_DTYPE), _sds((n_h, 1, s)), _sds(w_out.shape)],
        compiler_params=_params(("arbitrary",)),
    )(dy, ot, w_out)


def _loss_and_grad(y, target):
    s, d = y.shape
    tm = _tile(s, ROW_TILE)

    def body(y_ref, t_ref, loss_ref, dy_ref):
        @pl.when(pl.program_id(0) == 0)
        def _():
            loss_ref[...] = jnp.zeros_like(loss_ref)

        err = y_ref[...] - t_ref[...]
        dy_ref[...] = err * (1.0 / d)
        loss_ref[...] += 0.5 * jnp.sum(jnp.mean(err * err, axis=-1, keepdims=True), axis=0, keepdims=True)

    tok = pl.BlockSpec((tm, d), lambda i: (i, 0))
    return pl.pallas_call(
        body, name="loss_and_grad", grid=(s // tm,),
        in_specs=[tok, tok], out_specs=[_full((1, 1)), tok],
        out_shape=[_sds((1, 1)), _sds((s, d))],
        compiler_params=_params(("arbitrary",)),
    )(y, target)


def _mesh_position():
    return lax.axis_index("x"), lax.axis_index("y"), lax.axis_index("c")


def _other_chips(x, y):
    return [(1 - x, y), (x, 1 - y), (1 - x, 1 - y)]


ANY = pl.BlockSpec(memory_space=pl.ANY)


def _gather_over_chips(arrs):
    n = len(arrs)
    halves = [a.shape[0] // 2 for a in arrs]
    assert all(a.shape[0] % 2 == 0 for a in arrs)
    own = 2 * (N_CHIPS - 1)

    def body(*refs):
        srcs, outs = refs[:n], refs[n:2 * n]
        send_sems, recv_sems = refs[2 * n:]
        x, y, c = _mesh_position()
        me, sibling = (x, y, c), (x, y, 1 - c)
        chips = _other_chips(x, y)
        my_chip = 2 * x + y

        def rows(t, chip, half):
            return outs[t].at[chip, pl.ds(half * halves[t], halves[t])]

        def copy(t, k, src, dst, to):
            return pltpu.make_async_remote_copy(src_ref=src, dst_ref=dst, send_sem=send_sems.at[t, k],
                                                recv_sem=recv_sems.at[t, k], device_id=to, device_id_type=MESH)

        started = []
        for t in range(n):
            for k, (px, py) in enumerate(chips):
                cp = copy(t, k, srcs[t].at[pl.ds(c * halves[t], halves[t])], rows(t, my_chip, c), (px, py, c))
                cp.start()
                started.append(cp)
            cp = copy(t, own, srcs[t], outs[t].at[my_chip], sibling)
            cp.start()
            started.append(cp)
        for t in range(n):
            for k, (px, py) in enumerate(chips):
                landed = rows(t, 2 * px + py, c)
                copy(t, k, landed, landed, me).wait_recv()
                cp = copy(t, N_CHIPS - 1 + k, landed, landed, sibling)
                cp.start()
                started.append(cp)
        for t in range(n):
            for k, (px, py) in enumerate(chips):
                passed = rows(t, 2 * px + py, 1 - c)
                copy(t, N_CHIPS - 1 + k, passed, passed, me).wait_recv()
            copy(t, own, srcs[t], outs[t].at[my_chip], me).wait_recv()
        for cp in started:
            cp.wait_send()

    return pl.pallas_call(
        body, name="gather_over_chips",
        in_specs=[ANY] * n, out_specs=[ANY] * n,
        out_shape=[_sds((N_CHIPS,) + a.shape, a.dtype) for a in arrs],
        scratch_shapes=[pltpu.SemaphoreType.DMA((n, own + 1)), pltpu.SemaphoreType.DMA((n, own + 1))],
    )(*arrs)


def _send_other_half_to_sibling(grads):
    n = len(grads)
    halves = [g.shape[1] // 2 for g in grads]

    def body(*refs):
        srcs, outs = refs[:n], refs[n:2 * n]
        send_sems, recv_sems = refs[2 * n:]
        x, y, c = _mesh_position()
        copies = []
        for t in range(n):
            cp = pltpu.make_async_remote_copy(
                src_ref=srcs[t].at[pl.ds(0, N_CHIPS), pl.ds((1 - c) * halves[t], halves[t])], dst_ref=outs[t],
                send_sem=send_sems.at[t], recv_sem=recv_sems.at[t], device_id=(x, y, 1 - c), device_id_type=MESH)
            cp.start()
            copies.append(cp)
        for cp in copies:
            cp.wait_recv()
        for cp in copies:
            cp.wait_send()

    return pl.pallas_call(
        body, name="send_other_half_to_sibling",
        in_specs=[ANY] * n, out_specs=[ANY] * n,
        out_shape=[_sds((N_CHIPS, h) + g.shape[2:]) for g, h in zip(grads, halves)],
        scratch_shapes=[pltpu.SemaphoreType.DMA((n,)), pltpu.SemaphoreType.DMA((n,))],
    )(*grads)


def _send_blocks_to_chips(parts):
    n = len(parts)

    def body(*refs):
        srcs, outs = refs[:n], refs[n:2 * n]
        send_sems, recv_sems = refs[2 * n:]
        x, y, c = _mesh_position()
        copies = []
        for t in range(n):
            for k, (px, py) in enumerate(_other_chips(x, y)):
                cp = pltpu.make_async_remote_copy(
                    src_ref=srcs[t].at[2 * px + py], dst_ref=outs[t].at[k], send_sem=send_sems.at[t, k],
                    recv_sem=recv_sems.at[t, k], device_id=(px, py, c), device_id_type=MESH)
                cp.start()
                copies.append(cp)
        for cp in copies:
            cp.wait_recv()
        for cp in copies:
            cp.wait_send()

    return pl.pallas_call(
        body, name="send_blocks_to_chips",
        in_specs=[ANY] * n, out_specs=[ANY] * n,
        out_shape=[_sds((N_CHIPS - 1,) + p.shape[1:], p.dtype) for p in parts],
        scratch_shapes=[pltpu.SemaphoreType.DMA((n, N_CHIPS - 1)), pltpu.SemaphoreType.DMA((n, N_CHIPS - 1))],
    )(*parts)


def _join_halves_with_sibling(sums):
    n = len(sums)

    def body(*refs):
        srcs, outs = refs[:n], refs[n:2 * n]
        send_sems, recv_sems = refs[2 * n:]
        x, y, c = _mesh_position()
        copies = []
        for t in range(n):
            h = srcs[t].shape[0] // 2
            mine = pl.ds(c * h, h)
            cp = pltpu.make_async_remote_copy(
                src_ref=srcs[t].at[mine], dst_ref=outs[t].at[mine], send_sem=send_sems.at[t],
                recv_sem=recv_sems.at[t], device_id=(x, y, 1 - c), device_id_type=MESH)
            cp.start()
            copies.append(cp)
        for t in range(n):
            h = srcs[t].shape[0] // 2
            theirs = pl.ds((1 - c) * h, h)
            pltpu.make_async_remote_copy(
                src_ref=srcs[t].at[theirs], dst_ref=outs[t].at[theirs], send_sem=send_sems.at[t],
                recv_sem=recv_sems.at[t], device_id=(x, y, 1 - c), device_id_type=MESH).wait_recv()
        for cp in copies:
            cp.wait_send()

    return pl.pallas_call(
        body, name="join_halves_with_sibling",
        in_specs=[ANY] * n, out_specs=[ANY] * n,
        out_shape=[_sds(a.shape) for a in sums],
        input_output_aliases={t: t for t in range(n)},
        scratch_shapes=[pltpu.SemaphoreType.DMA((n,)), pltpu.SemaphoreType.DMA((n,))],
    )(*sums)


def _gather_over_devices(rows):
    r = rows.shape[0]

    def body(in_ref, out_ref, send_sems, recv_sems, local_sem):
        x, y, c = _mesh_position()
        mine = pltpu.make_async_copy(in_ref, out_ref.at[4 * x + 2 * y + c], local_sem)
        mine.start()
        copies = []
        for mask in range(1, N_DEVICES):
            fx, fy, fc = (mask >> 2) & 1, (mask >> 1) & 1, mask & 1
            px, py, pc = (1 - x if fx else x), (1 - y if fy else y), (1 - c if fc else c)
            send = pltpu.make_async_remote_copy(
                src_ref=in_ref, dst_ref=out_ref.at[4 * x + 2 * y + c], send_sem=send_sems.at[mask - 1],
                recv_sem=recv_sems.at[mask - 1], device_id=(px, py, pc), device_id_type=MESH)
            send.start()
            recv = pltpu.make_async_remote_copy(
                src_ref=in_ref, dst_ref=out_ref.at[4 * px + 2 * py + pc], send_sem=send_sems.at[mask - 1],
                recv_sem=recv_sems.at[mask - 1], device_id=(px, py, pc), device_id_type=MESH)
            copies.append((send, recv))
        for _, recv in copies:
            recv.wait_recv()
        for send, _ in copies:
            send.wait_send()
        mine.wait()

    vm = pl.BlockSpec(memory_space=pltpu.VMEM)
    return pl.pallas_call(
        body, name="gather_over_devices", in_specs=[vm], out_specs=vm,
        out_shape=_sds((N_DEVICES, r, LANES)),
        scratch_shapes=[pltpu.SemaphoreType.DMA((N_DEVICES - 1,)), pltpu.SemaphoreType.DMA((N_DEVICES - 1,)),
                        pltpu.SemaphoreType.DMA],
    )(rows)


def _add_sibling_half(grad, received, chip, core):
    _, l, r, c = grad.shape
    half = l // 2

    def body(chip_ref, core_ref, g_ref, r_ref, wire_ref, own_ref):
        total = g_ref[...] + r_ref[...]
        wire_ref[...] = total.astype(WIRE_DTYPE)

        @pl.when(pl.program_id(1) == chip_ref[0])
        def _():
            own_ref[...] = total

    blk = lambda f: pl.BlockSpec((None, None, r, c), f)
    grid_spec = pltpu.PrefetchScalarGridSpec(
        num_scalar_prefetch=2, grid=(half, N_CHIPS),
        in_specs=[blk(lambda i, j, chip, core: (j, core[0] * half + i, 0, 0)),
                  blk(lambda i, j, chip, core: (j, i, 0, 0))],
        out_specs=[blk(lambda i, j, chip, core: (j, i, 0, 0)),
                   pl.BlockSpec((None, r, c), lambda i, j, chip, core: (i, 0, 0))])
    return pl.pallas_call(
        body, name="add_sibling_half", grid_spec=grid_spec,
        out_shape=[_sds((N_CHIPS, half, r, c), WIRE_DTYPE), _sds((half, r, c))],
        compiler_params=_params(("arbitrary", "arbitrary")),
    )(chip, core, grad, received)


def _add_chip_blocks(own, received, core):
    half, r, c = own.shape

    def body(core_ref, p_ref, r0_ref, r1_ref, r2_ref, o_ref):
        o_ref[...] = ((p_ref[...] + r0_ref[...].astype(F32)) + r1_ref[...].astype(F32)) + r2_ref[...].astype(F32)

    grid_spec = pltpu.PrefetchScalarGridSpec(
        num_scalar_prefetch=1, grid=(half,),
        in_specs=[pl.BlockSpec((None, r, c), lambda i, core: (i, 0, 0))] + [
            pl.BlockSpec((None, None, r, c), functools.partial(lambda i, core, k: (k, i, 0, 0), k=k))
            for k in range(N_CHIPS - 1)],
        out_specs=pl.BlockSpec((None, r, c), lambda i, core: (core[0] * half + i, 0, 0)))
    return pl.pallas_call(
        body, name="add_chip_blocks", grid_spec=grid_spec, out_shape=_sds((2 * half, r, c)),
        compiler_params=_params(("arbitrary",)),
    )(core, own, received, received, received)


def _sum_over_devices(parts):
    _, r, _ = parts.shape

    def body(p_ref, o_ref):
        acc = p_ref[0]
        for k in range(1, N_DEVICES):
            acc = acc + p_ref[k]
        o_ref[...] = acc

    return pl.pallas_call(body, name="sum_over_devices", out_shape=_sds((r, LANES)))(parts)


def _adamw_math(w, g, m, v):
    m = ADAM_B1 * m + (1.0 - ADAM_B1) * g
    v = ADAM_B2 * v + (1.0 - ADAM_B2) * (g * g)
    m_hat = m / (1.0 - ADAM_B1 ** ADAM_STEP)
    v_hat = v / (1.0 - ADAM_B2 ** ADAM_STEP)
    delta = -ADAM_LR * (m_hat / (jnp.sqrt(v_hat) + ADAM_EPS) + ADAM_WD * w)
    return delta, m, v


def _adamw_stacked(w, m, v, grads, offset):
    l, r, c = w.shape
    tr = r
    while tr * c * 4 > 2**20 and tr % 16 == 0:
        tr //= 2

    def body(w_ref, m_ref, v_ref, g_ref, go_ref, d_ref, mo_ref, vo_ref):
        g = g_ref[...]
        go_ref[...] = g
        d_ref[...], mo_ref[...], vo_ref[...] = _adamw_math(w_ref[...], g, m_ref[...], v_ref[...])

    blk = pl.BlockSpec((None, tr, c), lambda i, j: (i, j, 0))
    return pl.pallas_call(
        body, name="adamw_stacked", grid=(l, r // tr),
        in_specs=[blk, blk, blk, pl.BlockSpec((None, tr, c), lambda i, j: (offset + i, j, 0))],
        out_specs=[blk] * 4, out_shape=[_sds((l, r, c))] * 4,
        compiler_params=_params(("arbitrary", "arbitrary")),
    )(w, m, v, grads)


def _adamw_small(w, m, v, g):
    def body(w_ref, m_ref, v_ref, g_ref, d_ref, mo_ref, vo_ref):
        d_ref[...], mo_ref[...], vo_ref[...] = _adamw_math(w_ref[...], g_ref[...], m_ref[...], v_ref[...])

    return pl.pallas_call(body, name="adamw_small", out_shape=[_sds(w.shape)] * 3)(w, m, v, g)


def _pack_rows(arrs):
    flat = jnp.concatenate([a.reshape(-1) for a in arrs])
    pad = (-flat.shape[0]) % (8 * LANES)
    return jnp.pad(flat, (0, pad)).reshape(-1, LANES)


def _unpack_rows(rows, shapes, lead=()):
    flat = rows.reshape(lead + (-1,))
    out, at = [], 0
    for shp in shapes:
        size = int(np.prod(shp))
        out.append(flat[..., at:at + size].reshape(lead + tuple(shp)))
        at += size
    return out


WEIGHT_NAMES = ('ffn1_norm', 'ffn1_w_gate', 'ffn1_w_up', 'ffn1_w_down', 'mix_norm', 'pool_w', 'pool_scale',
                'mla_w_in', 'mla_q_norm', 'mla_w_q_up', 'mla_kv_norm', 'mla_w_kv_up', 'mla_q_head_norm',
                'mla_k_head_norm', 'mla_w_out', 'ffn2_norm', 'ffn2_w_gate', 'ffn2_w_up', 'ffn2_w_down')


def _chips_to_columns(g):
    return jnp.transpose(g, (1, 2, 0, 3)).reshape(g.shape[1], g.shape[2], -1)


def _columns_to_chips(full):
    n, r, c4 = full.shape
    return jnp.transpose(full.reshape(n, r, N_CHIPS, c4 // N_CHIPS), (2, 0, 1, 3))


def kernel(x, positions, ffn1_norm, ffn1_w_gate, ffn1_w_up, ffn1_w_down, mix_norm, pool_w, pool_scale, mla_w_in, mla_q_norm, mla_w_q_up, mla_kv_norm, mla_w_kv_up, mla_q_head_norm, mla_k_head_norm, mla_w_out, ffn2_norm, ffn2_w_gate, ffn2_w_up, ffn2_w_down, loss_target, m_ffn1_norm, m_ffn1_w_gate, m_ffn1_w_up, m_ffn1_w_down, m_mix_norm, m_pool_w, m_pool_scale, m_mla_w_in, m_mla_q_norm, m_mla_w_q_up, m_mla_kv_norm, m_mla_w_kv_up, m_mla_q_head_norm, m_mla_k_head_norm, m_mla_w_out, m_ffn2_norm, m_ffn2_w_gate, m_ffn2_w_up, m_ffn2_w_down, v_ffn1_norm, v_ffn1_w_gate, v_ffn1_w_up, v_ffn1_w_down, v_mix_norm, v_pool_w, v_pool_scale, v_mla_w_in, v_mla_q_norm, v_mla_w_q_up, v_mla_kv_norm, v_mla_w_kv_up, v_mla_q_head_norm, v_mla_k_head_norm, v_mla_w_out, v_ffn2_norm, v_ffn2_w_gate, v_ffn2_w_up, v_ffn2_w_down):
    env = dict(locals())
    w = {n: env[n] for n in WEIGHT_NAMES}
    mom = {n: env["m_" + n] for n in WEIGHT_NAMES}
    var = {n: env["v_" + n] for n in WEIGHT_NAMES}

    s, d = x.shape[1], x.shape[2]
    depth = ffn1_norm.shape[0]
    n_mla, n_pool, n_groups = mla_w_in.shape[0], pool_w.shape[0], pool_w.shape[1]
    pool_c = pool_w.shape[3]
    q_lora = N_CHIPS * mla_q_norm.shape[1]
    kv_lora = N_CHIPS * mla_kv_norm.shape[1]
    n_heads = N_CHIPS * mla_w_q_up.shape[2] // QK_HEAD
    t_attn = _tile(s, ATTN_TILE)
    n_attn = s // t_attn
    cx, cy, cc = _mesh_position()
    chip = 2 * cx + cy
    chip_arr = jnp.reshape(chip, (1,)).astype(jnp.int32)
    core_arr = jnp.reshape(cc, (1,)).astype(jnp.int32)

    shard_gu = _cast(jnp.concatenate([ffn1_w_gate, ffn1_w_up, ffn2_w_gate, ffn2_w_up], axis=0))
    shard_dn = _cast(jnp.concatenate([ffn1_w_down, ffn2_w_down], axis=0))
    shard_pool = _cast(pool_w.reshape((n_pool * n_groups,) + pool_w.shape[2:]))
    w_gu, w_dn, g_in, g_qup, g_kvup, g_out, g_pool = _gather_over_chips(
        [shard_gu, shard_dn, _cast(mla_w_in), _cast(mla_w_q_up), _cast(mla_w_kv_up), _cast(mla_w_out), shard_pool])
    small_shapes = [mla_q_norm.shape, mla_kv_norm.shape]
    small = _gather_over_devices(_pack_rows([mla_q_norm, mla_kv_norm]))[::2]
    qn_chips, kvn_chips = _unpack_rows(small, small_shapes, lead=(N_CHIPS,))
    q_norm_full = jnp.transpose(qn_chips, (1, 0, 2)).reshape(n_mla, 1, q_lora)
    kv_norm_full = jnp.transpose(kvn_chips, (1, 0, 2)).reshape(n_mla, 1, kv_lora)

    w_in_full = _chips_to_columns(g_in)
    w_q_heads = jnp.transpose(_chips_to_columns(g_qup).reshape(n_mla, q_lora, n_heads, QK_HEAD), (0, 2, 1, 3))
    w_kv = _chips_to_columns(g_kvup).reshape(n_mla, kv_lora, n_heads, QK_NOPE + V_HEAD)
    w_kn_heads = jnp.transpose(w_kv[..., :QK_NOPE], (0, 2, 1, 3))
    w_v_full = w_kv[..., QK_NOPE:].reshape(n_mla, kv_lora, n_heads * V_HEAD)
    w_out_full = jnp.transpose(g_out, (1, 0, 2, 3)).reshape(n_mla, n_heads * V_HEAD, d)
    pool_full = jnp.transpose(g_pool.reshape(N_CHIPS, n_pool, n_groups, pool_c // N_CHIPS, pool_c),
                              (1, 2, 0, 3, 4)).reshape(n_pool, n_groups, pool_c, pool_c)

    inv_freq = (1.0 / (ROPE_THETA ** (jnp.arange(0, QK_ROPE, 2, dtype=F32) / QK_ROPE))).reshape(1, -1)
    cos_t, sin_t = _rope_tables(positions.reshape(s, 1), inv_freq)

    row = lambda a, i: a[i].reshape(1, -1)
    i_gate1, i_up1, i_gate2, i_up2 = (lambda i: i), (lambda i: depth + i), (lambda i: 2 * depth + i), (lambda i: 3 * depth + i)
    i_dn1, i_dn2 = (lambda i: i), (lambda i: depth + i)

    h = x.reshape(s, d)
    saved = []
    for i in range(depth):
        rec = {"x_ffn1": h}
        h, *rec["ffn1"] = _ffn_fwd(h, row(ffn1_norm, i), w_gu, w_dn, i_gate1(i), i_up1(i), i_dn1(i))
        rec["x_mix"] = h
        j = i // 2
        if i % 2 == 0:
            h = _pool_fwd(h, row(mix_norm, i), pool_full[j], row(pool_scale, j))
        else:
            lat, q, k, v = _mla_qkv_fwd(h, row(mix_norm, i), w_in_full[j], q_norm_full[j], kv_norm_full[j],
                                        w_q_heads[j], w_kn_heads[j], w_v_full[j], row(mla_q_head_norm, j),
                                        row(mla_k_head_norm, j), cos_t, sin_t)
            vt = jnp.transpose(v.reshape(n_attn, t_attn, n_heads, V_HEAD), (2, 0, 3, 1))
            ot, lse = _flash_fwd(q, k, vt)
            rec.update(lat=lat, q=q, k=k, v=v, ot=ot, lse=lse)
            h = _mla_out_fwd(h, ot, w_out_full[j])
        rec["x_ffn2"] = h
        h, *rec["ffn2"] = _ffn_fwd(h, row(ffn2_norm, i), w_gu, w_dn, i_gate2(i), i_up2(i), i_dn2(i))
        saved.append(rec)

    loss_part, dy = _loss_and_grad(h, loss_target.reshape(s, d))
    loss = lax.psum(loss_part[0, 0], ("x", "y", "c"))

    g_gu = [None] * (4 * depth)
    g_dn = [None] * (2 * depth)
    g_norm = {n: [None] * depth for n in ("ffn1_norm", "mix_norm", "ffn2_norm")}
    g_pool_w, g_pool_scale = [None] * n_pool, [None] * n_pool
    g_mla = {n: [None] * n_mla for n in ("w_in", "q_norm", "kv_norm", "w_q", "w_kv", "qhn", "khn", "w_out")}
    for i in reversed(range(depth)):
        rec = saved[i]
        hb, gate, up = rec["ffn2"]
        dy, g_norm["ffn2_norm"][i], dyb, dgt, dup, act = _ffn_bwd_dgrad(
            rec["x_ffn2"], row(ffn2_norm, i), dy, gate, up, w_gu, w_dn, i_gate2(i), i_up2(i), i_dn2(i))
        g_gu[i_gate2(i)], g_gu[i_up2(i)], g_dn[i_dn2(i)] = _ffn_wgrad(hb, dyb, dgt, dup, act)
        j = i // 2
        if i % 2 == 0:
            dy, g_norm["mix_norm"][i], g_pool_w[j], g_pool_scale[j] = _pool_bwd(
                rec["x_mix"], row(mix_norm, i), pool_full[j], row(pool_scale, j), dy)
        else:
            do, delta, g_mla["w_out"][j] = _mla_out_bwd(dy, rec["ot"], w_out_full[j])
            by_tile = lambda a: a.reshape(n_heads, n_attn, 1, t_attn)
            dqt, dk, dv = _flash_bwd(rec["q"], rec["k"], jnp.transpose(rec["k"], (0, 2, 1)), rec["v"], do,
                                     by_tile(rec["lse"]), by_tile(delta))
            dq = jnp.transpose(dqt, (0, 1, 3, 2)).reshape(n_heads, s, QK_HEAD)
            (dy, g_norm["mix_norm"][i], g_mla["w_in"][j], g_mla["q_norm"][j], g_mla["kv_norm"][j], dwq, dwkn, dwv,
             g_mla["qhn"][j], g_mla["khn"][j]) = _mla_qkv_bwd(
                rec["x_mix"], rec["lat"], dy, dq, dk, dv, row(mix_norm, i), w_in_full[j], q_norm_full[j],
                kv_norm_full[j], w_q_heads[j], w_kn_heads[j], w_v_full[j], row(mla_q_head_norm, j),
                row(mla_k_head_norm, j), cos_t, sin_t)
            g_mla["w_q"][j] = jnp.transpose(dwq, (1, 0, 2)).reshape(q_lora, n_heads * QK_HEAD)
            g_mla["w_kv"][j] = jnp.concatenate(
                [jnp.transpose(dwkn, (1, 0, 2)), dwv.reshape(kv_lora, n_heads, V_HEAD)], axis=-1
            ).reshape(kv_lora, n_heads * (QK_NOPE + V_HEAD))
        hb, gate, up = rec["ffn1"]
        dy, g_norm["ffn1_norm"][i], dyb, dgt, dup, act = _ffn_bwd_dgrad(
            rec["x_ffn1"], row(ffn1_norm, i), dy, gate, up, w_gu, w_dn, i_gate1(i), i_up1(i), i_dn1(i))
        g_gu[i_gate1(i)], g_gu[i_up1(i)], g_dn[i_dn1(i)] = _ffn_wgrad(hb, dyb, dgt, dup, act)
    grad_x = dy.reshape(x.shape)

    full_grads = [
        jnp.stack(g_gu, axis=1),
        jnp.stack(g_dn, axis=1),
        _columns_to_chips(jnp.stack(g_mla["w_in"])),
        _columns_to_chips(jnp.stack(g_mla["w_q"])),
        _columns_to_chips(jnp.stack(g_mla["w_kv"])),
        jnp.transpose(jnp.stack(g_mla["w_out"]).reshape(n_mla, N_CHIPS, -1, d), (1, 0, 2, 3)),
        jnp.transpose(jnp.stack(g_pool_w).reshape(n_pool, n_groups, N_CHIPS, pool_c // N_CHIPS, pool_c),
                      (2, 0, 1, 3, 4)).reshape(N_CHIPS, n_pool * n_groups, pool_c // N_CHIPS, pool_c),
    ]
    from_sibling = _send_other_half_to_sibling(full_grads)
    chip_sums = [_add_sibling_half(g, r, chip_arr, core_arr) for g, r in zip(full_grads, from_sibling)]
    from_chips = _send_blocks_to_chips([wire for wire, _ in chip_sums])
    half_sums = [_add_chip_blocks(own, r, core_arr) for (_, own), r in zip(chip_sums, from_chips)]
    r_gu, r_dn, r_in, r_qup, r_kvup, r_out, r_pool = _join_halves_with_sibling(half_sums)

    small_grads = [jnp.concatenate(g_norm["ffn1_norm"]), jnp.concatenate(g_norm["mix_norm"]),
                   jnp.concatenate(g_norm["ffn2_norm"]), jnp.concatenate(g_pool_scale),
                   jnp.concatenate(g_mla["qhn"]), jnp.concatenate(g_mla["khn"]),
                   jnp.concatenate(g_mla["q_norm"]), jnp.concatenate(g_mla["kv_norm"])]
    small_sum = _sum_over_devices(_gather_over_devices(_pack_rows(small_grads)))
    (s_ffn1, s_mix, s_ffn2, s_pscale, s_qhn, s_khn, s_qn, s_kvn) = _unpack_rows(small_sum, [g.shape for g in small_grads])
    qn_w, kvn_w = mla_q_norm.shape[1], mla_kv_norm.shape[1]
    s_qn = lax.dynamic_slice_in_dim(s_qn, chip * qn_w, qn_w, axis=1)
    s_kvn = lax.dynamic_slice_in_dim(s_kvn, chip * kvn_w, kvn_w, axis=1)

    grads, deltas, new_m, new_v = {}, {}, {}, {}

    def stacked(name, reduced, offset):
        shape = w[name].shape
        as3 = lambda a: a.reshape((-1,) + shape[-2:])
        out = _adamw_stacked(as3(w[name]), as3(mom[name]), as3(var[name]), reduced, offset)
        grads[name], deltas[name], new_m[name], new_v[name] = [o.reshape(shape) for o in out]

    def small_update(name, g):
        grads[name] = g
        deltas[name], new_m[name], new_v[name] = _adamw_small(w[name], mom[name], var[name], g)

    stacked("ffn1_w_gate", r_gu, 0)
    stacked("ffn1_w_up", r_gu, depth)
    stacked("ffn2_w_gate", r_gu, 2 * depth)
    stacked("ffn2_w_up", r_gu, 3 * depth)
    stacked("ffn1_w_down", r_dn, 0)
    stacked("ffn2_w_down", r_dn, depth)
    stacked("mla_w_in", r_in, 0)
    stacked("mla_w_q_up", r_qup, 0)
    stacked("mla_w_kv_up", r_kvup, 0)
    stacked("mla_w_out", r_out, 0)
    stacked("pool_w", r_pool, 0)
    small_update("ffn1_norm", s_ffn1)
    small_update("mix_norm", s_mix)
    small_update("ffn2_norm", s_ffn2)
    small_update("pool_scale", s_pscale)
    small_update("mla_q_head_norm", s_qhn)
    small_update("mla_k_head_norm", s_khn)
    small_update("mla_q_norm", s_qn)
    small_update("mla_kv_norm", s_kvn)

    return (loss, grad_x, *[grads[n] for n in WEIGHT_NAMES], *[deltas[n] for n in WEIGHT_NAMES],
            *[new_m[n] for n in WEIGHT_NAMES], *[new_v[n] for n in WEIGHT_NAMES])
```

```python
import functools

import numpy as np

import jax
import jax.numpy as jnp
from jax import lax
from jax.experimental import pallas as pl
from jax.experimental.pallas import tpu as pltpu

F32 = jnp.float32
MXU_DTYPE = jnp.bfloat16
WIRE_DTYPE = jnp.bfloat16
MESH = pl.DeviceIdType.MESH
N_CHIPS = 4
N_DEVICES = 8
LANES = 128
VMEM_LIMIT_BYTES = 56 * 2**20
NORM_EPS = 1e-6
QK_NOPE, QK_ROPE, V_HEAD = 128, 64, 128
QK_HEAD = QK_NOPE + QK_ROPE
ROPE_THETA = 10000.0
POOL_WINDOWS = (2, 4, 8, 16)
POOL_HALO = 16
FFN_HALF = 0.5
ADAM_LR, ADAM_B1, ADAM_B2, ADAM_EPS, ADAM_WD, ADAM_STEP = 0.001, 0.9, 0.999, 1e-08, 0.01, 10
FFN_TILE = 512
WGRAD_TILE = 512
MLA_TILE = 256
POOL_TILE = 512
ATTN_TILE = 512
ROW_TILE = 1024


def _cast(v):
    return v.astype(MXU_DTYPE)


def _mm(a, b):
    return jnp.dot(a, b, preferred_element_type=F32)


def _mm_nt(a, b):
    return lax.dot_general(a, b, (((1,), (1,)), ((), ())), preferred_element_type=F32)


def _mm_tn(a, b):
    return lax.dot_general(a, b, (((0,), (0,)), ((), ())), preferred_element_type=F32)


def _rms_fwd(v, gain):
    r = lax.rsqrt(jnp.mean(v * v, axis=-1, keepdims=True) + NORM_EPS)
    return v * r * gain, r


def _rms_bwd(v, r, gain, dy):
    vr = v * r
    gy = dy * gain
    dv = r * (gy - vr * jnp.mean(gy * vr, axis=-1, keepdims=True))
    return dv, jnp.sum(dy * vr, axis=0, keepdims=True)


def _params(semantics=None):
    return pltpu.CompilerParams(dimension_semantics=semantics, vmem_limit_bytes=VMEM_LIMIT_BYTES)


def _tile(n, want):
    t = min(n, want)
    assert n % t == 0, (n, want)
    return t


def _full(shape):
    nd = len(shape)
    return pl.BlockSpec(shape, lambda *_: (0,) * nd)


def _sds(shape, dtype=F32):
    return jax.ShapeDtypeStruct(shape, dtype)


def _ffn_fwd(x, gain, w_gu, w_dn, i_gate, i_up, i_down):
    s, d = x.shape
    fs = w_gu.shape[-1]
    tm = _tile(s, FFN_TILE)

    def body(x_ref, g_ref, wg_ref, wu_ref, wd_ref, y_ref, hb_ref, act_ref, to_dg_ref, to_du_ref, acc_sc):
        j = pl.program_id(1)

        @pl.when(j == 0)
        def _():
            h, _ = _rms_fwd(x_ref[...], g_ref[...])
            hb_ref[...] = _cast(h)
            acc_sc[...] = jnp.zeros_like(acc_sc)

        h = hb_ref[...]
        g = _mm(h, wg_ref[...])
        u = _mm(h, wu_ref[...])
        sg = jax.nn.sigmoid(g)
        silu = g * sg
        act = _cast(silu * u)
        act_ref[...] = act
        to_dg_ref[...] = _cast((FFN_HALF * u) * (sg * (1.0 + g * (1.0 - sg))))
        to_du_ref[...] = _cast(FFN_HALF * silu)
        acc_sc[...] += _mm(act, wd_ref[...])

        @pl.when(j == N_CHIPS - 1)
        def _():
            y_ref[...] = x_ref[...] + FFN_HALF * acc_sc[...]

    tok = pl.BlockSpec((tm, d), lambda i, j: (i, 0))
    chunk = pl.BlockSpec((None, tm, fs), lambda i, j: (j, i, 0))
    return pl.pallas_call(
        body, name="ffn_fwd", grid=(s // tm, N_CHIPS),
        in_specs=[
            tok, _full((1, d)),
            pl.BlockSpec((None, None, d, fs), lambda i, j: (j, i_gate, 0, 0)),
            pl.BlockSpec((None, None, d, fs), lambda i, j: (j, i_up, 0, 0)),
            pl.BlockSpec((None, None, fs, d), lambda i, j: (j, i_down, 0, 0)),
        ],
        out_specs=[tok, tok, chunk, chunk, chunk],
        out_shape=[_sds((s, d)), _sds((s, d), MXU_DTYPE)] + [_sds((N_CHIPS, s, fs), MXU_DTYPE)] * 3,
        scratch_shapes=[pltpu.VMEM((tm, d), F32)],
        compiler_params=_params(("arbitrary", "arbitrary")),
    )(x, gain, w_gu, w_gu, w_dn)


def _ffn_bwd_dgrad(x, gain, dy, to_dg, to_du, w_gu, w_dn, i_gate, i_up, i_down):
    s, d = x.shape
    fs = w_gu.shape[-1]
    tm = _tile(s, FFN_TILE)
    n_tiles = s // tm

    def body(x_ref, g_ref, dy_ref, to_dg_ref, to_du_ref, wg_ref, wu_ref, wd_ref,
             dx_ref, dgain_ref, dyb_ref, dg_ref, du_ref, dh_sc):
        i, j = pl.program_id(0), pl.program_id(1)

        @pl.when(j == 0)
        def _():
            dh_sc[...] = jnp.zeros_like(dh_sc)
            dyb_ref[...] = _cast(dy_ref[...])

        dact = _mm_nt(dyb_ref[...], wd_ref[...])
        dgb = _cast(dact * to_dg_ref[...].astype(F32))
        dub = _cast(dact * to_du_ref[...].astype(F32))
        dg_ref[...] = dgb
        du_ref[...] = dub
        dh_sc[...] += _mm_nt(dgb, wg_ref[...]) + _mm_nt(dub, wu_ref[...])

        @pl.when(j == N_CHIPS - 1)
        def _():
            _, r = _rms_fwd(x_ref[...], g_ref[...])
            dxn, dgn = _rms_bwd(x_ref[...], r, g_ref[...], dh_sc[...])
            dx_ref[...] = dy_ref[...] + dxn

            @pl.when(i == 0)
            def _():
                dgain_ref[...] = dgn

            @pl.when(i > 0)
            def _():
                dgain_ref[...] += dgn

    tok = pl.BlockSpec((tm, d), lambda i, j: (i, 0))
    chunk = pl.BlockSpec((None, tm, fs), lambda i, j: (j, i, 0))
    return pl.pallas_call(
        body, name="ffn_bwd_dgrad", grid=(n_tiles, N_CHIPS),
        in_specs=[
            tok, _full((1, d)), tok, chunk, chunk,
            pl.BlockSpec((None, None, d, fs), lambda i, j: (j, i_gate, 0, 0)),
            pl.BlockSpec((None, None, d, fs), lambda i, j: (j, i_up, 0, 0)),
            pl.BlockSpec((None, None, fs, d), lambda i, j: (j, i_down, 0, 0)),
        ],
        out_specs=[tok, _full((1, d)), tok, chunk, chunk],
        out_shape=[_sds((s, d)), _sds((1, d)), _sds((s, d), MXU_DTYPE),
                   _sds((N_CHIPS, s, fs), MXU_DTYPE), _sds((N_CHIPS, s, fs), MXU_DTYPE)],
        scratch_shapes=[pltpu.VMEM((tm, d), F32)],
        compiler_params=_params(("arbitrary", "arbitrary")),
    )(x, gain, dy, to_dg, to_du, w_gu, w_gu, w_dn)


def _ffn_wgrad(hb, dyb, dg, du, act):
    s, d = hb.shape
    fs = dg.shape[-1]
    tk = _tile(s, WGRAD_TILE)
    n_k = s // tk

    def body(h_ref, dy_ref, dg_ref, du_ref, act_ref, wg_ref, wu_ref, wd_ref):
        k = pl.program_id(1)

        @pl.when(k == 0)
        def _():
            wg_ref[...] = jnp.zeros_like(wg_ref)
            wu_ref[...] = jnp.zeros_like(wu_ref)
            wd_ref[...] = jnp.zeros_like(wd_ref)

        h = h_ref[...]
        wg_ref[...] += _mm_tn(h, dg_ref[...])
        wu_ref[...] += _mm_tn(h, du_ref[...])
        wd_ref[...] += FFN_HALF * _mm_tn(act_ref[...], dy_ref[...])

    tok = pl.BlockSpec((tk, d), lambda j, k: (k, 0))
    chunk = pl.BlockSpec((None, tk, fs), lambda j, k: (j, k, 0))
    return pl.pallas_call(
        body, name="ffn_wgrad", grid=(N_CHIPS, n_k),
        in_specs=[tok, tok, chunk, chunk, chunk],
        out_specs=[pl.BlockSpec((None, d, fs), lambda j, k: (j, 0, 0)),
                   pl.BlockSpec((None, d, fs), lambda j, k: (j, 0, 0)),
                   pl.BlockSpec((None, fs, d), lambda j, k: (j, 0, 0))],
        out_shape=[_sds((N_CHIPS, d, fs)), _sds((N_CHIPS, d, fs)), _sds((N_CHIPS, fs, d))],
        compiler_params=_params(("arbitrary", "arbitrary")),
    )(hb, dyb, dg, du, act)


def _inv_count(first_row, n_rows, window):
    t = first_row + lax.broadcasted_iota(jnp.int32, (n_rows, 1), 0)
    return 1.0 / jnp.minimum(t + 1, window).astype(F32)


def _trailing_sum(v, window):
    k = 1
    while k < window:
        v = v + pltpu.roll(v, k, 0)
        k *= 2
    return v


def _leading_sum(v, window):
    n = v.shape[0]
    k = 1
    while k < window:
        v = v + pltpu.roll(v, n - k, 0)
        k *= 2
    return v


def _pool_normed_rows(x_ref, prev_ref, g_ref, i):
    h, r = _rms_fwd(x_ref[...], g_ref[...])
    hp, _ = _rms_fwd(prev_ref[...], g_ref[...])
    hp = jnp.where(i > 0, hp, 0.0)
    return jnp.concatenate([hp, h], axis=0), r


def _pooled_group(he, g, pg, first_row, tm):
    ue = he[:, g * pg:(g + 1) * pg]
    win = _trailing_sum(ue, POOL_WINDOWS[g])[POOL_HALO:]
    return win * _inv_count(first_row, tm, POOL_WINDOWS[g]) - ue[POOL_HALO:]


def _pool_specs(s, d, tm):
    per = tm // POOL_HALO
    last = s // POOL_HALO - 1
    tok = pl.BlockSpec((tm, d), lambda i: (i, 0))
    prev = pl.BlockSpec((POOL_HALO, d), lambda i: (jnp.maximum(i * per - 1, 0), 0))
    nxt = pl.BlockSpec((POOL_HALO, d), lambda i: (jnp.minimum((i + 1) * per, last), 0))
    return tok, prev, nxt


def _pool_fwd(x, gain, w, scale):
    s, d = x.shape
    n_g, pg = w.shape[0], w.shape[-1]
    tm = _tile(s, POOL_TILE)
    tok, prev, _ = _pool_specs(s, d, tm)

    def body(x_ref, prev_ref, g_ref, w_ref, sc_ref, y_ref):
        i = pl.program_id(0)
        he, _ = _pool_normed_rows(x_ref, prev_ref, g_ref, i)
        z = [_mm(_cast(_pooled_group(he, g, pg, i * tm, tm)), w_ref[g]) for g in range(n_g)]
        y_ref[...] = x_ref[...] + jnp.concatenate(z, axis=-1) * sc_ref[...]

    return pl.pallas_call(
        body, name="pool_fwd", grid=(s // tm,),
        in_specs=[tok, prev, _full((1, d)), _full(w.shape), _full((1, d))],
        out_specs=tok, out_shape=_sds((s, d)),
        compiler_params=_params(("arbitrary",)),
    )(x, x, gain, w, scale)


def _pool_bwd(x, gain, w, scale, dy):
    s, d = x.shape
    n_g, pg = w.shape[0], w.shape[-1]
    tm = _tile(s, POOL_TILE)
    n_tiles = s // tm
    tok, prev, nxt = _pool_specs(s, d, tm)

    def body(x_ref, prev_ref, dy_ref, next_ref, g_ref, w_ref, sc_ref, dx_ref, dgain_ref, dw_ref, dsc_ref):
        i = pl.program_id(0)

        @pl.when(i == 0)
        def _():
            dgain_ref[...] = jnp.zeros_like(dgain_ref)
            dw_ref[...] = jnp.zeros_like(dw_ref)
            dsc_ref[...] = jnp.zeros_like(dsc_ref)

        he, r = _pool_normed_rows(x_ref, prev_ref, g_ref, i)
        dy = dy_ref[...]
        dyn = jnp.where(i < n_tiles - 1, next_ref[...], 0.0)
        dze = jnp.concatenate([dy, dyn], axis=0) * sc_ref[...]
        dh, dsc = [], []
        for g in range(n_g):
            cols = slice(g * pg, (g + 1) * pg)
            pooled = _cast(_pooled_group(he, g, pg, i * tm, tm))
            dsc.append(jnp.sum(dy[:, cols] * _mm(pooled, w_ref[g]), axis=0, keepdims=True))
            dzb = _cast(dze[:, cols])
            dw_ref[g] += _mm_tn(pooled, dzb[:tm])
            dpool = _mm_nt(dzb, w_ref[g])
            spread = _leading_sum(dpool * _inv_count(i * tm, tm + POOL_HALO, POOL_WINDOWS[g]), POOL_WINDOWS[g])
            dh.append(spread[:tm] - dpool[:tm])
        dsc_ref[...] += jnp.concatenate(dsc, axis=-1)
        dxn, dgn = _rms_bwd(x_ref[...], r, g_ref[...], jnp.concatenate(dh, axis=-1))
        dgain_ref[...] += dgn
        dx_ref[...] = dy + dxn

    return pl.pallas_call(
        body, name="pool_bwd", grid=(n_tiles,),
        in_specs=[tok, prev, tok, nxt, _full((1, d)), _full(w.shape), _full((1, d))],
        out_specs=[tok, _full((1, d)), _full(w.shape), _full((1, d))],
        out_shape=[_sds((s, d)), _sds((1, d)), _sds(w.shape), _sds((1, d))],
        compiler_params=_params(("arbitrary",)),
    )(x, x, dy, dy, gain, w, scale)


def _rope_tables(pos_col, inv_freq):
    s = pos_col.shape[0]
    tm = _tile(s, ROW_TILE)
    half = QK_ROPE // 2

    def body(p_ref, f_ref, c_ref, s_ref):
        ang = p_ref[...].astype(F32) * f_ref[...]
        cos, sin = jnp.cos(ang), jnp.sin(ang)
        c_ref[...] = jnp.concatenate([jnp.ones((tm, QK_NOPE), F32), cos, cos], axis=-1)
        s_ref[...] = jnp.concatenate([jnp.zeros((tm, QK_NOPE), F32), -sin, sin], axis=-1)

    tab = pl.BlockSpec((tm, QK_HEAD), lambda i: (i, 0))
    return pl.pallas_call(
        body, name="rope_tables", grid=(s // tm,),
        in_specs=[pl.BlockSpec((tm, 1), lambda i: (i, 0)), _full((1, half))],
        out_specs=[tab, tab], out_shape=[_sds((s, QK_HEAD)), _sds((s, QK_HEAD))],
        compiler_params=_params(("arbitrary",)),
    )(pos_col, inv_freq)


def _swap_rope_halves(v):
    half = QK_ROPE // 2
    return jnp.concatenate([v[:, :QK_NOPE], v[:, QK_NOPE + half:], v[:, QK_NOPE:QK_NOPE + half]], axis=-1)


def _rope(v, cos, sin):
    return v * cos + _swap_rope_halves(v) * sin


def _rope_transposed(dv, cos, sin):
    return dv * cos + _swap_rope_halves(dv * sin)


def _mla_qkv_fwd(x, gain, w_in, q_norm, kv_norm, w_q, w_kn, w_v, q_head_norm, k_head_norm, cos, sin):
    s, d = x.shape
    n_h, ql = w_q.shape[0], w_q.shape[1]
    kvl, lat_w = w_kn.shape[1], w_in.shape[1]
    tm = _tile(s, MLA_TILE)

    def body(x_ref, g_ref, win_ref, qn_ref, kvn_ref, wq_ref, wkn_ref, wv_ref, qhn_ref, khn_ref, c_ref, s_ref,
             lat_ref, q_ref, k_ref, v_ref):
        h, _ = _rms_fwd(x_ref[...], g_ref[...])
        lat = _mm(_cast(h), win_ref[...])
        lat_ref[...] = lat
        cqn, _ = _rms_fwd(lat[:, :ql], qn_ref[...])
        ckvn, _ = _rms_fwd(lat[:, ql:ql + kvl], kvn_ref[...])
        kpe = lat[:, ql + kvl:]
        cqb, ckb = _cast(cqn), _cast(ckvn)
        cos_t, sin_t = c_ref[...], s_ref[...]
        v_ref[...] = _cast(_mm(ckb, wv_ref[...]))
        for hh in range(n_h):
            qn, _ = _rms_fwd(_mm(cqb, wq_ref[hh]), qhn_ref[...])
            q_ref[hh] = _cast(_rope(qn, cos_t, sin_t))
            kn, _ = _rms_fwd(jnp.concatenate([_mm(ckb, wkn_ref[hh]), kpe], axis=-1), khn_ref[...])
            k_ref[hh] = _cast(_rope(kn, cos_t, sin_t))

    tok = lambda w: pl.BlockSpec((tm, w), lambda i: (i, 0))
    heads = pl.BlockSpec((n_h, tm, QK_HEAD), lambda i: (0, i, 0))
    return pl.pallas_call(
        body, name="mla_qkv_fwd", grid=(s // tm,),
        in_specs=[tok(d), _full((1, d)), _full(w_in.shape), _full((1, ql)), _full((1, kvl)), _full(w_q.shape),
                  _full(w_kn.shape), _full(w_v.shape), _full((1, QK_HEAD)), _full((1, QK_HEAD)),
                  tok(QK_HEAD), tok(QK_HEAD)],
        out_specs=[tok(lat_w), heads, heads, tok(n_h * V_HEAD)],
        out_shape=[_sds((s, lat_w)), _sds((n_h, s, QK_HEAD), MXU_DTYPE), _sds((n_h, s, QK_HEAD), MXU_DTYPE),
                   _sds((s, n_h * V_HEAD), MXU_DTYPE)],
        compiler_params=_params(("arbitrary",)),
    )(x, gain, w_in, q_norm, kv_norm, w_q, w_kn, w_v, q_head_norm, k_head_norm, cos, sin)


def _mla_qkv_bwd(x, lat, dy, dq, dk, dv, gain, w_in, q_norm, kv_norm, w_q, w_kn, w_v, q_head_norm, k_head_norm,
                 cos, sin):
    s, d = x.shape
    n_h, ql = w_q.shape[0], w_q.shape[1]
    kvl, lat_w = w_kn.shape[1], w_in.shape[1]
    tm = _tile(s, MLA_TILE)

    def body(x_ref, lat_ref, dy_ref, dq_ref, dk_ref, dv_ref, g_ref, win_ref, qn_ref, kvn_ref, wq_ref, wkn_ref,
             wv_ref, qhn_ref, khn_ref, c_ref, s_ref,
             dx_ref, dg_ref, dwin_ref, dqn_ref, dkvn_ref, dwq_ref, dwkn_ref, dwv_ref, dqhn_ref, dkhn_ref):
        @pl.when(pl.program_id(0) == 0)
        def _():
            for ref in (dg_ref, dwin_ref, dqn_ref, dkvn_ref, dwq_ref, dwkn_ref, dwv_ref, dqhn_ref, dkhn_ref):
                ref[...] = jnp.zeros_like(ref)

        x_t = x_ref[...]
        h, r = _rms_fwd(x_t, g_ref[...])
        hb = _cast(h)
        lat = lat_ref[...]
        cq, ckv, kpe = lat[:, :ql], lat[:, ql:ql + kvl], lat[:, ql + kvl:]
        cqn, rq = _rms_fwd(cq, qn_ref[...])
        ckvn, rkv = _rms_fwd(ckv, kvn_ref[...])
        cqb, ckb = _cast(cqn), _cast(ckvn)
        cos_t, sin_t = c_ref[...], s_ref[...]

        dvb = _cast(dv_ref[...])
        dwv_ref[...] += _mm_tn(ckb, dvb)
        dckvn = _mm_nt(dvb, wv_ref[...])
        dcqn = jnp.zeros((tm, ql), F32)
        dkpe = jnp.zeros((tm, QK_ROPE), F32)
        dqhn = jnp.zeros((1, QK_HEAD), F32)
        dkhn = jnp.zeros((1, QK_HEAD), F32)
        for hh in range(n_h):
            qp = _mm(cqb, wq_ref[hh])
            _, rqp = _rms_fwd(qp, qhn_ref[...])
            dqp, dgq = _rms_bwd(qp, rqp, qhn_ref[...], _rope_transposed(dq_ref[hh], cos_t, sin_t))
            dqhn += dgq
            dqpb = _cast(dqp)
            dwq_ref[hh] += _mm_tn(cqb, dqpb)
            dcqn += _mm_nt(dqpb, wq_ref[hh])

            kp = jnp.concatenate([_mm(ckb, wkn_ref[hh]), kpe], axis=-1)
            _, rkp = _rms_fwd(kp, khn_ref[...])
            dkp, dgk = _rms_bwd(kp, rkp, khn_ref[...], _rope_transposed(dk_ref[hh], cos_t, sin_t))
            dkhn += dgk
            dknb = _cast(dkp[:, :QK_NOPE])
            dkpe += dkp[:, QK_NOPE:]
            dwkn_ref[hh] += _mm_tn(ckb, dknb)
            dckvn += _mm_nt(dknb, wkn_ref[hh])
        dqhn_ref[...] += dqhn
        dkhn_ref[...] += dkhn

        dcq, dgn = _rms_bwd(cq, rq, qn_ref[...], dcqn)
        dqn_ref[...] += dgn
        dckv, dgn = _rms_bwd(ckv, rkv, kvn_ref[...], dckvn)
        dkvn_ref[...] += dgn
        dlb = _cast(jnp.concatenate([dcq, dckv, dkpe], axis=-1))
        dwin_ref[...] += _mm_tn(hb, dlb)
        dxn, dgn = _rms_bwd(x_t, r, g_ref[...], _mm_nt(dlb, win_ref[...]))
        dg_ref[...] += dgn
        dx_ref[...] = dy_ref[...] + dxn

    tok = lambda w: pl.BlockSpec((tm, w), lambda i: (i, 0))
    heads = pl.BlockSpec((n_h, tm, QK_HEAD), lambda i: (0, i, 0))
    return pl.pallas_call(
        body, name="mla_qkv_bwd", grid=(s // tm,),
        in_specs=[tok(d), tok(lat_w), tok(d), heads, heads, tok(n_h * V_HEAD), _full((1, d)), _full(w_in.shape),
                  _full((1, ql)), _full((1, kvl)), _full(w_q.shape), _full(w_kn.shape), _full(w_v.shape),
                  _full((1, QK_HEAD)), _full((1, QK_HEAD)), tok(QK_HEAD), tok(QK_HEAD)],
        out_specs=[tok(d), _full((1, d)), _full(w_in.shape), _full((1, ql)), _full((1, kvl)), _full(w_q.shape),
                   _full(w_kn.shape), _full(w_v.shape), _full((1, QK_HEAD)), _full((1, QK_HEAD))],
        out_shape=[_sds((s, d)), _sds((1, d)), _sds(w_in.shape), _sds((1, ql)), _sds((1, kvl)), _sds(w_q.shape),
                   _sds(w_kn.shape), _sds(w_v.shape), _sds((1, QK_HEAD)), _sds((1, QK_HEAD))],
        compiler_params=_params(("arbitrary",)),
    )(x, lat, dy, dq, dk, dv, gain, w_in, q_norm, kv_norm, w_q, w_kn, w_v, q_head_norm, k_head_norm, cos, sin)


def _scores_t(k_t, q_t):
    return _mm_nt(k_t, q_t) * (QK_HEAD ** -0.5)


def _mask_above_diagonal(z, t):
    key = lax.broadcasted_iota(jnp.int32, (t, t), 0)
    query = lax.broadcasted_iota(jnp.int32, (t, t), 1)
    return jnp.where(key <= query, z, -jnp.inf)


def _flash_fwd(q, k, vt):
    n_h, s, _ = q.shape
    t = _tile(s, ATTN_TILE)
    n = s // t

    def body(q_ref, k_ref, vt_ref, ot_ref, lse_ref, m_sc, l_sc, acc_sc, z_sc):
        i = pl.program_id(1)
        m_sc[...] = jnp.full_like(m_sc, -jnp.inf)
        l_sc[...] = jnp.zeros_like(l_sc)
        acc_sc[...] = jnp.zeros_like(acc_sc)
        q_t = q_ref[...]

        def fetch(j, slot):
            z_sc[slot] = _scores_t(k_ref[pl.ds(pl.multiple_of(j * t, t), t), :], q_t)

        def stage(j, slot, masked, prefetch=True):
            if prefetch:
                fetch(j + 1, 1 - slot)
            z = _mask_above_diagonal(z_sc[slot], t) if masked else z_sc[slot]
            m_old = m_sc[...]
            m_new = jnp.maximum(m_old, jnp.max(z, axis=0, keepdims=True))
            alpha = jnp.exp(m_old - m_new)
            pr = jnp.exp(z - m_new)
            l_sc[...] = alpha * l_sc[...] + jnp.sum(pr, axis=0, keepdims=True)
            acc_sc[...] = alpha * acc_sc[...] + _mm(vt_ref[j], _cast(pr))
            m_sc[...] = m_new

        def pair_below_diagonal(pair, carry):
            stage(2 * pair, 0, False)
            stage(2 * pair + 1, 1, False)
            return carry

        fetch(0, 0)
        lax.fori_loop(0, i >> 1, pair_below_diagonal, 0)

        @pl.when((i & 1) == 1)
        def _():
            stage(i - 1, 0, False)
            stage(i, 1, True, prefetch=False)

        @pl.when((i & 1) == 0)
        def _():
            stage(i, 0, True, prefetch=False)

        ot_ref[...] = acc_sc[...] / l_sc[...]
        lse_ref[...] = m_sc[...] + jnp.log(l_sc[...])

    whole_head = dict(pipeline_mode=pl.Buffered(1))
    return pl.pallas_call(
        body, name="flash_fwd", grid=(n_h, n),
        in_specs=[pl.BlockSpec((None, t, QK_HEAD), lambda h, i: (h, i, 0)),
                  pl.BlockSpec((None, s, QK_HEAD), lambda h, i: (h, 0, 0), **whole_head),
                  pl.BlockSpec((None, n, V_HEAD, t), lambda h, i: (h, 0, 0, 0), **whole_head)],
        out_specs=[pl.BlockSpec((V_HEAD, t), lambda h, i: (h, i)),
                   pl.BlockSpec((None, 1, t), lambda h, i: (h, 0, i))],
        out_shape=[_sds((n_h * V_HEAD, s)), _sds((n_h, 1, s))],
        scratch_shapes=[pltpu.VMEM((1, t), F32), pltpu.VMEM((1, t), F32), pltpu.VMEM((V_HEAD, t), F32),
                        pltpu.VMEM((2, t, t), F32)],
        compiler_params=_params(("arbitrary", "arbitrary")),
    )(q, k, vt)


def _flash_bwd(q, k, kt, v, do, lse, delta):
    n_h, s, _ = q.shape
    t = _tile(s, ATTN_TILE)
    n = s // t

    def body(q_ref, do_ref, lse_ref, dl_ref, k_ref, kt_ref, v_ref, dqt_ref, dk_ref, dv_ref, dk_sc, dv_sc, z_sc, dp_sc):
        j = pl.program_id(1)

        @pl.when(j == 0)
        def _():
            dqt_ref[...] = jnp.zeros_like(dqt_ref)

        dk_sc[...] = jnp.zeros_like(dk_sc)
        dv_sc[...] = jnp.zeros_like(dv_sc)
        k_t, kt_t, v_t = k_ref[...], kt_ref[...], v_ref[...]

        def rows(i):
            return pl.ds(pl.multiple_of(i * t, t), t)

        def fetch(i, slot):
            i = jnp.minimum(i, n - 1)
            z_sc[slot] = _scores_t(k_t, q_ref[rows(i), :])
            dp_sc[slot] = _mm_nt(v_t, do_ref[rows(i), :])

        def stage(i, slot, masked, prefetch=True):
            if prefetch:
                fetch(i + 1, 1 - slot)
            z = _mask_above_diagonal(z_sc[slot], t) if masked else z_sc[slot]
            pr = jnp.exp(z - lse_ref[i])
            dsb = _cast(pr * (dp_sc[slot] - dl_ref[i]) * (QK_HEAD ** -0.5))
            dv_sc[...] += _mm(_cast(pr), do_ref[rows(i), :])
            dk_sc[...] += _mm(dsb, q_ref[rows(i), :])
            dqt_ref[i] += _mm(kt_t, dsb)

        def pair_below_diagonal(pair, carry):
            stage(j + 1 + 2 * pair, 1, False)
            stage(j + 2 + 2 * pair, 0, False)
            return carry

        below = n - 1 - j
        fetch(j, 0)
        stage(j, 0, True)
        lax.fori_loop(0, below >> 1, pair_below_diagonal, 0)

        @pl.when((below & 1) == 1)
        def _():
            stage(n - 1, 1, False, prefetch=False)

        dk_ref[...] = dk_sc[...]
        dv_ref[...] = dv_sc[...]

    whole_head = dict(pipeline_mode=pl.Buffered(1))
    stat = pl.BlockSpec((None, n, 1, t), lambda h, j: (h, 0, 0, 0))
    return pl.pallas_call(
        body, name="flash_bwd", grid=(n_h, n),
        in_specs=[pl.BlockSpec((None, s, QK_HEAD), lambda h, j: (h, 0, 0), **whole_head),
                  pl.BlockSpec((s, V_HEAD), lambda h, j: (0, h), **whole_head),
                  stat, stat,
                  pl.BlockSpec((None, t, QK_HEAD), lambda h, j: (h, j, 0)),
                  pl.BlockSpec((None, QK_HEAD, t), lambda h, j: (h, 0, j)),
                  pl.BlockSpec((t, V_HEAD), lambda h, j: (j, h))],
        out_specs=[pl.BlockSpec((None, n, QK_HEAD, t), lambda h, j: (h, 0, 0, 0)),
                   pl.BlockSpec((None, t, QK_HEAD), lambda h, j: (h, j, 0)),
                   pl.BlockSpec((t, V_HEAD), lambda h, j: (j, h))],
        out_shape=[_sds((n_h, n, QK_HEAD, t)), _sds((n_h, s, QK_HEAD)), _sds((s, n_h * V_HEAD))],
        scratch_shapes=[pltpu.VMEM((t, QK_HEAD), F32), pltpu.VMEM((t, V_HEAD), F32),
                        pltpu.VMEM((2, t, t), F32), pltpu.VMEM((2, t, t), F32)],
        compiler_params=_params(("arbitrary", "arbitrary")),
    )(q, do, lse, delta, k, kt, v)


def _mla_out_fwd(x, ot, w_out):
    s, d = x.shape
    hv = ot.shape[0]
    tm = _tile(s, FFN_TILE)

    def body(x_ref, ot_ref, w_ref, y_ref):
        y_ref[...] = x_ref[...] + _mm_tn(_cast(ot_ref[...]), w_ref[...])

    tok = pl.BlockSpec((tm, d), lambda i: (i, 0))
    return pl.pallas_call(
        body, name="mla_out_fwd", grid=(s // tm,),
        in_specs=[tok, pl.BlockSpec((hv, tm), lambda i: (0, i)), _full(w_out.shape)],
        out_specs=tok, out_shape=_sds((s, d)),
        compiler_params=_params(("arbitrary",)),
    )(x, ot, w_out)


def _mla_out_bwd(dy, ot, w_out):
    s, d = dy.shape
    hv = ot.shape[0]
    n_h = hv // V_HEAD
    tm = _tile(s, FFN_TILE)

    def body(dy_ref, ot_ref, w_ref, do_ref, dl_ref, dw_ref):
        @pl.when(pl.program_id(0) == 0)
        def _():
            dw_ref[...] = jnp.zeros_like(dw_ref)

        dyb = _cast(dy_ref[...])
        o_t = ot_ref[...]
        do_ref[...] = _cast(_mm_nt(dyb, w_ref[...]))
        prod = _mm_nt(w_ref[...], dyb) * o_t
        for hh in range(n_h):
            dl_ref[hh] = jnp.sum(prod[hh * V_HEAD:(hh + 1) * V_HEAD], axis=0, keepdims=True)
        dw_ref[...] += _mm(_cast(o_t), dyb)

    return pl.pallas_call(
        body, name="mla_out_bwd", grid=(s // tm,),
        in_specs=[pl.BlockSpec((tm, d), lambda i: (i, 0)), pl.BlockSpec((hv, tm), lambda i: (0, i)),
                  _full(w_out.shape)],
        out_specs=[pl.BlockSpec((tm, hv), lambda i: (i, 0)), pl.BlockSpec((n_h, 1, tm), lambda i: (0, 0, i)),
                   _full(w_out.shape)],
        out_shape=[_sds((s, hv), MXU_DTYPE), _sds((n_h, 1, s)), _sds(w_out.shape)],
        compiler_params=_params(("arbitrary",)),
    )(dy, ot, w_out)


def _loss_and_grad(y, target):
    s, d = y.shape
    tm = _tile(s, ROW_TILE)

    def body(y_ref, t_ref, loss_ref, dy_ref):
        @pl.when(pl.program_id(0) == 0)
        def _():
            loss_ref[...] = jnp.zeros_like(loss_ref)

        err = y_ref[...] - t_ref[...]
        dy_ref[...] = err * (1.0 / d)
        loss_ref[...] += 0.5 * jnp.sum(jnp.mean(err * err, axis=-1, keepdims=True), axis=0, keepdims=True)

    tok = pl.BlockSpec((tm, d), lambda i: (i, 0))
    return pl.pallas_call(
        body, name="loss_and_grad", grid=(s // tm,),
        in_specs=[tok, tok], out_specs=[_full((1, 1)), tok],
        out_shape=[_sds((1, 1)), _sds((s, d))],
        compiler_params=_params(("arbitrary",)),
    )(y, target)


def _mesh_position():
    return lax.axis_index("x"), lax.axis_index("y"), lax.axis_index("c")


def _other_chips(x, y):
    return [(1 - x, y), (x, 1 - y), (1 - x, 1 - y)]


ANY = pl.BlockSpec(memory_space=pl.ANY)


def _gather_over_chips(arrs):
    n = len(arrs)
    halves = [a.shape[0] // 2 for a in arrs]
    assert all(a.shape[0] % 2 == 0 for a in arrs)
    own = 2 * (N_CHIPS - 1)

    def body(*refs):
        srcs, outs = refs[:n], refs[n:2 * n]
        send_sems, recv_sems = refs[2 * n:]
        x, y, c = _mesh_position()
        me, sibling = (x, y, c), (x, y, 1 - c)
        chips = _other_chips(x, y)
        my_chip = 2 * x + y

        def rows(t, chip, half):
            return outs[t].at[chip, pl.ds(half * halves[t], halves[t])]

        def copy(t, k, src, dst, to):
            return pltpu.make_async_remote_copy(src_ref=src, dst_ref=dst, send_sem=send_sems.at[t, k],
                                                recv_sem=recv_sems.at[t, k], device_id=to, device_id_type=MESH)

        started = []
        for t in range(n):
            for k, (px, py) in enumerate(chips):
                cp = copy(t, k, srcs[t].at[pl.ds(c * halves[t], halves[t])], rows(t, my_chip, c), (px, py, c))
                cp.start()
                started.append(cp)
            cp = copy(t, own, srcs[t], outs[t].at[my_chip], sibling)
            cp.start()
            started.append(cp)
        for t in range(n):
            for k, (px, py) in enumerate(chips):
                landed = rows(t, 2 * px + py, c)
                copy(t, k, landed, landed, me).wait_recv()
                cp = copy(t, N_CHIPS - 1 + k, landed, landed, sibling)
                cp.start()
                started.append(cp)
        for t in range(n):
            for k, (px, py) in enumerate(chips):
                passed = rows(t, 2 * px + py, 1 - c)
                copy(t, N_CHIPS - 1 + k, passed, passed, me).wait_recv()
            copy(t, own, srcs[t], outs[t].at[my_chip], me).wait_recv()
        for cp in started:
            cp.wait_send()

    return pl.pallas_call(
        body, name="gather_over_chips",
        in_specs=[ANY] * n, out_specs=[ANY] * n,
        out_shape=[_sds((N_CHIPS,) + a.shape, a.dtype) for a in arrs],
        scratch_shapes=[pltpu.SemaphoreType.DMA((n, own + 1)), pltpu.SemaphoreType.DMA((n, own + 1))],
    )(*arrs)


def _send_other_half_to_sibling(grads):
    n = len(grads)
    halves = [g.shape[1] // 2 for g in grads]

    def body(*refs):
        srcs, outs = refs[:n], refs[n:2 * n]
        send_sems, recv_sems = refs[2 * n:]
        x, y, c = _mesh_position()
        copies = []
        for t in range(n):
            cp = pltpu.make_async_remote_copy(
                src_ref=srcs[t].at[pl.ds(0, N_CHIPS), pl.ds((1 - c) * halves[t], halves[t])], dst_ref=outs[t],
                send_sem=send_sems.at[t], recv_sem=recv_sems.at[t], device_id=(x, y, 1 - c), device_id_type=MESH)
            cp.start()
            copies.append(cp)
        for cp in copies:
            cp.wait_recv()
        for cp in copies:
            cp.wait_send()

    return pl.pallas_call(
        body, name="send_other_half_to_sibling",
        in_specs=[ANY] * n, out_specs=[ANY] * n,
        out_shape=[_sds((N_CHIPS, h) + g.shape[2:]) for g, h in zip(grads, halves)],
        scratch_shapes=[pltpu.SemaphoreType.DMA((n,)), pltpu.SemaphoreType.DMA((n,))],
    )(*grads)


def _send_blocks_to_chips(parts):
    n = len(parts)

    def body(*refs):
        srcs, outs = refs[:n], refs[n:2 * n]
        send_sems, recv_sems = refs[2 * n:]
        x, y, c = _mesh_position()
        copies = []
        for t in range(n):
            for k, (px, py) in enumerate(_other_chips(x, y)):
                cp = pltpu.make_async_remote_copy(
                    src_ref=srcs[t].at[2 * px + py], dst_ref=outs[t].at[k], send_sem=send_sems.at[t, k],
                    recv_sem=recv_sems.at[t, k], device_id=(px, py, c), device_id_type=MESH)
                cp.start()
                copies.append(cp)
        for cp in copies:
            cp.wait_recv()
        for cp in copies:
            cp.wait_send()

    return pl.pallas_call(
        body, name="send_blocks_to_chips",
        in_specs=[ANY] * n, out_specs=[ANY] * n,
        out_shape=[_sds((N_CHIPS - 1,) + p.shape[1:], p.dtype) for p in parts],
        scratch_shapes=[pltpu.SemaphoreType.DMA((n, N_CHIPS - 1)), pltpu.SemaphoreType.DMA((n, N_CHIPS - 1))],
    )(*parts)


def _join_halves_with_sibling(sums):
    n = len(sums)

    def body(*refs):
        srcs, outs = refs[:n], refs[n:2 * n]
        send_sems, recv_sems = refs[2 * n:]
        x, y, c = _mesh_position()
        copies = []
        for t in range(n):
            h = srcs[t].shape[0] // 2
            mine = pl.ds(c * h, h)
            cp = pltpu.make_async_remote_copy(
                src_ref=srcs[t].at[mine], dst_ref=outs[t].at[mine], send_sem=send_sems.at[t],
                recv_sem=recv_sems.at[t], device_id=(x, y, 1 - c), device_id_type=MESH)
            cp.start()
            copies.append(cp)
        for t in range(n):
            h = srcs[t].shape[0] // 2
            theirs = pl.ds((1 - c) * h, h)
            pltpu.make_async_remote_copy(
                src_ref=srcs[t].at[theirs], dst_ref=outs[t].at[theirs], send_sem=send_sems.at[t],
                recv_sem=recv_sems.at[t], device_id=(x, y, 1 - c), device_id_type=MESH).wait_recv()
        for cp in copies:
            cp.wait_send()

    return pl.pallas_call(
        body, name="join_halves_with_sibling",
        in_specs=[ANY] * n, out_specs=[ANY] * n,
        out_shape=[_sds(a.shape) for a in sums],
        input_output_aliases={t: t for t in range(n)},
        scratch_shapes=[pltpu.SemaphoreType.DMA((n,)), pltpu.SemaphoreType.DMA((n,))],
    )(*sums)


def _gather_over_devices(rows):
    r = rows.shape[0]

    def body(in_ref, out_ref, send_sems, recv_sems, local_sem):
        x, y, c = _mesh_position()
        mine = pltpu.make_async_copy(in_ref, out_ref.at[4 * x + 2 * y + c], local_sem)
        mine.start()
        copies = []
        for mask in range(1, N_DEVICES):
            fx, fy, fc = (mask >> 2) & 1, (mask >> 1) & 1, mask & 1
            px, py, pc = (1 - x if fx else x), (1 - y if fy else y), (1 - c if fc else c)
            send = pltpu.make_async_remote_copy(
                src_ref=in_ref, dst_ref=out_ref.at[4 * x + 2 * y + c], send_sem=send_sems.at[mask - 1],
                recv_sem=recv_sems.at[mask - 1], device_id=(px, py, pc), device_id_type=MESH)
            send.start()
            recv = pltpu.make_async_remote_copy(
                src_ref=in_ref, dst_ref=out_ref.at[4 * px + 2 * py + pc], send_sem=send_sems.at[mask - 1],
                recv_sem=recv_sems.at[mask - 1], device_id=(px, py, pc), device_id_type=MESH)
            copies.append((send, recv))
        for _, recv in copies:
            recv.wait_recv()
        for send, _ in copies:
            send.wait_send()
        mine.wait()

    vm = pl.BlockSpec(memory_space=pltpu.VMEM)
    return pl.pallas_call(
        body, name="gather_over_devices", in_specs=[vm], out_specs=vm,
        out_shape=_sds((N_DEVICES, r, LANES)),
        scratch_shapes=[pltpu.SemaphoreType.DMA((N_DEVICES - 1,)), pltpu.SemaphoreType.DMA((N_DEVICES - 1,)),
                        pltpu.SemaphoreType.DMA],
    )(rows)


def _add_sibling_half(grad, received, chip, core):
    _, l, r, c = grad.shape
    half = l // 2

    def body(chip_ref, core_ref, g_ref, r_ref, wire_ref, own_ref):
        total = g_ref[...] + r_ref[...]
        wire_ref[...] = total.astype(WIRE_DTYPE)

        @pl.when(pl.program_id(1) == chip_ref[0])
        def _():
            own_ref[...] = total

    blk = lambda f: pl.BlockSpec((None, None, r, c), f)
    grid_spec = pltpu.PrefetchScalarGridSpec(
        num_scalar_prefetch=2, grid=(half, N_CHIPS),
        in_specs=[blk(lambda i, j, chip, core: (j, core[0] * half + i, 0, 0)),
                  blk(lambda i, j, chip, core: (j, i, 0, 0))],
        out_specs=[blk(lambda i, j, chip, core: (j, i, 0, 0)),
                   pl.BlockSpec((None, r, c), lambda i, j, chip, core: (i, 0, 0))])
    return pl.pallas_call(
        body, name="add_sibling_half", grid_spec=grid_spec,
        out_shape=[_sds((N_CHIPS, half, r, c), WIRE_DTYPE), _sds((half, r, c))],
        compiler_params=_params(("arbitrary", "arbitrary")),
    )(chip, core, grad, received)


def _add_chip_blocks(own, received, core):
    half, r, c = own.shape

    def body(core_ref, p_ref, r0_ref, r1_ref, r2_ref, o_ref):
        o_ref[...] = ((p_ref[...] + r0_ref[...].astype(F32)) + r1_ref[...].astype(F32)) + r2_ref[...].astype(F32)

    grid_spec = pltpu.PrefetchScalarGridSpec(
        num_scalar_prefetch=1, grid=(half,),
        in_specs=[pl.BlockSpec((None, r, c), lambda i, core: (i, 0, 0))] + [
            pl.BlockSpec((None, None, r, c), functools.partial(lambda i, core, k: (k, i, 0, 0), k=k))
            for k in range(N_CHIPS - 1)],
        out_specs=pl.BlockSpec((None, r, c), lambda i, core: (core[0] * half + i, 0, 0)))
    return pl.pallas_call(
        body, name="add_chip_blocks", grid_spec=grid_spec, out_shape=_sds((2 * half, r, c)),
        compiler_params=_params(("arbitrary",)),
    )(core, own, received, received, received)


def _sum_over_devices(parts):
    _, r, _ = parts.shape

    def body(p_ref, o_ref):
        acc = p_ref[0]
        for k in range(1, N_DEVICES):
            acc = acc + p_ref[k]
        o_ref[...] = acc

    return pl.pallas_call(body, name="sum_over_devices", out_shape=_sds((r, LANES)))(parts)


def _adamw_math(w, g, m, v):
    m = ADAM_B1 * m + (1.0 - ADAM_B1) * g
    v = ADAM_B2 * v + (1.0 - ADAM_B2) * (g * g)
    m_hat = m / (1.0 - ADAM_B1 ** ADAM_STEP)
    v_hat = v / (1.0 - ADAM_B2 ** ADAM_STEP)
    delta = -ADAM_LR * (m_hat / (jnp.sqrt(v_hat) + ADAM_EPS) + ADAM_WD * w)
    return delta, m, v


def _adamw_stacked(w, m, v, grads, offset):
    l, r, c = w.shape
    tr = r
    while tr * c * 4 > 2**20 and tr % 16 == 0:
        tr //= 2

    def body(w_ref, m_ref, v_ref, g_ref, go_ref, d_ref, mo_ref, vo_ref):
        g = g_ref[...]
        go_ref[...] = g
        d_ref[...], mo_ref[...], vo_ref[...] = _adamw_math(w_ref[...], g, m_ref[...], v_ref[...])

    blk = pl.BlockSpec((None, tr, c), lambda i, j: (i, j, 0))
    return pl.pallas_call(
        body, name="adamw_stacked", grid=(l, r // tr),
        in_specs=[blk, blk, blk, pl.BlockSpec((None, tr, c), lambda i, j: (offset + i, j, 0))],
        out_specs=[blk] * 4, out_shape=[_sds((l, r, c))] * 4,
        compiler_params=_params(("arbitrary", "arbitrary")),
    )(w, m, v, grads)


def _adamw_small(w, m, v, g):
    def body(w_ref, m_ref, v_ref, g_ref, d_ref, mo_ref, vo_ref):
        d_ref[...], mo_ref[...], vo_ref[...] = _adamw_math(w_ref[...], g_ref[...], m_ref[...], v_ref[...])

    return pl.pallas_call(body, name="adamw_small", out_shape=[_sds(w.shape)] * 3)(w, m, v, g)


def _pack_rows(arrs):
    flat = jnp.concatenate([a.reshape(-1) for a in arrs])
    pad = (-flat.shape[0]) % (8 * LANES)
    return jnp.pad(flat, (0, pad)).reshape(-1, LANES)


def _unpack_rows(rows, shapes, lead=()):
    flat = rows.reshape(lead + (-1,))
    out, at = [], 0
    for shp in shapes:
        size = int(np.prod(shp))
        out.append(flat[..., at:at + size].reshape(lead + tuple(shp)))
        at += size
    return out


WEIGHT_NAMES = ('ffn1_norm', 'ffn1_w_gate', 'ffn1_w_up', 'ffn1_w_down', 'mix_norm', 'pool_w', 'pool_scale',
                'mla_w_in', 'mla_q_norm', 'mla_w_q_up', 'mla_kv_norm', 'mla_w_kv_up', 'mla_q_head_norm',
                'mla_k_head_norm', 'mla_w_out', 'ffn2_norm', 'ffn2_w_gate', 'ffn2_w_up', 'ffn2_w_down')


def _chips_to_columns(g):
    return jnp.transpose(g, (1, 2, 0, 3)).reshape(g.shape[1], g.shape[2], -1)


def _columns_to_chips(full):
    n, r, c4 = full.shape
    return jnp.transpose(full.reshape(n, r, N_CHIPS, c4 // N_CHIPS), (2, 0, 1, 3))


def kernel(x, positions, ffn1_norm, ffn1_w_gate, ffn1_w_up, ffn1_w_down, mix_norm, pool_w, pool_scale, mla_w_in, mla_q_norm, mla_w_q_up, mla_kv_norm, mla_w_kv_up, mla_q_head_norm, mla_k_head_norm, mla_w_out, ffn2_norm, ffn2_w_gate, ffn2_w_up, ffn2_w_down, loss_target, m_ffn1_norm, m_ffn1_w_gate, m_ffn1_w_up, m_ffn1_w_down, m_mix_norm, m_pool_w, m_pool_scale, m_mla_w_in, m_mla_q_norm, m_mla_w_q_up, m_mla_kv_norm, m_mla_w_kv_up, m_mla_q_head_norm, m_mla_k_head_norm, m_mla_w_out, m_ffn2_norm, m_ffn2_w_gate, m_ffn2_w_up, m_ffn2_w_down, v_ffn1_norm, v_ffn1_w_gate, v_ffn1_w_up, v_ffn1_w_down, v_mix_norm, v_pool_w, v_pool_scale, v_mla_w_in, v_mla_q_norm, v_mla_w_q_up, v_mla_kv_norm, v_mla_w_kv_up, v_mla_q_head_norm, v_mla_k_head_norm, v_mla_w_out, v_ffn2_norm, v_ffn2_w_gate, v_ffn2_w_up, v_ffn2_w_down):
    env = dict(locals())
    w = {n: env[n] for n in WEIGHT_NAMES}
    mom = {n: env["m_" + n] for n in WEIGHT_NAMES}
    var = {n: env["v_" + n] for n in WEIGHT_NAMES}

    s, d = x.shape[1], x.shape[2]
    depth = ffn1_norm.shape[0]
    n_mla, n_pool, n_groups = mla_w_in.shape[0], pool_w.shape[0], pool_w.shape[1]
    pool_c = pool_w.shape[3]
    q_lora = N_CHIPS * mla_q_norm.shape[1]
    kv_lora = N_CHIPS * mla_kv_norm.shape[1]
    n_heads = N_CHIPS * mla_w_q_up.shape[2] // QK_HEAD
    t_attn = _tile(s, ATTN_TILE)
    n_attn = s // t_attn
    cx, cy, cc = _mesh_position()
    chip = 2 * cx + cy
    chip_arr = jnp.reshape(chip, (1,)).astype(jnp.int32)
    core_arr = jnp.reshape(cc, (1,)).astype(jnp.int32)

    shard_gu = _cast(jnp.concatenate([ffn1_w_gate, ffn1_w_up, ffn2_w_gate, ffn2_w_up], axis=0))
    shard_dn = _cast(jnp.concatenate([ffn1_w_down, ffn2_w_down], axis=0))
    shard_pool = _cast(pool_w.reshape((n_pool * n_groups,) + pool_w.shape[2:]))
    w_gu, w_dn, g_in, g_qup, g_kvup, g_out, g_pool = _gather_over_chips(
        [shard_gu, shard_dn, _cast(mla_w_in), _cast(mla_w_q_up), _cast(mla_w_kv_up), _cast(mla_w_out), shard_pool])
    small_shapes = [mla_q_norm.shape, mla_kv_norm.shape]
    small = _gather_over_devices(_pack_rows([mla_q_norm, mla_kv_norm]))[::2]
    qn_chips, kvn_chips = _unpack_rows(small, small_shapes, lead=(N_CHIPS,))
    q_norm_full = jnp.transpose(qn_chips, (1, 0, 2)).reshape(n_mla, 1, q_lora)
    kv_norm_full = jnp.transpose(kvn_chips, (1, 0, 2)).reshape(n_mla, 1, kv_lora)

    w_in_full = _chips_to_columns(g_in)
    w_q_heads = jnp.transpose(_chips_to_columns(g_qup).reshape(n_mla, q_lora, n_heads, QK_HEAD), (0, 2, 1, 3))
    w_kv = _chips_to_columns(g_kvup).reshape(n_mla, kv_lora, n_heads, QK_NOPE + V_HEAD)
    w_kn_heads = jnp.transpose(w_kv[..., :QK_NOPE], (0, 2, 1, 3))
    w_v_full = w_kv[..., QK_NOPE:].reshape(n_mla, kv_lora, n_heads * V_HEAD)
    w_out_full = jnp.transpose(g_out, (1, 0, 2, 3)).reshape(n_mla, n_heads * V_HEAD, d)
    pool_full = jnp.transpose(g_pool.reshape(N_CHIPS, n_pool, n_groups, pool_c // N_CHIPS, pool_c),
                              (1, 2, 0, 3, 4)).reshape(n_pool, n_groups, pool_c, pool_c)

    inv_freq = (1.0 / (ROPE_THETA ** (jnp.arange(0, QK_ROPE, 2, dtype=F32) / QK_ROPE))).reshape(1, -1)
    cos_t, sin_t = _rope_tables(positions.reshape(s, 1), inv_freq)

    row = lambda a, i: a[i].reshape(1, -1)
    i_gate1, i_up1, i_gate2, i_up2 = (lambda i: i), (lambda i: depth + i), (lambda i: 2 * depth + i), (lambda i: 3 * depth + i)
    i_dn1, i_dn2 = (lambda i: i), (lambda i: depth + i)

    h = x.reshape(s, d)
    saved = []
    for i in range(depth):
        rec = {"x_ffn1": h}
        h, *rec["ffn1"] = _ffn_fwd(h, row(ffn1_norm, i), w_gu, w_dn, i_gate1(i), i_up1(i), i_dn1(i))
        rec["x_mix"] = h
        j = i // 2
        if i % 2 == 0:
            h = _pool_fwd(h, row(mix_norm, i), pool_full[j], row(pool_scale, j))
        else:
            lat, q, k, v = _mla_qkv_fwd(h, row(mix_norm, i), w_in_full[j], q_norm_full[j], kv_norm_full[j],
                                        w_q_heads[j], w_kn_heads[j], w_v_full[j], row(mla_q_head_norm, j),
                                        row(mla_k_head_norm, j), cos_t, sin_t)
            vt = jnp.transpose(v.reshape(n_attn, t_attn, n_heads, V_HEAD), (2, 0, 3, 1))
            ot, lse = _flash_fwd(q, k, vt)
            rec.update(lat=lat, q=q, k=k, v=v, ot=ot, lse=lse)
            h = _mla_out_fwd(h, ot, w_out_full[j])
        rec["x_ffn2"] = h
        h, *rec["ffn2"] = _ffn_fwd(h, row(ffn2_norm, i), w_gu, w_dn, i_gate2(i), i_up2(i), i_dn2(i))
        saved.append(rec)

    loss_part, dy = _loss_and_grad(h, loss_target.reshape(s, d))
    loss = lax.psum(loss_part[0, 0], ("x", "y", "c"))

    g_gu = [None] * (4 * depth)
    g_dn = [None] * (2 * depth)
    g_norm = {n: [None] * depth for n in ("ffn1_norm", "mix_norm", "ffn2_norm")}
    g_pool_w, g_pool_scale = [None] * n_pool, [None] * n_pool
    g_mla = {n: [None] * n_mla for n in ("w_in", "q_norm", "kv_norm", "w_q", "w_kv", "qhn", "khn", "w_out")}
    for i in reversed(range(depth)):
        rec = saved[i]
        hb, act, to_dg, to_du = rec["ffn2"]
        dy, g_norm["ffn2_norm"][i], dyb, dgt, dup = _ffn_bwd_dgrad(
            rec["x_ffn2"], row(ffn2_norm, i), dy, to_dg, to_du, w_gu, w_dn, i_gate2(i), i_up2(i), i_dn2(i))
        g_gu[i_gate2(i)], g_gu[i_up2(i)], g_dn[i_dn2(i)] = _ffn_wgrad(hb, dyb, dgt, dup, act)
        j = i // 2
        if i % 2 == 0:
            dy, g_norm["mix_norm"][i], g_pool_w[j], g_pool_scale[j] = _pool_bwd(
                rec["x_mix"], row(mix_norm, i), pool_full[j], row(pool_scale, j), dy)
        else:
            do, delta, g_mla["w_out"][j] = _mla_out_bwd(dy, rec["ot"], w_out_full[j])
            by_tile = lambda a: a.reshape(n_heads, n_attn, 1, t_attn)
            dqt, dk, dv = _flash_bwd(rec["q"], rec["k"], jnp.transpose(rec["k"], (0, 2, 1)), rec["v"], do,
                                     by_tile(rec["lse"]), by_tile(delta))
            dq = jnp.transpose(dqt, (0, 1, 3, 2)).reshape(n_heads, s, QK_HEAD)
            (dy, g_norm["mix_norm"][i], g_mla["w_in"][j], g_mla["q_norm"][j], g_mla["kv_norm"][j], dwq, dwkn, dwv,
             g_mla["qhn"][j], g_mla["khn"][j]) = _mla_qkv_bwd(
                rec["x_mix"], rec["lat"], dy, dq, dk, dv, row(mix_norm, i), w_in_full[j], q_norm_full[j],
                kv_norm_full[j], w_q_heads[j], w_kn_heads[j], w_v_full[j], row(mla_q_head_norm, j),
                row(mla_k_head_norm, j), cos_t, sin_t)
            g_mla["w_q"][j] = jnp.transpose(dwq, (1, 0, 2)).reshape(q_lora, n_heads * QK_HEAD)
            g_mla["w_kv"][j] = jnp.concatenate(
                [jnp.transpose(dwkn, (1, 0, 2)), dwv.reshape(kv_lora, n_heads, V_HEAD)], axis=-1
            ).reshape(kv_lora, n_heads * (QK_NOPE + V_HEAD))
        hb, act, to_dg, to_du = rec["ffn1"]
        dy, g_norm["ffn1_norm"][i], dyb, dgt, dup = _ffn_bwd_dgrad(
            rec["x_ffn1"], row(ffn1_norm, i), dy, to_dg, to_du, w_gu, w_dn, i_gate1(i), i_up1(i), i_dn1(i))
        g_gu[i_gate1(i)], g_gu[i_up1(i)], g_dn[i_dn1(i)] = _ffn_wgrad(hb, dyb, dgt, dup, act)
    grad_x = dy.reshape(x.shape)

    full_grads = [
        jnp.stack(g_gu, axis=1),
        jnp.stack(g_dn, axis=1),
        _columns_to_chips(jnp.stack(g_mla["w_in"])),
        _columns_to_chips(jnp.stack(g_mla["w_q"])),
        _columns_to_chips(jnp.stack(g_mla["w_kv"])),
        jnp.transpose(jnp.stack(g_mla["w_out"]).reshape(n_mla, N_CHIPS, -1, d), (1, 0, 2, 3)),
        jnp.transpose(jnp.stack(g_pool_w).reshape(n_pool, n_groups, N_CHIPS, pool_c // N_CHIPS, pool_c),
                      (2, 0, 1, 3, 4)).reshape(N_CHIPS, n_pool * n_groups, pool_c // N_CHIPS, pool_c),
    ]
    from_sibling = _send_other_half_to_sibling(full_grads)
    chip_sums = [_add_sibling_half(g, r, chip_arr, core_arr) for g, r in zip(full_grads, from_sibling)]
    from_chips = _send_blocks_to_chips([wire for wire, _ in chip_sums])
    half_sums = [_add_chip_blocks(own, r, core_arr) for (_, own), r in zip(chip_sums, from_chips)]
    r_gu, r_dn, r_in, r_qup, r_kvup, r_out, r_pool = _join_halves_with_sibling(half_sums)

    small_grads = [jnp.concatenate(g_norm["ffn1_norm"]), jnp.concatenate(g_norm["mix_norm"]),
                   jnp.concatenate(g_norm["ffn2_norm"]), jnp.concatenate(g_pool_scale),
                   jnp.concatenate(g_mla["qhn"]), jnp.concatenate(g_mla["khn"]),
                   jnp.concatenate(g_mla["q_norm"]), jnp.concatenate(g_mla["kv_norm"])]
    small_sum = _sum_over_devices(_gather_over_devices(_pack_rows(small_grads)))
    (s_ffn1, s_mix, s_ffn2, s_pscale, s_qhn, s_khn, s_qn, s_kvn) = _unpack_rows(small_sum, [g.shape for g in small_grads])
    qn_w, kvn_w = mla_q_norm.shape[1], mla_kv_norm.shape[1]
    s_qn = lax.dynamic_slice_in_dim(s_qn, chip * qn_w, qn_w, axis=1)
    s_kvn = lax.dynamic_slice_in_dim(s_kvn, chip * kvn_w, kvn_w, axis=1)

    grads, deltas, new_m, new_v = {}, {}, {}, {}

    def stacked(name, reduced, offset):
        shape = w[name].shape
        as3 = lambda a: a.reshape((-1,) + shape[-2:])
        out = _adamw_stacked(as3(w[name]), as3(mom[name]), as3(var[name]), reduced, offset)
        grads[name], deltas[name], new_m[name], new_v[name] = [o.reshape(shape) for o in out]

    def small_update(name, g):
        grads[name] = g
        deltas[name], new_m[name], new_v[name] = _adamw_small(w[name], mom[name], var[name], g)

    stacked("ffn1_w_gate", r_gu, 0)
    stacked("ffn1_w_up", r_gu, depth)
    stacked("ffn2_w_gate", r_gu, 2 * depth)
    stacked("ffn2_w_up", r_gu, 3 * depth)
    stacked("ffn1_w_down", r_dn, 0)
    stacked("ffn2_w_down", r_dn, depth)
    stacked("mla_w_in", r_in, 0)
    stacked("mla_w_q_up", r_qup, 0)
    stacked("mla_w_kv_up", r_kvup, 0)
    stacked("mla_w_out", r_out, 0)
    stacked("pool_w", r_pool, 0)
    small_update("ffn1_norm", s_ffn1)
    small_update("mix_norm", s_mix)
    small_update("ffn2_norm", s_ffn2)
    small_update("pool_scale", s_pscale)
    small_update("mla_q_head_norm", s_qhn)
    small_update("mla_k_head_norm", s_khn)
    small_update("mla_q_norm", s_qn)
    small_update("mla_kv_norm", s_kvn)

    return (loss, grad_x, *[grads[n] for n in WEIGHT_NAMES], *[deltas[n] for n in WEIGHT_NAMES],
            *[new_m[n] for n in WEIGHT_NAMES], *[new_v[n] for n in WEIGHT_NAMES])
```

```python
import functools

import numpy as np

import jax
import jax.numpy as jnp
from jax import lax
from jax.experimental import pallas as pl
from jax.experimental.pallas import tpu as pltpu

F32 = jnp.float32
MXU_DTYPE = jnp.bfloat16
WIRE_DTYPE = jnp.bfloat16
MESH = pl.DeviceIdType.MESH
N_CHIPS = 4
N_DEVICES = 8
LANES = 128
VMEM_LIMIT_BYTES = 56 * 2**20
NORM_EPS = 1e-6
QK_NOPE, QK_ROPE, V_HEAD = 128, 64, 128
QK_HEAD = QK_NOPE + QK_ROPE
ROPE_THETA = 10000.0
POOL_WINDOWS = (2, 4, 8, 16)
POOL_HALO = 16
FFN_HALF = 0.5
ADAM_LR, ADAM_B1, ADAM_B2, ADAM_EPS, ADAM_WD, ADAM_STEP = 0.001, 0.9, 0.999, 1e-08, 0.01, 10
FFN_TILE = 512
FFN_BWD_TILE = 256
WGRAD_TILE = 2048
MLA_TILE = 256
POOL_TILE = 512
ATTN_TILE = 512
ROW_TILE = 1024


def _cast(v):
    return v.astype(MXU_DTYPE)


def _mm(a, b):
    return jnp.dot(a, b, preferred_element_type=F32)


def _mm_nt(a, b):
    return lax.dot_general(a, b, (((1,), (1,)), ((), ())), preferred_element_type=F32)


def _mm_tn(a, b):
    return lax.dot_general(a, b, (((0,), (0,)), ((), ())), preferred_element_type=F32)


def _rms_fwd(v, gain):
    r = lax.rsqrt(jnp.mean(v * v, axis=-1, keepdims=True) + NORM_EPS)
    return v * r * gain, r


def _rms_bwd(v, r, gain, dy):
    vr = v * r
    gy = dy * gain
    dv = r * (gy - vr * jnp.mean(gy * vr, axis=-1, keepdims=True))
    return dv, jnp.sum(dy * vr, axis=0, keepdims=True)


def _params(semantics=None):
    return pltpu.CompilerParams(dimension_semantics=semantics, vmem_limit_bytes=VMEM_LIMIT_BYTES)


def _tile(n, want):
    t = min(n, want)
    assert n % t == 0, (n, want)
    return t


def _full(shape):
    nd = len(shape)
    return pl.BlockSpec(shape, lambda *_: (0,) * nd)


def _sds(shape, dtype=F32):
    return jax.ShapeDtypeStruct(shape, dtype)


def _ffn_fwd(x, gain, w_gu, w_dn, i_gate, i_up, i_down):
    s, d = x.shape
    fs = w_gu.shape[-1]
    tm = _tile(s, FFN_TILE)

    def body(x_ref, g_ref, wg_ref, wu_ref, wd_ref, y_ref, hb_ref, gate_ref, up_ref):
        h, _ = _rms_fwd(x_ref[...], g_ref[...])
        hb = _cast(h)
        hb_ref[...] = hb
        pre = [(_mm(hb, wg_ref[c]), _mm(hb, wu_ref[c])) for c in range(N_CHIPS)]
        out = None
        for c, (g, u) in enumerate(pre):
            gate_ref[c] = _cast(g)
            up_ref[c] = _cast(u)
            part = _mm(_cast((g * jax.nn.sigmoid(g)) * u), wd_ref[c])
            out = part if out is None else out + part
        y_ref[...] = x_ref[...] + FFN_HALF * out

    resident = dict(pipeline_mode=pl.Buffered(1))
    tok = pl.BlockSpec((tm, d), lambda i: (i, 0))
    chunks = pl.BlockSpec((N_CHIPS, tm, fs), lambda i: (0, i, 0))
    return pl.pallas_call(
        body, name="ffn_fwd", grid=(s // tm,),
        in_specs=[
            tok, _full((1, d)),
            pl.BlockSpec((N_CHIPS, None, d, fs), lambda i: (0, i_gate, 0, 0), **resident),
            pl.BlockSpec((N_CHIPS, None, d, fs), lambda i: (0, i_up, 0, 0), **resident),
            pl.BlockSpec((N_CHIPS, None, fs, d), lambda i: (0, i_down, 0, 0), **resident),
        ],
        out_specs=[tok, tok, chunks, chunks],
        out_shape=[_sds((s, d)), _sds((s, d), MXU_DTYPE), _sds((N_CHIPS, s, fs), MXU_DTYPE),
                   _sds((N_CHIPS, s, fs), MXU_DTYPE)],
        compiler_params=_params(("arbitrary",)),
    )(x, gain, w_gu, w_gu, w_dn)


def _ffn_bwd_dgrad(x, gain, dy, gate, up, w_gu, w_dn, i_gate, i_up, i_down):
    s, d = x.shape
    fs = w_gu.shape[-1]
    tm = _tile(s, FFN_BWD_TILE)

    def body(x_ref, g_ref, dy_ref, gate_ref, up_ref, wg_ref, wu_ref, wd_ref,
             dx_ref, dgain_ref, dyb_ref, dg_ref, du_ref, act_ref):
        i = pl.program_id(0)
        dyb = _cast(dy_ref[...])
        dyb_ref[...] = dyb
        dacts = [_mm_nt(dyb, wd_ref[c]) for c in range(N_CHIPS)]
        dh = None
        for c in range(N_CHIPS):
            g = gate_ref[c].astype(F32)
            u = up_ref[c].astype(F32)
            sg = jax.nn.sigmoid(g)
            silu = g * sg
            dact = FFN_HALF * dacts[c]
            dgb = _cast(dact * u * (sg * (1.0 + g * (1.0 - sg))))
            dub = _cast(dact * silu)
            dg_ref[c] = dgb
            du_ref[c] = dub
            act_ref[c] = _cast(silu * u)
            part = _mm_nt(dgb, wg_ref[c]) + _mm_nt(dub, wu_ref[c])
            dh = part if dh is None else dh + part
        _, r = _rms_fwd(x_ref[...], g_ref[...])
        dxn, dgn = _rms_bwd(x_ref[...], r, g_ref[...], dh)
        dx_ref[...] = dy_ref[...] + dxn

        @pl.when(i == 0)
        def _():
            dgain_ref[...] = dgn

        @pl.when(i > 0)
        def _():
            dgain_ref[...] += dgn

    resident = dict(pipeline_mode=pl.Buffered(1))
    tok = pl.BlockSpec((tm, d), lambda i: (i, 0))
    chunks = pl.BlockSpec((N_CHIPS, tm, fs), lambda i: (0, i, 0))
    return pl.pallas_call(
        body, name="ffn_bwd_dgrad", grid=(s // tm,),
        in_specs=[
            tok, _full((1, d)), tok, chunks, chunks,
            pl.BlockSpec((N_CHIPS, None, d, fs), lambda i: (0, i_gate, 0, 0), **resident),
            pl.BlockSpec((N_CHIPS, None, d, fs), lambda i: (0, i_up, 0, 0), **resident),
            pl.BlockSpec((N_CHIPS, None, fs, d), lambda i: (0, i_down, 0, 0), **resident),
        ],
        out_specs=[tok, _full((1, d)), tok, chunks, chunks, chunks],
        out_shape=[_sds((s, d)), _sds((1, d)), _sds((s, d), MXU_DTYPE),
                   _sds((N_CHIPS, s, fs), MXU_DTYPE), _sds((N_CHIPS, s, fs), MXU_DTYPE),
                   _sds((N_CHIPS, s, fs), MXU_DTYPE)],
        compiler_params=_params(("arbitrary",)),
    )(x, gain, dy, gate, up, w_gu, w_gu, w_dn)


def _ffn_wgrad(hb, dyb, dg, du, act):
    s, d = hb.shape
    fs = dg.shape[-1]
    tk = _tile(s, WGRAD_TILE)
    n_k = s // tk

    def body(h_ref, dy_ref, dg_ref, du_ref, act_ref, wg_ref, wu_ref, wd_ref):
        k = pl.program_id(1)

        @pl.when(k == 0)
        def _():
            wg_ref[...] = jnp.zeros_like(wg_ref)
            wu_ref[...] = jnp.zeros_like(wu_ref)
            wd_ref[...] = jnp.zeros_like(wd_ref)

        h = h_ref[...]
        wg_ref[...] += _mm_tn(h, dg_ref[...])
        wu_ref[...] += _mm_tn(h, du_ref[...])
        wd_ref[...] += FFN_HALF * _mm_tn(act_ref[...], dy_ref[...])

    tok = pl.BlockSpec((tk, d), lambda j, k: (k, 0))
    chunk = pl.BlockSpec((None, tk, fs), lambda j, k: (j, k, 0))
    return pl.pallas_call(
        body, name="ffn_wgrad", grid=(N_CHIPS, n_k),
        in_specs=[tok, tok, chunk, chunk, chunk],
        out_specs=[pl.BlockSpec((None, d, fs), lambda j, k: (j, 0, 0)),
                   pl.BlockSpec((None, d, fs), lambda j, k: (j, 0, 0)),
                   pl.BlockSpec((None, fs, d), lambda j, k: (j, 0, 0))],
        out_shape=[_sds((N_CHIPS, d, fs)), _sds((N_CHIPS, d, fs)), _sds((N_CHIPS, fs, d))],
        compiler_params=_params(("arbitrary", "arbitrary")),
    )(hb, dyb, dg, du, act)


def _inv_count(first_row, n_rows, window):
    t = first_row + lax.broadcasted_iota(jnp.int32, (n_rows, 1), 0)
    return 1.0 / jnp.minimum(t + 1, window).astype(F32)


def _trailing_sum(v, window):
    k = 1
    while k < window:
        v = v + pltpu.roll(v, k, 0)
        k *= 2
    return v


def _leading_sum(v, window):
    n = v.shape[0]
    k = 1
    while k < window:
        v = v + pltpu.roll(v, n - k, 0)
        k *= 2
    return v


def _pool_normed_rows(x_ref, prev_ref, g_ref, i):
    h, r = _rms_fwd(x_ref[...], g_ref[...])
    hp, _ = _rms_fwd(prev_ref[...], g_ref[...])
    hp = jnp.where(i > 0, hp, 0.0)
    return jnp.concatenate([hp, h], axis=0), r


def _pooled_group(he, g, pg, first_row, tm):
    ue = he[:, g * pg:(g + 1) * pg]
    win = _trailing_sum(ue, POOL_WINDOWS[g])[POOL_HALO:]
    return win * _inv_count(first_row, tm, POOL_WINDOWS[g]) - ue[POOL_HALO:]


def _pool_specs(s, d, tm):
    per = tm // POOL_HALO
    last = s // POOL_HALO - 1
    tok = pl.BlockSpec((tm, d), lambda i: (i, 0))
    prev = pl.BlockSpec((POOL_HALO, d), lambda i: (jnp.maximum(i * per - 1, 0), 0))
    nxt = pl.BlockSpec((POOL_HALO, d), lambda i: (jnp.minimum((i + 1) * per, last), 0))
    return tok, prev, nxt


def _pool_fwd(x, gain, w, scale):
    s, d = x.shape
    n_g, pg = w.shape[0], w.shape[-1]
    tm = _tile(s, POOL_TILE)
    tok, prev, _ = _pool_specs(s, d, tm)

    def body(x_ref, prev_ref, g_ref, w_ref, sc_ref, y_ref):
        i = pl.program_id(0)
        he, _ = _pool_normed_rows(x_ref, prev_ref, g_ref, i)
        z = [_mm(_cast(_pooled_group(he, g, pg, i * tm, tm)), w_ref[g]) for g in range(n_g)]
        y_ref[...] = x_ref[...] + jnp.concatenate(z, axis=-1) * sc_ref[...]

    return pl.pallas_call(
        body, name="pool_fwd", grid=(s // tm,),
        in_specs=[tok, prev, _full((1, d)), _full(w.shape), _full((1, d))],
        out_specs=tok, out_shape=_sds((s, d)),
        compiler_params=_params(("arbitrary",)),
    )(x, x, gain, w, scale)


def _pool_bwd(x, gain, w, scale, dy):
    s, d = x.shape
    n_g, pg = w.shape[0], w.shape[-1]
    tm = _tile(s, POOL_TILE)
    n_tiles = s // tm
    tok, prev, nxt = _pool_specs(s, d, tm)

    def body(x_ref, prev_ref, dy_ref, next_ref, g_ref, w_ref, sc_ref, dx_ref, dgain_ref, dw_ref, dsc_ref):
        i = pl.program_id(0)

        @pl.when(i == 0)
        def _():
            dgain_ref[...] = jnp.zeros_like(dgain_ref)
            dw_ref[...] = jnp.zeros_like(dw_ref)
            dsc_ref[...] = jnp.zeros_like(dsc_ref)

        he, r = _pool_normed_rows(x_ref, prev_ref, g_ref, i)
        dy = dy_ref[...]
        dyn = jnp.where(i < n_tiles - 1, next_ref[...], 0.0)
        dze = jnp.concatenate([dy, dyn], axis=0) * sc_ref[...]
        dh, dsc = [], []
        for g in range(n_g):
            cols = slice(g * pg, (g + 1) * pg)
            pooled = _cast(_pooled_group(he, g, pg, i * tm, tm))
            dsc.append(jnp.sum(dy[:, cols] * _mm(pooled, w_ref[g]), axis=0, keepdims=True))
            dzb = _cast(dze[:, cols])
            dw_ref[g] += _mm_tn(pooled, dzb[:tm])
            dpool = _mm_nt(dzb, w_ref[g])
            spread = _leading_sum(dpool * _inv_count(i * tm, tm + POOL_HALO, POOL_WINDOWS[g]), POOL_WINDOWS[g])
            dh.append(spread[:tm] - dpool[:tm])
        dsc_ref[...] += jnp.concatenate(dsc, axis=-1)
        dxn, dgn = _rms_bwd(x_ref[...], r, g_ref[...], jnp.concatenate(dh, axis=-1))
        dgain_ref[...] += dgn
        dx_ref[...] = dy + dxn

    return pl.pallas_call(
        body, name="pool_bwd", grid=(n_tiles,),
        in_specs=[tok, prev, tok, nxt, _full((1, d)), _full(w.shape), _full((1, d))],
        out_specs=[tok, _full((1, d)), _full(w.shape), _full((1, d))],
        out_shape=[_sds((s, d)), _sds((1, d)), _sds(w.shape), _sds((1, d))],
        compiler_params=_params(("arbitrary",)),
    )(x, x, dy, dy, gain, w, scale)


def _rope_tables(pos_col, inv_freq):
    s = pos_col.shape[0]
    tm = _tile(s, ROW_TILE)
    half = QK_ROPE // 2

    def body(p_ref, f_ref, c_ref, s_ref):
        ang = p_ref[...].astype(F32) * f_ref[...]
        cos, sin = jnp.cos(ang), jnp.sin(ang)
        c_ref[...] = jnp.concatenate([jnp.ones((tm, QK_NOPE), F32), cos, cos], axis=-1)
        s_ref[...] = jnp.concatenate([jnp.zeros((tm, QK_NOPE), F32), -sin, sin], axis=-1)

    tab = pl.BlockSpec((tm, QK_HEAD), lambda i: (i, 0))
    return pl.pallas_call(
        body, name="rope_tables", grid=(s // tm,),
        in_specs=[pl.BlockSpec((tm, 1), lambda i: (i, 0)), _full((1, half))],
        out_specs=[tab, tab], out_shape=[_sds((s, QK_HEAD)), _sds((s, QK_HEAD))],
        compiler_params=_params(("arbitrary",)),
    )(pos_col, inv_freq)


def _swap_rope_halves(v):
    half = QK_ROPE // 2
    return jnp.concatenate([v[:, :QK_NOPE], v[:, QK_NOPE + half:], v[:, QK_NOPE:QK_NOPE + half]], axis=-1)


def _rope(v, cos, sin):
    return v * cos + _swap_rope_halves(v) * sin


def _rope_transposed(dv, cos, sin):
    return dv * cos + _swap_rope_halves(dv * sin)


def _mla_qkv_fwd(x, gain, w_in, q_norm, kv_norm, w_q, w_kn, w_v, q_head_norm, k_head_norm, cos, sin):
    s, d = x.shape
    n_h, ql = w_q.shape[0], w_q.shape[1]
    kvl, lat_w = w_kn.shape[1], w_in.shape[1]
    tm = _tile(s, MLA_TILE)

    def body(x_ref, g_ref, win_ref, qn_ref, kvn_ref, wq_ref, wkn_ref, wv_ref, qhn_ref, khn_ref, c_ref, s_ref,
             lat_ref, q_ref, k_ref, v_ref):
        h, _ = _rms_fwd(x_ref[...], g_ref[...])
        lat = _mm(_cast(h), win_ref[...])
        lat_ref[...] = lat
        cqn, _ = _rms_fwd(lat[:, :ql], qn_ref[...])
        ckvn, _ = _rms_fwd(lat[:, ql:ql + kvl], kvn_ref[...])
        kpe = lat[:, ql + kvl:]
        cqb, ckb = _cast(cqn), _cast(ckvn)
        cos_t, sin_t = c_ref[...], s_ref[...]
        v_ref[...] = _cast(_mm(ckb, wv_ref[...]))
        for hh in range(n_h):
            qn, _ = _rms_fwd(_mm(cqb, wq_ref[hh]), qhn_ref[...])
            q_ref[hh] = _cast(_rope(qn, cos_t, sin_t))
            kn, _ = _rms_fwd(jnp.concatenate([_mm(ckb, wkn_ref[hh]), kpe], axis=-1), khn_ref[...])
            k_ref[hh] = _cast(_rope(kn, cos_t, sin_t))

    tok = lambda w: pl.BlockSpec((tm, w), lambda i: (i, 0))
    heads = pl.BlockSpec((n_h, tm, QK_HEAD), lambda i: (0, i, 0))
    return pl.pallas_call(
        body, name="mla_qkv_fwd", grid=(s // tm,),
        in_specs=[tok(d), _full((1, d)), _full(w_in.shape), _full((1, ql)), _full((1, kvl)), _full(w_q.shape),
                  _full(w_kn.shape), _full(w_v.shape), _full((1, QK_HEAD)), _full((1, QK_HEAD)),
                  tok(QK_HEAD), tok(QK_HEAD)],
        out_specs=[tok(lat_w), heads, heads, tok(n_h * V_HEAD)],
        out_shape=[_sds((s, lat_w)), _sds((n_h, s, QK_HEAD), MXU_DTYPE), _sds((n_h, s, QK_HEAD), MXU_DTYPE),
                   _sds((s, n_h * V_HEAD), MXU_DTYPE)],
        compiler_params=_params(("arbitrary",)),
    )(x, gain, w_in, q_norm, kv_norm, w_q, w_kn, w_v, q_head_norm, k_head_norm, cos, sin)


def _mla_qkv_bwd(x, lat, dy, dq, dk, dv, gain, w_in, q_norm, kv_norm, w_q, w_kn, w_v, q_head_norm, k_head_norm,
                 cos, sin):
    s, d = x.shape
    n_h, ql = w_q.shape[0], w_q.shape[1]
    kvl, lat_w = w_kn.shape[1], w_in.shape[1]
    tm = _tile(s, MLA_TILE)

    def body(x_ref, lat_ref, dy_ref, dq_ref, dk_ref, dv_ref, g_ref, win_ref, qn_ref, kvn_ref, wq_ref, wkn_ref,
             wv_ref, qhn_ref, khn_ref, c_ref, s_ref,
             dx_ref, dg_ref, dwin_ref, dqn_ref, dkvn_ref, dwq_ref, dwkn_ref, dwv_ref, dqhn_ref, dkhn_ref):
        @pl.when(pl.program_id(0) == 0)
        def _():
            for ref in (dg_ref, dwin_ref, dqn_ref, dkvn_ref, dwq_ref, dwkn_ref, dwv_ref, dqhn_ref, dkhn_ref):
                ref[...] = jnp.zeros_like(ref)

        x_t = x_ref[...]
        h, r = _rms_fwd(x_t, g_ref[...])
        hb = _cast(h)
        lat = lat_ref[...]
        cq, ckv, kpe = lat[:, :ql], lat[:, ql:ql + kvl], lat[:, ql + kvl:]
        cqn, rq = _rms_fwd(cq, qn_ref[...])
        ckvn, rkv = _rms_fwd(ckv, kvn_ref[...])
        cqb, ckb = _cast(cqn), _cast(ckvn)
        cos_t, sin_t = c_ref[...], s_ref[...]

        dvb = _cast(dv_ref[...])
        dwv_ref[...] += _mm_tn(ckb, dvb)
        dckvn = _mm_nt(dvb, wv_ref[...])
        dcqn = jnp.zeros((tm, ql), F32)
        dkpe = jnp.zeros((tm, QK_ROPE), F32)
        dqhn = jnp.zeros((1, QK_HEAD), F32)
        dkhn = jnp.zeros((1, QK_HEAD), F32)
        for hh in range(n_h):
            qp = _mm(cqb, wq_ref[hh])
            _, rqp = _rms_fwd(qp, qhn_ref[...])
            dqp, dgq = _rms_bwd(qp, rqp, qhn_ref[...], _rope_transposed(dq_ref[hh], cos_t, sin_t))
            dqhn += dgq
            dqpb = _cast(dqp)
            dwq_ref[hh] += _mm_tn(cqb, dqpb)
            dcqn += _mm_nt(dqpb, wq_ref[hh])

            kp = jnp.concatenate([_mm(ckb, wkn_ref[hh]), kpe], axis=-1)
            _, rkp = _rms_fwd(kp, khn_ref[...])
            dkp, dgk = _rms_bwd(kp, rkp, khn_ref[...], _rope_transposed(dk_ref[hh], cos_t, sin_t))
            dkhn += dgk
            dknb = _cast(dkp[:, :QK_NOPE])
            dkpe += dkp[:, QK_NOPE:]
            dwkn_ref[hh] += _mm_tn(ckb, dknb)
            dckvn += _mm_nt(dknb, wkn_ref[hh])
        dqhn_ref[...] += dqhn
        dkhn_ref[...] += dkhn

        dcq, dgn = _rms_bwd(cq, rq, qn_ref[...], dcqn)
        dqn_ref[...] += dgn
        dckv, dgn = _rms_bwd(ckv, rkv, kvn_ref[...], dckvn)
        dkvn_ref[...] += dgn
        dlb = _cast(jnp.concatenate([dcq, dckv, dkpe], axis=-1))
        dwin_ref[...] += _mm_tn(hb, dlb)
        dxn, dgn = _rms_bwd(x_t, r, g_ref[...], _mm_nt(dlb, win_ref[...]))
        dg_ref[...] += dgn
        dx_ref[...] = dy_ref[...] + dxn

    tok = lambda w: pl.BlockSpec((tm, w), lambda i: (i, 0))
    heads = pl.BlockSpec((n_h, tm, QK_HEAD), lambda i: (0, i, 0))
    return pl.pallas_call(
        body, name="mla_qkv_bwd", grid=(s // tm,),
        in_specs=[tok(d), tok(lat_w), tok(d), heads, heads, tok(n_h * V_HEAD), _full((1, d)), _full(w_in.shape),
                  _full((1, ql)), _full((1, kvl)), _full(w_q.shape), _full(w_kn.shape), _full(w_v.shape),
                  _full((1, QK_HEAD)), _full((1, QK_HEAD)), tok(QK_HEAD), tok(QK_HEAD)],
        out_specs=[tok(d), _full((1, d)), _full(w_in.shape), _full((1, ql)), _full((1, kvl)), _full(w_q.shape),
                   _full(w_kn.shape), _full(w_v.shape), _full((1, QK_HEAD)), _full((1, QK_HEAD))],
        out_shape=[_sds((s, d)), _sds((1, d)), _sds(w_in.shape), _sds((1, ql)), _sds((1, kvl)), _sds(w_q.shape),
                   _sds(w_kn.shape), _sds(w_v.shape), _sds((1, QK_HEAD)), _sds((1, QK_HEAD))],
        compiler_params=_params(("arbitrary",)),
    )(x, lat, dy, dq, dk, dv, gain, w_in, q_norm, kv_norm, w_q, w_kn, w_v, q_head_norm, k_head_norm, cos, sin)


def _scores_t(k_t, q_t):
    return _mm_nt(k_t, q_t) * (QK_HEAD ** -0.5)


def _mask_above_diagonal(z, t):
    key = lax.broadcasted_iota(jnp.int32, (t, t), 0)
    query = lax.broadcasted_iota(jnp.int32, (t, t), 1)
    return jnp.where(key <= query, z, -jnp.inf)


def _flash_fwd(q, k, vt):
    n_h, s, _ = q.shape
    t = _tile(s, ATTN_TILE)
    n = s // t

    def body(q_ref, k_ref, vt_ref, ot_ref, lse_ref, m_sc, l_sc, acc_sc, z_sc):
        i = pl.program_id(1)
        m_sc[...] = jnp.full_like(m_sc, -jnp.inf)
        l_sc[...] = jnp.zeros_like(l_sc)
        acc_sc[...] = jnp.zeros_like(acc_sc)
        q_t = q_ref[...]

        def fetch(j, slot):
            z_sc[slot] = _scores_t(k_ref[pl.ds(pl.multiple_of(j * t, t), t), :], q_t)

        def stage(j, slot, masked, prefetch=True):
            if prefetch:
                fetch(j + 1, 1 - slot)
            z = _mask_above_diagonal(z_sc[slot], t) if masked else z_sc[slot]
            m_old = m_sc[...]
            m_new = jnp.maximum(m_old, jnp.max(z, axis=0, keepdims=True))
            alpha = jnp.exp(m_old - m_new)
            pr = jnp.exp(z - m_new)
            l_sc[...] = alpha * l_sc[...] + jnp.sum(pr, axis=0, keepdims=True)
            acc_sc[...] = alpha * acc_sc[...] + _mm(vt_ref[j], _cast(pr))
            m_sc[...] = m_new

        def pair_below_diagonal(pair, carry):
            stage(2 * pair, 0, False)
            stage(2 * pair + 1, 1, False)
            return carry

        fetch(0, 0)
        lax.fori_loop(0, i >> 1, pair_below_diagonal, 0)

        @pl.when((i & 1) == 1)
        def _():
            stage(i - 1, 0, False)
            stage(i, 1, True, prefetch=False)

        @pl.when((i & 1) == 0)
        def _():
            stage(i, 0, True, prefetch=False)

        ot_ref[...] = acc_sc[...] / l_sc[...]
        lse_ref[...] = m_sc[...] + jnp.log(l_sc[...])

    whole_head = dict(pipeline_mode=pl.Buffered(1))
    return pl.pallas_call(
        body, name="flash_fwd", grid=(n_h, n),
        in_specs=[pl.BlockSpec((None, t, QK_HEAD), lambda h, i: (h, i, 0)),
                  pl.BlockSpec((None, s, QK_HEAD), lambda h, i: (h, 0, 0), **whole_head),
                  pl.BlockSpec((None, n, V_HEAD, t), lambda h, i: (h, 0, 0, 0), **whole_head)],
        out_specs=[pl.BlockSpec((V_HEAD, t), lambda h, i: (h, i)),
                   pl.BlockSpec((None, 1, t), lambda h, i: (h, 0, i))],
        out_shape=[_sds((n_h * V_HEAD, s)), _sds((n_h, 1, s))],
        scratch_shapes=[pltpu.VMEM((1, t), F32), pltpu.VMEM((1, t), F32), pltpu.VMEM((V_HEAD, t), F32),
                        pltpu.VMEM((2, t, t), F32)],
        compiler_params=_params(("arbitrary", "arbitrary")),
    )(q, k, vt)


def _flash_bwd(q, k, kt, v, do, lse, delta):
    n_h, s, _ = q.shape
    t = _tile(s, ATTN_TILE)
    n = s // t

    def body(q_ref, do_ref, lse_ref, dl_ref, k_ref, kt_ref, v_ref, dqt_ref, dk_ref, dv_ref, dk_sc, dv_sc, z_sc, dp_sc):
        j = pl.program_id(1)

        @pl.when(j == 0)
        def _():
            dqt_ref[...] = jnp.zeros_like(dqt_ref)

        dk_sc[...] = jnp.zeros_like(dk_sc)
        dv_sc[...] = jnp.zeros_like(dv_sc)
        k_t, kt_t, v_t = k_ref[...], kt_ref[...], v_ref[...]

        def rows(i):
            return pl.ds(pl.multiple_of(i * t, t), t)

        def fetch(i, slot):
            i = jnp.minimum(i, n - 1)
            z_sc[slot] = _scores_t(k_t, q_ref[rows(i), :])
            dp_sc[slot] = _mm_nt(v_t, do_ref[rows(i), :])

        def stage(i, slot, masked, prefetch=True):
            if prefetch:
                fetch(i + 1, 1 - slot)
            z = _mask_above_diagonal(z_sc[slot], t) if masked else z_sc[slot]
            pr = jnp.exp(z - lse_ref[i])
            dsb = _cast(pr * (dp_sc[slot] - dl_ref[i]) * (QK_HEAD ** -0.5))
            dv_sc[...] += _mm(_cast(pr), do_ref[rows(i), :])
            dk_sc[...] += _mm(dsb, q_ref[rows(i), :])
            dqt_ref[i] += _mm(kt_t, dsb)

        def pair_below_diagonal(pair, carry):
            stage(j + 1 + 2 * pair, 1, False)
            stage(j + 2 + 2 * pair, 0, False)
            return carry

        below = n - 1 - j
        fetch(j, 0)
        stage(j, 0, True)
        lax.fori_loop(0, below >> 1, pair_below_diagonal, 0)

        @pl.when((below & 1) == 1)
        def _():
            stage(n - 1, 1, False, prefetch=False)

        dk_ref[...] = dk_sc[...]
        dv_ref[...] = dv_sc[...]

    whole_head = dict(pipeline_mode=pl.Buffered(1))
    stat = pl.BlockSpec((None, n, 1, t), lambda h, j: (h, 0, 0, 0))
    return pl.pallas_call(
        body, name="flash_bwd", grid=(n_h, n),
        in_specs=[pl.BlockSpec((None, s, QK_HEAD), lambda h, j: (h, 0, 0), **whole_head),
                  pl.BlockSpec((s, V_HEAD), lambda h, j: (0, h), **whole_head),
                  stat, stat,
                  pl.BlockSpec((None, t, QK_HEAD), lambda h, j: (h, j, 0)),
                  pl.BlockSpec((None, QK_HEAD, t), lambda h, j: (h, 0, j)),
                  pl.BlockSpec((t, V_HEAD), lambda h, j: (j, h))],
        out_specs=[pl.BlockSpec((None, n, QK_HEAD, t), lambda h, j: (h, 0, 0, 0)),
                   pl.BlockSpec((None, t, QK_HEAD), lambda h, j: (h, j, 0)),
                   pl.BlockSpec((t, V_HEAD), lambda h, j: (j, h))],
        out_shape=[_sds((n_h, n, QK_HEAD, t)), _sds((n_h, s, QK_HEAD)), _sds((s, n_h * V_HEAD))],
        scratch_shapes=[pltpu.VMEM((t, QK_HEAD), F32), pltpu.VMEM((t, V_HEAD), F32),
                        pltpu.VMEM((2, t, t), F32), pltpu.VMEM((2, t, t), F32)],
        compiler_params=_params(("arbitrary", "arbitrary")),
    )(q, do, lse, delta, k, kt, v)


def _mla_out_fwd(x, ot, w_out):
    s, d = x.shape
    hv = ot.shape[0]
    tm = _tile(s, FFN_TILE)

    def body(x_ref, ot_ref, w_ref, y_ref):
        y_ref[...] = x_ref[...] + _mm_tn(_cast(ot_ref[...]), w_ref[...])

    tok = pl.BlockSpec((tm, d), lambda i: (i, 0))
    return pl.pallas_call(
        body, name="mla_out_fwd", grid=(s // tm,),
        in_specs=[tok, pl.BlockSpec((hv, tm), lambda i: (0, i)), _full(w_out.shape)],
        out_specs=tok, out_shape=_sds((s, d)),
        compiler_params=_params(("arbitrary",)),
    )(x, ot, w_out)


def _mla_out_bwd(dy, ot, w_out):
    s, d = dy.shape
    hv = ot.shape[0]
    n_h = hv // V_HEAD
    tm = _tile(s, FFN_TILE)

    def body(dy_ref, ot_ref, w_ref, do_ref, dl_ref, dw_ref):
        @pl.when(pl.program_id(0) == 0)
        def _():
            dw_ref[...] = jnp.zeros_like(dw_ref)

        dyb = _cast(dy_ref[...])
        o_t = ot_ref[...]
        do_ref[...] = _cast(_mm_nt(dyb, w_ref[...]))
        prod = _mm_nt(w_ref[...], dyb) * o_t
        for hh in range(n_h):
            dl_ref[hh] = jnp.sum(prod[hh * V_HEAD:(hh + 1) * V_HEAD], axis=0, keepdims=True)
        dw_ref[...] += _mm(_cast(o_t), dyb)

    return pl.pallas_call(
        body, name="mla_out_bwd", grid=(s // tm,),
        in_specs=[pl.BlockSpec((tm, d), lambda i: (i, 0)), pl.BlockSpec((hv, tm), lambda i: (0, i)),
                  _full(w_out.shape)],
        out_specs=[pl.BlockSpec((tm, hv), lambda i: (i, 0)), pl.BlockSpec((n_h, 1, tm), lambda i: (0, 0, i)),
                   _full(w_out.shape)],
        out_shape=[_sds((s, hv), MXU_DTYPE), _sds((n_h, 1, s)), _sds(w_out.shape)],
        compiler_params=_params(("arbitrary",)),
    )(dy, ot, w_out)


def _loss_and_grad(y, target):
    s, d = y.shape
    tm = _tile(s, ROW_TILE)

    def body(y_ref, t_ref, loss_ref, dy_ref):
        @pl.when(pl.program_id(0) == 0)
        def _():
            loss_ref[...] = jnp.zeros_like(loss_ref)

        err = y_ref[...] - t_ref[...]
        dy_ref[...] = err * (1.0 / d)
        loss_ref[...] += 0.5 * jnp.sum(jnp.mean(err * err, axis=-1, keepdims=True), axis=0, keepdims=True)

    tok = pl.BlockSpec((tm, d), lambda i: (i, 0))
    return pl.pallas_call(
        body, name="loss_and_grad", grid=(s // tm,),
        in_specs=[tok, tok], out_specs=[_full((1, 1)), tok],
        out_shape=[_sds((1, 1)), _sds((s, d))],
        compiler_params=_params(("arbitrary",)),
    )(y, target)


def _mesh_position():
    return lax.axis_index("x"), lax.axis_index("y"), lax.axis_index("c")


def _other_chips(x, y):
    return [(1 - x, y), (x, 1 - y), (1 - x, 1 - y)]


ANY = pl.BlockSpec(memory_space=pl.ANY)


def _gather_over_chips(arrs):
    n = len(arrs)
    halves = [a.shape[0] // 2 for a in arrs]
    assert all(a.shape[0] % 2 == 0 for a in arrs)
    own = 2 * (N_CHIPS - 1)

    def body(*refs):
        srcs, outs = refs[:n], refs[n:2 * n]
        send_sems, recv_sems = refs[2 * n:]
        x, y, c = _mesh_position()
        me, sibling = (x, y, c), (x, y, 1 - c)
        chips = _other_chips(x, y)
        my_chip = 2 * x + y

        def rows(t, chip, half):
            return outs[t].at[chip, pl.ds(half * halves[t], halves[t])]

        def copy(t, k, src, dst, to):
            return pltpu.make_async_remote_copy(src_ref=src, dst_ref=dst, send_sem=send_sems.at[t, k],
                                                recv_sem=recv_sems.at[t, k], device_id=to, device_id_type=MESH)

        started = []
        for t in range(n):
            for k, (px, py) in enumerate(chips):
                cp = copy(t, k, srcs[t].at[pl.ds(c * halves[t], halves[t])], rows(t, my_chip, c), (px, py, c))
                cp.start()
                started.append(cp)
            cp = copy(t, own, srcs[t], outs[t].at[my_chip], sibling)
            cp.start()
            started.append(cp)
        for t in range(n):
            for k, (px, py) in enumerate(chips):
                landed = rows(t, 2 * px + py, c)
                copy(t, k, landed, landed, me).wait_recv()
                cp = copy(t, N_CHIPS - 1 + k, landed, landed, sibling)
                cp.start()
                started.append(cp)
        for t in range(n):
            for k, (px, py) in enumerate(chips):
                passed = rows(t, 2 * px + py, 1 - c)
                copy(t, N_CHIPS - 1 + k, passed, passed, me).wait_recv()
            copy(t, own, srcs[t], outs[t].at[my_chip], me).wait_recv()
        for cp in started:
            cp.wait_send()

    return pl.pallas_call(
        body, name="gather_over_chips",
        in_specs=[ANY] * n, out_specs=[ANY] * n,
        out_shape=[_sds((N_CHIPS,) + a.shape, a.dtype) for a in arrs],
        scratch_shapes=[pltpu.SemaphoreType.DMA((n, own + 1)), pltpu.SemaphoreType.DMA((n, own + 1))],
    )(*arrs)


def _send_other_half_to_sibling(grads):
    n = len(grads)
    halves = [g.shape[1] // 2 for g in grads]

    def body(*refs):
        srcs, outs = refs[:n], refs[n:2 * n]
        send_sems, recv_sems = refs[2 * n:]
        x, y, c = _mesh_position()
        copies = []
        for t in range(n):
            cp = pltpu.make_async_remote_copy(
                src_ref=srcs[t].at[pl.ds(0, N_CHIPS), pl.ds((1 - c) * halves[t], halves[t])], dst_ref=outs[t],
                send_sem=send_sems.at[t], recv_sem=recv_sems.at[t], device_id=(x, y, 1 - c), device_id_type=MESH)
            cp.start()
            copies.append(cp)
        for cp in copies:
            cp.wait_recv()
        for cp in copies:
            cp.wait_send()

    return pl.pallas_call(
        body, name="send_other_half_to_sibling",
        in_specs=[ANY] * n, out_specs=[ANY] * n,
        out_shape=[_sds((N_CHIPS, h) + g.shape[2:]) for g, h in zip(grads, halves)],
        scratch_shapes=[pltpu.SemaphoreType.DMA((n,)), pltpu.SemaphoreType.DMA((n,))],
    )(*grads)


def _send_blocks_to_chips(parts):
    n = len(parts)

    def body(*refs):
        srcs, outs = refs[:n], refs[n:2 * n]
        send_sems, recv_sems = refs[2 * n:]
        x, y, c = _mesh_position()
        copies = []
        for t in range(n):
            for k, (px, py) in enumerate(_other_chips(x, y)):
                cp = pltpu.make_async_remote_copy(
                    src_ref=srcs[t].at[2 * px + py], dst_ref=outs[t].at[k], send_sem=send_sems.at[t, k],
                    recv_sem=recv_sems.at[t, k], device_id=(px, py, c), device_id_type=MESH)
                cp.start()
                copies.append(cp)
        for cp in copies:
            cp.wait_recv()
        for cp in copies:
            cp.wait_send()

    return pl.pallas_call(
        body, name="send_blocks_to_chips",
        in_specs=[ANY] * n, out_specs=[ANY] * n,
        out_shape=[_sds((N_CHIPS - 1,) + p.shape[1:], p.dtype) for p in parts],
        scratch_shapes=[pltpu.SemaphoreType.DMA((n, N_CHIPS - 1)), pltpu.SemaphoreType.DMA((n, N_CHIPS - 1))],
    )(*parts)


def _join_halves_with_sibling(sums):
    n = len(sums)

    def body(*refs):
        srcs, outs = refs[:n], refs[n:2 * n]
        send_sems, recv_sems = refs[2 * n:]
        x, y, c = _mesh_position()
        copies = []
        for t in range(n):
            h = srcs[t].shape[0] // 2
            mine = pl.ds(c * h, h)
            cp = pltpu.make_async_remote_copy(
                src_ref=srcs[t].at[mine], dst_ref=outs[t].at[mine], send_sem=send_sems.at[t],
                recv_sem=recv_sems.at[t], device_id=(x, y, 1 - c), device_id_type=MESH)
            cp.start()
            copies.append(cp)
        for t in range(n):
            h = srcs[t].shape[0] // 2
            theirs = pl.ds((1 - c) * h, h)
            pltpu.make_async_remote_copy(
                src_ref=srcs[t].at[theirs], dst_ref=outs[t].at[theirs], send_sem=send_sems.at[t],
                recv_sem=recv_sems.at[t], device_id=(x, y, 1 - c), device_id_type=MESH).wait_recv()
        for cp in copies:
            cp.wait_send()

    return pl.pallas_call(
        body, name="join_halves_with_sibling",
        in_specs=[ANY] * n, out_specs=[ANY] * n,
        out_shape=[_sds(a.shape) for a in sums],
        input_output_aliases={t: t for t in range(n)},
        scratch_shapes=[pltpu.SemaphoreType.DMA((n,)), pltpu.SemaphoreType.DMA((n,))],
    )(*sums)


def _gather_over_devices(rows):
    r = rows.shape[0]

    def body(in_ref, out_ref, send_sems, recv_sems, local_sem):
        x, y, c = _mesh_position()
        mine = pltpu.make_async_copy(in_ref, out_ref.at[4 * x + 2 * y + c], local_sem)
        mine.start()
        copies = []
        for mask in range(1, N_DEVICES):
            fx, fy, fc = (mask >> 2) & 1, (mask >> 1) & 1, mask & 1
            px, py, pc = (1 - x if fx else x), (1 - y if fy else y), (1 - c if fc else c)
            send = pltpu.make_async_remote_copy(
                src_ref=in_ref, dst_ref=out_ref.at[4 * x + 2 * y + c], send_sem=send_sems.at[mask - 1],
                recv_sem=recv_sems.at[mask - 1], device_id=(px, py, pc), device_id_type=MESH)
            send.start()
            recv = pltpu.make_async_remote_copy(
                src_ref=in_ref, dst_ref=out_ref.at[4 * px + 2 * py + pc], send_sem=send_sems.at[mask - 1],
                recv_sem=recv_sems.at[mask - 1], device_id=(px, py, pc), device_id_type=MESH)
            copies.append((send, recv))
        for _, recv in copies:
            recv.wait_recv()
        for send, _ in copies:
            send.wait_send()
        mine.wait()

    vm = pl.BlockSpec(memory_space=pltpu.VMEM)
    return pl.pallas_call(
        body, name="gather_over_devices", in_specs=[vm], out_specs=vm,
        out_shape=_sds((N_DEVICES, r, LANES)),
        scratch_shapes=[pltpu.SemaphoreType.DMA((N_DEVICES - 1,)), pltpu.SemaphoreType.DMA((N_DEVICES - 1,)),
                        pltpu.SemaphoreType.DMA],
    )(rows)


def _add_sibling_half(grad, received, chip, core):
    _, l, r, c = grad.shape
    half = l // 2

    def body(chip_ref, core_ref, g_ref, r_ref, wire_ref, own_ref):
        total = g_ref[...] + r_ref[...]
        wire_ref[...] = total.astype(WIRE_DTYPE)

        @pl.when(pl.program_id(1) == chip_ref[0])
        def _():
            own_ref[...] = total

    blk = lambda f: pl.BlockSpec((None, None, r, c), f)
    grid_spec = pltpu.PrefetchScalarGridSpec(
        num_scalar_prefetch=2, grid=(half, N_CHIPS),
        in_specs=[blk(lambda i, j, chip, core: (j, core[0] * half + i, 0, 0)),
                  blk(lambda i, j, chip, core: (j, i, 0, 0))],
        out_specs=[blk(lambda i, j, chip, core: (j, i, 0, 0)),
                   pl.BlockSpec((None, r, c), lambda i, j, chip, core: (i, 0, 0))])
    return pl.pallas_call(
        body, name="add_sibling_half", grid_spec=grid_spec,
        out_shape=[_sds((N_CHIPS, half, r, c), WIRE_DTYPE), _sds((half, r, c))],
        compiler_params=_params(("arbitrary", "arbitrary")),
    )(chip, core, grad, received)


def _add_chip_blocks(own, received, core):
    half, r, c = own.shape

    def body(core_ref, p_ref, r0_ref, r1_ref, r2_ref, o_ref):
        o_ref[...] = ((p_ref[...] + r0_ref[...].astype(F32)) + r1_ref[...].astype(F32)) + r2_ref[...].astype(F32)

    grid_spec = pltpu.PrefetchScalarGridSpec(
        num_scalar_prefetch=1, grid=(half,),
        in_specs=[pl.BlockSpec((None, r, c), lambda i, core: (i, 0, 0))] + [
            pl.BlockSpec((None, None, r, c), functools.partial(lambda i, core, k: (k, i, 0, 0), k=k))
            for k in range(N_CHIPS - 1)],
        out_specs=pl.BlockSpec((None, r, c), lambda i, core: (core[0] * half + i, 0, 0)))
    return pl.pallas_call(
        body, name="add_chip_blocks", grid_spec=grid_spec, out_shape=_sds((2 * half, r, c)),
        compiler_params=_params(("arbitrary",)),
    )(core, own, received, received, received)


def _sum_over_devices(parts):
    _, r, _ = parts.shape

    def body(p_ref, o_ref):
        acc = p_ref[0]
        for k in range(1, N_DEVICES):
            acc = acc + p_ref[k]
        o_ref[...] = acc

    return pl.pallas_call(body, name="sum_over_devices", out_shape=_sds((r, LANES)))(parts)


def _adamw_math(w, g, m, v):
    m = ADAM_B1 * m + (1.0 - ADAM_B1) * g
    v = ADAM_B2 * v + (1.0 - ADAM_B2) * (g * g)
    m_hat = m / (1.0 - ADAM_B1 ** ADAM_STEP)
    v_hat = v / (1.0 - ADAM_B2 ** ADAM_STEP)
    delta = -ADAM_LR * (m_hat / (jnp.sqrt(v_hat) + ADAM_EPS) + ADAM_WD * w)
    return delta, m, v


def _adamw_stacked(w, m, v, grads, offset):
    l, r, c = w.shape
    tr = r
    while tr * c * 4 > 2**20 and tr % 16 == 0:
        tr //= 2

    def body(w_ref, m_ref, v_ref, g_ref, go_ref, d_ref, mo_ref, vo_ref):
        g = g_ref[...]
        go_ref[...] = g
        d_ref[...], mo_ref[...], vo_ref[...] = _adamw_math(w_ref[...], g, m_ref[...], v_ref[...])

    blk = pl.BlockSpec((None, tr, c), lambda i, j: (i, j, 0))
    return pl.pallas_call(
        body, name="adamw_stacked", grid=(l, r // tr),
        in_specs=[blk, blk, blk, pl.BlockSpec((None, tr, c), lambda i, j: (offset + i, j, 0))],
        out_specs=[blk] * 4, out_shape=[_sds((l, r, c))] * 4,
        compiler_params=_params(("arbitrary", "arbitrary")),
    )(w, m, v, grads)


def _adamw_small(w, m, v, g):
    def body(w_ref, m_ref, v_ref, g_ref, d_ref, mo_ref, vo_ref):
        d_ref[...], mo_ref[...], vo_ref[...] = _adamw_math(w_ref[...], g_ref[...], m_ref[...], v_ref[...])

    return pl.pallas_call(body, name="adamw_small", out_shape=[_sds(w.shape)] * 3)(w, m, v, g)


def _pack_rows(arrs):
    flat = jnp.concatenate([a.reshape(-1) for a in arrs])
    pad = (-flat.shape[0]) % (8 * LANES)
    return jnp.pad(flat, (0, pad)).reshape(-1, LANES)


def _unpack_rows(rows, shapes, lead=()):
    flat = rows.reshape(lead + (-1,))
    out, at = [], 0
    for shp in shapes:
        size = int(np.prod(shp))
        out.append(flat[..., at:at + size].reshape(lead + tuple(shp)))
        at += size
    return out


WEIGHT_NAMES = ('ffn1_norm', 'ffn1_w_gate', 'ffn1_w_up', 'ffn1_w_down', 'mix_norm', 'pool_w', 'pool_scale',
                'mla_w_in', 'mla_q_norm', 'mla_w_q_up', 'mla_kv_norm', 'mla_w_kv_up', 'mla_q_head_norm',
                'mla_k_head_norm', 'mla_w_out', 'ffn2_norm', 'ffn2_w_gate', 'ffn2_w_up', 'ffn2_w_down')


def _chips_to_columns(g):
    return jnp.transpose(g, (1, 2, 0, 3)).reshape(g.shape[1], g.shape[2], -1)


def _columns_to_chips(full):
    n, r, c4 = full.shape
    return jnp.transpose(full.reshape(n, r, N_CHIPS, c4 // N_CHIPS), (2, 0, 1, 3))


def kernel(x, positions, ffn1_norm, ffn1_w_gate, ffn1_w_up, ffn1_w_down, mix_norm, pool_w, pool_scale, mla_w_in, mla_q_norm, mla_w_q_up, mla_kv_norm, mla_w_kv_up, mla_q_head_norm, mla_k_head_norm, mla_w_out, ffn2_norm, ffn2_w_gate, ffn2_w_up, ffn2_w_down, loss_target, m_ffn1_norm, m_ffn1_w_gate, m_ffn1_w_up, m_ffn1_w_down, m_mix_norm, m_pool_w, m_pool_scale, m_mla_w_in, m_mla_q_norm, m_mla_w_q_up, m_mla_kv_norm, m_mla_w_kv_up, m_mla_q_head_norm, m_mla_k_head_norm, m_mla_w_out, m_ffn2_norm, m_ffn2_w_gate, m_ffn2_w_up, m_ffn2_w_down, v_ffn1_norm, v_ffn1_w_gate, v_ffn1_w_up, v_ffn1_w_down, v_mix_norm, v_pool_w, v_pool_scale, v_mla_w_in, v_mla_q_norm, v_mla_w_q_up, v_mla_kv_norm, v_mla_w_kv_up, v_mla_q_head_norm, v_mla_k_head_norm, v_mla_w_out, v_ffn2_norm, v_ffn2_w_gate, v_ffn2_w_up, v_ffn2_w_down):
    env = dict(locals())
    w = {n: env[n] for n in WEIGHT_NAMES}
    mom = {n: env["m_" + n] for n in WEIGHT_NAMES}
    var = {n: env["v_" + n] for n in WEIGHT_NAMES}

    s, d = x.shape[1], x.shape[2]
    depth = ffn1_norm.shape[0]
    n_mla, n_pool, n_groups = mla_w_in.shape[0], pool_w.shape[0], pool_w.shape[1]
    pool_c = pool_w.shape[3]
    q_lora = N_CHIPS * mla_q_norm.shape[1]
    kv_lora = N_CHIPS * mla_kv_norm.shape[1]
    n_heads = N_CHIPS * mla_w_q_up.shape[2] // QK_HEAD
    t_attn = _tile(s, ATTN_TILE)
    n_attn = s // t_attn
    cx, cy, cc = _mesh_position()
    chip = 2 * cx + cy
    chip_arr = jnp.reshape(chip, (1,)).astype(jnp.int32)
    core_arr = jnp.reshape(cc, (1,)).astype(jnp.int32)

    shard_gu = _cast(jnp.concatenate([ffn1_w_gate, ffn1_w_up, ffn2_w_gate, ffn2_w_up], axis=0))
    shard_dn = _cast(jnp.concatenate([ffn1_w_down, ffn2_w_down], axis=0))
    shard_pool = _cast(pool_w.reshape((n_pool * n_groups,) + pool_w.shape[2:]))
    w_gu, w_dn, g_in, g_qup, g_kvup, g_out, g_pool = _gather_over_chips(
        [shard_gu, shard_dn, _cast(mla_w_in), _cast(mla_w_q_up), _cast(mla_w_kv_up), _cast(mla_w_out), shard_pool])
    small_shapes = [mla_q_norm.shape, mla_kv_norm.shape]
    small = _gather_over_devices(_pack_rows([mla_q_norm, mla_kv_norm]))[::2]
    qn_chips, kvn_chips = _unpack_rows(small, small_shapes, lead=(N_CHIPS,))
    q_norm_full = jnp.transpose(qn_chips, (1, 0, 2)).reshape(n_mla, 1, q_lora)
    kv_norm_full = jnp.transpose(kvn_chips, (1, 0, 2)).reshape(n_mla, 1, kv_lora)

    w_in_full = _chips_to_columns(g_in)
    w_q_heads = jnp.transpose(_chips_to_columns(g_qup).reshape(n_mla, q_lora, n_heads, QK_HEAD), (0, 2, 1, 3))
    w_kv = _chips_to_columns(g_kvup).reshape(n_mla, kv_lora, n_heads, QK_NOPE + V_HEAD)
    w_kn_heads = jnp.transpose(w_kv[..., :QK_NOPE], (0, 2, 1, 3))
    w_v_full = w_kv[..., QK_NOPE:].reshape(n_mla, kv_lora, n_heads * V_HEAD)
    w_out_full = jnp.transpose(g_out, (1, 0, 2, 3)).reshape(n_mla, n_heads * V_HEAD, d)
    pool_full = jnp.transpose(g_pool.reshape(N_CHIPS, n_pool, n_groups, pool_c // N_CHIPS, pool_c),
                              (1, 2, 0, 3, 4)).reshape(n_pool, n_groups, pool_c, pool_c)

    inv_freq = (1.0 / (ROPE_THETA ** (jnp.arange(0, QK_ROPE, 2, dtype=F32) / QK_ROPE))).reshape(1, -1)
    cos_t, sin_t = _rope_tables(positions.reshape(s, 1), inv_freq)

    row = lambda a, i: a[i].reshape(1, -1)
    i_gate1, i_up1, i_gate2, i_up2 = (lambda i: i), (lambda i: depth + i), (lambda i: 2 * depth + i), (lambda i: 3 * depth + i)
    i_dn1, i_dn2 = (lambda i: i), (lambda i: depth + i)

    h = x.reshape(s, d)
    saved = []
    for i in range(depth):
        rec = {"x_ffn1": h}
        h, *rec["ffn1"] = _ffn_fwd(h, row(ffn1_norm, i), w_gu, w_dn, i_gate1(i), i_up1(i), i_dn1(i))
        rec["x_mix"] = h
        j = i // 2
        if i % 2 == 0:
            h = _pool_fwd(h, row(mix_norm, i), pool_full[j], row(pool_scale, j))
        else:
            lat, q, k, v = _mla_qkv_fwd(h, row(mix_norm, i), w_in_full[j], q_norm_full[j], kv_norm_full[j],
                                        w_q_heads[j], w_kn_heads[j], w_v_full[j], row(mla_q_head_norm, j),
                                        row(mla_k_head_norm, j), cos_t, sin_t)
            vt = jnp.transpose(v.reshape(n_attn, t_attn, n_heads, V_HEAD), (2, 0, 3, 1))
            ot, lse = _flash_fwd(q, k, vt)
            rec.update(lat=lat, q=q, k=k, v=v, ot=ot, lse=lse)
            h = _mla_out_fwd(h, ot, w_out_full[j])
        rec["x_ffn2"] = h
        h, *rec["ffn2"] = _ffn_fwd(h, row(ffn2_norm, i), w_gu, w_dn, i_gate2(i), i_up2(i), i_dn2(i))
        saved.append(rec)

    loss_part, dy = _loss_and_grad(h, loss_target.reshape(s, d))
    loss = lax.psum(loss_part[0, 0], ("x", "y", "c"))

    g_gu = [None] * (4 * depth)
    g_dn = [None] * (2 * depth)
    g_norm = {n: [None] * depth for n in ("ffn1_norm", "mix_norm", "ffn2_norm")}
    g_pool_w, g_pool_scale = [None] * n_pool, [None] * n_pool
    g_mla = {n: [None] * n_mla for n in ("w_in", "q_norm", "kv_norm", "w_q", "w_kv", "qhn", "khn", "w_out")}
    for i in reversed(range(depth)):
        rec = saved[i]
        hb, gate, up = rec["ffn2"]
        dy, g_norm["ffn2_norm"][i], dyb, dgt, dup, act = _ffn_bwd_dgrad(
            rec["x_ffn2"], row(ffn2_norm, i), dy, gate, up, w_gu, w_dn, i_gate2(i), i_up2(i), i_dn2(i))
        g_gu[i_gate2(i)], g_gu[i_up2(i)], g_dn[i_dn2(i)] = _ffn_wgrad(hb, dyb, dgt, dup, act)
        j = i // 2
        if i % 2 == 0:
            dy, g_norm["mix_norm"][i], g_pool_w[j], g_pool_scale[j] = _pool_bwd(
                rec["x_mix"], row(mix_norm, i), pool_full[j], row(pool_scale, j), dy)
        else:
            do, delta, g_mla["w_out"][j] = _mla_out_bwd(dy, rec["ot"], w_out_full[j])
            by_tile = lambda a: a.reshape(n_heads, n_attn, 1, t_attn)
            dqt, dk, dv = _flash_bwd(rec["q"], rec["k"], jnp.transpose(rec["k"], (0, 2, 1)), rec["v"], do,
                                     by_tile(rec["lse"]), by_tile(delta))
            dq = jnp.transpose(dqt, (0, 1, 3, 2)).reshape(n_heads, s, QK_HEAD)
            (dy, g_norm["mix_norm"][i], g_mla["w_in"][j], g_mla["q_norm"][j], g_mla["kv_norm"][j], dwq, dwkn, dwv,
             g_mla["qhn"][j], g_mla["khn"][j]) = _mla_qkv_bwd(
                rec["x_mix"], rec["lat"], dy, dq, dk, dv, row(mix_norm, i), w_in_full[j], q_norm_full[j],
                kv_norm_full[j], w_q_heads[j], w_kn_heads[j], w_v_full[j], row(mla_q_head_norm, j),
                row(mla_k_head_norm, j), cos_t, sin_t)
            g_mla["w_q"][j] = jnp.transpose(dwq, (1, 0, 2)).reshape(q_lora, n_heads * QK_HEAD)
            g_mla["w_kv"][j] = jnp.concatenate(
                [jnp.transpose(dwkn, (1, 0, 2)), dwv.reshape(kv_lora, n_heads, V_HEAD)], axis=-1
            ).reshape(kv_lora, n_heads * (QK_NOPE + V_HEAD))
        hb, gate, up = rec["ffn1"]
        dy, g_norm["ffn1_norm"][i], dyb, dgt, dup, act = _ffn_bwd_dgrad(
            rec["x_ffn1"], row(ffn1_norm, i), dy, gate, up, w_gu, w_dn, i_gate1(i), i_up1(i), i_dn1(i))
        g_gu[i_gate1(i)], g_gu[i_up1(i)], g_dn[i_dn1(i)] = _ffn_wgrad(hb, dyb, dgt, dup, act)
    grad_x = dy.reshape(x.shape)

    full_grads = [
        jnp.stack(g_gu, axis=1),
        jnp.stack(g_dn, axis=1),
        _columns_to_chips(jnp.stack(g_mla["w_in"])),
        _columns_to_chips(jnp.stack(g_mla["w_q"])),
        _columns_to_chips(jnp.stack(g_mla["w_kv"])),
        jnp.transpose(jnp.stack(g_mla["w_out"]).reshape(n_mla, N_CHIPS, -1, d), (1, 0, 2, 3)),
        jnp.transpose(jnp.stack(g_pool_w).reshape(n_pool, n_groups, N_CHIPS, pool_c // N_CHIPS, pool_c),
                      (2, 0, 1, 3, 4)).reshape(N_CHIPS, n_pool * n_groups, pool_c // N_CHIPS, pool_c),
    ]
    from_sibling = _send_other_half_to_sibling(full_grads)
    chip_sums = [_add_sibling_half(g, r, chip_arr, core_arr) for g, r in zip(full_grads, from_sibling)]
    from_chips = _send_blocks_to_chips([wire for wire, _ in chip_sums])
    half_sums = [_add_chip_blocks(own, r, core_arr) for (_, own), r in zip(chip_sums, from_chips)]
    r_gu, r_dn, r_in, r_qup, r_kvup, r_out, r_pool = _join_halves_with_sibling(half_sums)

    small_grads = [jnp.concatenate(g_norm["ffn1_norm"]), jnp.concatenate(g_norm["mix_norm"]),
                   jnp.concatenate(g_norm["ffn2_norm"]), jnp.concatenate(g_pool_scale),
                   jnp.concatenate(g_mla["qhn"]), jnp.concatenate(g_mla["khn"]),
                   jnp.concatenate(g_mla["q_norm"]), jnp.concatenate(g_mla["kv_norm"])]
    small_sum = _sum_over_devices(_gather_over_devices(_pack_rows(small_grads)))
    (s_ffn1, s_mix, s_ffn2, s_pscale, s_qhn, s_khn, s_qn, s_kvn) = _unpack_rows(small_sum, [g.shape for g in small_grads])
    qn_w, kvn_w = mla_q_norm.shape[1], mla_kv_norm.shape[1]
    s_qn = lax.dynamic_slice_in_dim(s_qn, chip * qn_w, qn_w, axis=1)
    s_kvn = lax.dynamic_slice_in_dim(s_kvn, chip * kvn_w, kvn_w, axis=1)

    grads, deltas, new_m, new_v = {}, {}, {}, {}

    def stacked(name, reduced, offset):
        shape = w[name].shape
        as3 = lambda a: a.reshape((-1,) + shape[-2:])
        out = _adamw_stacked(as3(w[name]), as3(mom[name]), as3(var[name]), reduced, offset)
        grads[name], deltas[name], new_m[name], new_v[name] = [o.reshape(shape) for o in out]

    def small_update(name, g):
        grads[name] = g
        deltas[name], new_m[name], new_v[name] = _adamw_small(w[name], mom[name], var[name], g)

    stacked("ffn1_w_gate", r_gu, 0)
    stacked("ffn1_w_up", r_gu, depth)
    stacked("ffn2_w_gate", r_gu, 2 * depth)
    stacked("ffn2_w_up", r_gu, 3 * depth)
    stacked("ffn1_w_down", r_dn, 0)
    stacked("ffn2_w_down", r_dn, depth)
    stacked("mla_w_in", r_in, 0)
    stacked("mla_w_q_up", r_qup, 0)
    stacked("mla_w_kv_up", r_kvup, 0)
    stacked("mla_w_out", r_out, 0)
    stacked("pool_w", r_pool, 0)
    small_update("ffn1_norm", s_ffn1)
    small_update("mix_norm", s_mix)
    small_update("ffn2_norm", s_ffn2)
    small_update("pool_scale", s_pscale)
    small_update("mla_q_head_norm", s_qhn)
    small_update("mla_k_head_norm", s_khn)
    small_update("mla_q_norm", s_qn)
    small_update("mla_kv_norm", s_kvn)

    return (loss, grad_x, *[grads[n] for n in WEIGHT_NAMES], *[deltas[n] for n in WEIGHT_NAMES],
            *[new_m[n] for n in WEIGHT_NAMES], *[new_v[n] for n in WEIGHT_NAMES])
```

```python
import functools

import numpy as np

import jax
import jax.numpy as jnp
from jax import lax
from jax.experimental import pallas as pl
from jax.experimental.pallas import tpu as pltpu

F32 = jnp.float32
MXU_DTYPE = jnp.bfloat16
WIRE_DTYPE = jnp.bfloat16
MESH = pl.DeviceIdType.MESH
N_CHIPS = 4
N_DEVICES = 8
LANES = 128
VMEM_LIMIT_BYTES = 56 * 2**20
NORM_EPS = 1e-6
QK_NOPE, QK_ROPE, V_HEAD = 128, 64, 128
QK_HEAD = QK_NOPE + QK_ROPE
SCORE_SCALE = QK_HEAD ** -0.5
ONES_ROWS = 8
ROPE_THETA = 10000.0
POOL_WINDOWS = (2, 4, 8, 16)
POOL_HALO = 16
FFN_HALF = 0.5
ADAM_LR, ADAM_B1, ADAM_B2, ADAM_EPS, ADAM_WD, ADAM_STEP = 0.001, 0.9, 0.999, 1e-08, 0.01, 10
FFN_TILE = 512
FFN_BWD_TILE = 256
WGRAD_TILE = 2048
MLA_TILE = 256
POOL_TILE = 512
ATTN_FWD_TILE = 1024
ATTN_BWD_TILE = 512
ROW_TILE = 1024


def _cast(v):
    return v.astype(MXU_DTYPE)


def _mm(a, b):
    return jnp.dot(a, b, preferred_element_type=F32)


def _mm_nt(a, b):
    return lax.dot_general(a, b, (((1,), (1,)), ((), ())), preferred_element_type=F32)


def _mm_tn(a, b):
    return lax.dot_general(a, b, (((0,), (0,)), ((), ())), preferred_element_type=F32)


def _rms_fwd(v, gain):
    r = lax.rsqrt(jnp.mean(v * v, axis=-1, keepdims=True) + NORM_EPS)
    return v * r * gain, r


def _rms_bwd(v, r, gain, dy):
    vr = v * r
    gy = dy * gain
    dv = r * (gy - vr * jnp.mean(gy * vr, axis=-1, keepdims=True))
    return dv, jnp.sum(dy * vr, axis=0, keepdims=True)


def _params(semantics=None):
    return pltpu.CompilerParams(dimension_semantics=semantics, vmem_limit_bytes=VMEM_LIMIT_BYTES)


def _tile(n, want):
    t = min(n, want)
    assert n % t == 0, (n, want)
    return t


def _full(shape):
    nd = len(shape)
    return pl.BlockSpec(shape, lambda *_: (0,) * nd)


def _sds(shape, dtype=F32):
    return jax.ShapeDtypeStruct(shape, dtype)


def _ffn_fwd(x, gain, w_gu, w_dn, i_gate, i_up, i_down):
    s, d = x.shape
    fs = w_gu.shape[-1]
    tm = _tile(s, FFN_TILE)

    def body(x_ref, g_ref, wg_ref, wu_ref, wd_ref, y_ref, hb_ref, gate_ref, up_ref):
        h, _ = _rms_fwd(x_ref[...], g_ref[...])
        hb = _cast(h)
        hb_ref[...] = hb
        pre = [(_mm(hb, wg_ref[c]), _mm(hb, wu_ref[c])) for c in range(N_CHIPS)]
        out = None
        for c, (g, u) in enumerate(pre):
            gate_ref[c] = _cast(g)
            up_ref[c] = _cast(u)
            part = _mm(_cast((g * jax.nn.sigmoid(g)) * u), wd_ref[c])
            out = part if out is None else out + part
        y_ref[...] = x_ref[...] + FFN_HALF * out

    resident = dict(pipeline_mode=pl.Buffered(1))
    tok = pl.BlockSpec((tm, d), lambda i: (i, 0))
    chunks = pl.BlockSpec((N_CHIPS, tm, fs), lambda i: (0, i, 0))
    return pl.pallas_call(
        body, name="ffn_fwd", grid=(s // tm,),
        in_specs=[
            tok, _full((1, d)),
            pl.BlockSpec((N_CHIPS, None, d, fs), lambda i: (0, i_gate, 0, 0), **resident),
            pl.BlockSpec((N_CHIPS, None, d, fs), lambda i: (0, i_up, 0, 0), **resident),
            pl.BlockSpec((N_CHIPS, None, fs, d), lambda i: (0, i_down, 0, 0), **resident),
        ],
        out_specs=[tok, tok, chunks, chunks],
        out_shape=[_sds((s, d)), _sds((s, d), MXU_DTYPE), _sds((N_CHIPS, s, fs), MXU_DTYPE),
                   _sds((N_CHIPS, s, fs), MXU_DTYPE)],
        compiler_params=_params(("arbitrary",)),
    )(x, gain, w_gu, w_gu, w_dn)


def _ffn_bwd_dgrad(x, gain, dy, gate, up, w_gu, w_dn, i_gate, i_up, i_down):
    s, d = x.shape
    fs = w_gu.shape[-1]
    tm = _tile(s, FFN_BWD_TILE)

    def body(x_ref, g_ref, dy_ref, gate_ref, up_ref, wg_ref, wu_ref, wd_ref,
             dx_ref, dgain_ref, dyb_ref, dg_ref, du_ref, act_ref):
        i = pl.program_id(0)
        dyb = _cast(dy_ref[...])
        dyb_ref[...] = dyb
        dacts = [_mm_nt(dyb, wd_ref[c]) for c in range(N_CHIPS)]
        dh = None
        for c in range(N_CHIPS):
            g = gate_ref[c].astype(F32)
            u = up_ref[c].astype(F32)
            sg = jax.nn.sigmoid(g)
            silu = g * sg
            dact = FFN_HALF * dacts[c]
            dgb = _cast(dact * u * (sg * (1.0 + g * (1.0 - sg))))
            dub = _cast(dact * silu)
            dg_ref[c] = dgb
            du_ref[c] = dub
            act_ref[c] = _cast(silu * u)
            part = _mm_nt(dgb, wg_ref[c]) + _mm_nt(dub, wu_ref[c])
            dh = part if dh is None else dh + part
        _, r = _rms_fwd(x_ref[...], g_ref[...])
        dxn, dgn = _rms_bwd(x_ref[...], r, g_ref[...], dh)
        dx_ref[...] = dy_ref[...] + dxn

        @pl.when(i == 0)
        def _():
            dgain_ref[...] = dgn

        @pl.when(i > 0)
        def _():
            dgain_ref[...] += dgn

    resident = dict(pipeline_mode=pl.Buffered(1))
    tok = pl.BlockSpec((tm, d), lambda i: (i, 0))
    chunks = pl.BlockSpec((N_CHIPS, tm, fs), lambda i: (0, i, 0))
    return pl.pallas_call(
        body, name="ffn_bwd_dgrad", grid=(s // tm,),
        in_specs=[
            tok, _full((1, d)), tok, chunks, chunks,
            pl.BlockSpec((N_CHIPS, None, d, fs), lambda i: (0, i_gate, 0, 0), **resident),
            pl.BlockSpec((N_CHIPS, None, d, fs), lambda i: (0, i_up, 0, 0), **resident),
            pl.BlockSpec((N_CHIPS, None, fs, d), lambda i: (0, i_down, 0, 0), **resident),
        ],
        out_specs=[tok, _full((1, d)), tok, chunks, chunks, chunks],
        out_shape=[_sds((s, d)), _sds((1, d)), _sds((s, d), MXU_DTYPE),
                   _sds((N_CHIPS, s, fs), MXU_DTYPE), _sds((N_CHIPS, s, fs), MXU_DTYPE),
                   _sds((N_CHIPS, s, fs), MXU_DTYPE)],
        compiler_params=_params(("arbitrary",)),
    )(x, gain, dy, gate, up, w_gu, w_gu, w_dn)


def _ffn_wgrad(hb, dyb, dg, du, act):
    s, d = hb.shape
    fs = dg.shape[-1]
    tk = _tile(s, WGRAD_TILE)
    n_k = s // tk

    def body(h_ref, dy_ref, dg_ref, du_ref, act_ref, wg_ref, wu_ref, wd_ref):
        k = pl.program_id(1)

        @pl.when(k == 0)
        def _():
            wg_ref[...] = jnp.zeros_like(wg_ref)
            wu_ref[...] = jnp.zeros_like(wu_ref)
            wd_ref[...] = jnp.zeros_like(wd_ref)

        h = h_ref[...]
        wg_ref[...] += _mm_tn(h, dg_ref[...])
        wu_ref[...] += _mm_tn(h, du_ref[...])
        wd_ref[...] += FFN_HALF * _mm_tn(act_ref[...], dy_ref[...])

    tok = pl.BlockSpec((tk, d), lambda j, k: (k, 0))
    chunk = pl.BlockSpec((None, tk, fs), lambda j, k: (j, k, 0))
    return pl.pallas_call(
        body, name="ffn_wgrad", grid=(N_CHIPS, n_k),
        in_specs=[tok, tok, chunk, chunk, chunk],
        out_specs=[pl.BlockSpec((None, d, fs), lambda j, k: (j, 0, 0)),
                   pl.BlockSpec((None, d, fs), lambda j, k: (j, 0, 0)),
                   pl.BlockSpec((None, fs, d), lambda j, k: (j, 0, 0))],
        out_shape=[_sds((N_CHIPS, d, fs)), _sds((N_CHIPS, d, fs)), _sds((N_CHIPS, fs, d))],
        compiler_params=_params(("arbitrary", "arbitrary")),
    )(hb, dyb, dg, du, act)


def _inv_count(first_row, n_rows, window):
    t = first_row + lax.broadcasted_iota(jnp.int32, (n_rows, 1), 0)
    return 1.0 / jnp.minimum(t + 1, window).astype(F32)


def _trailing_sum(v, window):
    k = 1
    while k < window:
        v = v + pltpu.roll(v, k, 0)
        k *= 2
    return v


def _leading_sum(v, window):
    n = v.shape[0]
    k = 1
    while k < window:
        v = v + pltpu.roll(v, n - k, 0)
        k *= 2
    return v


def _pool_normed_rows(x_ref, prev_ref, g_ref, i):
    h, r = _rms_fwd(x_ref[...], g_ref[...])
    hp, _ = _rms_fwd(prev_ref[...], g_ref[...])
    hp = jnp.where(i > 0, hp, 0.0)
    return jnp.concatenate([hp, h], axis=0), r


def _pooled_group(he, g, pg, first_row, tm):
    ue = he[:, g * pg:(g + 1) * pg]
    win = _trailing_sum(ue, POOL_WINDOWS[g])[POOL_HALO:]
    return win * _inv_count(first_row, tm, POOL_WINDOWS[g]) - ue[POOL_HALO:]


def _pool_specs(s, d, tm):
    per = tm // POOL_HALO
    last = s // POOL_HALO - 1
    tok = pl.BlockSpec((tm, d), lambda i: (i, 0))
    prev = pl.BlockSpec((POOL_HALO, d), lambda i: (jnp.maximum(i * per - 1, 0), 0))
    nxt = pl.BlockSpec((POOL_HALO, d), lambda i: (jnp.minimum((i + 1) * per, last), 0))
    return tok, prev, nxt


def _pool_fwd(x, gain, w, scale):
    s, d = x.shape
    n_g, pg = w.shape[0], w.shape[-1]
    tm = _tile(s, POOL_TILE)
    tok, prev, _ = _pool_specs(s, d, tm)

    def body(x_ref, prev_ref, g_ref, w_ref, sc_ref, y_ref):
        i = pl.program_id(0)
        he, _ = _pool_normed_rows(x_ref, prev_ref, g_ref, i)
        z = [_mm(_cast(_pooled_group(he, g, pg, i * tm, tm)), w_ref[g]) for g in range(n_g)]
        y_ref[...] = x_ref[...] + jnp.concatenate(z, axis=-1) * sc_ref[...]

    return pl.pallas_call(
        body, name="pool_fwd", grid=(s // tm,),
        in_specs=[tok, prev, _full((1, d)), _full(w.shape), _full((1, d))],
        out_specs=tok, out_shape=_sds((s, d)),
        compiler_params=_params(("arbitrary",)),
    )(x, x, gain, w, scale)


def _pool_bwd(x, gain, w, scale, dy):
    s, d = x.shape
    n_g, pg = w.shape[0], w.shape[-1]
    tm = _tile(s, POOL_TILE)
    n_tiles = s // tm
    tok, prev, nxt = _pool_specs(s, d, tm)

    def body(x_ref, prev_ref, dy_ref, next_ref, g_ref, w_ref, sc_ref, dx_ref, dgain_ref, dw_ref, dsc_ref):
        i = pl.program_id(0)

        @pl.when(i == 0)
        def _():
            dgain_ref[...] = jnp.zeros_like(dgain_ref)
            dw_ref[...] = jnp.zeros_like(dw_ref)
            dsc_ref[...] = jnp.zeros_like(dsc_ref)

        he, r = _pool_normed_rows(x_ref, prev_ref, g_ref, i)
        dy = dy_ref[...]
        dyn = jnp.where(i < n_tiles - 1, next_ref[...], 0.0)
        dze = jnp.concatenate([dy, dyn], axis=0) * sc_ref[...]
        dh, dsc = [], []
        for g in range(n_g):
            cols = slice(g * pg, (g + 1) * pg)
            pooled = _cast(_pooled_group(he, g, pg, i * tm, tm))
            dsc.append(jnp.sum(dy[:, cols] * _mm(pooled, w_ref[g]), axis=0, keepdims=True))
            dzb = _cast(dze[:, cols])
            dw_ref[g] += _mm_tn(pooled, dzb[:tm])
            dpool = _mm_nt(dzb, w_ref[g])
            spread = _leading_sum(dpool * _inv_count(i * tm, tm + POOL_HALO, POOL_WINDOWS[g]), POOL_WINDOWS[g])
            dh.append(spread[:tm] - dpool[:tm])
        dsc_ref[...] += jnp.concatenate(dsc, axis=-1)
        dxn, dgn = _rms_bwd(x_ref[...], r, g_ref[...], jnp.concatenate(dh, axis=-1))
        dgain_ref[...] += dgn
        dx_ref[...] = dy + dxn

    return pl.pallas_call(
        body, name="pool_bwd", grid=(n_tiles,),
        in_specs=[tok, prev, tok, nxt, _full((1, d)), _full(w.shape), _full((1, d))],
        out_specs=[tok, _full((1, d)), _full(w.shape), _full((1, d))],
        out_shape=[_sds((s, d)), _sds((1, d)), _sds(w.shape), _sds((1, d))],
        compiler_params=_params(("arbitrary",)),
    )(x, x, dy, dy, gain, w, scale)


def _rope_tables(pos_col, inv_freq):
    s = pos_col.shape[0]
    tm = _tile(s, ROW_TILE)
    half = QK_ROPE // 2

    def body(p_ref, f_ref, c_ref, s_ref):
        ang = p_ref[...].astype(F32) * f_ref[...]
        cos, sin = jnp.cos(ang), jnp.sin(ang)
        c_ref[...] = jnp.concatenate([jnp.ones((tm, QK_NOPE), F32), cos, cos], axis=-1)
        s_ref[...] = jnp.concatenate([jnp.zeros((tm, QK_NOPE), F32), -sin, sin], axis=-1)

    tab = pl.BlockSpec((tm, QK_HEAD), lambda i: (i, 0))
    return pl.pallas_call(
        body, name="rope_tables", grid=(s // tm,),
        in_specs=[pl.BlockSpec((tm, 1), lambda i: (i, 0)), _full((1, half))],
        out_specs=[tab, tab], out_shape=[_sds((s, QK_HEAD)), _sds((s, QK_HEAD))],
        compiler_params=_params(("arbitrary",)),
    )(pos_col, inv_freq)


def _swap_rope_halves(v):
    half = QK_ROPE // 2
    return jnp.concatenate([v[:, :QK_NOPE], v[:, QK_NOPE + half:], v[:, QK_NOPE:QK_NOPE + half]], axis=-1)


def _rope(v, cos, sin):
    return v * cos + _swap_rope_halves(v) * sin


def _rope_transposed(dv, cos, sin):
    return dv * cos + _swap_rope_halves(dv * sin)


def _mla_qkv_fwd(x, gain, w_in, q_norm, kv_norm, w_q, w_kn, w_v, q_head_norm, k_head_norm, cos, sin):
    s, d = x.shape
    n_h, ql = w_q.shape[0], w_q.shape[1]
    kvl, lat_w = w_kn.shape[1], w_in.shape[1]
    tm = _tile(s, MLA_TILE)

    def body(x_ref, g_ref, win_ref, qn_ref, kvn_ref, wq_ref, wkn_ref, wv_ref, qhn_ref, khn_ref, c_ref, s_ref,
             lat_ref, q_ref, k_ref, v_ref):
        h, _ = _rms_fwd(x_ref[...], g_ref[...])
        lat = _mm(_cast(h), win_ref[...])
        lat_ref[...] = lat
        cqn, _ = _rms_fwd(lat[:, :ql], qn_ref[...])
        ckvn, _ = _rms_fwd(lat[:, ql:ql + kvl], kvn_ref[...])
        kpe = lat[:, ql + kvl:]
        cqb, ckb = _cast(cqn), _cast(ckvn)
        cos_t, sin_t = c_ref[...], s_ref[...]
        v_ref[...] = _cast(_mm(ckb, wv_ref[...]))
        for hh in range(n_h):
            qn, _ = _rms_fwd(_mm(cqb, wq_ref[hh]), qhn_ref[...])
            q_ref[hh] = _cast(_rope(qn, cos_t, sin_t) * SCORE_SCALE)
            kn, _ = _rms_fwd(jnp.concatenate([_mm(ckb, wkn_ref[hh]), kpe], axis=-1), khn_ref[...])
            k_ref[hh] = _cast(_rope(kn, cos_t, sin_t))

    tok = lambda w: pl.BlockSpec((tm, w), lambda i: (i, 0))
    heads = pl.BlockSpec((n_h, tm, QK_HEAD), lambda i: (0, i, 0))
    return pl.pallas_call(
        body, name="mla_qkv_fwd", grid=(s // tm,),
        in_specs=[tok(d), _full((1, d)), _full(w_in.shape), _full((1, ql)), _full((1, kvl)), _full(w_q.shape),
                  _full(w_kn.shape), _full(w_v.shape), _full((1, QK_HEAD)), _full((1, QK_HEAD)),
                  tok(QK_HEAD), tok(QK_HEAD)],
        out_specs=[tok(lat_w), heads, heads, tok(n_h * V_HEAD)],
        out_shape=[_sds((s, lat_w)), _sds((n_h, s, QK_HEAD), MXU_DTYPE), _sds((n_h, s, QK_HEAD), MXU_DTYPE),
                   _sds((s, n_h * V_HEAD), MXU_DTYPE)],
        compiler_params=_params(("arbitrary",)),
    )(x, gain, w_in, q_norm, kv_norm, w_q, w_kn, w_v, q_head_norm, k_head_norm, cos, sin)


def _mla_qkv_bwd(x, lat, dy, dq, dk, dv, gain, w_in, q_norm, kv_norm, w_q, w_kn, w_v, q_head_norm, k_head_norm,
                 cos, sin):
    s, d = x.shape
    n_h, ql = w_q.shape[0], w_q.shape[1]
    kvl, lat_w = w_kn.shape[1], w_in.shape[1]
    tm = _tile(s, MLA_TILE)

    def body(x_ref, lat_ref, dy_ref, dq_ref, dk_ref, dv_ref, g_ref, win_ref, qn_ref, kvn_ref, wq_ref, wkn_ref,
             wv_ref, qhn_ref, khn_ref, c_ref, s_ref,
             dx_ref, dg_ref, dwin_ref, dqn_ref, dkvn_ref, dwq_ref, dwkn_ref, dwv_ref, dqhn_ref, dkhn_ref):
        @pl.when(pl.program_id(0) == 0)
        def _():
            for ref in (dg_ref, dwin_ref, dqn_ref, dkvn_ref, dwq_ref, dwkn_ref, dwv_ref, dqhn_ref, dkhn_ref):
                ref[...] = jnp.zeros_like(ref)

        x_t = x_ref[...]
        h, r = _rms_fwd(x_t, g_ref[...])
        hb = _cast(h)
        lat = lat_ref[...]
        cq, ckv, kpe = lat[:, :ql], lat[:, ql:ql + kvl], lat[:, ql + kvl:]
        cqn, rq = _rms_fwd(cq, qn_ref[...])
        ckvn, rkv = _rms_fwd(ckv, kvn_ref[...])
        cqb, ckb = _cast(cqn), _cast(ckvn)
        cos_t, sin_t = c_ref[...], s_ref[...]

        dvb = _cast(dv_ref[...])
        dwv_ref[...] += _mm_tn(ckb, dvb)
        dckvn = _mm_nt(dvb, wv_ref[...])
        dcqn = jnp.zeros((tm, ql), F32)
        dkpe = jnp.zeros((tm, QK_ROPE), F32)
        dqhn = jnp.zeros((1, QK_HEAD), F32)
        dkhn = jnp.zeros((1, QK_HEAD), F32)
        for hh in range(n_h):
            qp = _mm(cqb, wq_ref[hh])
            _, rqp = _rms_fwd(qp, qhn_ref[...])
            dqp, dgq = _rms_bwd(qp, rqp, qhn_ref[...], _rope_transposed(dq_ref[hh] * SCORE_SCALE, cos_t, sin_t))
            dqhn += dgq
            dqpb = _cast(dqp)
            dwq_ref[hh] += _mm_tn(cqb, dqpb)
            dcqn += _mm_nt(dqpb, wq_ref[hh])

            kp = jnp.concatenate([_mm(ckb, wkn_ref[hh]), kpe], axis=-1)
            _, rkp = _rms_fwd(kp, khn_ref[...])
            dkp, dgk = _rms_bwd(kp, rkp, khn_ref[...], _rope_transposed(dk_ref[hh], cos_t, sin_t))
            dkhn += dgk
            dknb = _cast(dkp[:, :QK_NOPE])
            dkpe += dkp[:, QK_NOPE:]
            dwkn_ref[hh] += _mm_tn(ckb, dknb)
            dckvn += _mm_nt(dknb, wkn_ref[hh])
        dqhn_ref[...] += dqhn
        dkhn_ref[...] += dkhn

        dcq, dgn = _rms_bwd(cq, rq, qn_ref[...], dcqn)
        dqn_ref[...] += dgn
        dckv, dgn = _rms_bwd(ckv, rkv, kvn_ref[...], dckvn)
        dkvn_ref[...] += dgn
        dlb = _cast(jnp.concatenate([dcq, dckv, dkpe], axis=-1))
        dwin_ref[...] += _mm_tn(hb, dlb)
        dxn, dgn = _rms_bwd(x_t, r, g_ref[...], _mm_nt(dlb, win_ref[...]))
        dg_ref[...] += dgn
        dx_ref[...] = dy_ref[...] + dxn

    tok = lambda w: pl.BlockSpec((tm, w), lambda i: (i, 0))
    heads = pl.BlockSpec((n_h, tm, QK_HEAD), lambda i: (0, i, 0))
    return pl.pallas_call(
        body, name="mla_qkv_bwd", grid=(s // tm,),
        in_specs=[tok(d), tok(lat_w), tok(d), heads, heads, tok(n_h * V_HEAD), _full((1, d)), _full(w_in.shape),
                  _full((1, ql)), _full((1, kvl)), _full(w_q.shape), _full(w_kn.shape), _full(w_v.shape),
                  _full((1, QK_HEAD)), _full((1, QK_HEAD)), tok(QK_HEAD), tok(QK_HEAD)],
        out_specs=[tok(d), _full((1, d)), _full(w_in.shape), _full((1, ql)), _full((1, kvl)), _full(w_q.shape),
                   _full(w_kn.shape), _full(w_v.shape), _full((1, QK_HEAD)), _full((1, QK_HEAD))],
        out_shape=[_sds((s, d)), _sds((1, d)), _sds(w_in.shape), _sds((1, ql)), _sds((1, kvl)), _sds(w_q.shape),
                   _sds(w_kn.shape), _sds(w_v.shape), _sds((1, QK_HEAD)), _sds((1, QK_HEAD))],
        compiler_params=_params(("arbitrary",)),
    )(x, lat, dy, dq, dk, dv, gain, w_in, q_norm, kv_norm, w_q, w_kn, w_v, q_head_norm, k_head_norm, cos, sin)


def _scores_t(k_t, q_t):
    return _mm_nt(k_t, q_t)


def _mask_above_diagonal(z, t):
    key = lax.broadcasted_iota(jnp.int32, (t, t), 0)
    query = lax.broadcasted_iota(jnp.int32, (t, t), 1)
    return jnp.where(key <= query, z, -jnp.inf)


def _flash_fwd(q, k, vt):
    n_h, s, _ = q.shape
    t = _tile(s, ATTN_FWD_TILE)
    n = s // t
    rows_v = vt.shape[2]

    def body(q_ref, k_ref, vt_ref, ot_ref, lse_ref, m_sc, acc_sc, z_sc):
        i = pl.program_id(1)
        m_sc[...] = jnp.full_like(m_sc, -jnp.inf)
        acc_sc[...] = jnp.zeros_like(acc_sc)
        q_t = q_ref[...]

        def fetch(j, slot):
            z_sc[slot] = _scores_t(k_ref[pl.ds(pl.multiple_of(j * t, t), t), :], q_t)

        def stage(j, slot, masked, prefetch=True):
            if prefetch:
                fetch(j + 1, 1 - slot)
            z = _mask_above_diagonal(z_sc[slot], t) if masked else z_sc[slot]
            m_old = m_sc[...]
            m_new = jnp.maximum(m_old, jnp.max(z, axis=0, keepdims=True))
            alpha = jnp.exp(m_old - m_new)
            acc_sc[...] = alpha * acc_sc[...] + _mm(vt_ref[j], _cast(jnp.exp(z - m_new)))
            m_sc[...] = m_new

        def pair_below_diagonal(pair, carry):
            stage(2 * pair, 0, False)
            stage(2 * pair + 1, 1, False)
            return carry

        fetch(0, 0)
        lax.fori_loop(0, i >> 1, pair_below_diagonal, 0)

        @pl.when((i & 1) == 1)
        def _():
            stage(i - 1, 0, False)
            stage(i, 1, True, prefetch=False)

        @pl.when((i & 1) == 0)
        def _():
            stage(i, 0, True, prefetch=False)

        denom = acc_sc[V_HEAD:V_HEAD + 1, :]
        ot_ref[...] = acc_sc[:V_HEAD, :] / denom
        lse_ref[...] = m_sc[...] + jnp.log(denom)

    whole_head = dict(pipeline_mode=pl.Buffered(1))
    return pl.pallas_call(
        body, name="flash_fwd", grid=(n_h, n),
        in_specs=[pl.BlockSpec((None, t, QK_HEAD), lambda h, i: (h, i, 0)),
                  pl.BlockSpec((None, s, QK_HEAD), lambda h, i: (h, 0, 0), **whole_head),
                  pl.BlockSpec((None, n, rows_v, t), lambda h, i: (h, 0, 0, 0), **whole_head)],
        out_specs=[pl.BlockSpec((V_HEAD, t), lambda h, i: (h, i)),
                   pl.BlockSpec((None, 1, t), lambda h, i: (h, 0, i))],
        out_shape=[_sds((n_h * V_HEAD, s)), _sds((n_h, 1, s))],
        scratch_shapes=[pltpu.VMEM((1, t), F32), pltpu.VMEM((rows_v, t), F32), pltpu.VMEM((2, t, t), F32)],
        compiler_params=_params(("arbitrary", "arbitrary")),
    )(q, k, vt)


def _flash_bwd(q, k, kt, v, do, lse, delta):
    n_h, s, _ = q.shape
    t = _tile(s, ATTN_BWD_TILE)
    n = s // t

    def body(q_ref, do_ref, lse_ref, dl_ref, k_ref, kt_ref, v_ref, dqt_ref, dk_ref, dv_ref, dk_sc, dv_sc, z_sc, dp_sc):
        j = pl.program_id(1)

        @pl.when(j == 0)
        def _():
            dqt_ref[...] = jnp.zeros_like(dqt_ref)

        dk_sc[...] = jnp.zeros_like(dk_sc)
        dv_sc[...] = jnp.zeros_like(dv_sc)
        k_t, kt_t, v_t = k_ref[...], kt_ref[...], v_ref[...]

        def rows(i):
            return pl.ds(pl.multiple_of(i * t, t), t)

        def fetch(i, slot):
            i = jnp.minimum(i, n - 1)
            z_sc[slot] = _scores_t(k_t, q_ref[rows(i), :])
            dp_sc[slot] = _mm_nt(v_t, do_ref[rows(i), :])

        def stage(i, slot, masked, prefetch=True):
            if prefetch:
                fetch(i + 1, 1 - slot)
            z = _mask_above_diagonal(z_sc[slot], t) if masked else z_sc[slot]
            pr = jnp.exp(z - lse_ref[i])
            dsb = _cast(pr * (dp_sc[slot] - dl_ref[i]))
            dv_sc[...] += _mm(_cast(pr), do_ref[rows(i), :])
            dk_sc[...] += _mm(dsb, q_ref[rows(i), :])
            dqt_ref[i] += _mm(kt_t, dsb)

        def pair_below_diagonal(pair, carry):
            stage(j + 1 + 2 * pair, 1, False)
            stage(j + 2 + 2 * pair, 0, False)
            return carry

        below = n - 1 - j
        fetch(j, 0)
        stage(j, 0, True)
        lax.fori_loop(0, below >> 1, pair_below_diagonal, 0)

        @pl.when((below & 1) == 1)
        def _():
            stage(n - 1, 1, False, prefetch=False)

        dk_ref[...] = dk_sc[...]
        dv_ref[...] = dv_sc[...]

    whole_head = dict(pipeline_mode=pl.Buffered(1))
    stat = pl.BlockSpec((None, n, 1, t), lambda h, j: (h, 0, 0, 0))
    return pl.pallas_call(
        body, name="flash_bwd", grid=(n_h, n),
        in_specs=[pl.BlockSpec((None, s, QK_HEAD), lambda h, j: (h, 0, 0), **whole_head),
                  pl.BlockSpec((s, V_HEAD), lambda h, j: (0, h), **whole_head),
                  stat, stat,
                  pl.BlockSpec((None, t, QK_HEAD), lambda h, j: (h, j, 0)),
                  pl.BlockSpec((None, QK_HEAD, t), lambda h, j: (h, 0, j)),
                  pl.BlockSpec((t, V_HEAD), lambda h, j: (j, h))],
        out_specs=[pl.BlockSpec((None, n, QK_HEAD, t), lambda h, j: (h, 0, 0, 0)),
                   pl.BlockSpec((None, t, QK_HEAD), lambda h, j: (h, j, 0)),
                   pl.BlockSpec((t, V_HEAD), lambda h, j: (j, h))],
        out_shape=[_sds((n_h, n, QK_HEAD, t)), _sds((n_h, s, QK_HEAD)), _sds((s, n_h * V_HEAD))],
        scratch_shapes=[pltpu.VMEM((t, QK_HEAD), F32), pltpu.VMEM((t, V_HEAD), F32),
                        pltpu.VMEM((2, t, t), F32), pltpu.VMEM((2, t, t), F32)],
        compiler_params=_params(("arbitrary", "arbitrary")),
    )(q, do, lse, delta, k, kt, v)


def _mla_out_fwd(x, ot, w_out):
    s, d = x.shape
    hv = ot.shape[0]
    tm = _tile(s, FFN_TILE)

    def body(x_ref, ot_ref, w_ref, y_ref):
        y_ref[...] = x_ref[...] + _mm_tn(_cast(ot_ref[...]), w_ref[...])

    tok = pl.BlockSpec((tm, d), lambda i: (i, 0))
    return pl.pallas_call(
        body, name="mla_out_fwd", grid=(s // tm,),
        in_specs=[tok, pl.BlockSpec((hv, tm), lambda i: (0, i)), _full(w_out.shape)],
        out_specs=tok, out_shape=_sds((s, d)),
        compiler_params=_params(("arbitrary",)),
    )(x, ot, w_out)


def _mla_out_bwd(dy, ot, w_out):
    s, d = dy.shape
    hv = ot.shape[0]
    n_h = hv // V_HEAD
    tm = _tile(s, FFN_TILE)

    def body(dy_ref, ot_ref, w_ref, do_ref, dl_ref, dw_ref):
        @pl.when(pl.program_id(0) == 0)
        def _():
            dw_ref[...] = jnp.zeros_like(dw_ref)

        dyb = _cast(dy_ref[...])
        o_t = ot_ref[...]
        do_ref[...] = _cast(_mm_nt(dyb, w_ref[...]))
        prod = _mm_nt(w_ref[...], dyb) * o_t
        for hh in range(n_h):
            dl_ref[hh] = jnp.sum(prod[hh * V_HEAD:(hh + 1) * V_HEAD], axis=0, keepdims=True)
        dw_ref[...] += _mm(_cast(o_t), dyb)

    return pl.pallas_call(
        body, name="mla_out_bwd", grid=(s // tm,),
        in_specs=[pl.BlockSpec((tm, d), lambda i: (i, 0)), pl.BlockSpec((hv, tm), lambda i: (0, i)),
                  _full(w_out.shape)],
        out_specs=[pl.BlockSpec((tm, hv), lambda i: (i, 0)), pl.BlockSpec((n_h, 1, tm), lambda i: (0, 0, i)),
                   _full(w_out.shape)],
        out_shape=[_sds((s, hv), MXU_DTYPE), _sds((n_h, 1, s)), _sds(w_out.shape)],
        compiler_params=_params(("arbitrary",)),
    )(dy, ot, w_out)


def _loss_and_grad(y, target):
    s, d = y.shape
    tm = _tile(s, ROW_TILE)

    def body(y_ref, t_ref, loss_ref, dy_ref):
        @pl.when(pl.program_id(0) == 0)
        def _():
            loss_ref[...] = jnp.zeros_like(loss_ref)

        err = y_ref[...] - t_ref[...]
        dy_ref[...] = err * (1.0 / d)
        loss_ref[...] += 0.5 * jnp.sum(jnp.mean(err * err, axis=-1, keepdims=True), axis=0, keepdims=True)

    tok = pl.BlockSpec((tm, d), lambda i: (i, 0))
    return pl.pallas_call(
        body, name="loss_and_grad", grid=(s // tm,),
        in_specs=[tok, tok], out_specs=[_full((1, 1)), tok],
        out_shape=[_sds((1, 1)), _sds((s, d))],
        compiler_params=_params(("arbitrary",)),
    )(y, target)


def _mesh_position():
    return lax.axis_index("x"), lax.axis_index("y"), lax.axis_index("c")


def _other_chips(x, y):
    return [(1 - x, y), (x, 1 - y), (1 - x, 1 - y)]


ANY = pl.BlockSpec(memory_space=pl.ANY)


def _gather_over_chips(arrs):
    n = len(arrs)
    halves = [a.shape[0] // 2 for a in arrs]
    assert all(a.shape[0] % 2 == 0 for a in arrs)
    own = 2 * (N_CHIPS - 1)

    def body(*refs):
        srcs, outs = refs[:n], refs[n:2 * n]
        send_sems, recv_sems = refs[2 * n:]
        x, y, c = _mesh_position()
        me, sibling = (x, y, c), (x, y, 1 - c)
        chips = _other_chips(x, y)
        my_chip = 2 * x + y

        def rows(t, chip, half):
            return outs[t].at[chip, pl.ds(half * halves[t], halves[t])]

        def copy(t, k, src, dst, to):
            return pltpu.make_async_remote_copy(src_ref=src, dst_ref=dst, send_sem=send_sems.at[t, k],
                                                recv_sem=recv_sems.at[t, k], device_id=to, device_id_type=MESH)

        started = []
        for t in range(n):
            for k, (px, py) in enumerate(chips):
                cp = copy(t, k, srcs[t].at[pl.ds(c * halves[t], halves[t])], rows(t, my_chip, c), (px, py, c))
                cp.start()
                started.append(cp)
            cp = copy(t, own, srcs[t], outs[t].at[my_chip], sibling)
            cp.start()
            started.append(cp)
        for t in range(n):
            for k, (px, py) in enumerate(chips):
                landed = rows(t, 2 * px + py, c)
                copy(t, k, landed, landed, me).wait_recv()
                cp = copy(t, N_CHIPS - 1 + k, landed, landed, sibling)
                cp.start()
                started.append(cp)
        for t in range(n):
            for k, (px, py) in enumerate(chips):
                passed = rows(t, 2 * px + py, 1 - c)
                copy(t, N_CHIPS - 1 + k, passed, passed, me).wait_recv()
            copy(t, own, srcs[t], outs[t].at[my_chip], me).wait_recv()
        for cp in started:
            cp.wait_send()

    return pl.pallas_call(
        body, name="gather_over_chips",
        in_specs=[ANY] * n, out_specs=[ANY] * n,
        out_shape=[_sds((N_CHIPS,) + a.shape, a.dtype) for a in arrs],
        scratch_shapes=[pltpu.SemaphoreType.DMA((n, own + 1)), pltpu.SemaphoreType.DMA((n, own + 1))],
    )(*arrs)


def _send_other_half_to_sibling(grads):
    n = len(grads)
    halves = [g.shape[1] // 2 for g in grads]

    def body(*refs):
        srcs, outs = refs[:n], refs[n:2 * n]
        send_sems, recv_sems = refs[2 * n:]
        x, y, c = _mesh_position()
        copies = []
        for t in range(n):
            cp = pltpu.make_async_remote_copy(
                src_ref=srcs[t].at[pl.ds(0, N_CHIPS), pl.ds((1 - c) * halves[t], halves[t])], dst_ref=outs[t],
                send_sem=send_sems.at[t], recv_sem=recv_sems.at[t], device_id=(x, y, 1 - c), device_id_type=MESH)
            cp.start()
            copies.append(cp)
        for cp in copies:
            cp.wait_recv()
        for cp in copies:
            cp.wait_send()

    return pl.pallas_call(
        body, name="send_other_half_to_sibling",
        in_specs=[ANY] * n, out_specs=[ANY] * n,
        out_shape=[_sds((N_CHIPS, h) + g.shape[2:]) for g, h in zip(grads, halves)],
        scratch_shapes=[pltpu.SemaphoreType.DMA((n,)), pltpu.SemaphoreType.DMA((n,))],
    )(*grads)


def _send_blocks_to_chips(parts):
    n = len(parts)

    def body(*refs):
        srcs, outs = refs[:n], refs[n:2 * n]
        send_sems, recv_sems = refs[2 * n:]
        x, y, c = _mesh_position()
        copies = []
        for t in range(n):
            for k, (px, py) in enumerate(_other_chips(x, y)):
                cp = pltpu.make_async_remote_copy(
                    src_ref=srcs[t].at[2 * px + py], dst_ref=outs[t].at[k], send_sem=send_sems.at[t, k],
                    recv_sem=recv_sems.at[t, k], device_id=(px, py, c), device_id_type=MESH)
                cp.start()
                copies.append(cp)
        for cp in copies:
            cp.wait_recv()
        for cp in copies:
            cp.wait_send()

    return pl.pallas_call(
        body, name="send_blocks_to_chips",
        in_specs=[ANY] * n, out_specs=[ANY] * n,
        out_shape=[_sds((N_CHIPS - 1,) + p.shape[1:], p.dtype) for p in parts],
        scratch_shapes=[pltpu.SemaphoreType.DMA((n, N_CHIPS - 1)), pltpu.SemaphoreType.DMA((n, N_CHIPS - 1))],
    )(*parts)


def _join_halves_with_sibling(sums):
    n = len(sums)

    def body(*refs):
        srcs, outs = refs[:n], refs[n:2 * n]
        send_sems, recv_sems = refs[2 * n:]
        x, y, c = _mesh_position()
        copies = []
        for t in range(n):
            h = srcs[t].shape[0] // 2
            mine = pl.ds(c * h, h)
            cp = pltpu.make_async_remote_copy(
                src_ref=srcs[t].at[mine], dst_ref=outs[t].at[mine], send_sem=send_sems.at[t],
                recv_sem=recv_sems.at[t], device_id=(x, y, 1 - c), device_id_type=MESH)
            cp.start()
            copies.append(cp)
        for t in range(n):
            h = srcs[t].shape[0] // 2
            theirs = pl.ds((1 - c) * h, h)
            pltpu.make_async_remote_copy(
                src_ref=srcs[t].at[theirs], dst_ref=outs[t].at[theirs], send_sem=send_sems.at[t],
                recv_sem=recv_sems.at[t], device_id=(x, y, 1 - c), device_id_type=MESH).wait_recv()
        for cp in copies:
            cp.wait_send()

    return pl.pallas_call(
        body, name="join_halves_with_sibling",
        in_specs=[ANY] * n, out_specs=[ANY] * n,
        out_shape=[_sds(a.shape) for a in sums],
        input_output_aliases={t: t for t in range(n)},
        scratch_shapes=[pltpu.SemaphoreType.DMA((n,)), pltpu.SemaphoreType.DMA((n,))],
    )(*sums)


def _gather_over_devices(rows):
    r = rows.shape[0]

    def body(in_ref, out_ref, send_sems, recv_sems, local_sem):
        x, y, c = _mesh_position()
        mine = pltpu.make_async_copy(in_ref, out_ref.at[4 * x + 2 * y + c], local_sem)
        mine.start()
        copies = []
        for mask in range(1, N_DEVICES):
            fx, fy, fc = (mask >> 2) & 1, (mask >> 1) & 1, mask & 1
            px, py, pc = (1 - x if fx else x), (1 - y if fy else y), (1 - c if fc else c)
            send = pltpu.make_async_remote_copy(
                src_ref=in_ref, dst_ref=out_ref.at[4 * x + 2 * y + c], send_sem=send_sems.at[mask - 1],
                recv_sem=recv_sems.at[mask - 1], device_id=(px, py, pc), device_id_type=MESH)
            send.start()
            recv = pltpu.make_async_remote_copy(
                src_ref=in_ref, dst_ref=out_ref.at[4 * px + 2 * py + pc], send_sem=send_sems.at[mask - 1],
                recv_sem=recv_sems.at[mask - 1], device_id=(px, py, pc), device_id_type=MESH)
            copies.append((send, recv))
        for _, recv in copies:
            recv.wait_recv()
        for send, _ in copies:
            send.wait_send()
        mine.wait()

    vm = pl.BlockSpec(memory_space=pltpu.VMEM)
    return pl.pallas_call(
        body, name="gather_over_devices", in_specs=[vm], out_specs=vm,
        out_shape=_sds((N_DEVICES, r, LANES)),
        scratch_shapes=[pltpu.SemaphoreType.DMA((N_DEVICES - 1,)), pltpu.SemaphoreType.DMA((N_DEVICES - 1,)),
                        pltpu.SemaphoreType.DMA],
    )(rows)


def _add_sibling_half(grad, received, chip, core):
    _, l, r, c = grad.shape
    half = l // 2

    def body(chip_ref, core_ref, g_ref, r_ref, wire_ref, own_ref):
        total = g_ref[...] + r_ref[...]
        wire_ref[...] = total.astype(WIRE_DTYPE)

        @pl.when(pl.program_id(1) == chip_ref[0])
        def _():
            own_ref[...] = total

    blk = lambda f: pl.BlockSpec((None, None, r, c), f)
    grid_spec = pltpu.PrefetchScalarGridSpec(
        num_scalar_prefetch=2, grid=(half, N_CHIPS),
        in_specs=[blk(lambda i, j, chip, core: (j, core[0] * half + i, 0, 0)),
                  blk(lambda i, j, chip, core: (j, i, 0, 0))],
        out_specs=[blk(lambda i, j, chip, core: (j, i, 0, 0)),
                   pl.BlockSpec((None, r, c), lambda i, j, chip, core: (i, 0, 0))])
    return pl.pallas_call(
        body, name="add_sibling_half", grid_spec=grid_spec,
        out_shape=[_sds((N_CHIPS, half, r, c), WIRE_DTYPE), _sds((half, r, c))],
        compiler_params=_params(("arbitrary", "arbitrary")),
    )(chip, core, grad, received)


def _add_chip_blocks(own, received, core):
    half, r, c = own.shape

    def body(core_ref, p_ref, r0_ref, r1_ref, r2_ref, o_ref):
        o_ref[...] = ((p_ref[...] + r0_ref[...].astype(F32)) + r1_ref[...].astype(F32)) + r2_ref[...].astype(F32)

    grid_spec = pltpu.PrefetchScalarGridSpec(
        num_scalar_prefetch=1, grid=(half,),
        in_specs=[pl.BlockSpec((None, r, c), lambda i, core: (i, 0, 0))] + [
            pl.BlockSpec((None, None, r, c), functools.partial(lambda i, core, k: (k, i, 0, 0), k=k))
            for k in range(N_CHIPS - 1)],
        out_specs=pl.BlockSpec((None, r, c), lambda i, core: (core[0] * half + i, 0, 0)))
    return pl.pallas_call(
        body, name="add_chip_blocks", grid_spec=grid_spec, out_shape=_sds((2 * half, r, c)),
        compiler_params=_params(("arbitrary",)),
    )(core, own, received, received, received)


def _sum_over_devices(parts):
    _, r, _ = parts.shape

    def body(p_ref, o_ref):
        acc = p_ref[0]
        for k in range(1, N_DEVICES):
            acc = acc + p_ref[k]
        o_ref[...] = acc

    return pl.pallas_call(body, name="sum_over_devices", out_shape=_sds((r, LANES)))(parts)


def _adamw_math(w, g, m, v):
    m = ADAM_B1 * m + (1.0 - ADAM_B1) * g
    v = ADAM_B2 * v + (1.0 - ADAM_B2) * (g * g)
    m_hat = m / (1.0 - ADAM_B1 ** ADAM_STEP)
    v_hat = v / (1.0 - ADAM_B2 ** ADAM_STEP)
    delta = -ADAM_LR * (m_hat / (jnp.sqrt(v_hat) + ADAM_EPS) + ADAM_WD * w)
    return delta, m, v


def _adamw_stacked(w, m, v, grads, offset):
    l, r, c = w.shape
    tr = r
    while tr * c * 4 > 2**20 and tr % 16 == 0:
        tr //= 2

    def body(w_ref, m_ref, v_ref, g_ref, go_ref, d_ref, mo_ref, vo_ref):
        g = g_ref[...]
        go_ref[...] = g
        d_ref[...], mo_ref[...], vo_ref[...] = _adamw_math(w_ref[...], g, m_ref[...], v_ref[...])

    blk = pl.BlockSpec((None, tr, c), lambda i, j: (i, j, 0))
    return pl.pallas_call(
        body, name="adamw_stacked", grid=(l, r // tr),
        in_specs=[blk, blk, blk, pl.BlockSpec((None, tr, c), lambda i, j: (offset + i, j, 0))],
        out_specs=[blk] * 4, out_shape=[_sds((l, r, c))] * 4,
        compiler_params=_params(("arbitrary", "arbitrary")),
    )(w, m, v, grads)


def _adamw_small(w, m, v, g):
    def body(w_ref, m_ref, v_ref, g_ref, d_ref, mo_ref, vo_ref):
        d_ref[...], mo_ref[...], vo_ref[...] = _adamw_math(w_ref[...], g_ref[...], m_ref[...], v_ref[...])

    return pl.pallas_call(body, name="adamw_small", out_shape=[_sds(w.shape)] * 3)(w, m, v, g)


def _pack_rows(arrs):
    flat = jnp.concatenate([a.reshape(-1) for a in arrs])
    pad = (-flat.shape[0]) % (8 * LANES)
    return jnp.pad(flat, (0, pad)).reshape(-1, LANES)


def _unpack_rows(rows, shapes, lead=()):
    flat = rows.reshape(lead + (-1,))
    out, at = [], 0
    for shp in shapes:
        size = int(np.prod(shp))
        out.append(flat[..., at:at + size].reshape(lead + tuple(shp)))
        at += size
    return out


WEIGHT_NAMES = ('ffn1_norm', 'ffn1_w_gate', 'ffn1_w_up', 'ffn1_w_down', 'mix_norm', 'pool_w', 'pool_scale',
                'mla_w_in', 'mla_q_norm', 'mla_w_q_up', 'mla_kv_norm', 'mla_w_kv_up', 'mla_q_head_norm',
                'mla_k_head_norm', 'mla_w_out', 'ffn2_norm', 'ffn2_w_gate', 'ffn2_w_up', 'ffn2_w_down')


def _chips_to_columns(g):
    return jnp.transpose(g, (1, 2, 0, 3)).reshape(g.shape[1], g.shape[2], -1)


def _columns_to_chips(full):
    n, r, c4 = full.shape
    return jnp.transpose(full.reshape(n, r, N_CHIPS, c4 // N_CHIPS), (2, 0, 1, 3))


def kernel(x, positions, ffn1_norm, ffn1_w_gate, ffn1_w_up, ffn1_w_down, mix_norm, pool_w, pool_scale, mla_w_in, mla_q_norm, mla_w_q_up, mla_kv_norm, mla_w_kv_up, mla_q_head_norm, mla_k_head_norm, mla_w_out, ffn2_norm, ffn2_w_gate, ffn2_w_up, ffn2_w_down, loss_target, m_ffn1_norm, m_ffn1_w_gate, m_ffn1_w_up, m_ffn1_w_down, m_mix_norm, m_pool_w, m_pool_scale, m_mla_w_in, m_mla_q_norm, m_mla_w_q_up, m_mla_kv_norm, m_mla_w_kv_up, m_mla_q_head_norm, m_mla_k_head_norm, m_mla_w_out, m_ffn2_norm, m_ffn2_w_gate, m_ffn2_w_up, m_ffn2_w_down, v_ffn1_norm, v_ffn1_w_gate, v_ffn1_w_up, v_ffn1_w_down, v_mix_norm, v_pool_w, v_pool_scale, v_mla_w_in, v_mla_q_norm, v_mla_w_q_up, v_mla_kv_norm, v_mla_w_kv_up, v_mla_q_head_norm, v_mla_k_head_norm, v_mla_w_out, v_ffn2_norm, v_ffn2_w_gate, v_ffn2_w_up, v_ffn2_w_down):
    env = dict(locals())
    w = {n: env[n] for n in WEIGHT_NAMES}
    mom = {n: env["m_" + n] for n in WEIGHT_NAMES}
    var = {n: env["v_" + n] for n in WEIGHT_NAMES}

    s, d = x.shape[1], x.shape[2]
    depth = ffn1_norm.shape[0]
    n_mla, n_pool, n_groups = mla_w_in.shape[0], pool_w.shape[0], pool_w.shape[1]
    pool_c = pool_w.shape[3]
    q_lora = N_CHIPS * mla_q_norm.shape[1]
    kv_lora = N_CHIPS * mla_kv_norm.shape[1]
    n_heads = N_CHIPS * mla_w_q_up.shape[2] // QK_HEAD
    t_fwd, t_bwd = _tile(s, ATTN_FWD_TILE), _tile(s, ATTN_BWD_TILE)
    cx, cy, cc = _mesh_position()
    chip = 2 * cx + cy
    chip_arr = jnp.reshape(chip, (1,)).astype(jnp.int32)
    core_arr = jnp.reshape(cc, (1,)).astype(jnp.int32)

    shard_gu = _cast(jnp.concatenate([ffn1_w_gate, ffn1_w_up, ffn2_w_gate, ffn2_w_up], axis=0))
    shard_dn = _cast(jnp.concatenate([ffn1_w_down, ffn2_w_down], axis=0))
    shard_pool = _cast(pool_w.reshape((n_pool * n_groups,) + pool_w.shape[2:]))
    w_gu, w_dn, g_in, g_qup, g_kvup, g_out, g_pool = _gather_over_chips(
        [shard_gu, shard_dn, _cast(mla_w_in), _cast(mla_w_q_up), _cast(mla_w_kv_up), _cast(mla_w_out), shard_pool])
    small_shapes = [mla_q_norm.shape, mla_kv_norm.shape]
    small = _gather_over_devices(_pack_rows([mla_q_norm, mla_kv_norm]))[::2]
    qn_chips, kvn_chips = _unpack_rows(small, small_shapes, lead=(N_CHIPS,))
    q_norm_full = jnp.transpose(qn_chips, (1, 0, 2)).reshape(n_mla, 1, q_lora)
    kv_norm_full = jnp.transpose(kvn_chips, (1, 0, 2)).reshape(n_mla, 1, kv_lora)

    w_in_full = _chips_to_columns(g_in)
    w_q_heads = jnp.transpose(_chips_to_columns(g_qup).reshape(n_mla, q_lora, n_heads, QK_HEAD), (0, 2, 1, 3))
    w_kv = _chips_to_columns(g_kvup).reshape(n_mla, kv_lora, n_heads, QK_NOPE + V_HEAD)
    w_kn_heads = jnp.transpose(w_kv[..., :QK_NOPE], (0, 2, 1, 3))
    w_v_full = w_kv[..., QK_NOPE:].reshape(n_mla, kv_lora, n_heads * V_HEAD)
    w_out_full = jnp.transpose(g_out, (1, 0, 2, 3)).reshape(n_mla, n_heads * V_HEAD, d)
    pool_full = jnp.transpose(g_pool.reshape(N_CHIPS, n_pool, n_groups, pool_c // N_CHIPS, pool_c),
                              (1, 2, 0, 3, 4)).reshape(n_pool, n_groups, pool_c, pool_c)

    inv_freq = (1.0 / (ROPE_THETA ** (jnp.arange(0, QK_ROPE, 2, dtype=F32) / QK_ROPE))).reshape(1, -1)
    cos_t, sin_t = _rope_tables(positions.reshape(s, 1), inv_freq)

    row = lambda a, i: a[i].reshape(1, -1)
    i_gate1, i_up1, i_gate2, i_up2 = (lambda i: i), (lambda i: depth + i), (lambda i: 2 * depth + i), (lambda i: 3 * depth + i)
    i_dn1, i_dn2 = (lambda i: i), (lambda i: depth + i)

    h = x.reshape(s, d)
    saved = []
    for i in range(depth):
        rec = {"x_ffn1": h}
        h, *rec["ffn1"] = _ffn_fwd(h, row(ffn1_norm, i), w_gu, w_dn, i_gate1(i), i_up1(i), i_dn1(i))
        rec["x_mix"] = h
        j = i // 2
        if i % 2 == 0:
            h = _pool_fwd(h, row(mix_norm, i), pool_full[j], row(pool_scale, j))
        else:
            lat, q, k, v = _mla_qkv_fwd(h, row(mix_norm, i), w_in_full[j], q_norm_full[j], kv_norm_full[j],
                                        w_q_heads[j], w_kn_heads[j], w_v_full[j], row(mla_q_head_norm, j),
                                        row(mla_k_head_norm, j), cos_t, sin_t)
            vt = jnp.transpose(v.reshape(s // t_fwd, t_fwd, n_heads, V_HEAD), (2, 0, 3, 1))
            vt = jnp.concatenate([vt, jnp.ones((n_heads, s // t_fwd, ONES_ROWS, t_fwd), vt.dtype)], axis=2)
            ot, lse = _flash_fwd(q, k, vt)
            rec.update(lat=lat, q=q, k=k, v=v, ot=ot, lse=lse)
            h = _mla_out_fwd(h, ot, w_out_full[j])
        rec["x_ffn2"] = h
        h, *rec["ffn2"] = _ffn_fwd(h, row(ffn2_norm, i), w_gu, w_dn, i_gate2(i), i_up2(i), i_dn2(i))
        saved.append(rec)

    loss_part, dy = _loss_and_grad(h, loss_target.reshape(s, d))
    loss = lax.psum(loss_part[0, 0], ("x", "y", "c"))

    g_gu = [None] * (4 * depth)
    g_dn = [None] * (2 * depth)
    g_norm = {n: [None] * depth for n in ("ffn1_norm", "mix_norm", "ffn2_norm")}
    g_pool_w, g_pool_scale = [None] * n_pool, [None] * n_pool
    g_mla = {n: [None] * n_mla for n in ("w_in", "q_norm", "kv_norm", "w_q", "w_kv", "qhn", "khn", "w_out")}
    for i in reversed(range(depth)):
        rec = saved[i]
        hb, gate, up = rec["ffn2"]
        dy, g_norm["ffn2_norm"][i], dyb, dgt, dup, act = _ffn_bwd_dgrad(
            rec["x_ffn2"], row(ffn2_norm, i), dy, gate, up, w_gu, w_dn, i_gate2(i), i_up2(i), i_dn2(i))
        g_gu[i_gate2(i)], g_gu[i_up2(i)], g_dn[i_dn2(i)] = _ffn_wgrad(hb, dyb, dgt, dup, act)
        j = i // 2
        if i % 2 == 0:
            dy, g_norm["mix_norm"][i], g_pool_w[j], g_pool_scale[j] = _pool_bwd(
                rec["x_mix"], row(mix_norm, i), pool_full[j], row(pool_scale, j), dy)
        else:
            do, delta, g_mla["w_out"][j] = _mla_out_bwd(dy, rec["ot"], w_out_full[j])
            by_tile = lambda a: a.reshape(n_heads, s // t_bwd, 1, t_bwd)
            dqt, dk, dv = _flash_bwd(rec["q"], rec["k"], jnp.transpose(rec["k"], (0, 2, 1)), rec["v"], do,
                                     by_tile(rec["lse"]), by_tile(delta))
            dq = jnp.transpose(dqt, (0, 1, 3, 2)).reshape(n_heads, s, QK_HEAD)
            (dy, g_norm["mix_norm"][i], g_mla["w_in"][j], g_mla["q_norm"][j], g_mla["kv_norm"][j], dwq, dwkn, dwv,
             g_mla["qhn"][j], g_mla["khn"][j]) = _mla_qkv_bwd(
                rec["x_mix"], rec["lat"], dy, dq, dk, dv, row(mix_norm, i), w_in_full[j], q_norm_full[j],
                kv_norm_full[j], w_q_heads[j], w_kn_heads[j], w_v_full[j], row(mla_q_head_norm, j),
                row(mla_k_head_norm, j), cos_t, sin_t)
            g_mla["w_q"][j] = jnp.transpose(dwq, (1, 0, 2)).reshape(q_lora, n_heads * QK_HEAD)
            g_mla["w_kv"][j] = jnp.concatenate(
                [jnp.transpose(dwkn, (1, 0, 2)), dwv.reshape(kv_lora, n_heads, V_HEAD)], axis=-1
            ).reshape(kv_lora, n_heads * (QK_NOPE + V_HEAD))
        hb, gate, up = rec["ffn1"]
        dy, g_norm["ffn1_norm"][i], dyb, dgt, dup, act = _ffn_bwd_dgrad(
            rec["x_ffn1"], row(ffn1_norm, i), dy, gate, up, w_gu, w_dn, i_gate1(i), i_up1(i), i_dn1(i))
        g_gu[i_gate1(i)], g_gu[i_up1(i)], g_dn[i_dn1(i)] = _ffn_wgrad(hb, dyb, dgt, dup, act)
    grad_x = dy.reshape(x.shape)

    full_grads = [
        jnp.stack(g_gu, axis=1),
        jnp.stack(g_dn, axis=1),
        _columns_to_chips(jnp.stack(g_mla["w_in"])),
        _columns_to_chips(jnp.stack(g_mla["w_q"])),
        _columns_to_chips(jnp.stack(g_mla["w_kv"])),
        jnp.transpose(jnp.stack(g_mla["w_out"]).reshape(n_mla, N_CHIPS, -1, d), (1, 0, 2, 3)),
        jnp.transpose(jnp.stack(g_pool_w).reshape(n_pool, n_groups, N_CHIPS, pool_c // N_CHIPS, pool_c),
                      (2, 0, 1, 3, 4)).reshape(N_CHIPS, n_pool * n_groups, pool_c // N_CHIPS, pool_c),
    ]
    from_sibling = _send_other_half_to_sibling(full_grads)
    chip_sums = [_add_sibling_half(g, r, chip_arr, core_arr) for g, r in zip(full_grads, from_sibling)]
    from_chips = _send_blocks_to_chips([wire for wire, _ in chip_sums])
    half_sums = [_add_chip_blocks(own, r, core_arr) for (_, own), r in zip(chip_sums, from_chips)]
    r_gu, r_dn, r_in, r_qup, r_kvup, r_out, r_pool = _join_halves_with_sibling(half_sums)

    small_grads = [jnp.concatenate(g_norm["ffn1_norm"]), jnp.concatenate(g_norm["mix_norm"]),
                   jnp.concatenate(g_norm["ffn2_norm"]), jnp.concatenate(g_pool_scale),
                   jnp.concatenate(g_mla["qhn"]), jnp.concatenate(g_mla["khn"]),
                   jnp.concatenate(g_mla["q_norm"]), jnp.concatenate(g_mla["kv_norm"])]
    small_sum = _sum_over_devices(_gather_over_devices(_pack_rows(small_grads)))
    (s_ffn1, s_mix, s_ffn2, s_pscale, s_qhn, s_khn, s_qn, s_kvn) = _unpack_rows(small_sum, [g.shape for g in small_grads])
    qn_w, kvn_w = mla_q_norm.shape[1], mla_kv_norm.shape[1]
    s_qn = lax.dynamic_slice_in_dim(s_qn, chip * qn_w, qn_w, axis=1)
    s_kvn = lax.dynamic_slice_in_dim(s_kvn, chip * kvn_w, kvn_w, axis=1)

    grads, deltas, new_m, new_v = {}, {}, {}, {}

    def stacked(name, reduced, offset):
        shape = w[name].shape
        as3 = lambda a: a.reshape((-1,) + shape[-2:])
        out = _adamw_stacked(as3(w[name]), as3(mom[name]), as3(var[name]), reduced, offset)
        grads[name], deltas[name], new_m[name], new_v[name] = [o.reshape(shape) for o in out]

    def small_update(name, g):
        grads[name] = g
        deltas[name], new_m[name], new_v[name] = _adamw_small(w[name], mom[name], var[name], g)

    stacked("ffn1_w_gate", r_gu, 0)
    stacked("ffn1_w_up", r_gu, depth)
    stacked("ffn2_w_gate", r_gu, 2 * depth)
    stacked("ffn2_w_up", r_gu, 3 * depth)
    stacked("ffn1_w_down", r_dn, 0)
    stacked("ffn2_w_down", r_dn, depth)
    stacked("mla_w_in", r_in, 0)
    stacked("mla_w_q_up", r_qup, 0)
    stacked("mla_w_kv_up", r_kvup, 0)
    stacked("mla_w_out", r_out, 0)
    stacked("pool_w", r_pool, 0)
    small_update("ffn1_norm", s_ffn1)
    small_update("mix_norm", s_mix)
    small_update("ffn2_norm", s_ffn2)
    small_update("pool_scale", s_pscale)
    small_update("mla_q_head_norm", s_qhn)
    small_update("mla_k_head_norm", s_khn)
    small_update("mla_q_norm", s_qn)
    small_update("mla_kv_norm", s_kvn)

    return (loss, grad_x, *[grads[n] for n in WEIGHT_NAMES], *[deltas[n] for n in WEIGHT_NAMES],
            *[new_m[n] for n in WEIGHT_NAMES], *[new_v[n] for n in WEIGHT_NAMES])
```

```python
import functools

import numpy as np

import jax
import jax.numpy as jnp
from jax import lax
from jax.experimental import pallas as pl
from jax.experimental.pallas import tpu as pltpu

F32 = jnp.float32
MXU_DTYPE = jnp.bfloat16
WIRE_DTYPE = jnp.bfloat16
MESH = pl.DeviceIdType.MESH
N_CHIPS = 4
N_DEVICES = 8
LANES = 128
VMEM_LIMIT_BYTES = 56 * 2**20
NORM_EPS = 1e-6
QK_NOPE, QK_ROPE, V_HEAD = 128, 64, 128
QK_HEAD = QK_NOPE + QK_ROPE
SCORE_SCALE = QK_HEAD ** -0.5
ONES_ROWS = 8
ROPE_THETA = 10000.0
POOL_WINDOWS = (2, 4, 8, 16)
POOL_HALO = 16
FFN_HALF = 0.5
ADAM_LR, ADAM_B1, ADAM_B2, ADAM_EPS, ADAM_WD, ADAM_STEP = 0.001, 0.9, 0.999, 1e-08, 0.01, 10
FFN_TILE = 512
FFN_BWD_TILE = 256
WGRAD_TILE = 2048
MLA_TILE = 256
POOL_TILE = 512
ATTN_FWD_TILE = 1024
ATTN_BWD_TILE = 512
ROW_TILE = 1024


def _cast(v):
    return v.astype(MXU_DTYPE)


def _mm(a, b):
    return jnp.dot(a, b, preferred_element_type=F32)


def _mm_nt(a, b):
    return lax.dot_general(a, b, (((1,), (1,)), ((), ())), preferred_element_type=F32)


def _mm_tn(a, b):
    return lax.dot_general(a, b, (((0,), (0,)), ((), ())), preferred_element_type=F32)


def _rms_fwd(v, gain):
    r = lax.rsqrt(jnp.mean(v * v, axis=-1, keepdims=True) + NORM_EPS)
    return v * r * gain, r


def _rms_bwd(v, r, gain, dy):
    vr = v * r
    gy = dy * gain
    dv = r * (gy - vr * jnp.mean(gy * vr, axis=-1, keepdims=True))
    return dv, jnp.sum(dy * vr, axis=0, keepdims=True)


def _params(semantics=None):
    return pltpu.CompilerParams(dimension_semantics=semantics, vmem_limit_bytes=VMEM_LIMIT_BYTES)


def _tile(n, want):
    t = min(n, want)
    assert n % t == 0, (n, want)
    return t


def _full(shape):
    nd = len(shape)
    return pl.BlockSpec(shape, lambda *_: (0,) * nd)


def _sds(shape, dtype=F32):
    return jax.ShapeDtypeStruct(shape, dtype)


def _ffn_fwd(x, gain, w_gu, w_dn, i_gate, i_up, i_down):
    s, d = x.shape
    fs = w_gu.shape[-2]
    tm = _tile(s, FFN_TILE)

    def body(x_ref, g_ref, wg_ref, wu_ref, wd_ref, y_ref, hb_ref, gate_ref, up_ref):
        h, _ = _rms_fwd(x_ref[...], g_ref[...])
        hb = _cast(h)
        hb_ref[...] = hb
        pre = [(_mm_nt(hb, wg_ref[c]), _mm_nt(hb, wu_ref[c])) for c in range(N_CHIPS)]
        out = None
        for c, (g, u) in enumerate(pre):
            gate_ref[c] = _cast(g)
            up_ref[c] = _cast(u)
            part = _mm(_cast((g * jax.nn.sigmoid(g)) * u), wd_ref[c])
            out = part if out is None else out + part
        y_ref[...] = x_ref[...] + FFN_HALF * out

    resident = dict(pipeline_mode=pl.Buffered(1))
    tok = pl.BlockSpec((tm, d), lambda i: (i, 0))
    chunks = pl.BlockSpec((N_CHIPS, tm, fs), lambda i: (0, i, 0))
    return pl.pallas_call(
        body, name="ffn_fwd", grid=(s // tm,),
        in_specs=[
            tok, _full((1, d)),
            pl.BlockSpec((N_CHIPS, None, fs, d), lambda i: (0, i_gate, 0, 0), **resident),
            pl.BlockSpec((N_CHIPS, None, fs, d), lambda i: (0, i_up, 0, 0), **resident),
            pl.BlockSpec((N_CHIPS, None, fs, d), lambda i: (0, i_down, 0, 0), **resident),
        ],
        out_specs=[tok, tok, chunks, chunks],
        out_shape=[_sds((s, d)), _sds((s, d), MXU_DTYPE), _sds((N_CHIPS, s, fs), MXU_DTYPE),
                   _sds((N_CHIPS, s, fs), MXU_DTYPE)],
        compiler_params=_params(("arbitrary",)),
    )(x, gain, w_gu, w_gu, w_dn)


def _ffn_bwd_dgrad(x, gain, dy, gate, up, w_gu, w_dn, i_gate, i_up, i_down):
    s, d = x.shape
    fs = w_gu.shape[-2]
    tm = _tile(s, FFN_BWD_TILE)

    def body(x_ref, g_ref, dy_ref, gate_ref, up_ref, wg_ref, wu_ref, wd_ref,
             dx_ref, dgain_ref, dyb_ref, dg_ref, du_ref, act_ref):
        i = pl.program_id(0)
        dyb = _cast(dy_ref[...])
        dyb_ref[...] = dyb
        dacts = [_mm_nt(dyb, wd_ref[c]) for c in range(N_CHIPS)]
        dh = None
        for c in range(N_CHIPS):
            g = gate_ref[c].astype(F32)
            u = up_ref[c].astype(F32)
            sg = jax.nn.sigmoid(g)
            silu = g * sg
            dact = FFN_HALF * dacts[c]
            dgb = _cast(dact * u * (sg * (1.0 + g * (1.0 - sg))))
            dub = _cast(dact * silu)
            dg_ref[c] = dgb
            du_ref[c] = dub
            act_ref[c] = _cast(silu * u)
            part = _mm(dgb, wg_ref[c]) + _mm(dub, wu_ref[c])
            dh = part if dh is None else dh + part
        _, r = _rms_fwd(x_ref[...], g_ref[...])
        dxn, dgn = _rms_bwd(x_ref[...], r, g_ref[...], dh)
        dx_ref[...] = dy_ref[...] + dxn

        @pl.when(i == 0)
        def _():
            dgain_ref[...] = dgn

        @pl.when(i > 0)
        def _():
            dgain_ref[...] += dgn

    resident = dict(pipeline_mode=pl.Buffered(1))
    tok = pl.BlockSpec((tm, d), lambda i: (i, 0))
    chunks = pl.BlockSpec((N_CHIPS, tm, fs), lambda i: (0, i, 0))
    return pl.pallas_call(
        body, name="ffn_bwd_dgrad", grid=(s // tm,),
        in_specs=[
            tok, _full((1, d)), tok, chunks, chunks,
            pl.BlockSpec((N_CHIPS, None, fs, d), lambda i: (0, i_gate, 0, 0), **resident),
            pl.BlockSpec((N_CHIPS, None, fs, d), lambda i: (0, i_up, 0, 0), **resident),
            pl.BlockSpec((N_CHIPS, None, fs, d), lambda i: (0, i_down, 0, 0), **resident),
        ],
        out_specs=[tok, _full((1, d)), tok, chunks, chunks, chunks],
        out_shape=[_sds((s, d)), _sds((1, d)), _sds((s, d), MXU_DTYPE),
                   _sds((N_CHIPS, s, fs), MXU_DTYPE), _sds((N_CHIPS, s, fs), MXU_DTYPE),
                   _sds((N_CHIPS, s, fs), MXU_DTYPE)],
        compiler_params=_params(("arbitrary",)),
    )(x, gain, dy, gate, up, w_gu, w_gu, w_dn)


def _ffn_wgrad(hb, dyb, dg, du, act):
    s, d = hb.shape
    fs = dg.shape[-1]
    tk = _tile(s, WGRAD_TILE)
    n_k = s // tk

    def body(h_ref, dy_ref, dg_ref, du_ref, act_ref, wg_ref, wu_ref, wd_ref):
        k = pl.program_id(1)

        @pl.when(k == 0)
        def _():
            wg_ref[...] = jnp.zeros_like(wg_ref)
            wu_ref[...] = jnp.zeros_like(wu_ref)
            wd_ref[...] = jnp.zeros_like(wd_ref)

        h = h_ref[...]
        wg_ref[...] += _mm_tn(dg_ref[...], h)
        wu_ref[...] += _mm_tn(du_ref[...], h)
        wd_ref[...] += FFN_HALF * _mm_tn(act_ref[...], dy_ref[...])

    tok = pl.BlockSpec((tk, d), lambda j, k: (k, 0))
    chunk = pl.BlockSpec((None, tk, fs), lambda j, k: (j, k, 0))
    return pl.pallas_call(
        body, name="ffn_wgrad", grid=(N_CHIPS, n_k),
        in_specs=[tok, tok, chunk, chunk, chunk],
        out_specs=[pl.BlockSpec((None, fs, d), lambda j, k: (j, 0, 0)),
                   pl.BlockSpec((None, fs, d), lambda j, k: (j, 0, 0)),
                   pl.BlockSpec((None, fs, d), lambda j, k: (j, 0, 0))],
        out_shape=[_sds((N_CHIPS, fs, d))] * 3,
        compiler_params=_params(("arbitrary", "arbitrary")),
    )(hb, dyb, dg, du, act)


def _inv_count(first_row, n_rows, window):
    t = first_row + lax.broadcasted_iota(jnp.int32, (n_rows, 1), 0)
    return 1.0 / jnp.minimum(t + 1, window).astype(F32)


def _trailing_sum(v, window):
    k = 1
    while k < window:
        v = v + pltpu.roll(v, k, 0)
        k *= 2
    return v


def _leading_sum(v, window):
    n = v.shape[0]
    k = 1
    while k < window:
        v = v + pltpu.roll(v, n - k, 0)
        k *= 2
    return v


def _pool_normed_rows(x_ref, prev_ref, g_ref, i):
    h, r = _rms_fwd(x_ref[...], g_ref[...])
    hp, _ = _rms_fwd(prev_ref[...], g_ref[...])
    hp = jnp.where(i > 0, hp, 0.0)
    return jnp.concatenate([hp, h], axis=0), r


def _pooled_group(he, g, pg, first_row, tm):
    ue = he[:, g * pg:(g + 1) * pg]
    win = _trailing_sum(ue, POOL_WINDOWS[g])[POOL_HALO:]
    return win * _inv_count(first_row, tm, POOL_WINDOWS[g]) - ue[POOL_HALO:]


def _pool_specs(s, d, tm):
    per = tm // POOL_HALO
    last = s // POOL_HALO - 1
    tok = pl.BlockSpec((tm, d), lambda i: (i, 0))
    prev = pl.BlockSpec((POOL_HALO, d), lambda i: (jnp.maximum(i * per - 1, 0), 0))
    nxt = pl.BlockSpec((POOL_HALO, d), lambda i: (jnp.minimum((i + 1) * per, last), 0))
    return tok, prev, nxt


def _pool_fwd(x, gain, w, scale):
    s, d = x.shape
    n_g, pg = w.shape[0], w.shape[-1]
    tm = _tile(s, POOL_TILE)
    tok, prev, _ = _pool_specs(s, d, tm)

    def body(x_ref, prev_ref, g_ref, w_ref, sc_ref, y_ref):
        i = pl.program_id(0)
        he, _ = _pool_normed_rows(x_ref, prev_ref, g_ref, i)
        z = [_mm(_cast(_pooled_group(he, g, pg, i * tm, tm)), w_ref[g]) for g in range(n_g)]
        y_ref[...] = x_ref[...] + jnp.concatenate(z, axis=-1) * sc_ref[...]

    return pl.pallas_call(
        body, name="pool_fwd", grid=(s // tm,),
        in_specs=[tok, prev, _full((1, d)), _full(w.shape), _full((1, d))],
        out_specs=tok, out_shape=_sds((s, d)),
        compiler_params=_params(("arbitrary",)),
    )(x, x, gain, w, scale)


def _pool_bwd(x, gain, w, scale, dy):
    s, d = x.shape
    n_g, pg = w.shape[0], w.shape[-1]
    tm = _tile(s, POOL_TILE)
    n_tiles = s // tm
    tok, prev, nxt = _pool_specs(s, d, tm)

    def body(x_ref, prev_ref, dy_ref, next_ref, g_ref, w_ref, sc_ref, dx_ref, dgain_ref, dw_ref, dsc_ref):
        i = pl.program_id(0)

        @pl.when(i == 0)
        def _():
            dgain_ref[...] = jnp.zeros_like(dgain_ref)
            dw_ref[...] = jnp.zeros_like(dw_ref)
            dsc_ref[...] = jnp.zeros_like(dsc_ref)

        he, r = _pool_normed_rows(x_ref, prev_ref, g_ref, i)
        dy = dy_ref[...]
        dyn = jnp.where(i < n_tiles - 1, next_ref[...], 0.0)
        dze = jnp.concatenate([dy, dyn], axis=0) * sc_ref[...]
        dh, dsc = [], []
        for g in range(n_g):
            cols = slice(g * pg, (g + 1) * pg)
            pooled = _cast(_pooled_group(he, g, pg, i * tm, tm))
            dsc.append(jnp.sum(dy[:, cols] * _mm(pooled, w_ref[g]), axis=0, keepdims=True))
            dzb = _cast(dze[:, cols])
            dw_ref[g] += _mm_tn(pooled, dzb[:tm])
            dpool = _mm_nt(dzb, w_ref[g])
            spread = _leading_sum(dpool * _inv_count(i * tm, tm + POOL_HALO, POOL_WINDOWS[g]), POOL_WINDOWS[g])
            dh.append(spread[:tm] - dpool[:tm])
        dsc_ref[...] += jnp.concatenate(dsc, axis=-1)
        dxn, dgn = _rms_bwd(x_ref[...], r, g_ref[...], jnp.concatenate(dh, axis=-1))
        dgain_ref[...] += dgn
        dx_ref[...] = dy + dxn

    return pl.pallas_call(
        body, name="pool_bwd", grid=(n_tiles,),
        in_specs=[tok, prev, tok, nxt, _full((1, d)), _full(w.shape), _full((1, d))],
        out_specs=[tok, _full((1, d)), _full(w.shape), _full((1, d))],
        out_shape=[_sds((s, d)), _sds((1, d)), _sds(w.shape), _sds((1, d))],
        compiler_params=_params(("arbitrary",)),
    )(x, x, dy, dy, gain, w, scale)


def _rope_tables(pos_col, inv_freq):
    s = pos_col.shape[0]
    tm = _tile(s, ROW_TILE)
    half = QK_ROPE // 2

    def body(p_ref, f_ref, c_ref, s_ref):
        ang = p_ref[...].astype(F32) * f_ref[...]
        cos, sin = jnp.cos(ang), jnp.sin(ang)
        c_ref[...] = jnp.concatenate([jnp.ones((tm, QK_NOPE), F32), cos, cos], axis=-1)
        s_ref[...] = jnp.concatenate([jnp.zeros((tm, QK_NOPE), F32), -sin, sin], axis=-1)

    tab = pl.BlockSpec((tm, QK_HEAD), lambda i: (i, 0))
    return pl.pallas_call(
        body, name="rope_tables", grid=(s // tm,),
        in_specs=[pl.BlockSpec((tm, 1), lambda i: (i, 0)), _full((1, half))],
        out_specs=[tab, tab], out_shape=[_sds((s, QK_HEAD)), _sds((s, QK_HEAD))],
        compiler_params=_params(("arbitrary",)),
    )(pos_col, inv_freq)


def _swap_rope_halves(v):
    half = QK_ROPE // 2
    return jnp.concatenate([v[:, :QK_NOPE], v[:, QK_NOPE + half:], v[:, QK_NOPE:QK_NOPE + half]], axis=-1)


def _rope(v, cos, sin):
    return v * cos + _swap_rope_halves(v) * sin


def _rope_transposed(dv, cos, sin):
    return dv * cos + _swap_rope_halves(dv * sin)


def _mla_qkv_fwd(x, gain, w_in, q_norm, kv_norm, w_q, w_kn, w_v, q_head_norm, k_head_norm, cos, sin):
    s, d = x.shape
    n_h, ql = w_q.shape[0], w_q.shape[1]
    kvl, lat_w = w_kn.shape[1], w_in.shape[1]
    tm = _tile(s, MLA_TILE)

    def body(x_ref, g_ref, win_ref, qn_ref, kvn_ref, wq_ref, wkn_ref, wv_ref, qhn_ref, khn_ref, c_ref, s_ref,
             lat_ref, q_ref, k_ref, v_ref):
        h, _ = _rms_fwd(x_ref[...], g_ref[...])
        lat = _mm(_cast(h), win_ref[...])
        lat_ref[...] = lat
        cqn, _ = _rms_fwd(lat[:, :ql], qn_ref[...])
        ckvn, _ = _rms_fwd(lat[:, ql:ql + kvl], kvn_ref[...])
        kpe = lat[:, ql + kvl:]
        cqb, ckb = _cast(cqn), _cast(ckvn)
        cos_t, sin_t = c_ref[...], s_ref[...]
        v_ref[...] = _cast(_mm(ckb, wv_ref[...]))
        for hh in range(n_h):
            qn, _ = _rms_fwd(_mm(cqb, wq_ref[hh]), qhn_ref[...])
            q_ref[hh] = _cast(_rope(qn, cos_t, sin_t) * SCORE_SCALE)
            kn, _ = _rms_fwd(jnp.concatenate([_mm(ckb, wkn_ref[hh]), kpe], axis=-1), khn_ref[...])
            k_ref[hh] = _cast(_rope(kn, cos_t, sin_t))

    tok = lambda w: pl.BlockSpec((tm, w), lambda i: (i, 0))
    heads = pl.BlockSpec((n_h, tm, QK_HEAD), lambda i: (0, i, 0))
    return pl.pallas_call(
        body, name="mla_qkv_fwd", grid=(s // tm,),
        in_specs=[tok(d), _full((1, d)), _full(w_in.shape), _full((1, ql)), _full((1, kvl)), _full(w_q.shape),
                  _full(w_kn.shape), _full(w_v.shape), _full((1, QK_HEAD)), _full((1, QK_HEAD)),
                  tok(QK_HEAD), tok(QK_HEAD)],
        out_specs=[tok(lat_w), heads, heads, tok(n_h * V_HEAD)],
        out_shape=[_sds((s, lat_w)), _sds((n_h, s, QK_HEAD), MXU_DTYPE), _sds((n_h, s, QK_HEAD), MXU_DTYPE),
                   _sds((s, n_h * V_HEAD), MXU_DTYPE)],
        compiler_params=_params(("arbitrary",)),
    )(x, gain, w_in, q_norm, kv_norm, w_q, w_kn, w_v, q_head_norm, k_head_norm, cos, sin)


def _mla_qkv_bwd(x, lat, dy, dq, dk, dv, gain, w_in, q_norm, kv_norm, w_q, w_kn, w_v, q_head_norm, k_head_norm,
                 cos, sin):
    s, d = x.shape
    n_h, ql = w_q.shape[0], w_q.shape[1]
    kvl, lat_w = w_kn.shape[1], w_in.shape[1]
    tm = _tile(s, MLA_TILE)

    def body(x_ref, lat_ref, dy_ref, dq_ref, dk_ref, dv_ref, g_ref, win_ref, qn_ref, kvn_ref, wq_ref, wkn_ref,
             wv_ref, qhn_ref, khn_ref, c_ref, s_ref,
             dx_ref, dg_ref, dwin_ref, dqn_ref, dkvn_ref, dwq_ref, dwkn_ref, dwv_ref, dqhn_ref, dkhn_ref):
        @pl.when(pl.program_id(0) == 0)
        def _():
            for ref in (dg_ref, dwin_ref, dqn_ref, dkvn_ref, dwq_ref, dwkn_ref, dwv_ref, dqhn_ref, dkhn_ref):
                ref[...] = jnp.zeros_like(ref)

        x_t = x_ref[...]
        h, r = _rms_fwd(x_t, g_ref[...])
        hb = _cast(h)
        lat = lat_ref[...]
        cq, ckv, kpe = lat[:, :ql], lat[:, ql:ql + kvl], lat[:, ql + kvl:]
        cqn, rq = _rms_fwd(cq, qn_ref[...])
        ckvn, rkv = _rms_fwd(ckv, kvn_ref[...])
        cqb, ckb = _cast(cqn), _cast(ckvn)
        cos_t, sin_t = c_ref[...], s_ref[...]

        dvb = _cast(dv_ref[...])
        dwv_ref[...] += _mm_tn(ckb, dvb)
        dckvn = _mm_nt(dvb, wv_ref[...])
        dcqn = jnp.zeros((tm, ql), F32)
        dkpe = jnp.zeros((tm, QK_ROPE), F32)
        dqhn = jnp.zeros((1, QK_HEAD), F32)
        dkhn = jnp.zeros((1, QK_HEAD), F32)
        for hh in range(n_h):
            qp = _mm(cqb, wq_ref[hh])
            _, rqp = _rms_fwd(qp, qhn_ref[...])
            dqp, dgq = _rms_bwd(qp, rqp, qhn_ref[...], _rope_transposed(dq_ref[hh] * SCORE_SCALE, cos_t, sin_t))
            dqhn += dgq
            dqpb = _cast(dqp)
            dwq_ref[hh] += _mm_tn(cqb, dqpb)
            dcqn += _mm_nt(dqpb, wq_ref[hh])

            kp = jnp.concatenate([_mm(ckb, wkn_ref[hh]), kpe], axis=-1)
            _, rkp = _rms_fwd(kp, khn_ref[...])
            dkp, dgk = _rms_bwd(kp, rkp, khn_ref[...], _rope_transposed(dk_ref[hh], cos_t, sin_t))
            dkhn += dgk
            dknb = _cast(dkp[:, :QK_NOPE])
            dkpe += dkp[:, QK_NOPE:]
            dwkn_ref[hh] += _mm_tn(ckb, dknb)
            dckvn += _mm_nt(dknb, wkn_ref[hh])
        dqhn_ref[...] += dqhn
        dkhn_ref[...] += dkhn

        dcq, dgn = _rms_bwd(cq, rq, qn_ref[...], dcqn)
        dqn_ref[...] += dgn
        dckv, dgn = _rms_bwd(ckv, rkv, kvn_ref[...], dckvn)
        dkvn_ref[...] += dgn
        dlb = _cast(jnp.concatenate([dcq, dckv, dkpe], axis=-1))
        dwin_ref[...] += _mm_tn(hb, dlb)
        dxn, dgn = _rms_bwd(x_t, r, g_ref[...], _mm_nt(dlb, win_ref[...]))
        dg_ref[...] += dgn
        dx_ref[...] = dy_ref[...] + dxn

    tok = lambda w: pl.BlockSpec((tm, w), lambda i: (i, 0))
    heads = pl.BlockSpec((n_h, tm, QK_HEAD), lambda i: (0, i, 0))
    return pl.pallas_call(
        body, name="mla_qkv_bwd", grid=(s // tm,),
        in_specs=[tok(d), tok(lat_w), tok(d), heads, heads, tok(n_h * V_HEAD), _full((1, d)), _full(w_in.shape),
                  _full((1, ql)), _full((1, kvl)), _full(w_q.shape), _full(w_kn.shape), _full(w_v.shape),
                  _full((1, QK_HEAD)), _full((1, QK_HEAD)), tok(QK_HEAD), tok(QK_HEAD)],
        out_specs=[tok(d), _full((1, d)), _full(w_in.shape), _full((1, ql)), _full((1, kvl)), _full(w_q.shape),
                   _full(w_kn.shape), _full(w_v.shape), _full((1, QK_HEAD)), _full((1, QK_HEAD))],
        out_shape=[_sds((s, d)), _sds((1, d)), _sds(w_in.shape), _sds((1, ql)), _sds((1, kvl)), _sds(w_q.shape),
                   _sds(w_kn.shape), _sds(w_v.shape), _sds((1, QK_HEAD)), _sds((1, QK_HEAD))],
        compiler_params=_params(("arbitrary",)),
    )(x, lat, dy, dq, dk, dv, gain, w_in, q_norm, kv_norm, w_q, w_kn, w_v, q_head_norm, k_head_norm, cos, sin)


def _scores_t(k_t, q_t):
    return _mm_nt(k_t, q_t)


def _mask_above_diagonal(z, t):
    key = lax.broadcasted_iota(jnp.int32, (t, t), 0)
    query = lax.broadcasted_iota(jnp.int32, (t, t), 1)
    return jnp.where(key <= query, z, -jnp.inf)


def _flash_fwd(q, k, vt):
    n_h, s, _ = q.shape
    t = _tile(s, ATTN_FWD_TILE)
    n = s // t
    rows_v = vt.shape[2]

    def body(q_ref, k_ref, vt_ref, ot_ref, lse_ref, m_sc, acc_sc, z_sc):
        i = pl.program_id(1)
        m_sc[...] = jnp.full_like(m_sc, -jnp.inf)
        acc_sc[...] = jnp.zeros_like(acc_sc)
        q_t = q_ref[...]

        def fetch(j, slot):
            z_sc[slot] = _scores_t(k_ref[pl.ds(pl.multiple_of(j * t, t), t), :], q_t)

        def stage(j, slot, masked, prefetch=True):
            if prefetch:
                fetch(j + 1, 1 - slot)
            z = _mask_above_diagonal(z_sc[slot], t) if masked else z_sc[slot]
            m_old = m_sc[...]
            m_new = jnp.maximum(m_old, jnp.max(z, axis=0, keepdims=True))
            alpha = jnp.exp(m_old - m_new)
            acc_sc[...] = alpha * acc_sc[...] + _mm(vt_ref[j], _cast(jnp.exp(z - m_new)))
            m_sc[...] = m_new

        def pair_below_diagonal(pair, carry):
            stage(2 * pair, 0, False)
            stage(2 * pair + 1, 1, False)
            return carry

        fetch(0, 0)
        lax.fori_loop(0, i >> 1, pair_below_diagonal, 0)

        @pl.when((i & 1) == 1)
        def _():
            stage(i - 1, 0, False)
            stage(i, 1, True, prefetch=False)

        @pl.when((i & 1) == 0)
        def _():
            stage(i, 0, True, prefetch=False)

        denom = acc_sc[V_HEAD:V_HEAD + 1, :]
        ot_ref[...] = acc_sc[:V_HEAD, :] / denom
        lse_ref[...] = m_sc[...] + jnp.log(denom)

    whole_head = dict(pipeline_mode=pl.Buffered(1))
    return pl.pallas_call(
        body, name="flash_fwd", grid=(n_h, n),
        in_specs=[pl.BlockSpec((None, t, QK_HEAD), lambda h, i: (h, i, 0)),
                  pl.BlockSpec((None, s, QK_HEAD), lambda h, i: (h, 0, 0), **whole_head),
                  pl.BlockSpec((None, n, rows_v, t), lambda h, i: (h, 0, 0, 0), **whole_head)],
        out_specs=[pl.BlockSpec((V_HEAD, t), lambda h, i: (h, i)),
                   pl.BlockSpec((None, 1, t), lambda h, i: (h, 0, i))],
        out_shape=[_sds((n_h * V_HEAD, s)), _sds((n_h, 1, s))],
        scratch_shapes=[pltpu.VMEM((1, t), F32), pltpu.VMEM((rows_v, t), F32), pltpu.VMEM((2, t, t), F32)],
        compiler_params=_params(("arbitrary", "arbitrary")),
    )(q, k, vt)


def _flash_bwd(q, k, kt, v, do, lse, delta):
    n_h, s, _ = q.shape
    t = _tile(s, ATTN_BWD_TILE)
    n = s // t

    def body(q_ref, do_ref, lse_ref, dl_ref, k_ref, kt_ref, v_ref, dqt_ref, dk_ref, dv_ref, dk_sc, dv_sc, z_sc, dp_sc):
        j = pl.program_id(1)

        @pl.when(j == 0)
        def _():
            dqt_ref[...] = jnp.zeros_like(dqt_ref)

        dk_sc[...] = jnp.zeros_like(dk_sc)
        dv_sc[...] = jnp.zeros_like(dv_sc)
        k_t, kt_t, v_t = k_ref[...], kt_ref[...], v_ref[...]

        def rows(i):
            return pl.ds(pl.multiple_of(i * t, t), t)

        def fetch(i, slot):
            i = jnp.minimum(i, n - 1)
            z_sc[slot] = _scores_t(k_t, q_ref[rows(i), :])
            dp_sc[slot] = _mm_nt(v_t, do_ref[rows(i), :])

        def stage(i, slot, masked, prefetch=True):
            if prefetch:
                fetch(i + 1, 1 - slot)
            z = _mask_above_diagonal(z_sc[slot], t) if masked else z_sc[slot]
            pr = jnp.exp(z - lse_ref[i])
            dsb = _cast(pr * (dp_sc[slot] - dl_ref[i]))
            dv_sc[...] += _mm(_cast(pr), do_ref[rows(i), :])
            dk_sc[...] += _mm(dsb, q_ref[rows(i), :])
            dqt_ref[i] += _mm(kt_t, dsb)

        def pair_below_diagonal(pair, carry):
            stage(j + 1 + 2 * pair, 1, False)
            stage(j + 2 + 2 * pair, 0, False)
            return carry

        below = n - 1 - j
        fetch(j, 0)
        stage(j, 0, True)
        lax.fori_loop(0, below >> 1, pair_below_diagonal, 0)

        @pl.when((below & 1) == 1)
        def _():
            stage(n - 1, 1, False, prefetch=False)

        dk_ref[...] = dk_sc[...]
        dv_ref[...] = dv_sc[...]

    whole_head = dict(pipeline_mode=pl.Buffered(1))
    stat = pl.BlockSpec((None, n, 1, t), lambda h, j: (h, 0, 0, 0))
    return pl.pallas_call(
        body, name="flash_bwd", grid=(n_h, n),
        in_specs=[pl.BlockSpec((None, s, QK_HEAD), lambda h, j: (h, 0, 0), **whole_head),
                  pl.BlockSpec((s, V_HEAD), lambda h, j: (0, h), **whole_head),
                  stat, stat,
                  pl.BlockSpec((None, t, QK_HEAD), lambda h, j: (h, j, 0)),
                  pl.BlockSpec((None, QK_HEAD, t), lambda h, j: (h, 0, j)),
                  pl.BlockSpec((t, V_HEAD), lambda h, j: (j, h))],
        out_specs=[pl.BlockSpec((None, n, QK_HEAD, t), lambda h, j: (h, 0, 0, 0)),
                   pl.BlockSpec((None, t, QK_HEAD), lambda h, j: (h, j, 0)),
                   pl.BlockSpec((t, V_HEAD), lambda h, j: (j, h))],
        out_shape=[_sds((n_h, n, QK_HEAD, t)), _sds((n_h, s, QK_HEAD)), _sds((s, n_h * V_HEAD))],
        scratch_shapes=[pltpu.VMEM((t, QK_HEAD), F32), pltpu.VMEM((t, V_HEAD), F32),
                        pltpu.VMEM((2, t, t), F32), pltpu.VMEM((2, t, t), F32)],
        compiler_params=_params(("arbitrary", "arbitrary")),
    )(q, do, lse, delta, k, kt, v)


def _mla_out_fwd(x, ot, w_out):
    s, d = x.shape
    hv = ot.shape[0]
    tm = _tile(s, FFN_TILE)

    def body(x_ref, ot_ref, w_ref, y_ref):
        y_ref[...] = x_ref[...] + _mm_tn(_cast(ot_ref[...]), w_ref[...])

    tok = pl.BlockSpec((tm, d), lambda i: (i, 0))
    return pl.pallas_call(
        body, name="mla_out_fwd", grid=(s // tm,),
        in_specs=[tok, pl.BlockSpec((hv, tm), lambda i: (0, i)), _full(w_out.shape)],
        out_specs=tok, out_shape=_sds((s, d)),
        compiler_params=_params(("arbitrary",)),
    )(x, ot, w_out)


def _mla_out_bwd(dy, ot, w_out):
    s, d = dy.shape
    hv = ot.shape[0]
    n_h = hv // V_HEAD
    tm = _tile(s, FFN_TILE)

    def body(dy_ref, ot_ref, w_ref, do_ref, dl_ref, dw_ref):
        @pl.when(pl.program_id(0) == 0)
        def _():
            dw_ref[...] = jnp.zeros_like(dw_ref)

        dyb = _cast(dy_ref[...])
        o_t = ot_ref[...]
        do_ref[...] = _cast(_mm_nt(dyb, w_ref[...]))
        prod = _mm_nt(w_ref[...], dyb) * o_t
        for hh in range(n_h):
            dl_ref[hh] = jnp.sum(prod[hh * V_HEAD:(hh + 1) * V_HEAD], axis=0, keepdims=True)
        dw_ref[...] += _mm(_cast(o_t), dyb)

    return pl.pallas_call(
        body, name="mla_out_bwd", grid=(s // tm,),
        in_specs=[pl.BlockSpec((tm, d), lambda i: (i, 0)), pl.BlockSpec((hv, tm), lambda i: (0, i)),
                  _full(w_out.shape)],
        out_specs=[pl.BlockSpec((tm, hv), lambda i: (i, 0)), pl.BlockSpec((n_h, 1, tm), lambda i: (0, 0, i)),
                   _full(w_out.shape)],
        out_shape=[_sds((s, hv), MXU_DTYPE), _sds((n_h, 1, s)), _sds(w_out.shape)],
        compiler_params=_params(("arbitrary",)),
    )(dy, ot, w_out)


def _loss_and_grad(y, target):
    s, d = y.shape
    tm = _tile(s, ROW_TILE)

    def body(y_ref, t_ref, loss_ref, dy_ref):
        @pl.when(pl.program_id(0) == 0)
        def _():
            loss_ref[...] = jnp.zeros_like(loss_ref)

        err = y_ref[...] - t_ref[...]
        dy_ref[...] = err * (1.0 / d)
        loss_ref[...] += 0.5 * jnp.sum(jnp.mean(err * err, axis=-1, keepdims=True), axis=0, keepdims=True)

    tok = pl.BlockSpec((tm, d), lambda i: (i, 0))
    return pl.pallas_call(
        body, name="loss_and_grad", grid=(s // tm,),
        in_specs=[tok, tok], out_specs=[_full((1, 1)), tok],
        out_shape=[_sds((1, 1)), _sds((s, d))],
        compiler_params=_params(("arbitrary",)),
    )(y, target)


def _mesh_position():
    return lax.axis_index("x"), lax.axis_index("y"), lax.axis_index("c")


def _other_chips(x, y):
    return [(1 - x, y), (x, 1 - y), (1 - x, 1 - y)]


ANY = pl.BlockSpec(memory_space=pl.ANY)


def _gather_over_chips(arrs):
    n = len(arrs)
    halves = [a.shape[0] // 2 for a in arrs]
    assert all(a.shape[0] % 2 == 0 for a in arrs)
    own = 2 * (N_CHIPS - 1)

    def body(*refs):
        srcs, outs = refs[:n], refs[n:2 * n]
        send_sems, recv_sems = refs[2 * n:]
        x, y, c = _mesh_position()
        me, sibling = (x, y, c), (x, y, 1 - c)
        chips = _other_chips(x, y)
        my_chip = 2 * x + y

        def rows(t, chip, half):
            return outs[t].at[chip, pl.ds(half * halves[t], halves[t])]

        def copy(t, k, src, dst, to):
            return pltpu.make_async_remote_copy(src_ref=src, dst_ref=dst, send_sem=send_sems.at[t, k],
                                                recv_sem=recv_sems.at[t, k], device_id=to, device_id_type=MESH)

        started = []
        for t in range(n):
            for k, (px, py) in enumerate(chips):
                cp = copy(t, k, srcs[t].at[pl.ds(c * halves[t], halves[t])], rows(t, my_chip, c), (px, py, c))
                cp.start()
                started.append(cp)
            cp = copy(t, own, srcs[t], outs[t].at[my_chip], sibling)
            cp.start()
            started.append(cp)
        for t in range(n):
            for k, (px, py) in enumerate(chips):
                landed = rows(t, 2 * px + py, c)
                copy(t, k, landed, landed, me).wait_recv()
                cp = copy(t, N_CHIPS - 1 + k, landed, landed, sibling)
                cp.start()
                started.append(cp)
        for t in range(n):
            for k, (px, py) in enumerate(chips):
                passed = rows(t, 2 * px + py, 1 - c)
                copy(t, N_CHIPS - 1 + k, passed, passed, me).wait_recv()
            copy(t, own, srcs[t], outs[t].at[my_chip], me).wait_recv()
        for cp in started:
            cp.wait_send()

    return pl.pallas_call(
        body, name="gather_over_chips",
        in_specs=[ANY] * n, out_specs=[ANY] * n,
        out_shape=[_sds((N_CHIPS,) + a.shape, a.dtype) for a in arrs],
        scratch_shapes=[pltpu.SemaphoreType.DMA((n, own + 1)), pltpu.SemaphoreType.DMA((n, own + 1))],
    )(*arrs)


def _send_other_half_to_sibling(grads):
    n = len(grads)
    halves = [g.shape[1] // 2 for g in grads]

    def body(*refs):
        srcs, outs = refs[:n], refs[n:2 * n]
        send_sems, recv_sems = refs[2 * n:]
        x, y, c = _mesh_position()
        copies = []
        for t in range(n):
            cp = pltpu.make_async_remote_copy(
                src_ref=srcs[t].at[pl.ds(0, N_CHIPS), pl.ds((1 - c) * halves[t], halves[t])], dst_ref=outs[t],
                send_sem=send_sems.at[t], recv_sem=recv_sems.at[t], device_id=(x, y, 1 - c), device_id_type=MESH)
            cp.start()
            copies.append(cp)
        for cp in copies:
            cp.wait_recv()
        for cp in copies:
            cp.wait_send()

    return pl.pallas_call(
        body, name="send_other_half_to_sibling",
        in_specs=[ANY] * n, out_specs=[ANY] * n,
        out_shape=[_sds((N_CHIPS, h) + g.shape[2:]) for g, h in zip(grads, halves)],
        scratch_shapes=[pltpu.SemaphoreType.DMA((n,)), pltpu.SemaphoreType.DMA((n,))],
    )(*grads)


def _send_blocks_to_chips(parts):
    n = len(parts)

    def body(*refs):
        srcs, outs = refs[:n], refs[n:2 * n]
        send_sems, recv_sems = refs[2 * n:]
        x, y, c = _mesh_position()
        copies = []
        for t in range(n):
            for k, (px, py) in enumerate(_other_chips(x, y)):
                cp = pltpu.make_async_remote_copy(
                    src_ref=srcs[t].at[2 * px + py], dst_ref=outs[t].at[k], send_sem=send_sems.at[t, k],
                    recv_sem=recv_sems.at[t, k], device_id=(px, py, c), device_id_type=MESH)
                cp.start()
                copies.append(cp)
        for cp in copies:
            cp.wait_recv()
        for cp in copies:
            cp.wait_send()

    return pl.pallas_call(
        body, name="send_blocks_to_chips",
        in_specs=[ANY] * n, out_specs=[ANY] * n,
        out_shape=[_sds((N_CHIPS - 1,) + p.shape[1:], p.dtype) for p in parts],
        scratch_shapes=[pltpu.SemaphoreType.DMA((n, N_CHIPS - 1)), pltpu.SemaphoreType.DMA((n, N_CHIPS - 1))],
    )(*parts)


def _join_halves_with_sibling(sums):
    n = len(sums)

    def body(*refs):
        srcs, outs = refs[:n], refs[n:2 * n]
        send_sems, recv_sems = refs[2 * n:]
        x, y, c = _mesh_position()
        copies = []
        for t in range(n):
            h = srcs[t].shape[0] // 2
            mine = pl.ds(c * h, h)
            cp = pltpu.make_async_remote_copy(
                src_ref=srcs[t].at[mine], dst_ref=outs[t].at[mine], send_sem=send_sems.at[t],
                recv_sem=recv_sems.at[t], device_id=(x, y, 1 - c), device_id_type=MESH)
            cp.start()
            copies.append(cp)
        for t in range(n):
            h = srcs[t].shape[0] // 2
            theirs = pl.ds((1 - c) * h, h)
            pltpu.make_async_remote_copy(
                src_ref=srcs[t].at[theirs], dst_ref=outs[t].at[theirs], send_sem=send_sems.at[t],
                recv_sem=recv_sems.at[t], device_id=(x, y, 1 - c), device_id_type=MESH).wait_recv()
        for cp in copies:
            cp.wait_send()

    return pl.pallas_call(
        body, name="join_halves_with_sibling",
        in_specs=[ANY] * n, out_specs=[ANY] * n,
        out_shape=[_sds(a.shape) for a in sums],
        input_output_aliases={t: t for t in range(n)},
        scratch_shapes=[pltpu.SemaphoreType.DMA((n,)), pltpu.SemaphoreType.DMA((n,))],
    )(*sums)


def _gather_over_devices(rows):
    r = rows.shape[0]

    def body(in_ref, out_ref, send_sems, recv_sems, local_sem):
        x, y, c = _mesh_position()
        mine = pltpu.make_async_copy(in_ref, out_ref.at[4 * x + 2 * y + c], local_sem)
        mine.start()
        copies = []
        for mask in range(1, N_DEVICES):
            fx, fy, fc = (mask >> 2) & 1, (mask >> 1) & 1, mask & 1
            px, py, pc = (1 - x if fx else x), (1 - y if fy else y), (1 - c if fc else c)
            send = pltpu.make_async_remote_copy(
                src_ref=in_ref, dst_ref=out_ref.at[4 * x + 2 * y + c], send_sem=send_sems.at[mask - 1],
                recv_sem=recv_sems.at[mask - 1], device_id=(px, py, pc), device_id_type=MESH)
            send.start()
            recv = pltpu.make_async_remote_copy(
                src_ref=in_ref, dst_ref=out_ref.at[4 * px + 2 * py + pc], send_sem=send_sems.at[mask - 1],
                recv_sem=recv_sems.at[mask - 1], device_id=(px, py, pc), device_id_type=MESH)
            copies.append((send, recv))
        for _, recv in copies:
            recv.wait_recv()
        for send, _ in copies:
            send.wait_send()
        mine.wait()

    vm = pl.BlockSpec(memory_space=pltpu.VMEM)
    return pl.pallas_call(
        body, name="gather_over_devices", in_specs=[vm], out_specs=vm,
        out_shape=_sds((N_DEVICES, r, LANES)),
        scratch_shapes=[pltpu.SemaphoreType.DMA((N_DEVICES - 1,)), pltpu.SemaphoreType.DMA((N_DEVICES - 1,)),
                        pltpu.SemaphoreType.DMA],
    )(rows)


def _add_sibling_half(grad, received, chip, core):
    _, l, r, c = grad.shape
    half = l // 2

    def body(chip_ref, core_ref, g_ref, r_ref, wire_ref, own_ref):
        total = g_ref[...] + r_ref[...]
        wire_ref[...] = total.astype(WIRE_DTYPE)

        @pl.when(pl.program_id(1) == chip_ref[0])
        def _():
            own_ref[...] = total

    blk = lambda f: pl.BlockSpec((None, None, r, c), f)
    grid_spec = pltpu.PrefetchScalarGridSpec(
        num_scalar_prefetch=2, grid=(half, N_CHIPS),
        in_specs=[blk(lambda i, j, chip, core: (j, core[0] * half + i, 0, 0)),
                  blk(lambda i, j, chip, core: (j, i, 0, 0))],
        out_specs=[blk(lambda i, j, chip, core: (j, i, 0, 0)),
                   pl.BlockSpec((None, r, c), lambda i, j, chip, core: (i, 0, 0))])
    return pl.pallas_call(
        body, name="add_sibling_half", grid_spec=grid_spec,
        out_shape=[_sds((N_CHIPS, half, r, c), WIRE_DTYPE), _sds((half, r, c))],
        compiler_params=_params(("arbitrary", "arbitrary")),
    )(chip, core, grad, received)


def _add_chip_blocks(own, received, core):
    half, r, c = own.shape

    def body(core_ref, p_ref, r0_ref, r1_ref, r2_ref, o_ref):
        o_ref[...] = ((p_ref[...] + r0_ref[...].astype(F32)) + r1_ref[...].astype(F32)) + r2_ref[...].astype(F32)

    grid_spec = pltpu.PrefetchScalarGridSpec(
        num_scalar_prefetch=1, grid=(half,),
        in_specs=[pl.BlockSpec((None, r, c), lambda i, core: (i, 0, 0))] + [
            pl.BlockSpec((None, None, r, c), functools.partial(lambda i, core, k: (k, i, 0, 0), k=k))
            for k in range(N_CHIPS - 1)],
        out_specs=pl.BlockSpec((None, r, c), lambda i, core: (core[0] * half + i, 0, 0)))
    return pl.pallas_call(
        body, name="add_chip_blocks", grid_spec=grid_spec, out_shape=_sds((2 * half, r, c)),
        compiler_params=_params(("arbitrary",)),
    )(core, own, received, received, received)


def _sum_over_devices(parts):
    _, r, _ = parts.shape

    def body(p_ref, o_ref):
        acc = p_ref[0]
        for k in range(1, N_DEVICES):
            acc = acc + p_ref[k]
        o_ref[...] = acc

    return pl.pallas_call(body, name="sum_over_devices", out_shape=_sds((r, LANES)))(parts)


def _adamw_math(w, g, m, v):
    m = ADAM_B1 * m + (1.0 - ADAM_B1) * g
    v = ADAM_B2 * v + (1.0 - ADAM_B2) * (g * g)
    m_hat = m / (1.0 - ADAM_B1 ** ADAM_STEP)
    v_hat = v / (1.0 - ADAM_B2 ** ADAM_STEP)
    delta = -ADAM_LR * (m_hat / (jnp.sqrt(v_hat) + ADAM_EPS) + ADAM_WD * w)
    return delta, m, v


def _adamw_stacked(w, m, v, grads, offset):
    l, r, c = w.shape
    tr = r
    while tr * c * 4 > 2**20 and tr % 16 == 0:
        tr //= 2

    def body(w_ref, m_ref, v_ref, g_ref, go_ref, d_ref, mo_ref, vo_ref):
        g = g_ref[...]
        go_ref[...] = g
        d_ref[...], mo_ref[...], vo_ref[...] = _adamw_math(w_ref[...], g, m_ref[...], v_ref[...])

    blk = pl.BlockSpec((None, tr, c), lambda i, j: (i, j, 0))
    return pl.pallas_call(
        body, name="adamw_stacked", grid=(l, r // tr),
        in_specs=[blk, blk, blk, pl.BlockSpec((None, tr, c), lambda i, j: (offset + i, j, 0))],
        out_specs=[blk] * 4, out_shape=[_sds((l, r, c))] * 4,
        compiler_params=_params(("arbitrary", "arbitrary")),
    )(w, m, v, grads)


def _adamw_small(w, m, v, g):
    def body(w_ref, m_ref, v_ref, g_ref, d_ref, mo_ref, vo_ref):
        d_ref[...], mo_ref[...], vo_ref[...] = _adamw_math(w_ref[...], g_ref[...], m_ref[...], v_ref[...])

    return pl.pallas_call(body, name="adamw_small", out_shape=[_sds(w.shape)] * 3)(w, m, v, g)


def _pack_rows(arrs):
    flat = jnp.concatenate([a.reshape(-1) for a in arrs])
    pad = (-flat.shape[0]) % (8 * LANES)
    return jnp.pad(flat, (0, pad)).reshape(-1, LANES)


def _unpack_rows(rows, shapes, lead=()):
    flat = rows.reshape(lead + (-1,))
    out, at = [], 0
    for shp in shapes:
        size = int(np.prod(shp))
        out.append(flat[..., at:at + size].reshape(lead + tuple(shp)))
        at += size
    return out


WEIGHT_NAMES = ('ffn1_norm', 'ffn1_w_gate', 'ffn1_w_up', 'ffn1_w_down', 'mix_norm', 'pool_w', 'pool_scale',
                'mla_w_in', 'mla_q_norm', 'mla_w_q_up', 'mla_kv_norm', 'mla_w_kv_up', 'mla_q_head_norm',
                'mla_k_head_norm', 'mla_w_out', 'ffn2_norm', 'ffn2_w_gate', 'ffn2_w_up', 'ffn2_w_down')


def _chips_to_columns(g):
    return jnp.transpose(g, (1, 2, 0, 3)).reshape(g.shape[1], g.shape[2], -1)


def _columns_to_chips(full):
    n, r, c4 = full.shape
    return jnp.transpose(full.reshape(n, r, N_CHIPS, c4 // N_CHIPS), (2, 0, 1, 3))


def kernel(x, positions, ffn1_norm, ffn1_w_gate, ffn1_w_up, ffn1_w_down, mix_norm, pool_w, pool_scale, mla_w_in, mla_q_norm, mla_w_q_up, mla_kv_norm, mla_w_kv_up, mla_q_head_norm, mla_k_head_norm, mla_w_out, ffn2_norm, ffn2_w_gate, ffn2_w_up, ffn2_w_down, loss_target, m_ffn1_norm, m_ffn1_w_gate, m_ffn1_w_up, m_ffn1_w_down, m_mix_norm, m_pool_w, m_pool_scale, m_mla_w_in, m_mla_q_norm, m_mla_w_q_up, m_mla_kv_norm, m_mla_w_kv_up, m_mla_q_head_norm, m_mla_k_head_norm, m_mla_w_out, m_ffn2_norm, m_ffn2_w_gate, m_ffn2_w_up, m_ffn2_w_down, v_ffn1_norm, v_ffn1_w_gate, v_ffn1_w_up, v_ffn1_w_down, v_mix_norm, v_pool_w, v_pool_scale, v_mla_w_in, v_mla_q_norm, v_mla_w_q_up, v_mla_kv_norm, v_mla_w_kv_up, v_mla_q_head_norm, v_mla_k_head_norm, v_mla_w_out, v_ffn2_norm, v_ffn2_w_gate, v_ffn2_w_up, v_ffn2_w_down):
    env = dict(locals())
    w = {n: env[n] for n in WEIGHT_NAMES}
    mom = {n: env["m_" + n] for n in WEIGHT_NAMES}
    var = {n: env["v_" + n] for n in WEIGHT_NAMES}

    s, d = x.shape[1], x.shape[2]
    depth = ffn1_norm.shape[0]
    n_mla, n_pool, n_groups = mla_w_in.shape[0], pool_w.shape[0], pool_w.shape[1]
    pool_c = pool_w.shape[3]
    q_lora = N_CHIPS * mla_q_norm.shape[1]
    kv_lora = N_CHIPS * mla_kv_norm.shape[1]
    n_heads = N_CHIPS * mla_w_q_up.shape[2] // QK_HEAD
    t_fwd, t_bwd = _tile(s, ATTN_FWD_TILE), _tile(s, ATTN_BWD_TILE)
    cx, cy, cc = _mesh_position()
    chip = 2 * cx + cy
    chip_arr = jnp.reshape(chip, (1,)).astype(jnp.int32)
    core_arr = jnp.reshape(cc, (1,)).astype(jnp.int32)

    shard_gu = _cast(jnp.swapaxes(jnp.concatenate([ffn1_w_gate, ffn1_w_up, ffn2_w_gate, ffn2_w_up], axis=0), 1, 2))
    shard_dn = _cast(jnp.concatenate([ffn1_w_down, ffn2_w_down], axis=0))
    shard_pool = _cast(pool_w.reshape((n_pool * n_groups,) + pool_w.shape[2:]))
    w_gu, w_dn, g_in, g_qup, g_kvup, g_out, g_pool = _gather_over_chips(
        [shard_gu, shard_dn, _cast(mla_w_in), _cast(mla_w_q_up), _cast(mla_w_kv_up), _cast(mla_w_out), shard_pool])
    small_shapes = [mla_q_norm.shape, mla_kv_norm.shape]
    small = _gather_over_devices(_pack_rows([mla_q_norm, mla_kv_norm]))[::2]
    qn_chips, kvn_chips = _unpack_rows(small, small_shapes, lead=(N_CHIPS,))
    q_norm_full = jnp.transpose(qn_chips, (1, 0, 2)).reshape(n_mla, 1, q_lora)
    kv_norm_full = jnp.transpose(kvn_chips, (1, 0, 2)).reshape(n_mla, 1, kv_lora)

    w_in_full = _chips_to_columns(g_in)
    w_q_heads = jnp.transpose(_chips_to_columns(g_qup).reshape(n_mla, q_lora, n_heads, QK_HEAD), (0, 2, 1, 3))
    w_kv = _chips_to_columns(g_kvup).reshape(n_mla, kv_lora, n_heads, QK_NOPE + V_HEAD)
    w_kn_heads = jnp.transpose(w_kv[..., :QK_NOPE], (0, 2, 1, 3))
    w_v_full = w_kv[..., QK_NOPE:].reshape(n_mla, kv_lora, n_heads * V_HEAD)
    w_out_full = jnp.transpose(g_out, (1, 0, 2, 3)).reshape(n_mla, n_heads * V_HEAD, d)
    pool_full = jnp.transpose(g_pool.reshape(N_CHIPS, n_pool, n_groups, pool_c // N_CHIPS, pool_c),
                              (1, 2, 0, 3, 4)).reshape(n_pool, n_groups, pool_c, pool_c)

    inv_freq = (1.0 / (ROPE_THETA ** (jnp.arange(0, QK_ROPE, 2, dtype=F32) / QK_ROPE))).reshape(1, -1)
    cos_t, sin_t = _rope_tables(positions.reshape(s, 1), inv_freq)

    row = lambda a, i: a[i].reshape(1, -1)
    i_gate1, i_up1, i_gate2, i_up2 = (lambda i: i), (lambda i: depth + i), (lambda i: 2 * depth + i), (lambda i: 3 * depth + i)
    i_dn1, i_dn2 = (lambda i: i), (lambda i: depth + i)

    h = x.reshape(s, d)
    saved = []
    for i in range(depth):
        rec = {"x_ffn1": h}
        h, *rec["ffn1"] = _ffn_fwd(h, row(ffn1_norm, i), w_gu, w_dn, i_gate1(i), i_up1(i), i_dn1(i))
        rec["x_mix"] = h
        j = i // 2
        if i % 2 == 0:
            h = _pool_fwd(h, row(mix_norm, i), pool_full[j], row(pool_scale, j))
        else:
            lat, q, k, v = _mla_qkv_fwd(h, row(mix_norm, i), w_in_full[j], q_norm_full[j], kv_norm_full[j],
                                        w_q_heads[j], w_kn_heads[j], w_v_full[j], row(mla_q_head_norm, j),
                                        row(mla_k_head_norm, j), cos_t, sin_t)
            vt = jnp.transpose(v.reshape(s // t_fwd, t_fwd, n_heads, V_HEAD), (2, 0, 3, 1))
            vt = jnp.concatenate([vt, jnp.ones((n_heads, s // t_fwd, ONES_ROWS, t_fwd), vt.dtype)], axis=2)
            ot, lse = _flash_fwd(q, k, vt)
            rec.update(lat=lat, q=q, k=k, v=v, ot=ot, lse=lse)
            h = _mla_out_fwd(h, ot, w_out_full[j])
        rec["x_ffn2"] = h
        h, *rec["ffn2"] = _ffn_fwd(h, row(ffn2_norm, i), w_gu, w_dn, i_gate2(i), i_up2(i), i_dn2(i))
        saved.append(rec)

    loss_part, dy = _loss_and_grad(h, loss_target.reshape(s, d))
    loss = lax.psum(loss_part[0, 0], ("x", "y", "c"))

    g_gu = [None] * (4 * depth)
    g_dn = [None] * (2 * depth)
    g_norm = {n: [None] * depth for n in ("ffn1_norm", "mix_norm", "ffn2_norm")}
    g_pool_w, g_pool_scale = [None] * n_pool, [None] * n_pool
    g_mla = {n: [None] * n_mla for n in ("w_in", "q_norm", "kv_norm", "w_q", "w_kv", "qhn", "khn", "w_out")}
    for i in reversed(range(depth)):
        rec = saved[i]
        hb, gate, up = rec["ffn2"]
        dy, g_norm["ffn2_norm"][i], dyb, dgt, dup, act = _ffn_bwd_dgrad(
            rec["x_ffn2"], row(ffn2_norm, i), dy, gate, up, w_gu, w_dn, i_gate2(i), i_up2(i), i_dn2(i))
        g_gu[i_gate2(i)], g_gu[i_up2(i)], g_dn[i_dn2(i)] = _ffn_wgrad(hb, dyb, dgt, dup, act)
        j = i // 2
        if i % 2 == 0:
            dy, g_norm["mix_norm"][i], g_pool_w[j], g_pool_scale[j] = _pool_bwd(
                rec["x_mix"], row(mix_norm, i), pool_full[j], row(pool_scale, j), dy)
        else:
            do, delta, g_mla["w_out"][j] = _mla_out_bwd(dy, rec["ot"], w_out_full[j])
            by_tile = lambda a: a.reshape(n_heads, s // t_bwd, 1, t_bwd)
            dqt, dk, dv = _flash_bwd(rec["q"], rec["k"], jnp.transpose(rec["k"], (0, 2, 1)), rec["v"], do,
                                     by_tile(rec["lse"]), by_tile(delta))
            dq = jnp.transpose(dqt, (0, 1, 3, 2)).reshape(n_heads, s, QK_HEAD)
            (dy, g_norm["mix_norm"][i], g_mla["w_in"][j], g_mla["q_norm"][j], g_mla["kv_norm"][j], dwq, dwkn, dwv,
             g_mla["qhn"][j], g_mla["khn"][j]) = _mla_qkv_bwd(
                rec["x_mix"], rec["lat"], dy, dq, dk, dv, row(mix_norm, i), w_in_full[j], q_norm_full[j],
                kv_norm_full[j], w_q_heads[j], w_kn_heads[j], w_v_full[j], row(mla_q_head_norm, j),
                row(mla_k_head_norm, j), cos_t, sin_t)
            g_mla["w_q"][j] = jnp.transpose(dwq, (1, 0, 2)).reshape(q_lora, n_heads * QK_HEAD)
            g_mla["w_kv"][j] = jnp.concatenate(
                [jnp.transpose(dwkn, (1, 0, 2)), dwv.reshape(kv_lora, n_heads, V_HEAD)], axis=-1
            ).reshape(kv_lora, n_heads * (QK_NOPE + V_HEAD))
        hb, gate, up = rec["ffn1"]
        dy, g_norm["ffn1_norm"][i], dyb, dgt, dup, act = _ffn_bwd_dgrad(
            rec["x_ffn1"], row(ffn1_norm, i), dy, gate, up, w_gu, w_dn, i_gate1(i), i_up1(i), i_dn1(i))
        g_gu[i_gate1(i)], g_gu[i_up1(i)], g_dn[i_dn1(i)] = _ffn_wgrad(hb, dyb, dgt, dup, act)
    grad_x = dy.reshape(x.shape)

    full_grads = [
        jnp.stack(g_gu, axis=1),
        jnp.stack(g_dn, axis=1),
        _columns_to_chips(jnp.stack(g_mla["w_in"])),
        _columns_to_chips(jnp.stack(g_mla["w_q"])),
        _columns_to_chips(jnp.stack(g_mla["w_kv"])),
        jnp.transpose(jnp.stack(g_mla["w_out"]).reshape(n_mla, N_CHIPS, -1, d), (1, 0, 2, 3)),
        jnp.transpose(jnp.stack(g_pool_w).reshape(n_pool, n_groups, N_CHIPS, pool_c // N_CHIPS, pool_c),
                      (2, 0, 1, 3, 4)).reshape(N_CHIPS, n_pool * n_groups, pool_c // N_CHIPS, pool_c),
    ]
    from_sibling = _send_other_half_to_sibling(full_grads)
    chip_sums = [_add_sibling_half(g, r, chip_arr, core_arr) for g, r in zip(full_grads, from_sibling)]
    from_chips = _send_blocks_to_chips([wire for wire, _ in chip_sums])
    half_sums = [_add_chip_blocks(own, r, core_arr) for (_, own), r in zip(chip_sums, from_chips)]
    r_gu, r_dn, r_in, r_qup, r_kvup, r_out, r_pool = _join_halves_with_sibling(half_sums)

    small_grads = [jnp.concatenate(g_norm["ffn1_norm"]), jnp.concatenate(g_norm["mix_norm"]),
                   jnp.concatenate(g_norm["ffn2_norm"]), jnp.concatenate(g_pool_scale),
                   jnp.concatenate(g_mla["qhn"]), jnp.concatenate(g_mla["khn"]),
                   jnp.concatenate(g_mla["q_norm"]), jnp.concatenate(g_mla["kv_norm"])]
    small_sum = _sum_over_devices(_gather_over_devices(_pack_rows(small_grads)))
    (s_ffn1, s_mix, s_ffn2, s_pscale, s_qhn, s_khn, s_qn, s_kvn) = _unpack_rows(small_sum, [g.shape for g in small_grads])
    qn_w, kvn_w = mla_q_norm.shape[1], mla_kv_norm.shape[1]
    s_qn = lax.dynamic_slice_in_dim(s_qn, chip * qn_w, qn_w, axis=1)
    s_kvn = lax.dynamic_slice_in_dim(s_kvn, chip * kvn_w, kvn_w, axis=1)

    grads, deltas, new_m, new_v = {}, {}, {}, {}

    def stacked(name, reduced, offset, transposed=False):
        shape = w[name].shape
        if transposed:
            as3, back = (lambda a: jnp.swapaxes(a, 1, 2)), (lambda a: jnp.swapaxes(a, 1, 2))
        else:
            as3, back = (lambda a: a.reshape((-1,) + shape[-2:])), (lambda a: a.reshape(shape))
        out = _adamw_stacked(as3(w[name]), as3(mom[name]), as3(var[name]), reduced, offset)
        grads[name], deltas[name], new_m[name], new_v[name] = [back(o) for o in out]

    def small_update(name, g):
        grads[name] = g
        deltas[name], new_m[name], new_v[name] = _adamw_small(w[name], mom[name], var[name], g)

    stacked("ffn1_w_gate", r_gu, 0, transposed=True)
    stacked("ffn1_w_up", r_gu, depth, transposed=True)
    stacked("ffn2_w_gate", r_gu, 2 * depth, transposed=True)
    stacked("ffn2_w_up", r_gu, 3 * depth, transposed=True)
    stacked("ffn1_w_down", r_dn, 0)
    stacked("ffn2_w_down", r_dn, depth)
    stacked("mla_w_in", r_in, 0)
    stacked("mla_w_q_up", r_qup, 0)
    stacked("mla_w_kv_up", r_kvup, 0)
    stacked("mla_w_out", r_out, 0)
    stacked("pool_w", r_pool, 0)
    small_update("ffn1_norm", s_ffn1)
    small_update("mix_norm", s_mix)
    small_update("ffn2_norm", s_ffn2)
    small_update("pool_scale", s_pscale)
    small_update("mla_q_head_norm", s_qhn)
    small_update("mla_k_head_norm", s_khn)
    small_update("mla_q_norm", s_qn)
    small_update("mla_kv_norm", s_kvn)

    return (loss, grad_x, *[grads[n] for n in WEIGHT_NAMES], *[deltas[n] for n in WEIGHT_NAMES],
            *[new_m[n] for n in WEIGHT_NAMES], *[new_v[n] for n in WEIGHT_NAMES])
```

```python
import functools

import numpy as np

import jax
import jax.numpy as jnp
from jax import lax
from jax.experimental import pallas as pl
from jax.experimental.pallas import tpu as pltpu

F32 = jnp.float32
MXU_DTYPE = jnp.bfloat16
WIRE_DTYPE = jnp.bfloat16
MESH = pl.DeviceIdType.MESH
N_CHIPS = 4
N_DEVICES = 8
LANES = 128
VMEM_LIMIT_BYTES = 56 * 2**20
NORM_EPS = 1e-6
QK_NOPE, QK_ROPE, V_HEAD = 128, 64, 128
QK_HEAD = QK_NOPE + QK_ROPE
SCORE_SCALE = QK_HEAD ** -0.5
ONES_ROWS = 8
ROPE_THETA = 10000.0
POOL_WINDOWS = (2, 4, 8, 16)
POOL_HALO = 16
FFN_HALF = 0.5
ADAM_LR, ADAM_B1, ADAM_B2, ADAM_EPS, ADAM_WD, ADAM_STEP = 0.001, 0.9, 0.999, 1e-08, 0.01, 10
FFN_TILE = 512
FFN_BWD_TILE = 256
WGRAD_TILE = 2048
MLA_TILE = 256
POOL_TILE = 512
ATTN_FWD_TILE = 1024
ATTN_BWD_TILE = 1024
ROW_TILE = 1024


def _cast(v):
    return v.astype(MXU_DTYPE)


def _mm(a, b):
    return jnp.dot(a, b, preferred_element_type=F32)


def _mm_nt(a, b):
    return lax.dot_general(a, b, (((1,), (1,)), ((), ())), preferred_element_type=F32)


def _mm_tn(a, b):
    return lax.dot_general(a, b, (((0,), (0,)), ((), ())), preferred_element_type=F32)


def _rms_fwd(v, gain):
    r = lax.rsqrt(jnp.mean(v * v, axis=-1, keepdims=True) + NORM_EPS)
    return v * r * gain, r


def _rms_bwd(v, r, gain, dy):
    vr = v * r
    gy = dy * gain
    dv = r * (gy - vr * jnp.mean(gy * vr, axis=-1, keepdims=True))
    return dv, jnp.sum(dy * vr, axis=0, keepdims=True)


def _params(semantics=None):
    return pltpu.CompilerParams(dimension_semantics=semantics, vmem_limit_bytes=VMEM_LIMIT_BYTES)


def _tile(n, want):
    t = min(n, want)
    assert n % t == 0, (n, want)
    return t


def _full(shape):
    nd = len(shape)
    return pl.BlockSpec(shape, lambda *_: (0,) * nd)


def _sds(shape, dtype=F32):
    return jax.ShapeDtypeStruct(shape, dtype)


def _ffn_fwd(x, gain, w_gu, w_dn, i_gate, i_up, i_down):
    s, d = x.shape
    fs = w_gu.shape[-2]
    tm = _tile(s, FFN_TILE)

    def body(x_ref, g_ref, wg_ref, wu_ref, wd_ref, y_ref, hb_ref, gate_ref, up_ref):
        h, _ = _rms_fwd(x_ref[...], g_ref[...])
        hb = _cast(h)
        hb_ref[...] = hb
        pre = [(_mm_nt(hb, wg_ref[c]), _mm_nt(hb, wu_ref[c])) for c in range(N_CHIPS)]
        out = None
        for c, (g, u) in enumerate(pre):
            gate_ref[c] = _cast(g)
            up_ref[c] = _cast(u)
            part = _mm(_cast((g * jax.nn.sigmoid(g)) * u), wd_ref[c])
            out = part if out is None else out + part
        y_ref[...] = x_ref[...] + FFN_HALF * out

    resident = dict(pipeline_mode=pl.Buffered(1))
    tok = pl.BlockSpec((tm, d), lambda i: (i, 0))
    chunks = pl.BlockSpec((N_CHIPS, tm, fs), lambda i: (0, i, 0))
    return pl.pallas_call(
        body, name="ffn_fwd", grid=(s // tm,),
        in_specs=[
            tok, _full((1, d)),
            pl.BlockSpec((N_CHIPS, None, fs, d), lambda i: (0, i_gate, 0, 0), **resident),
            pl.BlockSpec((N_CHIPS, None, fs, d), lambda i: (0, i_up, 0, 0), **resident),
            pl.BlockSpec((N_CHIPS, None, fs, d), lambda i: (0, i_down, 0, 0), **resident),
        ],
        out_specs=[tok, tok, chunks, chunks],
        out_shape=[_sds((s, d)), _sds((s, d), MXU_DTYPE), _sds((N_CHIPS, s, fs), MXU_DTYPE),
                   _sds((N_CHIPS, s, fs), MXU_DTYPE)],
        compiler_params=_params(("arbitrary",)),
    )(x, gain, w_gu, w_gu, w_dn)


def _ffn_bwd_dgrad(x, gain, dy, gate, up, w_gu, w_dn, i_gate, i_up, i_down):
    s, d = x.shape
    fs = w_gu.shape[-2]
    tm = _tile(s, FFN_BWD_TILE)

    def body(x_ref, g_ref, dy_ref, gate_ref, up_ref, wg_ref, wu_ref, wd_ref,
             dx_ref, dgain_ref, dyb_ref, dg_ref, du_ref, act_ref):
        i = pl.program_id(0)
        dyb = _cast(dy_ref[...])
        dyb_ref[...] = dyb
        dacts = [_mm_nt(dyb, wd_ref[c]) for c in range(N_CHIPS)]
        dh = None
        for c in range(N_CHIPS):
            g = gate_ref[c].astype(F32)
            u = up_ref[c].astype(F32)
            sg = jax.nn.sigmoid(g)
            silu = g * sg
            dact = FFN_HALF * dacts[c]
            dgb = _cast(dact * u * (sg * (1.0 + g * (1.0 - sg))))
            dub = _cast(dact * silu)
            dg_ref[c] = dgb
            du_ref[c] = dub
            act_ref[c] = _cast(silu * u)
            part = _mm(dgb, wg_ref[c]) + _mm(dub, wu_ref[c])
            dh = part if dh is None else dh + part
        _, r = _rms_fwd(x_ref[...], g_ref[...])
        dxn, dgn = _rms_bwd(x_ref[...], r, g_ref[...], dh)
        dx_ref[...] = dy_ref[...] + dxn

        @pl.when(i == 0)
        def _():
            dgain_ref[...] = dgn

        @pl.when(i > 0)
        def _():
            dgain_ref[...] += dgn

    resident = dict(pipeline_mode=pl.Buffered(1))
    tok = pl.BlockSpec((tm, d), lambda i: (i, 0))
    chunks = pl.BlockSpec((N_CHIPS, tm, fs), lambda i: (0, i, 0))
    return pl.pallas_call(
        body, name="ffn_bwd_dgrad", grid=(s // tm,),
        in_specs=[
            tok, _full((1, d)), tok, chunks, chunks,
            pl.BlockSpec((N_CHIPS, None, fs, d), lambda i: (0, i_gate, 0, 0), **resident),
            pl.BlockSpec((N_CHIPS, None, fs, d), lambda i: (0, i_up, 0, 0), **resident),
            pl.BlockSpec((N_CHIPS, None, fs, d), lambda i: (0, i_down, 0, 0), **resident),
        ],
        out_specs=[tok, _full((1, d)), tok, chunks, chunks, chunks],
        out_shape=[_sds((s, d)), _sds((1, d)), _sds((s, d), MXU_DTYPE),
                   _sds((N_CHIPS, s, fs), MXU_DTYPE), _sds((N_CHIPS, s, fs), MXU_DTYPE),
                   _sds((N_CHIPS, s, fs), MXU_DTYPE)],
        compiler_params=_params(("arbitrary",)),
    )(x, gain, dy, gate, up, w_gu, w_gu, w_dn)


def _ffn_wgrad(hb, dyb, dg, du, act):
    s, d = hb.shape
    fs = dg.shape[-1]
    tk = _tile(s, WGRAD_TILE)
    n_k = s // tk

    def body(h_ref, dy_ref, dg_ref, du_ref, act_ref, wg_ref, wu_ref, wd_ref):
        k = pl.program_id(1)

        @pl.when(k == 0)
        def _():
            wg_ref[...] = jnp.zeros_like(wg_ref)
            wu_ref[...] = jnp.zeros_like(wu_ref)
            wd_ref[...] = jnp.zeros_like(wd_ref)

        h = h_ref[...]
        wg_ref[...] += _mm_tn(dg_ref[...], h)
        wu_ref[...] += _mm_tn(du_ref[...], h)
        wd_ref[...] += FFN_HALF * _mm_tn(act_ref[...], dy_ref[...])

    tok = pl.BlockSpec((tk, d), lambda j, k: (k, 0))
    chunk = pl.BlockSpec((None, tk, fs), lambda j, k: (j, k, 0))
    return pl.pallas_call(
        body, name="ffn_wgrad", grid=(N_CHIPS, n_k),
        in_specs=[tok, tok, chunk, chunk, chunk],
        out_specs=[pl.BlockSpec((None, fs, d), lambda j, k: (j, 0, 0)),
                   pl.BlockSpec((None, fs, d), lambda j, k: (j, 0, 0)),
                   pl.BlockSpec((None, fs, d), lambda j, k: (j, 0, 0))],
        out_shape=[_sds((N_CHIPS, fs, d))] * 3,
        compiler_params=_params(("arbitrary", "arbitrary")),
    )(hb, dyb, dg, du, act)


def _inv_count(first_row, n_rows, window):
    t = first_row + lax.broadcasted_iota(jnp.int32, (n_rows, 1), 0)
    return 1.0 / jnp.minimum(t + 1, window).astype(F32)


def _trailing_sum(v, window):
    k = 1
    while k < window:
        v = v + pltpu.roll(v, k, 0)
        k *= 2
    return v


def _leading_sum(v, window):
    n = v.shape[0]
    k = 1
    while k < window:
        v = v + pltpu.roll(v, n - k, 0)
        k *= 2
    return v


def _pool_normed_rows(x_ref, prev_ref, g_ref, i):
    h, r = _rms_fwd(x_ref[...], g_ref[...])
    hp, _ = _rms_fwd(prev_ref[...], g_ref[...])
    hp = jnp.where(i > 0, hp, 0.0)
    return jnp.concatenate([hp, h], axis=0), r


def _pooled_group(he, g, pg, first_row, tm):
    ue = he[:, g * pg:(g + 1) * pg]
    win = _trailing_sum(ue, POOL_WINDOWS[g])[POOL_HALO:]
    return win * _inv_count(first_row, tm, POOL_WINDOWS[g]) - ue[POOL_HALO:]


def _pool_specs(s, d, tm):
    per = tm // POOL_HALO
    last = s // POOL_HALO - 1
    tok = pl.BlockSpec((tm, d), lambda i: (i, 0))
    prev = pl.BlockSpec((POOL_HALO, d), lambda i: (jnp.maximum(i * per - 1, 0), 0))
    nxt = pl.BlockSpec((POOL_HALO, d), lambda i: (jnp.minimum((i + 1) * per, last), 0))
    return tok, prev, nxt


def _pool_fwd(x, gain, w, scale):
    s, d = x.shape
    n_g, pg = w.shape[0], w.shape[-1]
    tm = _tile(s, POOL_TILE)
    tok, prev, _ = _pool_specs(s, d, tm)

    def body(x_ref, prev_ref, g_ref, w_ref, sc_ref, y_ref):
        i = pl.program_id(0)
        he, _ = _pool_normed_rows(x_ref, prev_ref, g_ref, i)
        z = [_mm(_cast(_pooled_group(he, g, pg, i * tm, tm)), w_ref[g]) for g in range(n_g)]
        y_ref[...] = x_ref[...] + jnp.concatenate(z, axis=-1) * sc_ref[...]

    return pl.pallas_call(
        body, name="pool_fwd", grid=(s // tm,),
        in_specs=[tok, prev, _full((1, d)), _full(w.shape), _full((1, d))],
        out_specs=tok, out_shape=_sds((s, d)),
        compiler_params=_params(("arbitrary",)),
    )(x, x, gain, w, scale)


def _pool_bwd(x, gain, w, scale, dy):
    s, d = x.shape
    n_g, pg = w.shape[0], w.shape[-1]
    tm = _tile(s, POOL_TILE)
    n_tiles = s // tm
    tok, prev, nxt = _pool_specs(s, d, tm)

    def body(x_ref, prev_ref, dy_ref, next_ref, g_ref, w_ref, sc_ref, dx_ref, dgain_ref, dw_ref, dsc_ref):
        i = pl.program_id(0)

        @pl.when(i == 0)
        def _():
            dgain_ref[...] = jnp.zeros_like(dgain_ref)
            dw_ref[...] = jnp.zeros_like(dw_ref)
            dsc_ref[...] = jnp.zeros_like(dsc_ref)

        he, r = _pool_normed_rows(x_ref, prev_ref, g_ref, i)
        dy = dy_ref[...]
        dyn = jnp.where(i < n_tiles - 1, next_ref[...], 0.0)
        dze = jnp.concatenate([dy, dyn], axis=0) * sc_ref[...]
        dh, dsc = [], []
        for g in range(n_g):
            cols = slice(g * pg, (g + 1) * pg)
            pooled = _cast(_pooled_group(he, g, pg, i * tm, tm))
            dsc.append(jnp.sum(dy[:, cols] * _mm(pooled, w_ref[g]), axis=0, keepdims=True))
            dzb = _cast(dze[:, cols])
            dw_ref[g] += _mm_tn(pooled, dzb[:tm])
            dpool = _mm_nt(dzb, w_ref[g])
            spread = _leading_sum(dpool * _inv_count(i * tm, tm + POOL_HALO, POOL_WINDOWS[g]), POOL_WINDOWS[g])
            dh.append(spread[:tm] - dpool[:tm])
        dsc_ref[...] += jnp.concatenate(dsc, axis=-1)
        dxn, dgn = _rms_bwd(x_ref[...], r, g_ref[...], jnp.concatenate(dh, axis=-1))
        dgain_ref[...] += dgn
        dx_ref[...] = dy + dxn

    return pl.pallas_call(
        body, name="pool_bwd", grid=(n_tiles,),
        in_specs=[tok, prev, tok, nxt, _full((1, d)), _full(w.shape), _full((1, d))],
        out_specs=[tok, _full((1, d)), _full(w.shape), _full((1, d))],
        out_shape=[_sds((s, d)), _sds((1, d)), _sds(w.shape), _sds((1, d))],
        compiler_params=_params(("arbitrary",)),
    )(x, x, dy, dy, gain, w, scale)


def _rope_tables(pos_col, inv_freq):
    s = pos_col.shape[0]
    tm = _tile(s, ROW_TILE)
    half = QK_ROPE // 2

    def body(p_ref, f_ref, c_ref, s_ref):
        ang = p_ref[...].astype(F32) * f_ref[...]
        cos, sin = jnp.cos(ang), jnp.sin(ang)
        c_ref[...] = jnp.concatenate([jnp.ones((tm, QK_NOPE), F32), cos, cos], axis=-1)
        s_ref[...] = jnp.concatenate([jnp.zeros((tm, QK_NOPE), F32), -sin, sin], axis=-1)

    tab = pl.BlockSpec((tm, QK_HEAD), lambda i: (i, 0))
    return pl.pallas_call(
        body, name="rope_tables", grid=(s // tm,),
        in_specs=[pl.BlockSpec((tm, 1), lambda i: (i, 0)), _full((1, half))],
        out_specs=[tab, tab], out_shape=[_sds((s, QK_HEAD)), _sds((s, QK_HEAD))],
        compiler_params=_params(("arbitrary",)),
    )(pos_col, inv_freq)


def _swap_rope_halves(v):
    half = QK_ROPE // 2
    return jnp.concatenate([v[:, :QK_NOPE], v[:, QK_NOPE + half:], v[:, QK_NOPE:QK_NOPE + half]], axis=-1)


def _rope(v, cos, sin):
    return v * cos + _swap_rope_halves(v) * sin


def _rope_transposed(dv, cos, sin):
    return dv * cos + _swap_rope_halves(dv * sin)


def _mla_qkv_fwd(x, gain, w_in, q_norm, kv_norm, w_q, w_kn, w_v, q_head_norm, k_head_norm, cos, sin):
    s, d = x.shape
    n_h, ql = w_q.shape[0], w_q.shape[1]
    kvl, lat_w = w_kn.shape[1], w_in.shape[1]
    tm = _tile(s, MLA_TILE)

    def body(x_ref, g_ref, win_ref, qn_ref, kvn_ref, wq_ref, wkn_ref, wv_ref, qhn_ref, khn_ref, c_ref, s_ref,
             lat_ref, q_ref, k_ref, v_ref):
        h, _ = _rms_fwd(x_ref[...], g_ref[...])
        lat = _mm(_cast(h), win_ref[...])
        lat_ref[...] = lat
        cqn, _ = _rms_fwd(lat[:, :ql], qn_ref[...])
        ckvn, _ = _rms_fwd(lat[:, ql:ql + kvl], kvn_ref[...])
        kpe = lat[:, ql + kvl:]
        cqb, ckb = _cast(cqn), _cast(ckvn)
        cos_t, sin_t = c_ref[...], s_ref[...]
        v_ref[...] = _cast(_mm(ckb, wv_ref[...]))
        for hh in range(n_h):
            qn, _ = _rms_fwd(_mm(cqb, wq_ref[hh]), qhn_ref[...])
            q_ref[hh] = _cast(_rope(qn, cos_t, sin_t) * SCORE_SCALE)
            kn, _ = _rms_fwd(jnp.concatenate([_mm(ckb, wkn_ref[hh]), kpe], axis=-1), khn_ref[...])
            k_ref[hh] = _cast(_rope(kn, cos_t, sin_t))

    tok = lambda w: pl.BlockSpec((tm, w), lambda i: (i, 0))
    heads = pl.BlockSpec((n_h, tm, QK_HEAD), lambda i: (0, i, 0))
    return pl.pallas_call(
        body, name="mla_qkv_fwd", grid=(s // tm,),
        in_specs=[tok(d), _full((1, d)), _full(w_in.shape), _full((1, ql)), _full((1, kvl)), _full(w_q.shape),
                  _full(w_kn.shape), _full(w_v.shape), _full((1, QK_HEAD)), _full((1, QK_HEAD)),
                  tok(QK_HEAD), tok(QK_HEAD)],
        out_specs=[tok(lat_w), heads, heads, tok(n_h * V_HEAD)],
        out_shape=[_sds((s, lat_w)), _sds((n_h, s, QK_HEAD), MXU_DTYPE), _sds((n_h, s, QK_HEAD), MXU_DTYPE),
                   _sds((s, n_h * V_HEAD), MXU_DTYPE)],
        compiler_params=_params(("arbitrary",)),
    )(x, gain, w_in, q_norm, kv_norm, w_q, w_kn, w_v, q_head_norm, k_head_norm, cos, sin)


def _mla_qkv_bwd(x, lat, dy, dq, dk, dv, gain, w_in, q_norm, kv_norm, w_q, w_kn, w_v, q_head_norm, k_head_norm,
                 cos, sin):
    s, d = x.shape
    n_h, ql = w_q.shape[0], w_q.shape[1]
    kvl, lat_w = w_kn.shape[1], w_in.shape[1]
    tm = _tile(s, MLA_TILE)

    def body(x_ref, lat_ref, dy_ref, dq_ref, dk_ref, dv_ref, g_ref, win_ref, qn_ref, kvn_ref, wq_ref, wkn_ref,
             wv_ref, qhn_ref, khn_ref, c_ref, s_ref,
             dx_ref, dg_ref, dwin_ref, dqn_ref, dkvn_ref, dwq_ref, dwkn_ref, dwv_ref, dqhn_ref, dkhn_ref):
        @pl.when(pl.program_id(0) == 0)
        def _():
            for ref in (dg_ref, dwin_ref, dqn_ref, dkvn_ref, dwq_ref, dwkn_ref, dwv_ref, dqhn_ref, dkhn_ref):
                ref[...] = jnp.zeros_like(ref)

        x_t = x_ref[...]
        h, r = _rms_fwd(x_t, g_ref[...])
        hb = _cast(h)
        lat = lat_ref[...]
        cq, ckv, kpe = lat[:, :ql], lat[:, ql:ql + kvl], lat[:, ql + kvl:]
        cqn, rq = _rms_fwd(cq, qn_ref[...])
        ckvn, rkv = _rms_fwd(ckv, kvn_ref[...])
        cqb, ckb = _cast(cqn), _cast(ckvn)
        cos_t, sin_t = c_ref[...], s_ref[...]

        dvb = _cast(dv_ref[...])
        dwv_ref[...] += _mm_tn(ckb, dvb)
        dckvn = _mm_nt(dvb, wv_ref[...])
        dcqn = jnp.zeros((tm, ql), F32)
        dkpe = jnp.zeros((tm, QK_ROPE), F32)
        dqhn = jnp.zeros((1, QK_HEAD), F32)
        dkhn = jnp.zeros((1, QK_HEAD), F32)
        for hh in range(n_h):
            qp = _mm(cqb, wq_ref[hh])
            _, rqp = _rms_fwd(qp, qhn_ref[...])
            dqp, dgq = _rms_bwd(qp, rqp, qhn_ref[...], _rope_transposed(dq_ref[hh] * SCORE_SCALE, cos_t, sin_t))
            dqhn += dgq
            dqpb = _cast(dqp)
            dwq_ref[hh] += _mm_tn(cqb, dqpb)
            dcqn += _mm_nt(dqpb, wq_ref[hh])

            kp = jnp.concatenate([_mm(ckb, wkn_ref[hh]), kpe], axis=-1)
            _, rkp = _rms_fwd(kp, khn_ref[...])
            dkp, dgk = _rms_bwd(kp, rkp, khn_ref[...], _rope_transposed(dk_ref[hh], cos_t, sin_t))
            dkhn += dgk
            dknb = _cast(dkp[:, :QK_NOPE])
            dkpe += dkp[:, QK_NOPE:]
            dwkn_ref[hh] += _mm_tn(ckb, dknb)
            dckvn += _mm_nt(dknb, wkn_ref[hh])
        dqhn_ref[...] += dqhn
        dkhn_ref[...] += dkhn

        dcq, dgn = _rms_bwd(cq, rq, qn_ref[...], dcqn)
        dqn_ref[...] += dgn
        dckv, dgn = _rms_bwd(ckv, rkv, kvn_ref[...], dckvn)
        dkvn_ref[...] += dgn
        dlb = _cast(jnp.concatenate([dcq, dckv, dkpe], axis=-1))
        dwin_ref[...] += _mm_tn(hb, dlb)
        dxn, dgn = _rms_bwd(x_t, r, g_ref[...], _mm_nt(dlb, win_ref[...]))
        dg_ref[...] += dgn
        dx_ref[...] = dy_ref[...] + dxn

    tok = lambda w: pl.BlockSpec((tm, w), lambda i: (i, 0))
    heads = pl.BlockSpec((n_h, tm, QK_HEAD), lambda i: (0, i, 0))
    return pl.pallas_call(
        body, name="mla_qkv_bwd", grid=(s // tm,),
        in_specs=[tok(d), tok(lat_w), tok(d), heads, heads, tok(n_h * V_HEAD), _full((1, d)), _full(w_in.shape),
                  _full((1, ql)), _full((1, kvl)), _full(w_q.shape), _full(w_kn.shape), _full(w_v.shape),
                  _full((1, QK_HEAD)), _full((1, QK_HEAD)), tok(QK_HEAD), tok(QK_HEAD)],
        out_specs=[tok(d), _full((1, d)), _full(w_in.shape), _full((1, ql)), _full((1, kvl)), _full(w_q.shape),
                   _full(w_kn.shape), _full(w_v.shape), _full((1, QK_HEAD)), _full((1, QK_HEAD))],
        out_shape=[_sds((s, d)), _sds((1, d)), _sds(w_in.shape), _sds((1, ql)), _sds((1, kvl)), _sds(w_q.shape),
                   _sds(w_kn.shape), _sds(w_v.shape), _sds((1, QK_HEAD)), _sds((1, QK_HEAD))],
        compiler_params=_params(("arbitrary",)),
    )(x, lat, dy, dq, dk, dv, gain, w_in, q_norm, kv_norm, w_q, w_kn, w_v, q_head_norm, k_head_norm, cos, sin)


def _scores_t(k_t, q_t):
    return _mm_nt(k_t, q_t)


def _mask_above_diagonal(z, t):
    key = lax.broadcasted_iota(jnp.int32, (t, t), 0)
    query = lax.broadcasted_iota(jnp.int32, (t, t), 1)
    return jnp.where(key <= query, z, -jnp.inf)


def _flash_fwd(q, k, vt):
    n_h, s, _ = q.shape
    t = _tile(s, ATTN_FWD_TILE)
    n = s // t
    rows_v = vt.shape[2]

    def body(q_ref, k_ref, vt_ref, ot_ref, lse_ref, m_sc, acc_sc, z_sc):
        i = pl.program_id(1)
        m_sc[...] = jnp.full_like(m_sc, -jnp.inf)
        acc_sc[...] = jnp.zeros_like(acc_sc)
        q_t = q_ref[...]

        def fetch(j, slot):
            z_sc[slot] = _scores_t(k_ref[pl.ds(pl.multiple_of(j * t, t), t), :], q_t)

        def stage(j, slot, masked, prefetch=True):
            if prefetch:
                fetch(j + 1, 1 - slot)
            z = _mask_above_diagonal(z_sc[slot], t) if masked else z_sc[slot]
            m_old = m_sc[...]
            m_new = jnp.maximum(m_old, jnp.max(z, axis=0, keepdims=True))
            alpha = jnp.exp(m_old - m_new)
            acc_sc[...] = alpha * acc_sc[...] + _mm(vt_ref[j], _cast(jnp.exp(z - m_new)))
            m_sc[...] = m_new

        def pair_below_diagonal(pair, carry):
            stage(2 * pair, 0, False)
            stage(2 * pair + 1, 1, False)
            return carry

        fetch(0, 0)
        lax.fori_loop(0, i >> 1, pair_below_diagonal, 0)

        @pl.when((i & 1) == 1)
        def _():
            stage(i - 1, 0, False)
            stage(i, 1, True, prefetch=False)

        @pl.when((i & 1) == 0)
        def _():
            stage(i, 0, True, prefetch=False)

        denom = acc_sc[V_HEAD:V_HEAD + 1, :]
        ot_ref[...] = acc_sc[:V_HEAD, :] / denom
        lse_ref[...] = m_sc[...] + jnp.log(denom)

    whole_head = dict(pipeline_mode=pl.Buffered(1))
    return pl.pallas_call(
        body, name="flash_fwd", grid=(n_h, n),
        in_specs=[pl.BlockSpec((None, t, QK_HEAD), lambda h, i: (h, i, 0)),
                  pl.BlockSpec((None, s, QK_HEAD), lambda h, i: (h, 0, 0), **whole_head),
                  pl.BlockSpec((None, n, rows_v, t), lambda h, i: (h, 0, 0, 0), **whole_head)],
        out_specs=[pl.BlockSpec((V_HEAD, t), lambda h, i: (h, i)),
                   pl.BlockSpec((None, 1, t), lambda h, i: (h, 0, i))],
        out_shape=[_sds((n_h * V_HEAD, s)), _sds((n_h, 1, s))],
        scratch_shapes=[pltpu.VMEM((1, t), F32), pltpu.VMEM((rows_v, t), F32), pltpu.VMEM((2, t, t), F32)],
        compiler_params=_params(("arbitrary", "arbitrary")),
    )(q, k, vt)


def _flash_bwd(q, k, kt, v, do, lse, delta):
    n_h, s, _ = q.shape
    t = _tile(s, ATTN_BWD_TILE)
    n = s // t

    def body(q_ref, do_ref, lse_ref, dl_ref, k_ref, kt_ref, v_ref, dqt_ref, dk_ref, dv_ref, dk_sc, dv_sc, z_sc, dp_sc):
        j = pl.program_id(1)

        @pl.when(j == 0)
        def _():
            dqt_ref[...] = jnp.zeros_like(dqt_ref)

        dk_sc[...] = jnp.zeros_like(dk_sc)
        dv_sc[...] = jnp.zeros_like(dv_sc)
        k_t, kt_t, v_t = k_ref[...], kt_ref[...], v_ref[...]

        def rows(i):
            return pl.ds(pl.multiple_of(i * t, t), t)

        def fetch(i, slot):
            i = jnp.minimum(i, n - 1)
            z_sc[slot] = _scores_t(k_t, q_ref[rows(i), :])
            dp_sc[slot] = _mm_nt(v_t, do_ref[rows(i), :])

        def stage(i, slot, masked, prefetch=True):
            if prefetch:
                fetch(i + 1, 1 - slot)
            z = _mask_above_diagonal(z_sc[slot], t) if masked else z_sc[slot]
            pr = jnp.exp(z - lse_ref[i])
            dsb = _cast(pr * (dp_sc[slot] - dl_ref[i]))
            dv_sc[...] += _mm(_cast(pr), do_ref[rows(i), :])
            dk_sc[...] += _mm(dsb, q_ref[rows(i), :])
            dqt_ref[i] += _mm(kt_t, dsb)

        def pair_below_diagonal(pair, carry):
            stage(j + 1 + 2 * pair, 1, False)
            stage(j + 2 + 2 * pair, 0, False)
            return carry

        below = n - 1 - j
        fetch(j, 0)
        stage(j, 0, True)
        lax.fori_loop(0, below >> 1, pair_below_diagonal, 0)

        @pl.when((below & 1) == 1)
        def _():
            stage(n - 1, 1, False, prefetch=False)

        dk_ref[...] = dk_sc[...]
        dv_ref[...] = dv_sc[...]

    whole_head = dict(pipeline_mode=pl.Buffered(1))
    stat = pl.BlockSpec((None, n, 1, t), lambda h, j: (h, 0, 0, 0))
    return pl.pallas_call(
        body, name="flash_bwd", grid=(n_h, n),
        in_specs=[pl.BlockSpec((None, s, QK_HEAD), lambda h, j: (h, 0, 0), **whole_head),
                  pl.BlockSpec((s, V_HEAD), lambda h, j: (0, h), **whole_head),
                  stat, stat,
                  pl.BlockSpec((None, t, QK_HEAD), lambda h, j: (h, j, 0)),
                  pl.BlockSpec((None, QK_HEAD, t), lambda h, j: (h, 0, j)),
                  pl.BlockSpec((t, V_HEAD), lambda h, j: (j, h))],
        out_specs=[pl.BlockSpec((None, n, QK_HEAD, t), lambda h, j: (h, 0, 0, 0), **whole_head),
                   pl.BlockSpec((None, t, QK_HEAD), lambda h, j: (h, j, 0)),
                   pl.BlockSpec((t, V_HEAD), lambda h, j: (j, h))],
        out_shape=[_sds((n_h, n, QK_HEAD, t)), _sds((n_h, s, QK_HEAD)), _sds((s, n_h * V_HEAD))],
        scratch_shapes=[pltpu.VMEM((t, QK_HEAD), F32), pltpu.VMEM((t, V_HEAD), F32),
                        pltpu.VMEM((2, t, t), F32), pltpu.VMEM((2, t, t), F32)],
        compiler_params=_params(("arbitrary", "arbitrary")),
    )(q, do, lse, delta, k, kt, v)


def _mla_out_fwd(x, ot, w_out):
    s, d = x.shape
    hv = ot.shape[0]
    tm = _tile(s, FFN_TILE)

    def body(x_ref, ot_ref, w_ref, y_ref):
        y_ref[...] = x_ref[...] + _mm_tn(_cast(ot_ref[...]), w_ref[...])

    tok = pl.BlockSpec((tm, d), lambda i: (i, 0))
    return pl.pallas_call(
        body, name="mla_out_fwd", grid=(s // tm,),
        in_specs=[tok, pl.BlockSpec((hv, tm), lambda i: (0, i)), _full(w_out.shape)],
        out_specs=tok, out_shape=_sds((s, d)),
        compiler_params=_params(("arbitrary",)),
    )(x, ot, w_out)


def _mla_out_bwd(dy, ot, w_out):
    s, d = dy.shape
    hv = ot.shape[0]
    n_h = hv // V_HEAD
    tm = _tile(s, FFN_TILE)

    def body(dy_ref, ot_ref, w_ref, do_ref, dl_ref, dw_ref):
        @pl.when(pl.program_id(0) == 0)
        def _():
            dw_ref[...] = jnp.zeros_like(dw_ref)

        dyb = _cast(dy_ref[...])
        o_t = ot_ref[...]
        do_ref[...] = _cast(_mm_nt(dyb, w_ref[...]))
        prod = _mm_nt(w_ref[...], dyb) * o_t
        for hh in range(n_h):
            dl_ref[hh] = jnp.sum(prod[hh * V_HEAD:(hh + 1) * V_HEAD], axis=0, keepdims=True)
        dw_ref[...] += _mm(_cast(o_t), dyb)

    return pl.pallas_call(
        body, name="mla_out_bwd", grid=(s // tm,),
        in_specs=[pl.BlockSpec((tm, d), lambda i: (i, 0)), pl.BlockSpec((hv, tm), lambda i: (0, i)),
                  _full(w_out.shape)],
        out_specs=[pl.BlockSpec((tm, hv), lambda i: (i, 0)), pl.BlockSpec((n_h, 1, tm), lambda i: (0, 0, i)),
                   _full(w_out.shape)],
        out_shape=[_sds((s, hv), MXU_DTYPE), _sds((n_h, 1, s)), _sds(w_out.shape)],
        compiler_params=_params(("arbitrary",)),
    )(dy, ot, w_out)


def _loss_and_grad(y, target):
    s, d = y.shape
    tm = _tile(s, ROW_TILE)

    def body(y_ref, t_ref, loss_ref, dy_ref):
        @pl.when(pl.program_id(0) == 0)
        def _():
            loss_ref[...] = jnp.zeros_like(loss_ref)

        err = y_ref[...] - t_ref[...]
        dy_ref[...] = err * (1.0 / d)
        loss_ref[...] += 0.5 * jnp.sum(jnp.mean(err * err, axis=-1, keepdims=True), axis=0, keepdims=True)

    tok = pl.BlockSpec((tm, d), lambda i: (i, 0))
    return pl.pallas_call(
        body, name="loss_and_grad", grid=(s // tm,),
        in_specs=[tok, tok], out_specs=[_full((1, 1)), tok],
        out_shape=[_sds((1, 1)), _sds((s, d))],
        compiler_params=_params(("arbitrary",)),
    )(y, target)


def _mesh_position():
    return lax.axis_index("x"), lax.axis_index("y"), lax.axis_index("c")


def _other_chips(x, y):
    return [(1 - x, y), (x, 1 - y), (1 - x, 1 - y)]


ANY = pl.BlockSpec(memory_space=pl.ANY)


def _gather_over_chips(arrs):
    n = len(arrs)
    halves = [a.shape[0] // 2 for a in arrs]
    assert all(a.shape[0] % 2 == 0 for a in arrs)
    own = 2 * (N_CHIPS - 1)

    def body(*refs):
        srcs, outs = refs[:n], refs[n:2 * n]
        send_sems, recv_sems = refs[2 * n:]
        x, y, c = _mesh_position()
        me, sibling = (x, y, c), (x, y, 1 - c)
        chips = _other_chips(x, y)
        my_chip = 2 * x + y

        def rows(t, chip, half):
            return outs[t].at[chip, pl.ds(half * halves[t], halves[t])]

        def copy(t, k, src, dst, to):
            return pltpu.make_async_remote_copy(src_ref=src, dst_ref=dst, send_sem=send_sems.at[t, k],
                                                recv_sem=recv_sems.at[t, k], device_id=to, device_id_type=MESH)

        started = []
        for t in range(n):
            for k, (px, py) in enumerate(chips):
                cp = copy(t, k, srcs[t].at[pl.ds(c * halves[t], halves[t])], rows(t, my_chip, c), (px, py, c))
                cp.start()
                started.append(cp)
            cp = copy(t, own, srcs[t], outs[t].at[my_chip], sibling)
            cp.start()
            started.append(cp)
        for t in range(n):
            for k, (px, py) in enumerate(chips):
                landed = rows(t, 2 * px + py, c)
                copy(t, k, landed, landed, me).wait_recv()
                cp = copy(t, N_CHIPS - 1 + k, landed, landed, sibling)
                cp.start()
                started.append(cp)
        for t in range(n):
            for k, (px, py) in enumerate(chips):
                passed = rows(t, 2 * px + py, 1 - c)
                copy(t, N_CHIPS - 1 + k, passed, passed, me).wait_recv()
            copy(t, own, srcs[t], outs[t].at[my_chip], me).wait_recv()
        for cp in started:
            cp.wait_send()

    return pl.pallas_call(
        body, name="gather_over_chips",
        in_specs=[ANY] * n, out_specs=[ANY] * n,
        out_shape=[_sds((N_CHIPS,) + a.shape, a.dtype) for a in arrs],
        scratch_shapes=[pltpu.SemaphoreType.DMA((n, own + 1)), pltpu.SemaphoreType.DMA((n, own + 1))],
    )(*arrs)


def _send_other_half_to_sibling(grads):
    n = len(grads)
    halves = [g.shape[1] // 2 for g in grads]

    def body(*refs):
        srcs, outs = refs[:n], refs[n:2 * n]
        send_sems, recv_sems = refs[2 * n:]
        x, y, c = _mesh_position()
        copies = []
        for t in range(n):
            cp = pltpu.make_async_remote_copy(
                src_ref=srcs[t].at[pl.ds(0, N_CHIPS), pl.ds((1 - c) * halves[t], halves[t])], dst_ref=outs[t],
                send_sem=send_sems.at[t], recv_sem=recv_sems.at[t], device_id=(x, y, 1 - c), device_id_type=MESH)
            cp.start()
            copies.append(cp)
        for cp in copies:
            cp.wait_recv()
        for cp in copies:
            cp.wait_send()

    return pl.pallas_call(
        body, name="send_other_half_to_sibling",
        in_specs=[ANY] * n, out_specs=[ANY] * n,
        out_shape=[_sds((N_CHIPS, h) + g.shape[2:]) for g, h in zip(grads, halves)],
        scratch_shapes=[pltpu.SemaphoreType.DMA((n,)), pltpu.SemaphoreType.DMA((n,))],
    )(*grads)


def _send_blocks_to_chips(parts):
    n = len(parts)

    def body(*refs):
        srcs, outs = refs[:n], refs[n:2 * n]
        send_sems, recv_sems = refs[2 * n:]
        x, y, c = _mesh_position()
        copies = []
        for t in range(n):
            for k, (px, py) in enumerate(_other_chips(x, y)):
                cp = pltpu.make_async_remote_copy(
                    src_ref=srcs[t].at[2 * px + py], dst_ref=outs[t].at[k], send_sem=send_sems.at[t, k],
                    recv_sem=recv_sems.at[t, k], device_id=(px, py, c), device_id_type=MESH)
                cp.start()
                copies.append(cp)
        for cp in copies:
            cp.wait_recv()
        for cp in copies:
            cp.wait_send()

    return pl.pallas_call(
        body, name="send_blocks_to_chips",
        in_specs=[ANY] * n, out_specs=[ANY] * n,
        out_shape=[_sds((N_CHIPS - 1,) + p.shape[1:], p.dtype) for p in parts],
        scratch_shapes=[pltpu.SemaphoreType.DMA((n, N_CHIPS - 1)), pltpu.SemaphoreType.DMA((n, N_CHIPS - 1))],
    )(*parts)


def _join_halves_with_sibling(sums):
    n = len(sums)

    def body(*refs):
        srcs, outs = refs[:n], refs[n:2 * n]
        send_sems, recv_sems = refs[2 * n:]
        x, y, c = _mesh_position()
        copies = []
        for t in range(n):
            h = srcs[t].shape[0] // 2
            mine = pl.ds(c * h, h)
            cp = pltpu.make_async_remote_copy(
                src_ref=srcs[t].at[mine], dst_ref=outs[t].at[mine], send_sem=send_sems.at[t],
                recv_sem=recv_sems.at[t], device_id=(x, y, 1 - c), device_id_type=MESH)
            cp.start()
            copies.append(cp)
        for t in range(n):
            h = srcs[t].shape[0] // 2
            theirs = pl.ds((1 - c) * h, h)
            pltpu.make_async_remote_copy(
                src_ref=srcs[t].at[theirs], dst_ref=outs[t].at[theirs], send_sem=send_sems.at[t],
                recv_sem=recv_sems.at[t], device_id=(x, y, 1 - c), device_id_type=MESH).wait_recv()
        for cp in copies:
            cp.wait_send()

    return pl.pallas_call(
        body, name="join_halves_with_sibling",
        in_specs=[ANY] * n, out_specs=[ANY] * n,
        out_shape=[_sds(a.shape) for a in sums],
        input_output_aliases={t: t for t in range(n)},
        scratch_shapes=[pltpu.SemaphoreType.DMA((n,)), pltpu.SemaphoreType.DMA((n,))],
    )(*sums)


def _gather_over_devices(rows):
    r = rows.shape[0]

    def body(in_ref, out_ref, send_sems, recv_sems, local_sem):
        x, y, c = _mesh_position()
        mine = pltpu.make_async_copy(in_ref, out_ref.at[4 * x + 2 * y + c], local_sem)
        mine.start()
        copies = []
        for mask in range(1, N_DEVICES):
            fx, fy, fc = (mask >> 2) & 1, (mask >> 1) & 1, mask & 1
            px, py, pc = (1 - x if fx else x), (1 - y if fy else y), (1 - c if fc else c)
            send = pltpu.make_async_remote_copy(
                src_ref=in_ref, dst_ref=out_ref.at[4 * x + 2 * y + c], send_sem=send_sems.at[mask - 1],
                recv_sem=recv_sems.at[mask - 1], device_id=(px, py, pc), device_id_type=MESH)
            send.start()
            recv = pltpu.make_async_remote_copy(
                src_ref=in_ref, dst_ref=out_ref.at[4 * px + 2 * py + pc], send_sem=send_sems.at[mask - 1],
                recv_sem=recv_sems.at[mask - 1], device_id=(px, py, pc), device_id_type=MESH)
            copies.append((send, recv))
        for _, recv in copies:
            recv.wait_recv()
        for send, _ in copies:
            send.wait_send()
        mine.wait()

    vm = pl.BlockSpec(memory_space=pltpu.VMEM)
    return pl.pallas_call(
        body, name="gather_over_devices", in_specs=[vm], out_specs=vm,
        out_shape=_sds((N_DEVICES, r, LANES)),
        scratch_shapes=[pltpu.SemaphoreType.DMA((N_DEVICES - 1,)), pltpu.SemaphoreType.DMA((N_DEVICES - 1,)),
                        pltpu.SemaphoreType.DMA],
    )(rows)


def _add_sibling_half(grad, received, chip, core):
    _, l, r, c = grad.shape
    half = l // 2

    def body(chip_ref, core_ref, g_ref, r_ref, wire_ref, own_ref):
        total = g_ref[...] + r_ref[...]
        wire_ref[...] = total.astype(WIRE_DTYPE)

        @pl.when(pl.program_id(1) == chip_ref[0])
        def _():
            own_ref[...] = total

    blk = lambda f: pl.BlockSpec((None, None, r, c), f)
    grid_spec = pltpu.PrefetchScalarGridSpec(
        num_scalar_prefetch=2, grid=(half, N_CHIPS),
        in_specs=[blk(lambda i, j, chip, core: (j, core[0] * half + i, 0, 0)),
                  blk(lambda i, j, chip, core: (j, i, 0, 0))],
        out_specs=[blk(lambda i, j, chip, core: (j, i, 0, 0)),
                   pl.BlockSpec((None, r, c), lambda i, j, chip, core: (i, 0, 0))])
    return pl.pallas_call(
        body, name="add_sibling_half", grid_spec=grid_spec,
        out_shape=[_sds((N_CHIPS, half, r, c), WIRE_DTYPE), _sds((half, r, c))],
        compiler_params=_params(("arbitrary", "arbitrary")),
    )(chip, core, grad, received)


def _add_chip_blocks(own, received, core):
    half, r, c = own.shape

    def body(core_ref, p_ref, r0_ref, r1_ref, r2_ref, o_ref):
        o_ref[...] = ((p_ref[...] + r0_ref[...].astype(F32)) + r1_ref[...].astype(F32)) + r2_ref[...].astype(F32)

    grid_spec = pltpu.PrefetchScalarGridSpec(
        num_scalar_prefetch=1, grid=(half,),
        in_specs=[pl.BlockSpec((None, r, c), lambda i, core: (i, 0, 0))] + [
            pl.BlockSpec((None, None, r, c), functools.partial(lambda i, core, k: (k, i, 0, 0), k=k))
            for k in range(N_CHIPS - 1)],
        out_specs=pl.BlockSpec((None, r, c), lambda i, core: (core[0] * half + i, 0, 0)))
    return pl.pallas_call(
        body, name="add_chip_blocks", grid_spec=grid_spec, out_shape=_sds((2 * half, r, c)),
        compiler_params=_params(("arbitrary",)),
    )(core, own, received, received, received)


def _sum_over_devices(parts):
    _, r, _ = parts.shape

    def body(p_ref, o_ref):
        acc = p_ref[0]
        for k in range(1, N_DEVICES):
            acc = acc + p_ref[k]
        o_ref[...] = acc

    return pl.pallas_call(body, name="sum_over_devices", out_shape=_sds((r, LANES)))(parts)


def _adamw_math(w, g, m, v):
    m = ADAM_B1 * m + (1.0 - ADAM_B1) * g
    v = ADAM_B2 * v + (1.0 - ADAM_B2) * (g * g)
    m_hat = m / (1.0 - ADAM_B1 ** ADAM_STEP)
    v_hat = v / (1.0 - ADAM_B2 ** ADAM_STEP)
    delta = -ADAM_LR * (m_hat / (jnp.sqrt(v_hat) + ADAM_EPS) + ADAM_WD * w)
    return delta, m, v


def _adamw_stacked(w, m, v, grads, offset):
    l, r, c = w.shape
    tr = r
    while tr * c * 4 > 2**20 and tr % 16 == 0:
        tr //= 2

    def body(w_ref, m_ref, v_ref, g_ref, go_ref, d_ref, mo_ref, vo_ref):
        g = g_ref[...]
        go_ref[...] = g
        d_ref[...], mo_ref[...], vo_ref[...] = _adamw_math(w_ref[...], g, m_ref[...], v_ref[...])

    blk = pl.BlockSpec((None, tr, c), lambda i, j: (i, j, 0))
    return pl.pallas_call(
        body, name="adamw_stacked", grid=(l, r // tr),
        in_specs=[blk, blk, blk, pl.BlockSpec((None, tr, c), lambda i, j: (offset + i, j, 0))],
        out_specs=[blk] * 4, out_shape=[_sds((l, r, c))] * 4,
        compiler_params=_params(("arbitrary", "arbitrary")),
    )(w, m, v, grads)


def _adamw_small(w, m, v, g):
    def body(w_ref, m_ref, v_ref, g_ref, d_ref, mo_ref, vo_ref):
        d_ref[...], mo_ref[...], vo_ref[...] = _adamw_math(w_ref[...], g_ref[...], m_ref[...], v_ref[...])

    return pl.pallas_call(body, name="adamw_small", out_shape=[_sds(w.shape)] * 3)(w, m, v, g)


def _pack_rows(arrs):
    flat = jnp.concatenate([a.reshape(-1) for a in arrs])
    pad = (-flat.shape[0]) % (8 * LANES)
    return jnp.pad(flat, (0, pad)).reshape(-1, LANES)


def _unpack_rows(rows, shapes, lead=()):
    flat = rows.reshape(lead + (-1,))
    out, at = [], 0
    for shp in shapes:
        size = int(np.prod(shp))
        out.append(flat[..., at:at + size].reshape(lead + tuple(shp)))
        at += size
    return out


WEIGHT_NAMES = ('ffn1_norm', 'ffn1_w_gate', 'ffn1_w_up', 'ffn1_w_down', 'mix_norm', 'pool_w', 'pool_scale',
                'mla_w_in', 'mla_q_norm', 'mla_w_q_up', 'mla_kv_norm', 'mla_w_kv_up', 'mla_q_head_norm',
                'mla_k_head_norm', 'mla_w_out', 'ffn2_norm', 'ffn2_w_gate', 'ffn2_w_up', 'ffn2_w_down')


def _chips_to_columns(g):
    return jnp.transpose(g, (1, 2, 0, 3)).reshape(g.shape[1], g.shape[2], -1)


def _columns_to_chips(full):
    n, r, c4 = full.shape
    return jnp.transpose(full.reshape(n, r, N_CHIPS, c4 // N_CHIPS), (2, 0, 1, 3))


def kernel(x, positions, ffn1_norm, ffn1_w_gate, ffn1_w_up, ffn1_w_down, mix_norm, pool_w, pool_scale, mla_w_in, mla_q_norm, mla_w_q_up, mla_kv_norm, mla_w_kv_up, mla_q_head_norm, mla_k_head_norm, mla_w_out, ffn2_norm, ffn2_w_gate, ffn2_w_up, ffn2_w_down, loss_target, m_ffn1_norm, m_ffn1_w_gate, m_ffn1_w_up, m_ffn1_w_down, m_mix_norm, m_pool_w, m_pool_scale, m_mla_w_in, m_mla_q_norm, m_mla_w_q_up, m_mla_kv_norm, m_mla_w_kv_up, m_mla_q_head_norm, m_mla_k_head_norm, m_mla_w_out, m_ffn2_norm, m_ffn2_w_gate, m_ffn2_w_up, m_ffn2_w_down, v_ffn1_norm, v_ffn1_w_gate, v_ffn1_w_up, v_ffn1_w_down, v_mix_norm, v_pool_w, v_pool_scale, v_mla_w_in, v_mla_q_norm, v_mla_w_q_up, v_mla_kv_norm, v_mla_w_kv_up, v_mla_q_head_norm, v_mla_k_head_norm, v_mla_w_out, v_ffn2_norm, v_ffn2_w_gate, v_ffn2_w_up, v_ffn2_w_down):
    env = dict(locals())
    w = {n: env[n] for n in WEIGHT_NAMES}
    mom = {n: env["m_" + n] for n in WEIGHT_NAMES}
    var = {n: env["v_" + n] for n in WEIGHT_NAMES}

    s, d = x.shape[1], x.shape[2]
    depth = ffn1_norm.shape[0]
    n_mla, n_pool, n_groups = mla_w_in.shape[0], pool_w.shape[0], pool_w.shape[1]
    pool_c = pool_w.shape[3]
    q_lora = N_CHIPS * mla_q_norm.shape[1]
    kv_lora = N_CHIPS * mla_kv_norm.shape[1]
    n_heads = N_CHIPS * mla_w_q_up.shape[2] // QK_HEAD
    t_fwd, t_bwd = _tile(s, ATTN_FWD_TILE), _tile(s, ATTN_BWD_TILE)
    cx, cy, cc = _mesh_position()
    chip = 2 * cx + cy
    chip_arr = jnp.reshape(chip, (1,)).astype(jnp.int32)
    core_arr = jnp.reshape(cc, (1,)).astype(jnp.int32)

    shard_gu = _cast(jnp.swapaxes(jnp.concatenate([ffn1_w_gate, ffn1_w_up, ffn2_w_gate, ffn2_w_up], axis=0), 1, 2))
    shard_dn = _cast(jnp.concatenate([ffn1_w_down, ffn2_w_down], axis=0))
    shard_pool = _cast(pool_w.reshape((n_pool * n_groups,) + pool_w.shape[2:]))
    w_gu, w_dn, g_in, g_qup, g_kvup, g_out, g_pool = _gather_over_chips(
        [shard_gu, shard_dn, _cast(mla_w_in), _cast(mla_w_q_up), _cast(mla_w_kv_up), _cast(mla_w_out), shard_pool])
    small_shapes = [mla_q_norm.shape, mla_kv_norm.shape]
    small = _gather_over_devices(_pack_rows([mla_q_norm, mla_kv_norm]))[::2]
    qn_chips, kvn_chips = _unpack_rows(small, small_shapes, lead=(N_CHIPS,))
    q_norm_full = jnp.transpose(qn_chips, (1, 0, 2)).reshape(n_mla, 1, q_lora)
    kv_norm_full = jnp.transpose(kvn_chips, (1, 0, 2)).reshape(n_mla, 1, kv_lora)

    w_in_full = _chips_to_columns(g_in)
    w_q_heads = jnp.transpose(_chips_to_columns(g_qup).reshape(n_mla, q_lora, n_heads, QK_HEAD), (0, 2, 1, 3))
    w_kv = _chips_to_columns(g_kvup).reshape(n_mla, kv_lora, n_heads, QK_NOPE + V_HEAD)
    w_kn_heads = jnp.transpose(w_kv[..., :QK_NOPE], (0, 2, 1, 3))
    w_v_full = w_kv[..., QK_NOPE:].reshape(n_mla, kv_lora, n_heads * V_HEAD)
    w_out_full = jnp.transpose(g_out, (1, 0, 2, 3)).reshape(n_mla, n_heads * V_HEAD, d)
    pool_full = jnp.transpose(g_pool.reshape(N_CHIPS, n_pool, n_groups, pool_c // N_CHIPS, pool_c),
                              (1, 2, 0, 3, 4)).reshape(n_pool, n_groups, pool_c, pool_c)

    inv_freq = (1.0 / (ROPE_THETA ** (jnp.arange(0, QK_ROPE, 2, dtype=F32) / QK_ROPE))).reshape(1, -1)
    cos_t, sin_t = _rope_tables(positions.reshape(s, 1), inv_freq)

    row = lambda a, i: a[i].reshape(1, -1)
    i_gate1, i_up1, i_gate2, i_up2 = (lambda i: i), (lambda i: depth + i), (lambda i: 2 * depth + i), (lambda i: 3 * depth + i)
    i_dn1, i_dn2 = (lambda i: i), (lambda i: depth + i)

    h = x.reshape(s, d)
    saved = []
    for i in range(depth):
        rec = {"x_ffn1": h}
        h, *rec["ffn1"] = _ffn_fwd(h, row(ffn1_norm, i), w_gu, w_dn, i_gate1(i), i_up1(i), i_dn1(i))
        rec["x_mix"] = h
        j = i // 2
        if i % 2 == 0:
            h = _pool_fwd(h, row(mix_norm, i), pool_full[j], row(pool_scale, j))
        else:
            lat, q, k, v = _mla_qkv_fwd(h, row(mix_norm, i), w_in_full[j], q_norm_full[j], kv_norm_full[j],
                                        w_q_heads[j], w_kn_heads[j], w_v_full[j], row(mla_q_head_norm, j),
                                        row(mla_k_head_norm, j), cos_t, sin_t)
            vt = jnp.transpose(v.reshape(s // t_fwd, t_fwd, n_heads, V_HEAD), (2, 0, 3, 1))
            vt = jnp.concatenate([vt, jnp.ones((n_heads, s // t_fwd, ONES_ROWS, t_fwd), vt.dtype)], axis=2)
            ot, lse = _flash_fwd(q, k, vt)
            rec.update(lat=lat, q=q, k=k, v=v, ot=ot, lse=lse)
            h = _mla_out_fwd(h, ot, w_out_full[j])
        rec["x_ffn2"] = h
        h, *rec["ffn2"] = _ffn_fwd(h, row(ffn2_norm, i), w_gu, w_dn, i_gate2(i), i_up2(i), i_dn2(i))
        saved.append(rec)

    loss_part, dy = _loss_and_grad(h, loss_target.reshape(s, d))
    loss = lax.psum(loss_part[0, 0], ("x", "y", "c"))

    g_gu = [None] * (4 * depth)
    g_dn = [None] * (2 * depth)
    g_norm = {n: [None] * depth for n in ("ffn1_norm", "mix_norm", "ffn2_norm")}
    g_pool_w, g_pool_scale = [None] * n_pool, [None] * n_pool
    g_mla = {n: [None] * n_mla for n in ("w_in", "q_norm", "kv_norm", "w_q", "w_kv", "qhn", "khn", "w_out")}
    for i in reversed(range(depth)):
        rec = saved[i]
        hb, gate, up = rec["ffn2"]
        dy, g_norm["ffn2_norm"][i], dyb, dgt, dup, act = _ffn_bwd_dgrad(
            rec["x_ffn2"], row(ffn2_norm, i), dy, gate, up, w_gu, w_dn, i_gate2(i), i_up2(i), i_dn2(i))
        g_gu[i_gate2(i)], g_gu[i_up2(i)], g_dn[i_dn2(i)] = _ffn_wgrad(hb, dyb, dgt, dup, act)
        j = i // 2
        if i % 2 == 0:
            dy, g_norm["mix_norm"][i], g_pool_w[j], g_pool_scale[j] = _pool_bwd(
                rec["x_mix"], row(mix_norm, i), pool_full[j], row(pool_scale, j), dy)
        else:
            do, delta, g_mla["w_out"][j] = _mla_out_bwd(dy, rec["ot"], w_out_full[j])
            by_tile = lambda a: a.reshape(n_heads, s // t_bwd, 1, t_bwd)
            dqt, dk, dv = _flash_bwd(rec["q"], rec["k"], jnp.transpose(rec["k"], (0, 2, 1)), rec["v"], do,
                                     by_tile(rec["lse"]), by_tile(delta))
            dq = jnp.transpose(dqt, (0, 1, 3, 2)).reshape(n_heads, s, QK_HEAD)
            (dy, g_norm["mix_norm"][i], g_mla["w_in"][j], g_mla["q_norm"][j], g_mla["kv_norm"][j], dwq, dwkn, dwv,
             g_mla["qhn"][j], g_mla["khn"][j]) = _mla_qkv_bwd(
                rec["x_mix"], rec["lat"], dy, dq, dk, dv, row(mix_norm, i), w_in_full[j], q_norm_full[j],
                kv_norm_full[j], w_q_heads[j], w_kn_heads[j], w_v_full[j], row(mla_q_head_norm, j),
                row(mla_k_head_norm, j), cos_t, sin_t)
            g_mla["w_q"][j] = jnp.transpose(dwq, (1, 0, 2)).reshape(q_lora, n_heads * QK_HEAD)
            g_mla["w_kv"][j] = jnp.concatenate(
                [jnp.transpose(dwkn, (1, 0, 2)), dwv.reshape(kv_lora, n_heads, V_HEAD)], axis=-1
            ).reshape(kv_lora, n_heads * (QK_NOPE + V_HEAD))
        hb, gate, up = rec["ffn1"]
        dy, g_norm["ffn1_norm"][i], dyb, dgt, dup, act = _ffn_bwd_dgrad(
            rec["x_ffn1"], row(ffn1_norm, i), dy, gate, up, w_gu, w_dn, i_gate1(i), i_up1(i), i_dn1(i))
        g_gu[i_gate1(i)], g_gu[i_up1(i)], g_dn[i_dn1(i)] = _ffn_wgrad(hb, dyb, dgt, dup, act)
    grad_x = dy.reshape(x.shape)

    full_grads = [
        jnp.stack(g_gu, axis=1),
        jnp.stack(g_dn, axis=1),
        _columns_to_chips(jnp.stack(g_mla["w_in"])),
        _columns_to_chips(jnp.stack(g_mla["w_q"])),
        _columns_to_chips(jnp.stack(g_mla["w_kv"])),
        jnp.transpose(jnp.stack(g_mla["w_out"]).reshape(n_mla, N_CHIPS, -1, d), (1, 0, 2, 3)),
        jnp.transpose(jnp.stack(g_pool_w).reshape(n_pool, n_groups, N_CHIPS, pool_c // N_CHIPS, pool_c),
                      (2, 0, 1, 3, 4)).reshape(N_CHIPS, n_pool * n_groups, pool_c // N_CHIPS, pool_c),
    ]
    from_sibling = _send_other_half_to_sibling(full_grads)
    chip_sums = [_add_sibling_half(g, r, chip_arr, core_arr) for g, r in zip(full_grads, from_sibling)]
    from_chips = _send_blocks_to_chips([wire for wire, _ in chip_sums])
    half_sums = [_add_chip_blocks(own, r, core_arr) for (_, own), r in zip(chip_sums, from_chips)]
    r_gu, r_dn, r_in, r_qup, r_kvup, r_out, r_pool = _join_halves_with_sibling(half_sums)

    small_grads = [jnp.concatenate(g_norm["ffn1_norm"]), jnp.concatenate(g_norm["mix_norm"]),
                   jnp.concatenate(g_norm["ffn2_norm"]), jnp.concatenate(g_pool_scale),
                   jnp.concatenate(g_mla["qhn"]), jnp.concatenate(g_mla["khn"]),
                   jnp.concatenate(g_mla["q_norm"]), jnp.concatenate(g_mla["kv_norm"])]
    small_sum = _sum_over_devices(_gather_over_devices(_pack_rows(small_grads)))
    (s_ffn1, s_mix, s_ffn2, s_pscale, s_qhn, s_khn, s_qn, s_kvn) = _unpack_rows(small_sum, [g.shape for g in small_grads])
    qn_w, kvn_w = mla_q_norm.shape[1], mla_kv_norm.shape[1]
    s_qn = lax.dynamic_slice_in_dim(s_qn, chip * qn_w, qn_w, axis=1)
    s_kvn = lax.dynamic_slice_in_dim(s_kvn, chip * kvn_w, kvn_w, axis=1)

    grads, deltas, new_m, new_v = {}, {}, {}, {}

    def stacked(name, reduced, offset, transposed=False):
        shape = w[name].shape
        if transposed:
            as3, back = (lambda a: jnp.swapaxes(a, 1, 2)), (lambda a: jnp.swapaxes(a, 1, 2))
        else:
            as3, back = (lambda a: a.reshape((-1,) + shape[-2:])), (lambda a: a.reshape(shape))
        out = _adamw_stacked(as3(w[name]), as3(mom[name]), as3(var[name]), reduced, offset)
        grads[name], deltas[name], new_m[name], new_v[name] = [back(o) for o in out]

    def small_update(name, g):
        grads[name] = g
        deltas[name], new_m[name], new_v[name] = _adamw_small(w[name], mom[name], var[name], g)

    stacked("ffn1_w_gate", r_gu, 0, transposed=True)
    stacked("ffn1_w_up", r_gu, depth, transposed=True)
    stacked("ffn2_w_gate", r_gu, 2 * depth, transposed=True)
    stacked("ffn2_w_up", r_gu, 3 * depth, transposed=True)
    stacked("ffn1_w_down", r_dn, 0)
    stacked("ffn2_w_down", r_dn, depth)
    stacked("mla_w_in", r_in, 0)
    stacked("mla_w_q_up", r_qup, 0)
    stacked("mla_w_kv_up", r_kvup, 0)
    stacked("mla_w_out", r_out, 0)
    stacked("pool_w", r_pool, 0)
    small_update("ffn1_norm", s_ffn1)
    small_update("mix_norm", s_mix)
    small_update("ffn2_norm", s_ffn2)
    small_update("pool_scale", s_pscale)
    small_update("mla_q_head_norm", s_qhn)
    small_update("mla_k_head_norm", s_khn)
    small_update("mla_q_norm", s_qn)
    small_update("mla_kv_norm", s_kvn)

    return (loss, grad_x, *[grads[n] for n in WEIGHT_NAMES], *[deltas[n] for n in WEIGHT_NAMES],
            *[new_m[n] for n in WEIGHT_NAMES], *[new_v[n] for n in WEIGHT_NAMES])
```

```python
import functools

import numpy as np

import jax
import jax.numpy as jnp
from jax import lax
from jax.experimental import pallas as pl
from jax.experimental.pallas import tpu as pltpu

F32 = jnp.float32
MXU_DTYPE = jnp.bfloat16
WIRE_DTYPE = jnp.bfloat16
MESH = pl.DeviceIdType.MESH
N_CHIPS = 4
N_DEVICES = 8
LANES = 128
VMEM_LIMIT_BYTES = 56 * 2**20
NORM_EPS = 1e-6
QK_NOPE, QK_ROPE, V_HEAD = 128, 64, 128
QK_HEAD = QK_NOPE + QK_ROPE
SCORE_SCALE = QK_HEAD ** -0.5
ONES_ROWS = 8
ROPE_THETA = 10000.0
POOL_WINDOWS = (2, 4, 8, 16)
POOL_HALO = 16
FFN_HALF = 0.5
ADAM_LR, ADAM_B1, ADAM_B2, ADAM_EPS, ADAM_WD, ADAM_STEP = 0.001, 0.9, 0.999, 1e-08, 0.01, 10
FFN_TILE = 512
FFN_BWD_TILE = 256
WGRAD_TILE = 2048
MLA_TILE = 256
POOL_TILE = 512
ATTN_FWD_TILE = 1024
ATTN_BWD_TILE = 1024
ROW_TILE = 1024


def _cast(v):
    return v.astype(MXU_DTYPE)


def _mm(a, b):
    return jnp.dot(a, b, preferred_element_type=F32)


def _mm_nt(a, b):
    return lax.dot_general(a, b, (((1,), (1,)), ((), ())), preferred_element_type=F32)


def _mm_tn(a, b):
    return lax.dot_general(a, b, (((0,), (0,)), ((), ())), preferred_element_type=F32)


def _rms_fwd(v, gain):
    r = lax.rsqrt(jnp.mean(v * v, axis=-1, keepdims=True) + NORM_EPS)
    return v * r * gain, r


def _rms_bwd(v, r, gain, dy):
    vr = v * r
    gy = dy * gain
    dv = r * (gy - vr * jnp.mean(gy * vr, axis=-1, keepdims=True))
    return dv, jnp.sum(dy * vr, axis=0, keepdims=True)


def _params(semantics=None):
    return pltpu.CompilerParams(dimension_semantics=semantics, vmem_limit_bytes=VMEM_LIMIT_BYTES)


def _tile(n, want):
    t = min(n, want)
    assert n % t == 0, (n, want)
    return t


def _full(shape):
    nd = len(shape)
    return pl.BlockSpec(shape, lambda *_: (0,) * nd)


def _sds(shape, dtype=F32):
    return jax.ShapeDtypeStruct(shape, dtype)


def _ffn_fwd(x, gain, w_gu, w_dn, i_gate, i_up, i_down):
    s, d = x.shape
    fs = w_gu.shape[-2]
    tm = _tile(s, FFN_TILE)

    def body(x_ref, g_ref, wg_ref, wu_ref, wd_ref, y_ref, hb_ref, gate_ref, up_ref):
        h, _ = _rms_fwd(x_ref[...], g_ref[...])
        hb = _cast(h)
        hb_ref[...] = hb
        pre = [(_mm_nt(hb, wg_ref[c]), _mm_nt(hb, wu_ref[c])) for c in range(N_CHIPS)]
        out = None
        for c, (g, u) in enumerate(pre):
            gate_ref[c] = _cast(g)
            up_ref[c] = _cast(u)
            part = _mm(_cast((g * jax.nn.sigmoid(g)) * u), wd_ref[c])
            out = part if out is None else out + part
        y_ref[...] = x_ref[...] + FFN_HALF * out

    resident = dict(pipeline_mode=pl.Buffered(1))
    tok = pl.BlockSpec((tm, d), lambda i: (i, 0))
    chunks = pl.BlockSpec((N_CHIPS, tm, fs), lambda i: (0, i, 0))
    return pl.pallas_call(
        body, name="ffn_fwd", grid=(s // tm,),
        in_specs=[
            tok, _full((1, d)),
            pl.BlockSpec((N_CHIPS, None, fs, d), lambda i: (0, i_gate, 0, 0), **resident),
            pl.BlockSpec((N_CHIPS, None, fs, d), lambda i: (0, i_up, 0, 0), **resident),
            pl.BlockSpec((N_CHIPS, None, fs, d), lambda i: (0, i_down, 0, 0), **resident),
        ],
        out_specs=[tok, tok, chunks, chunks],
        out_shape=[_sds((s, d)), _sds((s, d), MXU_DTYPE), _sds((N_CHIPS, s, fs), MXU_DTYPE),
                   _sds((N_CHIPS, s, fs), MXU_DTYPE)],
        compiler_params=_params(("arbitrary",)),
    )(x, gain, w_gu, w_gu, w_dn)


def _ffn_bwd_dgrad(x, gain, dy, gate, up, w_gu, w_dn, i_gate, i_up, i_down):
    s, d = x.shape
    fs = w_gu.shape[-2]
    tm = _tile(s, FFN_BWD_TILE)

    def body(x_ref, g_ref, dy_ref, gate_ref, up_ref, wg_ref, wu_ref, wd_ref,
             dx_ref, dgain_ref, dyb_ref, dg_ref, du_ref, act_ref):
        i = pl.program_id(0)
        dyb = _cast(dy_ref[...])
        dyb_ref[...] = dyb
        dacts = [_mm_nt(dyb, wd_ref[c]) for c in range(N_CHIPS)]
        dh = None
        for c in range(N_CHIPS):
            g = gate_ref[c].astype(F32)
            u = up_ref[c].astype(F32)
            sg = jax.nn.sigmoid(g)
            silu = g * sg
            dact = FFN_HALF * dacts[c]
            dgb = _cast(dact * u * (sg * (1.0 + g * (1.0 - sg))))
            dub = _cast(dact * silu)
            dg_ref[c] = dgb
            du_ref[c] = dub
            act_ref[c] = _cast(silu * u)
            part = _mm(dgb, wg_ref[c]) + _mm(dub, wu_ref[c])
            dh = part if dh is None else dh + part
        _, r = _rms_fwd(x_ref[...], g_ref[...])
        dxn, dgn = _rms_bwd(x_ref[...], r, g_ref[...], dh)
        dx_ref[...] = dy_ref[...] + dxn

        @pl.when(i == 0)
        def _():
            dgain_ref[...] = dgn

        @pl.when(i > 0)
        def _():
            dgain_ref[...] += dgn

    resident = dict(pipeline_mode=pl.Buffered(1))
    tok = pl.BlockSpec((tm, d), lambda i: (i, 0))
    chunks = pl.BlockSpec((N_CHIPS, tm, fs), lambda i: (0, i, 0))
    return pl.pallas_call(
        body, name="ffn_bwd_dgrad", grid=(s // tm,),
        in_specs=[
            tok, _full((1, d)), tok, chunks, chunks,
            pl.BlockSpec((N_CHIPS, None, fs, d), lambda i: (0, i_gate, 0, 0), **resident),
            pl.BlockSpec((N_CHIPS, None, fs, d), lambda i: (0, i_up, 0, 0), **resident),
            pl.BlockSpec((N_CHIPS, None, fs, d), lambda i: (0, i_down, 0, 0), **resident),
        ],
        out_specs=[tok, _full((1, d)), tok, chunks, chunks, chunks],
        out_shape=[_sds((s, d)), _sds((1, d)), _sds((s, d), MXU_DTYPE),
                   _sds((N_CHIPS, s, fs), MXU_DTYPE), _sds((N_CHIPS, s, fs), MXU_DTYPE),
                   _sds((N_CHIPS, s, fs), MXU_DTYPE)],
        compiler_params=_params(("arbitrary",)),
    )(x, gain, dy, gate, up, w_gu, w_gu, w_dn)


def _ffn_wgrad(hb, dyb, dg, du, act):
    s, d = hb.shape
    fs = dg.shape[-1]
    tk = _tile(s, WGRAD_TILE)
    n_k = s // tk

    def body(h_ref, dy_ref, dg_ref, du_ref, act_ref, wg_ref, wu_ref, wd_ref):
        k = pl.program_id(1)

        @pl.when(k == 0)
        def _():
            wg_ref[...] = jnp.zeros_like(wg_ref)
            wu_ref[...] = jnp.zeros_like(wu_ref)
            wd_ref[...] = jnp.zeros_like(wd_ref)

        h = h_ref[...]
        wg_ref[...] += _mm_tn(dg_ref[...], h)
        wu_ref[...] += _mm_tn(du_ref[...], h)
        wd_ref[...] += FFN_HALF * _mm_tn(act_ref[...], dy_ref[...])

    tok = pl.BlockSpec((tk, d), lambda j, k: (k, 0))
    chunk = pl.BlockSpec((None, tk, fs), lambda j, k: (j, k, 0))
    return pl.pallas_call(
        body, name="ffn_wgrad", grid=(N_CHIPS, n_k),
        in_specs=[tok, tok, chunk, chunk, chunk],
        out_specs=[pl.BlockSpec((None, fs, d), lambda j, k: (j, 0, 0)),
                   pl.BlockSpec((None, fs, d), lambda j, k: (j, 0, 0)),
                   pl.BlockSpec((None, fs, d), lambda j, k: (j, 0, 0))],
        out_shape=[_sds((N_CHIPS, fs, d))] * 3,
        compiler_params=_params(("arbitrary", "arbitrary")),
    )(hb, dyb, dg, du, act)


def _inv_count(first_row, n_rows, window):
    t = first_row + lax.broadcasted_iota(jnp.int32, (n_rows, 1), 0)
    return 1.0 / jnp.minimum(t + 1, window).astype(F32)


def _trailing_sum(v, window):
    k = 1
    while k < window:
        v = v + pltpu.roll(v, k, 0)
        k *= 2
    return v


def _leading_sum(v, window):
    n = v.shape[0]
    k = 1
    while k < window:
        v = v + pltpu.roll(v, n - k, 0)
        k *= 2
    return v


def _pool_normed_rows(x_ref, prev_ref, g_ref, i):
    h, r = _rms_fwd(x_ref[...], g_ref[...])
    hp, _ = _rms_fwd(prev_ref[...], g_ref[...])
    hp = jnp.where(i > 0, hp, 0.0)
    return jnp.concatenate([hp, h], axis=0), r


def _pooled_group(he, g, pg, first_row, tm):
    ue = he[:, g * pg:(g + 1) * pg]
    win = _trailing_sum(ue, POOL_WINDOWS[g])[POOL_HALO:]
    return win * _inv_count(first_row, tm, POOL_WINDOWS[g]) - ue[POOL_HALO:]


def _pool_specs(s, d, tm):
    per = tm // POOL_HALO
    last = s // POOL_HALO - 1
    tok = pl.BlockSpec((tm, d), lambda i: (i, 0))
    prev = pl.BlockSpec((POOL_HALO, d), lambda i: (jnp.maximum(i * per - 1, 0), 0))
    nxt = pl.BlockSpec((POOL_HALO, d), lambda i: (jnp.minimum((i + 1) * per, last), 0))
    return tok, prev, nxt


def _pool_fwd(x, gain, w, scale):
    s, d = x.shape
    n_g, pg = w.shape[0], w.shape[-1]
    tm = _tile(s, POOL_TILE)
    tok, prev, _ = _pool_specs(s, d, tm)

    def body(x_ref, prev_ref, g_ref, w_ref, sc_ref, y_ref):
        i = pl.program_id(0)
        he, _ = _pool_normed_rows(x_ref, prev_ref, g_ref, i)
        z = [_mm(_cast(_pooled_group(he, g, pg, i * tm, tm)), w_ref[g]) for g in range(n_g)]
        y_ref[...] = x_ref[...] + jnp.concatenate(z, axis=-1) * sc_ref[...]

    return pl.pallas_call(
        body, name="pool_fwd", grid=(s // tm,),
        in_specs=[tok, prev, _full((1, d)), _full(w.shape), _full((1, d))],
        out_specs=tok, out_shape=_sds((s, d)),
        compiler_params=_params(("arbitrary",)),
    )(x, x, gain, w, scale)


def _pool_bwd(x, gain, w, scale, dy):
    s, d = x.shape
    n_g, pg = w.shape[0], w.shape[-1]
    tm = _tile(s, POOL_TILE)
    n_tiles = s // tm
    tok, prev, nxt = _pool_specs(s, d, tm)

    def body(x_ref, prev_ref, dy_ref, next_ref, g_ref, w_ref, sc_ref, dx_ref, dgain_ref, dw_ref, dsc_ref):
        i = pl.program_id(0)

        @pl.when(i == 0)
        def _():
            dgain_ref[...] = jnp.zeros_like(dgain_ref)
            dw_ref[...] = jnp.zeros_like(dw_ref)
            dsc_ref[...] = jnp.zeros_like(dsc_ref)

        he, r = _pool_normed_rows(x_ref, prev_ref, g_ref, i)
        dy = dy_ref[...]
        dyn = jnp.where(i < n_tiles - 1, next_ref[...], 0.0)
        dze = jnp.concatenate([dy, dyn], axis=0) * sc_ref[...]
        dh, dsc = [], []
        for g in range(n_g):
            cols = slice(g * pg, (g + 1) * pg)
            pooled = _cast(_pooled_group(he, g, pg, i * tm, tm))
            dsc.append(jnp.sum(dy[:, cols] * _mm(pooled, w_ref[g]), axis=0, keepdims=True))
            dzb = _cast(dze[:, cols])
            dw_ref[g] += _mm_tn(pooled, dzb[:tm])
            dpool = _mm_nt(dzb, w_ref[g])
            spread = _leading_sum(dpool * _inv_count(i * tm, tm + POOL_HALO, POOL_WINDOWS[g]), POOL_WINDOWS[g])
            dh.append(spread[:tm] - dpool[:tm])
        dsc_ref[...] += jnp.concatenate(dsc, axis=-1)
        dxn, dgn = _rms_bwd(x_ref[...], r, g_ref[...], jnp.concatenate(dh, axis=-1))
        dgain_ref[...] += dgn
        dx_ref[...] = dy + dxn

    return pl.pallas_call(
        body, name="pool_bwd", grid=(n_tiles,),
        in_specs=[tok, prev, tok, nxt, _full((1, d)), _full(w.shape), _full((1, d))],
        out_specs=[tok, _full((1, d)), _full(w.shape), _full((1, d))],
        out_shape=[_sds((s, d)), _sds((1, d)), _sds(w.shape), _sds((1, d))],
        compiler_params=_params(("arbitrary",)),
    )(x, x, dy, dy, gain, w, scale)


def _rope_tables(pos_col, inv_freq):
    s = pos_col.shape[0]
    tm = _tile(s, ROW_TILE)
    half = QK_ROPE // 2

    def body(p_ref, f_ref, c_ref, s_ref):
        ang = p_ref[...].astype(F32) * f_ref[...]
        cos, sin = jnp.cos(ang), jnp.sin(ang)
        c_ref[...] = jnp.concatenate([jnp.ones((tm, QK_NOPE), F32), cos, cos], axis=-1)
        s_ref[...] = jnp.concatenate([jnp.zeros((tm, QK_NOPE), F32), -sin, sin], axis=-1)

    tab = pl.BlockSpec((tm, QK_HEAD), lambda i: (i, 0))
    return pl.pallas_call(
        body, name="rope_tables", grid=(s // tm,),
        in_specs=[pl.BlockSpec((tm, 1), lambda i: (i, 0)), _full((1, half))],
        out_specs=[tab, tab], out_shape=[_sds((s, QK_HEAD)), _sds((s, QK_HEAD))],
        compiler_params=_params(("arbitrary",)),
    )(pos_col, inv_freq)


def _swap_rope_halves(v):
    half = QK_ROPE // 2
    return jnp.concatenate([v[:, :QK_NOPE], v[:, QK_NOPE + half:], v[:, QK_NOPE:QK_NOPE + half]], axis=-1)


def _rope(v, cos, sin):
    return v * cos + _swap_rope_halves(v) * sin


def _rope_transposed(dv, cos, sin):
    return dv * cos + _swap_rope_halves(dv * sin)


def _mla_qkv_fwd(x, gain, w_in, q_norm, kv_norm, w_q, w_kn, w_v, q_head_norm, k_head_norm, cos, sin):
    s, d = x.shape
    n_h, ql = w_q.shape[0], w_q.shape[1]
    kvl, lat_w = w_kn.shape[1], w_in.shape[1]
    tm = _tile(s, MLA_TILE)

    def body(x_ref, g_ref, win_ref, qn_ref, kvn_ref, wq_ref, wkn_ref, wv_ref, qhn_ref, khn_ref, c_ref, s_ref,
             lat_ref, q_ref, k_ref, v_ref):
        h, _ = _rms_fwd(x_ref[...], g_ref[...])
        lat = _mm(_cast(h), win_ref[...])
        lat_ref[...] = lat
        cqn, _ = _rms_fwd(lat[:, :ql], qn_ref[...])
        ckvn, _ = _rms_fwd(lat[:, ql:ql + kvl], kvn_ref[...])
        kpe = lat[:, ql + kvl:]
        cqb, ckb = _cast(cqn), _cast(ckvn)
        cos_t, sin_t = c_ref[...], s_ref[...]
        v_ref[...] = _cast(_mm(ckb, wv_ref[...]))
        for hh in range(n_h):
            qn, _ = _rms_fwd(_mm(cqb, wq_ref[hh]), qhn_ref[...])
            q_ref[hh] = _cast(_rope(qn, cos_t, sin_t) * SCORE_SCALE)
            kn, _ = _rms_fwd(jnp.concatenate([_mm(ckb, wkn_ref[hh]), kpe], axis=-1), khn_ref[...])
            k_ref[hh] = _cast(_rope(kn, cos_t, sin_t))

    tok = lambda w: pl.BlockSpec((tm, w), lambda i: (i, 0))
    heads = pl.BlockSpec((n_h, tm, QK_HEAD), lambda i: (0, i, 0))
    return pl.pallas_call(
        body, name="mla_qkv_fwd", grid=(s // tm,),
        in_specs=[tok(d), _full((1, d)), _full(w_in.shape), _full((1, ql)), _full((1, kvl)), _full(w_q.shape),
                  _full(w_kn.shape), _full(w_v.shape), _full((1, QK_HEAD)), _full((1, QK_HEAD)),
                  tok(QK_HEAD), tok(QK_HEAD)],
        out_specs=[tok(lat_w), heads, heads, tok(n_h * V_HEAD)],
        out_shape=[_sds((s, lat_w)), _sds((n_h, s, QK_HEAD), MXU_DTYPE), _sds((n_h, s, QK_HEAD), MXU_DTYPE),
                   _sds((s, n_h * V_HEAD), MXU_DTYPE)],
        compiler_params=_params(("arbitrary",)),
    )(x, gain, w_in, q_norm, kv_norm, w_q, w_kn, w_v, q_head_norm, k_head_norm, cos, sin)


def _mla_qkv_bwd(x, lat, dy, dq, dk, dv, gain, w_in, q_norm, kv_norm, w_q, w_kn, w_v, q_head_norm, k_head_norm,
                 cos, sin):
    s, d = x.shape
    n_h, ql = w_q.shape[0], w_q.shape[1]
    kvl, lat_w = w_kn.shape[1], w_in.shape[1]
    tm = _tile(s, MLA_TILE)

    def body(x_ref, lat_ref, dy_ref, dq_ref, dk_ref, dv_ref, g_ref, win_ref, qn_ref, kvn_ref, wq_ref, wkn_ref,
             wv_ref, qhn_ref, khn_ref, c_ref, s_ref,
             dx_ref, dg_ref, dwin_ref, dqn_ref, dkvn_ref, dwq_ref, dwkn_ref, dwv_ref, dqhn_ref, dkhn_ref):
        @pl.when(pl.program_id(0) == 0)
        def _():
            for ref in (dg_ref, dwin_ref, dqn_ref, dkvn_ref, dwq_ref, dwkn_ref, dwv_ref, dqhn_ref, dkhn_ref):
                ref[...] = jnp.zeros_like(ref)

        x_t = x_ref[...]
        h, r = _rms_fwd(x_t, g_ref[...])
        hb = _cast(h)
        lat = lat_ref[...]
        cq, ckv, kpe = lat[:, :ql], lat[:, ql:ql + kvl], lat[:, ql + kvl:]
        cqn, rq = _rms_fwd(cq, qn_ref[...])
        ckvn, rkv = _rms_fwd(ckv, kvn_ref[...])
        cqb, ckb = _cast(cqn), _cast(ckvn)
        cos_t, sin_t = c_ref[...], s_ref[...]

        dvb = _cast(dv_ref[...])
        dwv_ref[...] += _mm_tn(ckb, dvb)
        dckvn = _mm_nt(dvb, wv_ref[...])
        dcqn = jnp.zeros((tm, ql), F32)
        dkpe = jnp.zeros((tm, QK_ROPE), F32)
        dqhn = jnp.zeros((1, QK_HEAD), F32)
        dkhn = jnp.zeros((1, QK_HEAD), F32)
        for hh in range(n_h):
            qp = _mm(cqb, wq_ref[hh])
            _, rqp = _rms_fwd(qp, qhn_ref[...])
            dqp, dgq = _rms_bwd(qp, rqp, qhn_ref[...], _rope_transposed(dq_ref[hh] * SCORE_SCALE, cos_t, sin_t))
            dqhn += dgq
            dqpb = _cast(dqp)
            dwq_ref[hh] += _mm_tn(cqb, dqpb)
            dcqn += _mm_nt(dqpb, wq_ref[hh])

            kp = jnp.concatenate([_mm(ckb, wkn_ref[hh]), kpe], axis=-1)
            _, rkp = _rms_fwd(kp, khn_ref[...])
            dkp, dgk = _rms_bwd(kp, rkp, khn_ref[...], _rope_transposed(dk_ref[hh], cos_t, sin_t))
            dkhn += dgk
            dknb = _cast(dkp[:, :QK_NOPE])
            dkpe += dkp[:, QK_NOPE:]
            dwkn_ref[hh] += _mm_tn(ckb, dknb)
            dckvn += _mm_nt(dknb, wkn_ref[hh])
        dqhn_ref[...] += dqhn
        dkhn_ref[...] += dkhn

        dcq, dgn = _rms_bwd(cq, rq, qn_ref[...], dcqn)
        dqn_ref[...] += dgn
        dckv, dgn = _rms_bwd(ckv, rkv, kvn_ref[...], dckvn)
        dkvn_ref[...] += dgn
        dlb = _cast(jnp.concatenate([dcq, dckv, dkpe], axis=-1))
        dwin_ref[...] += _mm_tn(hb, dlb)
        dxn, dgn = _rms_bwd(x_t, r, g_ref[...], _mm_nt(dlb, win_ref[...]))
        dg_ref[...] += dgn
        dx_ref[...] = dy_ref[...] + dxn

    tok = lambda w: pl.BlockSpec((tm, w), lambda i: (i, 0))
    heads = pl.BlockSpec((n_h, tm, QK_HEAD), lambda i: (0, i, 0))
    return pl.pallas_call(
        body, name="mla_qkv_bwd", grid=(s // tm,),
        in_specs=[tok(d), tok(lat_w), tok(d), heads, heads, tok(n_h * V_HEAD), _full((1, d)), _full(w_in.shape),
                  _full((1, ql)), _full((1, kvl)), _full(w_q.shape), _full(w_kn.shape), _full(w_v.shape),
                  _full((1, QK_HEAD)), _full((1, QK_HEAD)), tok(QK_HEAD), tok(QK_HEAD)],
        out_specs=[tok(d), _full((1, d)), _full(w_in.shape), _full((1, ql)), _full((1, kvl)), _full(w_q.shape),
                   _full(w_kn.shape), _full(w_v.shape), _full((1, QK_HEAD)), _full((1, QK_HEAD))],
        out_shape=[_sds((s, d)), _sds((1, d)), _sds(w_in.shape), _sds((1, ql)), _sds((1, kvl)), _sds(w_q.shape),
                   _sds(w_kn.shape), _sds(w_v.shape), _sds((1, QK_HEAD)), _sds((1, QK_HEAD))],
        compiler_params=_params(("arbitrary",)),
    )(x, lat, dy, dq, dk, dv, gain, w_in, q_norm, kv_norm, w_q, w_kn, w_v, q_head_norm, k_head_norm, cos, sin)


def _scores_t(k_t, q_t):
    return _mm_nt(k_t, q_t)


def _mask_above_diagonal(z, t):
    key = lax.broadcasted_iota(jnp.int32, (t, t), 0)
    query = lax.broadcasted_iota(jnp.int32, (t, t), 1)
    return jnp.where(key <= query, z, -jnp.inf)


def _flash_fwd(q, k, vt):
    n_h, s, _ = q.shape
    t = _tile(s, ATTN_FWD_TILE)
    n = s // t
    rows_v = vt.shape[2]

    def body(q_ref, k_ref, vt_ref, ot_ref, lse_ref, m_sc, acc_sc, z_sc):
        i = pl.program_id(1)
        m_sc[...] = jnp.full_like(m_sc, -jnp.inf)
        acc_sc[...] = jnp.zeros_like(acc_sc)
        q_t = q_ref[...]

        def fetch(j, slot):
            z_sc[slot] = _scores_t(k_ref[pl.ds(pl.multiple_of(j * t, t), t), :], q_t)

        def stage(j, slot, masked, prefetch=True):
            if prefetch:
                fetch(j + 1, 1 - slot)
            z = _mask_above_diagonal(z_sc[slot], t) if masked else z_sc[slot]
            m_old = m_sc[...]
            m_new = jnp.maximum(m_old, jnp.max(z, axis=0, keepdims=True))
            alpha = jnp.exp(m_old - m_new)
            acc_sc[...] = alpha * acc_sc[...] + _mm(vt_ref[j], _cast(jnp.exp(z - m_new)))
            m_sc[...] = m_new

        def pair_below_diagonal(pair, carry):
            stage(2 * pair, 0, False)
            stage(2 * pair + 1, 1, False)
            return carry

        fetch(0, 0)
        lax.fori_loop(0, i >> 1, pair_below_diagonal, 0)

        @pl.when((i & 1) == 1)
        def _():
            stage(i - 1, 0, False)
            stage(i, 1, True, prefetch=False)

        @pl.when((i & 1) == 0)
        def _():
            stage(i, 0, True, prefetch=False)

        denom = acc_sc[V_HEAD:V_HEAD + 1, :]
        ot_ref[...] = acc_sc[:V_HEAD, :] / denom
        lse_ref[...] = m_sc[...] + jnp.log(denom)

    whole_head = dict(pipeline_mode=pl.Buffered(1))
    return pl.pallas_call(
        body, name="flash_fwd", grid=(n_h, n),
        in_specs=[pl.BlockSpec((None, t, QK_HEAD), lambda h, i: (h, i, 0)),
                  pl.BlockSpec((None, s, QK_HEAD), lambda h, i: (h, 0, 0), **whole_head),
                  pl.BlockSpec((None, n, rows_v, t), lambda h, i: (h, 0, 0, 0), **whole_head)],
        out_specs=[pl.BlockSpec((V_HEAD, t), lambda h, i: (h, i)),
                   pl.BlockSpec((None, 1, t), lambda h, i: (h, 0, i))],
        out_shape=[_sds((n_h * V_HEAD, s)), _sds((n_h, 1, s))],
        scratch_shapes=[pltpu.VMEM((1, t), F32), pltpu.VMEM((rows_v, t), F32), pltpu.VMEM((2, t, t), F32)],
        compiler_params=_params(("arbitrary", "arbitrary")),
    )(q, k, vt)


def _flash_bwd(q, k, v, do, lse, delta):
    n_h, s, _ = q.shape
    t = _tile(s, ATTN_BWD_TILE)
    n = s // t

    def body(q_ref, do_ref, lse_ref, dl_ref, k_ref, v_ref, dqt_ref, dk_ref, dv_ref, dk_sc, dv_sc, z_sc, dp_sc):
        j = pl.program_id(1)

        @pl.when(j == 0)
        def _():
            dqt_ref[...] = jnp.zeros_like(dqt_ref)

        dk_sc[...] = jnp.zeros_like(dk_sc)
        dv_sc[...] = jnp.zeros_like(dv_sc)
        k_t, v_t = k_ref[...], v_ref[...]

        def rows(i):
            return pl.ds(pl.multiple_of(i * t, t), t)

        def fetch(i, slot):
            i = jnp.minimum(i, n - 1)
            z_sc[slot] = _scores_t(k_t, q_ref[rows(i), :])
            dp_sc[slot] = _mm_nt(v_t, do_ref[rows(i), :])

        def stage(i, slot, masked, prefetch=True):
            if prefetch:
                fetch(i + 1, 1 - slot)
            z = _mask_above_diagonal(z_sc[slot], t) if masked else z_sc[slot]
            pr = jnp.exp(z - lse_ref[i])
            dsb = _cast(pr * (dp_sc[slot] - dl_ref[i]))
            dv_sc[...] += _mm(_cast(pr), do_ref[rows(i), :])
            dk_sc[...] += _mm(dsb, q_ref[rows(i), :])
            dqt_ref[i] += _mm_tn(k_t, dsb)

        def pair_below_diagonal(pair, carry):
            stage(j + 1 + 2 * pair, 1, False)
            stage(j + 2 + 2 * pair, 0, False)
            return carry

        below = n - 1 - j
        fetch(j, 0)
        stage(j, 0, True)
        lax.fori_loop(0, below >> 1, pair_below_diagonal, 0)

        @pl.when((below & 1) == 1)
        def _():
            stage(n - 1, 1, False, prefetch=False)

        dk_ref[...] = dk_sc[...]
        dv_ref[...] = dv_sc[...]

    whole_head = dict(pipeline_mode=pl.Buffered(1))
    stat = pl.BlockSpec((None, n, 1, t), lambda h, j: (h, 0, 0, 0))
    return pl.pallas_call(
        body, name="flash_bwd", grid=(n_h, n),
        in_specs=[pl.BlockSpec((None, s, QK_HEAD), lambda h, j: (h, 0, 0), **whole_head),
                  pl.BlockSpec((s, V_HEAD), lambda h, j: (0, h), **whole_head),
                  stat, stat,
                  pl.BlockSpec((None, t, QK_HEAD), lambda h, j: (h, j, 0)),
                  pl.BlockSpec((t, V_HEAD), lambda h, j: (j, h))],
        out_specs=[pl.BlockSpec((None, n, QK_HEAD, t), lambda h, j: (h, 0, 0, 0), **whole_head),
                   pl.BlockSpec((None, t, QK_HEAD), lambda h, j: (h, j, 0)),
                   pl.BlockSpec((t, V_HEAD), lambda h, j: (j, h))],
        out_shape=[_sds((n_h, n, QK_HEAD, t)), _sds((n_h, s, QK_HEAD)), _sds((s, n_h * V_HEAD))],
        scratch_shapes=[pltpu.VMEM((t, QK_HEAD), F32), pltpu.VMEM((t, V_HEAD), F32),
                        pltpu.VMEM((2, t, t), F32), pltpu.VMEM((2, t, t), F32)],
        compiler_params=_params(("arbitrary", "arbitrary")),
    )(q, do, lse, delta, k, v)


def _mla_out_fwd(x, ot, w_out):
    s, d = x.shape
    hv = ot.shape[0]
    tm = _tile(s, FFN_TILE)

    def body(x_ref, ot_ref, w_ref, y_ref):
        y_ref[...] = x_ref[...] + _mm_tn(_cast(ot_ref[...]), w_ref[...])

    tok = pl.BlockSpec((tm, d), lambda i: (i, 0))
    return pl.pallas_call(
        body, name="mla_out_fwd", grid=(s // tm,),
        in_specs=[tok, pl.BlockSpec((hv, tm), lambda i: (0, i)), _full(w_out.shape)],
        out_specs=tok, out_shape=_sds((s, d)),
        compiler_params=_params(("arbitrary",)),
    )(x, ot, w_out)


def _mla_out_bwd(dy, ot, w_out):
    s, d = dy.shape
    hv = ot.shape[0]
    n_h = hv // V_HEAD
    tm = _tile(s, FFN_TILE)

    def body(dy_ref, ot_ref, w_ref, do_ref, dl_ref, dw_ref):
        @pl.when(pl.program_id(0) == 0)
        def _():
            dw_ref[...] = jnp.zeros_like(dw_ref)

        dyb = _cast(dy_ref[...])
        o_t = ot_ref[...]
        do_ref[...] = _cast(_mm_nt(dyb, w_ref[...]))
        prod = _mm_nt(w_ref[...], dyb) * o_t
        for hh in range(n_h):
            dl_ref[hh] = jnp.sum(prod[hh * V_HEAD:(hh + 1) * V_HEAD], axis=0, keepdims=True)
        dw_ref[...] += _mm(_cast(o_t), dyb)

    return pl.pallas_call(
        body, name="mla_out_bwd", grid=(s // tm,),
        in_specs=[pl.BlockSpec((tm, d), lambda i: (i, 0)), pl.BlockSpec((hv, tm), lambda i: (0, i)),
                  _full(w_out.shape)],
        out_specs=[pl.BlockSpec((tm, hv), lambda i: (i, 0)), pl.BlockSpec((n_h, 1, tm), lambda i: (0, 0, i)),
                   _full(w_out.shape)],
        out_shape=[_sds((s, hv), MXU_DTYPE), _sds((n_h, 1, s)), _sds(w_out.shape)],
        compiler_params=_params(("arbitrary",)),
    )(dy, ot, w_out)


def _loss_and_grad(y, target):
    s, d = y.shape
    tm = _tile(s, ROW_TILE)

    def body(y_ref, t_ref, loss_ref, dy_ref):
        @pl.when(pl.program_id(0) == 0)
        def _():
            loss_ref[...] = jnp.zeros_like(loss_ref)

        err = y_ref[...] - t_ref[...]
        dy_ref[...] = err * (1.0 / d)
        loss_ref[...] += 0.5 * jnp.sum(jnp.mean(err * err, axis=-1, keepdims=True), axis=0, keepdims=True)

    tok = pl.BlockSpec((tm, d), lambda i: (i, 0))
    return pl.pallas_call(
        body, name="loss_and_grad", grid=(s // tm,),
        in_specs=[tok, tok], out_specs=[_full((1, 1)), tok],
        out_shape=[_sds((1, 1)), _sds((s, d))],
        compiler_params=_params(("arbitrary",)),
    )(y, target)


def _mesh_position():
    return lax.axis_index("x"), lax.axis_index("y"), lax.axis_index("c")


def _other_chips(x, y):
    return [(1 - x, y), (x, 1 - y), (1 - x, 1 - y)]


ANY = pl.BlockSpec(memory_space=pl.ANY)


def _gather_over_chips(arrs):
    n = len(arrs)
    halves = [a.shape[0] // 2 for a in arrs]
    assert all(a.shape[0] % 2 == 0 for a in arrs)
    own = 2 * (N_CHIPS - 1)

    def body(*refs):
        srcs, outs = refs[:n], refs[n:2 * n]
        send_sems, recv_sems = refs[2 * n:]
        x, y, c = _mesh_position()
        me, sibling = (x, y, c), (x, y, 1 - c)
        chips = _other_chips(x, y)
        my_chip = 2 * x + y

        def rows(t, chip, half):
            return outs[t].at[chip, pl.ds(half * halves[t], halves[t])]

        def copy(t, k, src, dst, to):
            return pltpu.make_async_remote_copy(src_ref=src, dst_ref=dst, send_sem=send_sems.at[t, k],
                                                recv_sem=recv_sems.at[t, k], device_id=to, device_id_type=MESH)

        started = []
        for t in range(n):
            for k, (px, py) in enumerate(chips):
                cp = copy(t, k, srcs[t].at[pl.ds(c * halves[t], halves[t])], rows(t, my_chip, c), (px, py, c))
                cp.start()
                started.append(cp)
            cp = copy(t, own, srcs[t], outs[t].at[my_chip], sibling)
            cp.start()
            started.append(cp)
        for t in range(n):
            for k, (px, py) in enumerate(chips):
                landed = rows(t, 2 * px + py, c)
                copy(t, k, landed, landed, me).wait_recv()
                cp = copy(t, N_CHIPS - 1 + k, landed, landed, sibling)
                cp.start()
                started.append(cp)
        for t in range(n):
            for k, (px, py) in enumerate(chips):
                passed = rows(t, 2 * px + py, 1 - c)
                copy(t, N_CHIPS - 1 + k, passed, passed, me).wait_recv()
            copy(t, own, srcs[t], outs[t].at[my_chip], me).wait_recv()
        for cp in started:
            cp.wait_send()

    return pl.pallas_call(
        body, name="gather_over_chips",
        in_specs=[ANY] * n, out_specs=[ANY] * n,
        out_shape=[_sds((N_CHIPS,) + a.shape, a.dtype) for a in arrs],
        scratch_shapes=[pltpu.SemaphoreType.DMA((n, own + 1)), pltpu.SemaphoreType.DMA((n, own + 1))],
    )(*arrs)


def _send_other_half_to_sibling(grads):
    n = len(grads)
    halves = [g.shape[1] // 2 for g in grads]

    def body(*refs):
        srcs, outs = refs[:n], refs[n:2 * n]
        send_sems, recv_sems = refs[2 * n:]
        x, y, c = _mesh_position()
        copies = []
        for t in range(n):
            cp = pltpu.make_async_remote_copy(
                src_ref=srcs[t].at[pl.ds(0, N_CHIPS), pl.ds((1 - c) * halves[t], halves[t])], dst_ref=outs[t],
                send_sem=send_sems.at[t], recv_sem=recv_sems.at[t], device_id=(x, y, 1 - c), device_id_type=MESH)
            cp.start()
            copies.append(cp)
        for cp in copies:
            cp.wait_recv()
        for cp in copies:
            cp.wait_send()

    return pl.pallas_call(
        body, name="send_other_half_to_sibling",
        in_specs=[ANY] * n, out_specs=[ANY] * n,
        out_shape=[_sds((N_CHIPS, h) + g.shape[2:]) for g, h in zip(grads, halves)],
        scratch_shapes=[pltpu.SemaphoreType.DMA((n,)), pltpu.SemaphoreType.DMA((n,))],
    )(*grads)


def _send_blocks_to_chips(parts):
    n = len(parts)

    def body(*refs):
        srcs, outs = refs[:n], refs[n:2 * n]
        send_sems, recv_sems = refs[2 * n:]
        x, y, c = _mesh_position()
        copies = []
        for t in range(n):
            for k, (px, py) in enumerate(_other_chips(x, y)):
                cp = pltpu.make_async_remote_copy(
                    src_ref=srcs[t].at[2 * px + py], dst_ref=outs[t].at[k], send_sem=send_sems.at[t, k],
                    recv_sem=recv_sems.at[t, k], device_id=(px, py, c), device_id_type=MESH)
                cp.start()
                copies.append(cp)
        for cp in copies:
            cp.wait_recv()
        for cp in copies:
            cp.wait_send()

    return pl.pallas_call(
        body, name="send_blocks_to_chips",
        in_specs=[ANY] * n, out_specs=[ANY] * n,
        out_shape=[_sds((N_CHIPS - 1,) + p.shape[1:], p.dtype) for p in parts],
        scratch_shapes=[pltpu.SemaphoreType.DMA((n, N_CHIPS - 1)), pltpu.SemaphoreType.DMA((n, N_CHIPS - 1))],
    )(*parts)


def _join_halves_with_sibling(sums):
    n = len(sums)

    def body(*refs):
        srcs, outs = refs[:n], refs[n:2 * n]
        send_sems, recv_sems = refs[2 * n:]
        x, y, c = _mesh_position()
        copies = []
        for t in range(n):
            h = srcs[t].shape[0] // 2
            mine = pl.ds(c * h, h)
            cp = pltpu.make_async_remote_copy(
                src_ref=srcs[t].at[mine], dst_ref=outs[t].at[mine], send_sem=send_sems.at[t],
                recv_sem=recv_sems.at[t], device_id=(x, y, 1 - c), device_id_type=MESH)
            cp.start()
            copies.append(cp)
        for t in range(n):
            h = srcs[t].shape[0] // 2
            theirs = pl.ds((1 - c) * h, h)
            pltpu.make_async_remote_copy(
                src_ref=srcs[t].at[theirs], dst_ref=outs[t].at[theirs], send_sem=send_sems.at[t],
                recv_sem=recv_sems.at[t], device_id=(x, y, 1 - c), device_id_type=MESH).wait_recv()
        for cp in copies:
            cp.wait_send()

    return pl.pallas_call(
        body, name="join_halves_with_sibling",
        in_specs=[ANY] * n, out_specs=[ANY] * n,
        out_shape=[_sds(a.shape) for a in sums],
        input_output_aliases={t: t for t in range(n)},
        scratch_shapes=[pltpu.SemaphoreType.DMA((n,)), pltpu.SemaphoreType.DMA((n,))],
    )(*sums)


def _gather_over_devices(rows):
    r = rows.shape[0]

    def body(in_ref, out_ref, send_sems, recv_sems, local_sem):
        x, y, c = _mesh_position()
        mine = pltpu.make_async_copy(in_ref, out_ref.at[4 * x + 2 * y + c], local_sem)
        mine.start()
        copies = []
        for mask in range(1, N_DEVICES):
            fx, fy, fc = (mask >> 2) & 1, (mask >> 1) & 1, mask & 1
            px, py, pc = (1 - x if fx else x), (1 - y if fy else y), (1 - c if fc else c)
            send = pltpu.make_async_remote_copy(
                src_ref=in_ref, dst_ref=out_ref.at[4 * x + 2 * y + c], send_sem=send_sems.at[mask - 1],
                recv_sem=recv_sems.at[mask - 1], device_id=(px, py, pc), device_id_type=MESH)
            send.start()
            recv = pltpu.make_async_remote_copy(
                src_ref=in_ref, dst_ref=out_ref.at[4 * px + 2 * py + pc], send_sem=send_sems.at[mask - 1],
                recv_sem=recv_sems.at[mask - 1], device_id=(px, py, pc), device_id_type=MESH)
            copies.append((send, recv))
        for _, recv in copies:
            recv.wait_recv()
        for send, _ in copies:
            send.wait_send()
        mine.wait()

    vm = pl.BlockSpec(memory_space=pltpu.VMEM)
    return pl.pallas_call(
        body, name="gather_over_devices", in_specs=[vm], out_specs=vm,
        out_shape=_sds((N_DEVICES, r, LANES)),
        scratch_shapes=[pltpu.SemaphoreType.DMA((N_DEVICES - 1,)), pltpu.SemaphoreType.DMA((N_DEVICES - 1,)),
                        pltpu.SemaphoreType.DMA],
    )(rows)


def _add_sibling_half(grad, received, chip, core):
    _, l, r, c = grad.shape
    half = l // 2

    def body(chip_ref, core_ref, g_ref, r_ref, wire_ref, own_ref):
        total = g_ref[...] + r_ref[...]
        wire_ref[...] = total.astype(WIRE_DTYPE)

        @pl.when(pl.program_id(1) == chip_ref[0])
        def _():
            own_ref[...] = total

    blk = lambda f: pl.BlockSpec((None, None, r, c), f)
    grid_spec = pltpu.PrefetchScalarGridSpec(
        num_scalar_prefetch=2, grid=(half, N_CHIPS),
        in_specs=[blk(lambda i, j, chip, core: (j, core[0] * half + i, 0, 0)),
                  blk(lambda i, j, chip, core: (j, i, 0, 0))],
        out_specs=[blk(lambda i, j, chip, core: (j, i, 0, 0)),
                   pl.BlockSpec((None, r, c), lambda i, j, chip, core: (i, 0, 0))])
    return pl.pallas_call(
        body, name="add_sibling_half", grid_spec=grid_spec,
        out_shape=[_sds((N_CHIPS, half, r, c), WIRE_DTYPE), _sds((half, r, c))],
        compiler_params=_params(("arbitrary", "arbitrary")),
    )(chip, core, grad, received)


def _add_chip_blocks(own, received, core):
    half, r, c = own.shape

    def body(core_ref, p_ref, r0_ref, r1_ref, r2_ref, o_ref):
        o_ref[...] = ((p_ref[...] + r0_ref[...].astype(F32)) + r1_ref[...].astype(F32)) + r2_ref[...].astype(F32)

    grid_spec = pltpu.PrefetchScalarGridSpec(
        num_scalar_prefetch=1, grid=(half,),
        in_specs=[pl.BlockSpec((None, r, c), lambda i, core: (i, 0, 0))] + [
            pl.BlockSpec((None, None, r, c), functools.partial(lambda i, core, k: (k, i, 0, 0), k=k))
            for k in range(N_CHIPS - 1)],
        out_specs=pl.BlockSpec((None, r, c), lambda i, core: (core[0] * half + i, 0, 0)))
    return pl.pallas_call(
        body, name="add_chip_blocks", grid_spec=grid_spec, out_shape=_sds((2 * half, r, c)),
        compiler_params=_params(("arbitrary",)),
    )(core, own, received, received, received)


def _sum_over_devices(parts):
    _, r, _ = parts.shape

    def body(p_ref, o_ref):
        acc = p_ref[0]
        for k in range(1, N_DEVICES):
            acc = acc + p_ref[k]
        o_ref[...] = acc

    return pl.pallas_call(body, name="sum_over_devices", out_shape=_sds((r, LANES)))(parts)


def _adamw_math(w, g, m, v):
    m = ADAM_B1 * m + (1.0 - ADAM_B1) * g
    v = ADAM_B2 * v + (1.0 - ADAM_B2) * (g * g)
    m_hat = m / (1.0 - ADAM_B1 ** ADAM_STEP)
    v_hat = v / (1.0 - ADAM_B2 ** ADAM_STEP)
    delta = -ADAM_LR * (m_hat / (jnp.sqrt(v_hat) + ADAM_EPS) + ADAM_WD * w)
    return delta, m, v


def _adamw_stacked(w, m, v, grads, offset):
    l, r, c = w.shape
    tr = r
    while tr * c * 4 > 2**20 and tr % 16 == 0:
        tr //= 2

    def body(w_ref, m_ref, v_ref, g_ref, go_ref, d_ref, mo_ref, vo_ref):
        g = g_ref[...]
        go_ref[...] = g
        d_ref[...], mo_ref[...], vo_ref[...] = _adamw_math(w_ref[...], g, m_ref[...], v_ref[...])

    blk = pl.BlockSpec((None, tr, c), lambda i, j: (i, j, 0))
    return pl.pallas_call(
        body, name="adamw_stacked", grid=(l, r // tr),
        in_specs=[blk, blk, blk, pl.BlockSpec((None, tr, c), lambda i, j: (offset + i, j, 0))],
        out_specs=[blk] * 4, out_shape=[_sds((l, r, c))] * 4,
        compiler_params=_params(("arbitrary", "arbitrary")),
    )(w, m, v, grads)


def _adamw_small(w, m, v, g):
    def body(w_ref, m_ref, v_ref, g_ref, d_ref, mo_ref, vo_ref):
        d_ref[...], mo_ref[...], vo_ref[...] = _adamw_math(w_ref[...], g_ref[...], m_ref[...], v_ref[...])

    return pl.pallas_call(body, name="adamw_small", out_shape=[_sds(w.shape)] * 3)(w, m, v, g)


def _pack_rows(arrs):
    flat = jnp.concatenate([a.reshape(-1) for a in arrs])
    pad = (-flat.shape[0]) % (8 * LANES)
    return jnp.pad(flat, (0, pad)).reshape(-1, LANES)


def _unpack_rows(rows, shapes, lead=()):
    flat = rows.reshape(lead + (-1,))
    out, at = [], 0
    for shp in shapes:
        size = int(np.prod(shp))
        out.append(flat[..., at:at + size].reshape(lead + tuple(shp)))
        at += size
    return out


WEIGHT_NAMES = ('ffn1_norm', 'ffn1_w_gate', 'ffn1_w_up', 'ffn1_w_down', 'mix_norm', 'pool_w', 'pool_scale',
                'mla_w_in', 'mla_q_norm', 'mla_w_q_up', 'mla_kv_norm', 'mla_w_kv_up', 'mla_q_head_norm',
                'mla_k_head_norm', 'mla_w_out', 'ffn2_norm', 'ffn2_w_gate', 'ffn2_w_up', 'ffn2_w_down')


def _chips_to_columns(g):
    return jnp.transpose(g, (1, 2, 0, 3)).reshape(g.shape[1], g.shape[2], -1)


def _columns_to_chips(full):
    n, r, c4 = full.shape
    return jnp.transpose(full.reshape(n, r, N_CHIPS, c4 // N_CHIPS), (2, 0, 1, 3))


def kernel(x, positions, ffn1_norm, ffn1_w_gate, ffn1_w_up, ffn1_w_down, mix_norm, pool_w, pool_scale, mla_w_in, mla_q_norm, mla_w_q_up, mla_kv_norm, mla_w_kv_up, mla_q_head_norm, mla_k_head_norm, mla_w_out, ffn2_norm, ffn2_w_gate, ffn2_w_up, ffn2_w_down, loss_target, m_ffn1_norm, m_ffn1_w_gate, m_ffn1_w_up, m_ffn1_w_down, m_mix_norm, m_pool_w, m_pool_scale, m_mla_w_in, m_mla_q_norm, m_mla_w_q_up, m_mla_kv_norm, m_mla_w_kv_up, m_mla_q_head_norm, m_mla_k_head_norm, m_mla_w_out, m_ffn2_norm, m_ffn2_w_gate, m_ffn2_w_up, m_ffn2_w_down, v_ffn1_norm, v_ffn1_w_gate, v_ffn1_w_up, v_ffn1_w_down, v_mix_norm, v_pool_w, v_pool_scale, v_mla_w_in, v_mla_q_norm, v_mla_w_q_up, v_mla_kv_norm, v_mla_w_kv_up, v_mla_q_head_norm, v_mla_k_head_norm, v_mla_w_out, v_ffn2_norm, v_ffn2_w_gate, v_ffn2_w_up, v_ffn2_w_down):
    env = dict(locals())
    w = {n: env[n] for n in WEIGHT_NAMES}
    mom = {n: env["m_" + n] for n in WEIGHT_NAMES}
    var = {n: env["v_" + n] for n in WEIGHT_NAMES}

    s, d = x.shape[1], x.shape[2]
    depth = ffn1_norm.shape[0]
    n_mla, n_pool, n_groups = mla_w_in.shape[0], pool_w.shape[0], pool_w.shape[1]
    pool_c = pool_w.shape[3]
    q_lora = N_CHIPS * mla_q_norm.shape[1]
    kv_lora = N_CHIPS * mla_kv_norm.shape[1]
    n_heads = N_CHIPS * mla_w_q_up.shape[2] // QK_HEAD
    t_fwd, t_bwd = _tile(s, ATTN_FWD_TILE), _tile(s, ATTN_BWD_TILE)
    cx, cy, cc = _mesh_position()
    chip = 2 * cx + cy
    chip_arr = jnp.reshape(chip, (1,)).astype(jnp.int32)
    core_arr = jnp.reshape(cc, (1,)).astype(jnp.int32)

    shard_gu = _cast(jnp.swapaxes(jnp.concatenate([ffn1_w_gate, ffn1_w_up, ffn2_w_gate, ffn2_w_up], axis=0), 1, 2))
    shard_dn = _cast(jnp.concatenate([ffn1_w_down, ffn2_w_down], axis=0))
    shard_pool = _cast(pool_w.reshape((n_pool * n_groups,) + pool_w.shape[2:]))
    w_gu, w_dn, g_in, g_qup, g_kvup, g_out, g_pool = _gather_over_chips(
        [shard_gu, shard_dn, _cast(mla_w_in), _cast(mla_w_q_up), _cast(mla_w_kv_up), _cast(mla_w_out), shard_pool])
    small_shapes = [mla_q_norm.shape, mla_kv_norm.shape]
    small = _gather_over_devices(_pack_rows([mla_q_norm, mla_kv_norm]))[::2]
    qn_chips, kvn_chips = _unpack_rows(small, small_shapes, lead=(N_CHIPS,))
    q_norm_full = jnp.transpose(qn_chips, (1, 0, 2)).reshape(n_mla, 1, q_lora)
    kv_norm_full = jnp.transpose(kvn_chips, (1, 0, 2)).reshape(n_mla, 1, kv_lora)

    w_in_full = _chips_to_columns(g_in)
    w_q_heads = jnp.transpose(_chips_to_columns(g_qup).reshape(n_mla, q_lora, n_heads, QK_HEAD), (0, 2, 1, 3))
    w_kv = _chips_to_columns(g_kvup).reshape(n_mla, kv_lora, n_heads, QK_NOPE + V_HEAD)
    w_kn_heads = jnp.transpose(w_kv[..., :QK_NOPE], (0, 2, 1, 3))
    w_v_full = w_kv[..., QK_NOPE:].reshape(n_mla, kv_lora, n_heads * V_HEAD)
    w_out_full = jnp.transpose(g_out, (1, 0, 2, 3)).reshape(n_mla, n_heads * V_HEAD, d)
    pool_full = jnp.transpose(g_pool.reshape(N_CHIPS, n_pool, n_groups, pool_c // N_CHIPS, pool_c),
                              (1, 2, 0, 3, 4)).reshape(n_pool, n_groups, pool_c, pool_c)

    inv_freq = (1.0 / (ROPE_THETA ** (jnp.arange(0, QK_ROPE, 2, dtype=F32) / QK_ROPE))).reshape(1, -1)
    cos_t, sin_t = _rope_tables(positions.reshape(s, 1), inv_freq)

    row = lambda a, i: a[i].reshape(1, -1)
    i_gate1, i_up1, i_gate2, i_up2 = (lambda i: i), (lambda i: depth + i), (lambda i: 2 * depth + i), (lambda i: 3 * depth + i)
    i_dn1, i_dn2 = (lambda i: i), (lambda i: depth + i)

    h = x.reshape(s, d)
    saved = []
    for i in range(depth):
        rec = {"x_ffn1": h}
        h, *rec["ffn1"] = _ffn_fwd(h, row(ffn1_norm, i), w_gu, w_dn, i_gate1(i), i_up1(i), i_dn1(i))
        rec["x_mix"] = h
        j = i // 2
        if i % 2 == 0:
            h = _pool_fwd(h, row(mix_norm, i), pool_full[j], row(pool_scale, j))
        else:
            lat, q, k, v = _mla_qkv_fwd(h, row(mix_norm, i), w_in_full[j], q_norm_full[j], kv_norm_full[j],
                                        w_q_heads[j], w_kn_heads[j], w_v_full[j], row(mla_q_head_norm, j),
                                        row(mla_k_head_norm, j), cos_t, sin_t)
            vt = jnp.transpose(v.reshape(s // t_fwd, t_fwd, n_heads, V_HEAD), (2, 0, 3, 1))
            vt = jnp.concatenate([vt, jnp.ones((n_heads, s // t_fwd, ONES_ROWS, t_fwd), vt.dtype)], axis=2)
            ot, lse = _flash_fwd(q, k, vt)
            rec.update(lat=lat, q=q, k=k, v=v, ot=ot, lse=lse)
            h = _mla_out_fwd(h, ot, w_out_full[j])
        rec["x_ffn2"] = h
        h, *rec["ffn2"] = _ffn_fwd(h, row(ffn2_norm, i), w_gu, w_dn, i_gate2(i), i_up2(i), i_dn2(i))
        saved.append(rec)

    loss_part, dy = _loss_and_grad(h, loss_target.reshape(s, d))
    loss = lax.psum(loss_part[0, 0], ("x", "y", "c"))

    g_gu = [None] * (4 * depth)
    g_dn = [None] * (2 * depth)
    g_norm = {n: [None] * depth for n in ("ffn1_norm", "mix_norm", "ffn2_norm")}
    g_pool_w, g_pool_scale = [None] * n_pool, [None] * n_pool
    g_mla = {n: [None] * n_mla for n in ("w_in", "q_norm", "kv_norm", "w_q", "w_kv", "qhn", "khn", "w_out")}
    for i in reversed(range(depth)):
        rec = saved[i]
        hb, gate, up = rec["ffn2"]
        dy, g_norm["ffn2_norm"][i], dyb, dgt, dup, act = _ffn_bwd_dgrad(
            rec["x_ffn2"], row(ffn2_norm, i), dy, gate, up, w_gu, w_dn, i_gate2(i), i_up2(i), i_dn2(i))
        g_gu[i_gate2(i)], g_gu[i_up2(i)], g_dn[i_dn2(i)] = _ffn_wgrad(hb, dyb, dgt, dup, act)
        j = i // 2
        if i % 2 == 0:
            dy, g_norm["mix_norm"][i], g_pool_w[j], g_pool_scale[j] = _pool_bwd(
                rec["x_mix"], row(mix_norm, i), pool_full[j], row(pool_scale, j), dy)
        else:
            do, delta, g_mla["w_out"][j] = _mla_out_bwd(dy, rec["ot"], w_out_full[j])
            by_tile = lambda a: a.reshape(n_heads, s // t_bwd, 1, t_bwd)
            dqt, dk, dv = _flash_bwd(rec["q"], rec["k"], rec["v"], do,
                                     by_tile(rec["lse"]), by_tile(delta))
            dq = jnp.transpose(dqt, (0, 1, 3, 2)).reshape(n_heads, s, QK_HEAD)
            (dy, g_norm["mix_norm"][i], g_mla["w_in"][j], g_mla["q_norm"][j], g_mla["kv_norm"][j], dwq, dwkn, dwv,
             g_mla["qhn"][j], g_mla["khn"][j]) = _mla_qkv_bwd(
                rec["x_mix"], rec["lat"], dy, dq, dk, dv, row(mix_norm, i), w_in_full[j], q_norm_full[j],
                kv_norm_full[j], w_q_heads[j], w_kn_heads[j], w_v_full[j], row(mla_q_head_norm, j),
                row(mla_k_head_norm, j), cos_t, sin_t)
            g_mla["w_q"][j] = jnp.transpose(dwq, (1, 0, 2)).reshape(q_lora, n_heads * QK_HEAD)
            g_mla["w_kv"][j] = jnp.concatenate(
                [jnp.transpose(dwkn, (1, 0, 2)), dwv.reshape(kv_lora, n_heads, V_HEAD)], axis=-1
            ).reshape(kv_lora, n_heads * (QK_NOPE + V_HEAD))
        hb, gate, up = rec["ffn1"]
        dy, g_norm["ffn1_norm"][i], dyb, dgt, dup, act = _ffn_bwd_dgrad(
            rec["x_ffn1"], row(ffn1_norm, i), dy, gate, up, w_gu, w_dn, i_gate1(i), i_up1(i), i_dn1(i))
        g_gu[i_gate1(i)], g_gu[i_up1(i)], g_dn[i_dn1(i)] = _ffn_wgrad(hb, dyb, dgt, dup, act)
    grad_x = dy.reshape(x.shape)

    full_grads = [
        jnp.stack(g_gu, axis=1),
        jnp.stack(g_dn, axis=1),
        _columns_to_chips(jnp.stack(g_mla["w_in"])),
        _columns_to_chips(jnp.stack(g_mla["w_q"])),
        _columns_to_chips(jnp.stack(g_mla["w_kv"])),
        jnp.transpose(jnp.stack(g_mla["w_out"]).reshape(n_mla, N_CHIPS, -1, d), (1, 0, 2, 3)),
        jnp.transpose(jnp.stack(g_pool_w).reshape(n_pool, n_groups, N_CHIPS, pool_c // N_CHIPS, pool_c),
                      (2, 0, 1, 3, 4)).reshape(N_CHIPS, n_pool * n_groups, pool_c // N_CHIPS, pool_c),
    ]
    from_sibling = _send_other_half_to_sibling(full_grads)
    chip_sums = [_add_sibling_half(g, r, chip_arr, core_arr) for g, r in zip(full_grads, from_sibling)]
    from_chips = _send_blocks_to_chips([wire for wire, _ in chip_sums])
    half_sums = [_add_chip_blocks(own, r, core_arr) for (_, own), r in zip(chip_sums, from_chips)]
    r_gu, r_dn, r_in, r_qup, r_kvup, r_out, r_pool = _join_halves_with_sibling(half_sums)

    small_grads = [jnp.concatenate(g_norm["ffn1_norm"]), jnp.concatenate(g_norm["mix_norm"]),
                   jnp.concatenate(g_norm["ffn2_norm"]), jnp.concatenate(g_pool_scale),
                   jnp.concatenate(g_mla["qhn"]), jnp.concatenate(g_mla["khn"]),
                   jnp.concatenate(g_mla["q_norm"]), jnp.concatenate(g_mla["kv_norm"])]
    small_sum = _sum_over_devices(_gather_over_devices(_pack_rows(small_grads)))
    (s_ffn1, s_mix, s_ffn2, s_pscale, s_qhn, s_khn, s_qn, s_kvn) = _unpack_rows(small_sum, [g.shape for g in small_grads])
    qn_w, kvn_w = mla_q_norm.shape[1], mla_kv_norm.shape[1]
    s_qn = lax.dynamic_slice_in_dim(s_qn, chip * qn_w, qn_w, axis=1)
    s_kvn = lax.dynamic_slice_in_dim(s_kvn, chip * kvn_w, kvn_w, axis=1)

    grads, deltas, new_m, new_v = {}, {}, {}, {}

    def stacked(name, reduced, offset, transposed=False):
        shape = w[name].shape
        if transposed:
            as3, back = (lambda a: jnp.swapaxes(a, 1, 2)), (lambda a: jnp.swapaxes(a, 1, 2))
        else:
            as3, back = (lambda a: a.reshape((-1,) + shape[-2:])), (lambda a: a.reshape(shape))
        out = _adamw_stacked(as3(w[name]), as3(mom[name]), as3(var[name]), reduced, offset)
        grads[name], deltas[name], new_m[name], new_v[name] = [back(o) for o in out]

    def small_update(name, g):
        grads[name] = g
        deltas[name], new_m[name], new_v[name] = _adamw_small(w[name], mom[name], var[name], g)

    stacked("ffn1_w_gate", r_gu, 0, transposed=True)
    stacked("ffn1_w_up", r_gu, depth, transposed=True)
    stacked("ffn2_w_gate", r_gu, 2 * depth, transposed=True)
    stacked("ffn2_w_up", r_gu, 3 * depth, transposed=True)
    stacked("ffn1_w_down", r_dn, 0)
    stacked("ffn2_w_down", r_dn, depth)
    stacked("mla_w_in", r_in, 0)
    stacked("mla_w_q_up", r_qup, 0)
    stacked("mla_w_kv_up", r_kvup, 0)
    stacked("mla_w_out", r_out, 0)
    stacked("pool_w", r_pool, 0)
    small_update("ffn1_norm", s_ffn1)
    small_update("mix_norm", s_mix)
    small_update("ffn2_norm", s_ffn2)
    small_update("pool_scale", s_pscale)
    small_update("mla_q_head_norm", s_qhn)
    small_update("mla_k_head_norm", s_khn)
    small_update("mla_q_norm", s_qn)
    small_update("mla_kv_norm", s_kvn)

    return (loss, grad_x, *[grads[n] for n in WEIGHT_NAMES], *[deltas[n] for n in WEIGHT_NAMES],
            *[new_m[n] for n in WEIGHT_NAMES], *[new_v[n] for n in WEIGHT_NAMES])
```

```python
import functools

import numpy as np

import jax
import jax.numpy as jnp
from jax import lax
from jax.experimental import pallas as pl
from jax.experimental.pallas import tpu as pltpu

F32 = jnp.float32
MXU_DTYPE = jnp.bfloat16
WIRE_DTYPE = jnp.bfloat16
MESH = pl.DeviceIdType.MESH
N_CHIPS = 4
N_DEVICES = 8
LANES = 128
VMEM_LIMIT_BYTES = 56 * 2**20
NORM_EPS = 1e-6
QK_NOPE, QK_ROPE, V_HEAD = 128, 64, 128
QK_HEAD = QK_NOPE + QK_ROPE
SCORE_SCALE = QK_HEAD ** -0.5
ONES_ROWS = 8
ROPE_THETA = 10000.0
POOL_WINDOWS = (2, 4, 8, 16)
POOL_HALO = 16
FFN_HALF = 0.5
ADAM_LR, ADAM_B1, ADAM_B2, ADAM_EPS, ADAM_WD, ADAM_STEP = 0.001, 0.9, 0.999, 1e-08, 0.01, 10
FFN_TILE = 512
FFN_BWD_TILE = 256
WGRAD_TILE = 2048
MLA_TILE = 256
POOL_TILE = 512
ATTN_FWD_TILE = 1024
ATTN_BWD_TILE = 1024
ROW_TILE = 1024


def _cast(v):
    return v.astype(MXU_DTYPE)


def _mm(a, b):
    return jnp.dot(a, b, preferred_element_type=F32)


def _mm_nt(a, b):
    return lax.dot_general(a, b, (((1,), (1,)), ((), ())), preferred_element_type=F32)


def _mm_tn(a, b):
    return lax.dot_general(a, b, (((0,), (0,)), ((), ())), preferred_element_type=F32)


def _rms_fwd(v, gain):
    r = lax.rsqrt(jnp.mean(v * v, axis=-1, keepdims=True) + NORM_EPS)
    return v * r * gain, r


def _rms_bwd(v, r, gain, dy):
    vr = v * r
    gy = dy * gain
    dv = r * (gy - vr * jnp.mean(gy * vr, axis=-1, keepdims=True))
    return dv, jnp.sum(dy * vr, axis=0, keepdims=True)


def _params(semantics=None):
    return pltpu.CompilerParams(dimension_semantics=semantics, vmem_limit_bytes=VMEM_LIMIT_BYTES)


def _tile(n, want):
    t = min(n, want)
    assert n % t == 0, (n, want)
    return t


def _full(shape):
    nd = len(shape)
    return pl.BlockSpec(shape, lambda *_: (0,) * nd)


def _sds(shape, dtype=F32):
    return jax.ShapeDtypeStruct(shape, dtype)


def _ffn_fwd(x, gain, w_gu, w_dn, i_gate, i_up, i_down):
    s, d = x.shape
    fs = w_gu.shape[-2]
    tm = _tile(s, FFN_TILE)

    def body(x_ref, g_ref, wg_ref, wu_ref, wd_ref, y_ref, hb_ref, gate_ref, up_ref):
        h, _ = _rms_fwd(x_ref[...], g_ref[...])
        hb = _cast(h)
        hb_ref[...] = hb
        pre = [(_mm_nt(hb, wg_ref[c]), _mm_nt(hb, wu_ref[c])) for c in range(N_CHIPS)]
        out = None
        for c, (g, u) in enumerate(pre):
            gate_ref[c] = _cast(g)
            up_ref[c] = _cast(u)
            part = _mm(_cast((g * jax.nn.sigmoid(g)) * u), wd_ref[c])
            out = part if out is None else out + part
        y_ref[...] = x_ref[...] + FFN_HALF * out

    resident = dict(pipeline_mode=pl.Buffered(1))
    tok = pl.BlockSpec((tm, d), lambda i: (i, 0))
    chunks = pl.BlockSpec((N_CHIPS, tm, fs), lambda i: (0, i, 0))
    return pl.pallas_call(
        body, name="ffn_fwd", grid=(s // tm,),
        in_specs=[
            tok, _full((1, d)),
            pl.BlockSpec((N_CHIPS, None, fs, d), lambda i: (0, i_gate, 0, 0), **resident),
            pl.BlockSpec((N_CHIPS, None, fs, d), lambda i: (0, i_up, 0, 0), **resident),
            pl.BlockSpec((N_CHIPS, None, fs, d), lambda i: (0, i_down, 0, 0), **resident),
        ],
        out_specs=[tok, tok, chunks, chunks],
        out_shape=[_sds((s, d)), _sds((s, d), MXU_DTYPE), _sds((N_CHIPS, s, fs), MXU_DTYPE),
                   _sds((N_CHIPS, s, fs), MXU_DTYPE)],
        compiler_params=_params(("arbitrary",)),
    )(x, gain, w_gu, w_gu, w_dn)


def _ffn_bwd_dgrad(x, gain, dy, gate, up, w_gu, w_dn, i_gate, i_up, i_down):
    s, d = x.shape
    fs = w_gu.shape[-2]
    tm = _tile(s, FFN_BWD_TILE)

    def body(x_ref, g_ref, dy_ref, gate_ref, up_ref, wg_ref, wu_ref, wd_ref,
             dx_ref, dgain_ref, dyb_ref, dg_ref, du_ref, act_ref):
        i = pl.program_id(0)
        dyb = _cast(dy_ref[...])
        dyb_ref[...] = dyb
        dacts = [_mm_nt(dyb, wd_ref[c]) for c in range(N_CHIPS)]
        dh = None
        for c in range(N_CHIPS):
            g = gate_ref[c].astype(F32)
            u = up_ref[c].astype(F32)
            sg = jax.nn.sigmoid(g)
            silu = g * sg
            dact = FFN_HALF * dacts[c]
            dgb = _cast(dact * u * (sg * (1.0 + g * (1.0 - sg))))
            dub = _cast(dact * silu)
            dg_ref[c] = dgb
            du_ref[c] = dub
            act_ref[c] = _cast(silu * u)
            part = _mm(dgb, wg_ref[c]) + _mm(dub, wu_ref[c])
            dh = part if dh is None else dh + part
        _, r = _rms_fwd(x_ref[...], g_ref[...])
        dxn, dgn = _rms_bwd(x_ref[...], r, g_ref[...], dh)
        dx_ref[...] = dy_ref[...] + dxn

        @pl.when(i == 0)
        def _():
            dgain_ref[...] = dgn

        @pl.when(i > 0)
        def _():
            dgain_ref[...] += dgn

    resident = dict(pipeline_mode=pl.Buffered(1))
    tok = pl.BlockSpec((tm, d), lambda i: (i, 0))
    chunks = pl.BlockSpec((N_CHIPS, tm, fs), lambda i: (0, i, 0))
    return pl.pallas_call(
        body, name="ffn_bwd_dgrad", grid=(s // tm,),
        in_specs=[
            tok, _full((1, d)), tok, chunks, chunks,
            pl.BlockSpec((N_CHIPS, None, fs, d), lambda i: (0, i_gate, 0, 0), **resident),
            pl.BlockSpec((N_CHIPS, None, fs, d), lambda i: (0, i_up, 0, 0), **resident),
            pl.BlockSpec((N_CHIPS, None, fs, d), lambda i: (0, i_down, 0, 0), **resident),
        ],
        out_specs=[tok, _full((1, d)), tok, chunks, chunks, chunks],
        out_shape=[_sds((s, d)), _sds((1, d)), _sds((s, d), MXU_DTYPE),
                   _sds((N_CHIPS, s, fs), MXU_DTYPE), _sds((N_CHIPS, s, fs), MXU_DTYPE),
                   _sds((N_CHIPS, s, fs), MXU_DTYPE)],
        compiler_params=_params(("arbitrary",)),
    )(x, gain, dy, gate, up, w_gu, w_gu, w_dn)


def _ffn_wgrad(hb, dyb, dg, du, act):
    s, d = hb.shape
    fs = dg.shape[-1]
    tk = _tile(s, WGRAD_TILE)
    n_k = s // tk

    def body(h_ref, dy_ref, dg_ref, du_ref, act_ref, wg_ref, wu_ref, wd_ref):
        k = pl.program_id(1)

        @pl.when(k == 0)
        def _():
            wg_ref[...] = jnp.zeros_like(wg_ref)
            wu_ref[...] = jnp.zeros_like(wu_ref)
            wd_ref[...] = jnp.zeros_like(wd_ref)

        h = h_ref[...]
        wg_ref[...] += _mm_tn(dg_ref[...], h)
        wu_ref[...] += _mm_tn(du_ref[...], h)
        wd_ref[...] += FFN_HALF * _mm_tn(act_ref[...], dy_ref[...])

    tok = pl.BlockSpec((tk, d), lambda j, k: (k, 0))
    chunk = pl.BlockSpec((None, tk, fs), lambda j, k: (j, k, 0))
    return pl.pallas_call(
        body, name="ffn_wgrad", grid=(N_CHIPS, n_k),
        in_specs=[tok, tok, chunk, chunk, chunk],
        out_specs=[pl.BlockSpec((None, fs, d), lambda j, k: (j, 0, 0)),
                   pl.BlockSpec((None, fs, d), lambda j, k: (j, 0, 0)),
                   pl.BlockSpec((None, fs, d), lambda j, k: (j, 0, 0))],
        out_shape=[_sds((N_CHIPS, fs, d))] * 3,
        compiler_params=_params(("arbitrary", "arbitrary")),
    )(hb, dyb, dg, du, act)


def _inv_count(first_row, n_rows, window):
    t = first_row + lax.broadcasted_iota(jnp.int32, (n_rows, 1), 0)
    return 1.0 / jnp.minimum(t + 1, window).astype(F32)


def _trailing_sum(v, window):
    k = 1
    while k < window:
        v = v + pltpu.roll(v, k, 0)
        k *= 2
    return v


def _leading_sum(v, window):
    n = v.shape[0]
    k = 1
    while k < window:
        v = v + pltpu.roll(v, n - k, 0)
        k *= 2
    return v


def _pool_normed_rows(x_ref, prev_ref, g_ref, i):
    h, r = _rms_fwd(x_ref[...], g_ref[...])
    hp, _ = _rms_fwd(prev_ref[...], g_ref[...])
    hp = jnp.where(i > 0, hp, 0.0)
    return jnp.concatenate([hp, h], axis=0), r


def _pooled_group(he, g, pg, first_row, tm):
    ue = he[:, g * pg:(g + 1) * pg]
    win = _trailing_sum(ue, POOL_WINDOWS[g])[POOL_HALO:]
    return win * _inv_count(first_row, tm, POOL_WINDOWS[g]) - ue[POOL_HALO:]


def _pool_specs(s, d, tm):
    per = tm // POOL_HALO
    last = s // POOL_HALO - 1
    tok = pl.BlockSpec((tm, d), lambda i: (i, 0))
    prev = pl.BlockSpec((POOL_HALO, d), lambda i: (jnp.maximum(i * per - 1, 0), 0))
    nxt = pl.BlockSpec((POOL_HALO, d), lambda i: (jnp.minimum((i + 1) * per, last), 0))
    return tok, prev, nxt


def _pool_fwd(x, gain, w, scale):
    s, d = x.shape
    n_g, pg = w.shape[0], w.shape[-1]
    tm = _tile(s, POOL_TILE)
    tok, prev, _ = _pool_specs(s, d, tm)

    def body(x_ref, prev_ref, g_ref, w_ref, sc_ref, y_ref):
        i = pl.program_id(0)
        he, _ = _pool_normed_rows(x_ref, prev_ref, g_ref, i)
        z = [_mm(_cast(_pooled_group(he, g, pg, i * tm, tm)), w_ref[g]) for g in range(n_g)]
        y_ref[...] = x_ref[...] + jnp.concatenate(z, axis=-1) * sc_ref[...]

    return pl.pallas_call(
        body, name="pool_fwd", grid=(s // tm,),
        in_specs=[tok, prev, _full((1, d)), _full(w.shape), _full((1, d))],
        out_specs=tok, out_shape=_sds((s, d)),
        compiler_params=_params(("arbitrary",)),
    )(x, x, gain, w, scale)


def _pool_bwd(x, gain, w, scale, dy):
    s, d = x.shape
    n_g, pg = w.shape[0], w.shape[-1]
    tm = _tile(s, POOL_TILE)
    n_tiles = s // tm
    tok, prev, nxt = _pool_specs(s, d, tm)

    def body(x_ref, prev_ref, dy_ref, next_ref, g_ref, w_ref, sc_ref, dx_ref, dgain_ref, dw_ref, dsc_ref):
        i = pl.program_id(0)

        @pl.when(i == 0)
        def _():
            dgain_ref[...] = jnp.zeros_like(dgain_ref)
            dw_ref[...] = jnp.zeros_like(dw_ref)
            dsc_ref[...] = jnp.zeros_like(dsc_ref)

        he, r = _pool_normed_rows(x_ref, prev_ref, g_ref, i)
        dy = dy_ref[...]
        dyn = jnp.where(i < n_tiles - 1, next_ref[...], 0.0)
        dze = jnp.concatenate([dy, dyn], axis=0) * sc_ref[...]
        dh, dsc = [], []
        for g in range(n_g):
            cols = slice(g * pg, (g + 1) * pg)
            pooled = _cast(_pooled_group(he, g, pg, i * tm, tm))
            dsc.append(jnp.sum(dy[:, cols] * _mm(pooled, w_ref[g]), axis=0, keepdims=True))
            dzb = _cast(dze[:, cols])
            dw_ref[g] += _mm_tn(pooled, dzb[:tm])
            dpool = _mm_nt(dzb, w_ref[g])
            spread = _leading_sum(dpool * _inv_count(i * tm, tm + POOL_HALO, POOL_WINDOWS[g]), POOL_WINDOWS[g])
            dh.append(spread[:tm] - dpool[:tm])
        dsc_ref[...] += jnp.concatenate(dsc, axis=-1)
        dxn, dgn = _rms_bwd(x_ref[...], r, g_ref[...], jnp.concatenate(dh, axis=-1))
        dgain_ref[...] += dgn
        dx_ref[...] = dy + dxn

    return pl.pallas_call(
        body, name="pool_bwd", grid=(n_tiles,),
        in_specs=[tok, prev, tok, nxt, _full((1, d)), _full(w.shape), _full((1, d))],
        out_specs=[tok, _full((1, d)), _full(w.shape), _full((1, d))],
        out_shape=[_sds((s, d)), _sds((1, d)), _sds(w.shape), _sds((1, d))],
        compiler_params=_params(("arbitrary",)),
    )(x, x, dy, dy, gain, w, scale)


def _rope_tables(pos_col, inv_freq):
    s = pos_col.shape[0]
    tm = _tile(s, ROW_TILE)
    half = QK_ROPE // 2

    def body(p_ref, f_ref, c_ref, s_ref):
        ang = p_ref[...].astype(F32) * f_ref[...]
        cos, sin = jnp.cos(ang), jnp.sin(ang)
        c_ref[...] = jnp.concatenate([jnp.ones((tm, QK_NOPE), F32), cos, cos], axis=-1)
        s_ref[...] = jnp.concatenate([jnp.zeros((tm, QK_NOPE), F32), -sin, sin], axis=-1)

    tab = pl.BlockSpec((tm, QK_HEAD), lambda i: (i, 0))
    return pl.pallas_call(
        body, name="rope_tables", grid=(s // tm,),
        in_specs=[pl.BlockSpec((tm, 1), lambda i: (i, 0)), _full((1, half))],
        out_specs=[tab, tab], out_shape=[_sds((s, QK_HEAD)), _sds((s, QK_HEAD))],
        compiler_params=_params(("arbitrary",)),
    )(pos_col, inv_freq)


def _swap_rope_halves(v):
    half = QK_ROPE // 2
    return jnp.concatenate([v[:, :QK_NOPE], v[:, QK_NOPE + half:], v[:, QK_NOPE:QK_NOPE + half]], axis=-1)


def _rope(v, cos, sin):
    return v * cos + _swap_rope_halves(v) * sin


def _rope_transposed(dv, cos, sin):
    return dv * cos + _swap_rope_halves(dv * sin)


def _mla_qkv_fwd(x, gain, w_in, q_norm, kv_norm, w_q, w_kn, w_v, q_head_norm, k_head_norm, cos, sin):
    s, d = x.shape
    n_h, ql = w_q.shape[0], w_q.shape[1]
    kvl, lat_w = w_kn.shape[1], w_in.shape[1]
    tm = _tile(s, MLA_TILE)

    def body(x_ref, g_ref, win_ref, qn_ref, kvn_ref, wq_ref, wkn_ref, wv_ref, qhn_ref, khn_ref, c_ref, s_ref,
             lat_ref, q_ref, k_ref, v_ref):
        h, _ = _rms_fwd(x_ref[...], g_ref[...])
        lat = _mm(_cast(h), win_ref[...])
        lat_ref[...] = lat
        cqn, _ = _rms_fwd(lat[:, :ql], qn_ref[...])
        ckvn, _ = _rms_fwd(lat[:, ql:ql + kvl], kvn_ref[...])
        kpe = lat[:, ql + kvl:]
        cqb, ckb = _cast(cqn), _cast(ckvn)
        cos_t, sin_t = c_ref[...], s_ref[...]
        v_ref[...] = _cast(_mm(ckb, wv_ref[...]))
        for hh in range(n_h):
            qn, _ = _rms_fwd(_mm(cqb, wq_ref[hh]), qhn_ref[...])
            q_ref[hh] = _cast(_rope(qn, cos_t, sin_t) * SCORE_SCALE)
            kn, _ = _rms_fwd(jnp.concatenate([_mm(ckb, wkn_ref[hh]), kpe], axis=-1), khn_ref[...])
            k_ref[hh] = _cast(_rope(kn, cos_t, sin_t))

    tok = lambda w: pl.BlockSpec((tm, w), lambda i: (i, 0))
    heads = pl.BlockSpec((n_h, tm, QK_HEAD), lambda i: (0, i, 0))
    return pl.pallas_call(
        body, name="mla_qkv_fwd", grid=(s // tm,),
        in_specs=[tok(d), _full((1, d)), _full(w_in.shape), _full((1, ql)), _full((1, kvl)), _full(w_q.shape),
                  _full(w_kn.shape), _full(w_v.shape), _full((1, QK_HEAD)), _full((1, QK_HEAD)),
                  tok(QK_HEAD), tok(QK_HEAD)],
        out_specs=[tok(lat_w), heads, heads, tok(n_h * V_HEAD)],
        out_shape=[_sds((s, lat_w)), _sds((n_h, s, QK_HEAD), MXU_DTYPE), _sds((n_h, s, QK_HEAD), MXU_DTYPE),
                   _sds((s, n_h * V_HEAD), MXU_DTYPE)],
        compiler_params=_params(("arbitrary",)),
    )(x, gain, w_in, q_norm, kv_norm, w_q, w_kn, w_v, q_head_norm, k_head_norm, cos, sin)


def _mla_qkv_bwd(x, lat, dy, dq, dk, dv, gain, w_in, q_norm, kv_norm, w_q, w_kn, w_v, q_head_norm, k_head_norm,
                 cos, sin):
    s, d = x.shape
    n_h, ql = w_q.shape[0], w_q.shape[1]
    kvl, lat_w = w_kn.shape[1], w_in.shape[1]
    tm = _tile(s, MLA_TILE)

    def body(x_ref, lat_ref, dy_ref, dq_ref, dk_ref, dv_ref, g_ref, win_ref, qn_ref, kvn_ref, wq_ref, wkn_ref,
             wv_ref, qhn_ref, khn_ref, c_ref, s_ref,
             dx_ref, dg_ref, dwin_ref, dqn_ref, dkvn_ref, dwq_ref, dwkn_ref, dwv_ref, dqhn_ref, dkhn_ref):
        @pl.when(pl.program_id(0) == 0)
        def _():
            for ref in (dg_ref, dwin_ref, dqn_ref, dkvn_ref, dwq_ref, dwkn_ref, dwv_ref, dqhn_ref, dkhn_ref):
                ref[...] = jnp.zeros_like(ref)

        x_t = x_ref[...]
        h, r = _rms_fwd(x_t, g_ref[...])
        hb = _cast(h)
        lat = lat_ref[...]
        cq, ckv, kpe = lat[:, :ql], lat[:, ql:ql + kvl], lat[:, ql + kvl:]
        cqn, rq = _rms_fwd(cq, qn_ref[...])
        ckvn, rkv = _rms_fwd(ckv, kvn_ref[...])
        cqb, ckb = _cast(cqn), _cast(ckvn)
        cos_t, sin_t = c_ref[...], s_ref[...]

        dvb = _cast(dv_ref[...])
        dwv_ref[...] += _mm_tn(ckb, dvb)
        dckvn = _mm_nt(dvb, wv_ref[...])
        dcqn = jnp.zeros((tm, ql), F32)
        dkpe = jnp.zeros((tm, QK_ROPE), F32)
        dqhn = jnp.zeros((1, QK_HEAD), F32)
        dkhn = jnp.zeros((1, QK_HEAD), F32)
        for hh in range(n_h):
            qp = _mm(cqb, wq_ref[hh])
            _, rqp = _rms_fwd(qp, qhn_ref[...])
            dqp, dgq = _rms_bwd(qp, rqp, qhn_ref[...], _rope_transposed(dq_ref[hh] * SCORE_SCALE, cos_t, sin_t))
            dqhn += dgq
            dqpb = _cast(dqp)
            dwq_ref[hh] += _mm_tn(cqb, dqpb)
            dcqn += _mm_nt(dqpb, wq_ref[hh])

            kp = jnp.concatenate([_mm(ckb, wkn_ref[hh]), kpe], axis=-1)
            _, rkp = _rms_fwd(kp, khn_ref[...])
            dkp, dgk = _rms_bwd(kp, rkp, khn_ref[...], _rope_transposed(dk_ref[hh], cos_t, sin_t))
            dkhn += dgk
            dknb = _cast(dkp[:, :QK_NOPE])
            dkpe += dkp[:, QK_NOPE:]
            dwkn_ref[hh] += _mm_tn(ckb, dknb)
            dckvn += _mm_nt(dknb, wkn_ref[hh])
        dqhn_ref[...] += dqhn
        dkhn_ref[...] += dkhn

        dcq, dgn = _rms_bwd(cq, rq, qn_ref[...], dcqn)
        dqn_ref[...] += dgn
        dckv, dgn = _rms_bwd(ckv, rkv, kvn_ref[...], dckvn)
        dkvn_ref[...] += dgn
        dlb = _cast(jnp.concatenate([dcq, dckv, dkpe], axis=-1))
        dwin_ref[...] += _mm_tn(hb, dlb)
        dxn, dgn = _rms_bwd(x_t, r, g_ref[...], _mm_nt(dlb, win_ref[...]))
        dg_ref[...] += dgn
        dx_ref[...] = dy_ref[...] + dxn

    tok = lambda w: pl.BlockSpec((tm, w), lambda i: (i, 0))
    heads = pl.BlockSpec((n_h, tm, QK_HEAD), lambda i: (0, i, 0))
    return pl.pallas_call(
        body, name="mla_qkv_bwd", grid=(s // tm,),
        in_specs=[tok(d), tok(lat_w), tok(d), heads, heads, tok(n_h * V_HEAD), _full((1, d)), _full(w_in.shape),
                  _full((1, ql)), _full((1, kvl)), _full(w_q.shape), _full(w_kn.shape), _full(w_v.shape),
                  _full((1, QK_HEAD)), _full((1, QK_HEAD)), tok(QK_HEAD), tok(QK_HEAD)],
        out_specs=[tok(d), _full((1, d)), _full(w_in.shape), _full((1, ql)), _full((1, kvl)), _full(w_q.shape),
                   _full(w_kn.shape), _full(w_v.shape), _full((1, QK_HEAD)), _full((1, QK_HEAD))],
        out_shape=[_sds((s, d)), _sds((1, d)), _sds(w_in.shape), _sds((1, ql)), _sds((1, kvl)), _sds(w_q.shape),
                   _sds(w_kn.shape), _sds(w_v.shape), _sds((1, QK_HEAD)), _sds((1, QK_HEAD))],
        compiler_params=_params(("arbitrary",)),
    )(x, lat, dy, dq, dk, dv, gain, w_in, q_norm, kv_norm, w_q, w_kn, w_v, q_head_norm, k_head_norm, cos, sin)


def _scores_t(k_t, q_t):
    return _mm_nt(k_t, q_t)


def _mask_above_diagonal(z, t):
    key = lax.broadcasted_iota(jnp.int32, (t, t), 0)
    query = lax.broadcasted_iota(jnp.int32, (t, t), 1)
    return jnp.where(key <= query, z, -jnp.inf)


def _flash_fwd(q, k, vt):
    n_h, s, _ = q.shape
    t = _tile(s, ATTN_FWD_TILE)
    n = s // t
    rows_v = vt.shape[2]

    def body(q_ref, k_ref, vt_ref, ot_ref, lse_ref, m_sc, acc_sc, z_sc):
        i = pl.program_id(1)
        m_sc[...] = jnp.full_like(m_sc, -jnp.inf)
        acc_sc[...] = jnp.zeros_like(acc_sc)
        q_t = q_ref[...]

        def fetch(j, slot):
            z_sc[slot] = _scores_t(k_ref[pl.ds(pl.multiple_of(j * t, t), t), :], q_t)

        def stage(j, slot, masked, prefetch=True):
            if prefetch:
                fetch(j + 1, 1 - slot)
            z = _mask_above_diagonal(z_sc[slot], t) if masked else z_sc[slot]
            m_old = m_sc[...]
            m_new = jnp.maximum(m_old, jnp.max(z, axis=0, keepdims=True))
            alpha = jnp.exp(m_old - m_new)
            acc_sc[...] = alpha * acc_sc[...] + _mm(vt_ref[j], _cast(jnp.exp(z - m_new)))
            m_sc[...] = m_new

        def pair_below_diagonal(pair, carry):
            stage(2 * pair, 0, False)
            stage(2 * pair + 1, 1, False)
            return carry

        fetch(0, 0)
        lax.fori_loop(0, i >> 1, pair_below_diagonal, 0)

        @pl.when((i & 1) == 1)
        def _():
            stage(i - 1, 0, False)
            stage(i, 1, True, prefetch=False)

        @pl.when((i & 1) == 0)
        def _():
            stage(i, 0, True, prefetch=False)

        denom = acc_sc[V_HEAD:V_HEAD + 1, :]
        ot_ref[...] = acc_sc[:V_HEAD, :] / denom
        lse_ref[...] = m_sc[...] + jnp.log(denom)

    whole_head = dict(pipeline_mode=pl.Buffered(1))
    return pl.pallas_call(
        body, name="flash_fwd", grid=(n_h, n),
        in_specs=[pl.BlockSpec((None, t, QK_HEAD), lambda h, i: (h, i, 0)),
                  pl.BlockSpec((None, s, QK_HEAD), lambda h, i: (h, 0, 0), **whole_head),
                  pl.BlockSpec((None, n, rows_v, t), lambda h, i: (h, 0, 0, 0), **whole_head)],
        out_specs=[pl.BlockSpec((V_HEAD, t), lambda h, i: (h, i)),
                   pl.BlockSpec((None, 1, t), lambda h, i: (h, 0, i))],
        out_shape=[_sds((n_h * V_HEAD, s)), _sds((n_h, 1, s))],
        scratch_shapes=[pltpu.VMEM((1, t), F32), pltpu.VMEM((rows_v, t), F32), pltpu.VMEM((2, t, t), F32)],
        compiler_params=_params(("arbitrary", "arbitrary")),
    )(q, k, vt)


def _flash_bwd(q, k, v, do, lse, delta):
    n_h, s, _ = q.shape
    t = _tile(s, ATTN_BWD_TILE)
    n = s // t

    def body(q_ref, do_ref, lse_ref, dl_ref, k_ref, v_ref, dqt_ref, dk_ref, dv_ref, dk_sc, dv_sc, z_sc, dp_sc):
        j = pl.program_id(1)

        @pl.when(j == 0)
        def _():
            dqt_ref[...] = jnp.zeros_like(dqt_ref)

        dk_sc[...] = jnp.zeros_like(dk_sc)
        dv_sc[...] = jnp.zeros_like(dv_sc)
        k_t, v_t = k_ref[...], v_ref[...]

        def rows(i):
            return pl.ds(pl.multiple_of(i * t, t), t)

        def fetch(i, slot):
            i = jnp.minimum(i, n - 1)
            z_sc[slot] = _scores_t(k_t, q_ref[rows(i), :])
            dp_sc[slot] = _mm_nt(v_t, do_ref[rows(i), :])

        def stage(i, slot, masked, prefetch=True):
            if prefetch:
                fetch(i + 1, 1 - slot)
            z = _mask_above_diagonal(z_sc[slot], t) if masked else z_sc[slot]
            pr = jnp.exp(z - lse_ref[i])
            dsb = _cast(pr * (dp_sc[slot] - dl_ref[i]))
            dv_sc[...] += _mm(_cast(pr), do_ref[rows(i), :])
            dk_sc[...] += _mm(dsb, q_ref[rows(i), :])
            dqt_ref[i] += _mm_tn(k_t, dsb)

        def pair_below_diagonal(pair, carry):
            stage(j + 1 + 2 * pair, 1, False)
            stage(j + 2 + 2 * pair, 0, False)
            return carry

        below = n - 1 - j
        fetch(j, 0)
        stage(j, 0, True)
        lax.fori_loop(0, below >> 1, pair_below_diagonal, 0)

        @pl.when((below & 1) == 1)
        def _():
            stage(n - 1, 1, False, prefetch=False)

        dk_ref[...] = dk_sc[...]
        dv_ref[...] = dv_sc[...]

    whole_head = dict(pipeline_mode=pl.Buffered(1))
    stat = pl.BlockSpec((None, n, 1, t), lambda h, j: (h, 0, 0, 0))
    return pl.pallas_call(
        body, name="flash_bwd", grid=(n_h, n),
        in_specs=[pl.BlockSpec((None, s, QK_HEAD), lambda h, j: (h, 0, 0), **whole_head),
                  pl.BlockSpec((s, V_HEAD), lambda h, j: (0, h), **whole_head),
                  stat, stat,
                  pl.BlockSpec((None, t, QK_HEAD), lambda h, j: (h, j, 0)),
                  pl.BlockSpec((t, V_HEAD), lambda h, j: (j, h))],
        out_specs=[pl.BlockSpec((None, n, QK_HEAD, t), lambda h, j: (h, 0, 0, 0), **whole_head),
                   pl.BlockSpec((None, t, QK_HEAD), lambda h, j: (h, j, 0)),
                   pl.BlockSpec((t, V_HEAD), lambda h, j: (j, h))],
        out_shape=[_sds((n_h, n, QK_HEAD, t)), _sds((n_h, s, QK_HEAD)), _sds((s, n_h * V_HEAD))],
        scratch_shapes=[pltpu.VMEM((t, QK_HEAD), F32), pltpu.VMEM((t, V_HEAD), F32),
                        pltpu.VMEM((2, t, t), F32), pltpu.VMEM((2, t, t), F32)],
        compiler_params=_params(("arbitrary", "arbitrary")),
    )(q, do, lse, delta, k, v)


def _mla_out_fwd(x, ot, w_out):
    s, d = x.shape
    hv = ot.shape[0]
    tm = _tile(s, FFN_TILE)

    def body(x_ref, ot_ref, w_ref, y_ref):
        y_ref[...] = x_ref[...] + _mm_tn(_cast(ot_ref[...]), w_ref[...])

    tok = pl.BlockSpec((tm, d), lambda i: (i, 0))
    return pl.pallas_call(
        body, name="mla_out_fwd", grid=(s // tm,),
        in_specs=[tok, pl.BlockSpec((hv, tm), lambda i: (0, i)), _full(w_out.shape)],
        out_specs=tok, out_shape=_sds((s, d)),
        compiler_params=_params(("arbitrary",)),
    )(x, ot, w_out)


def _mla_out_bwd(dy, ot, w_out):
    s, d = dy.shape
    hv = ot.shape[0]
    n_h = hv // V_HEAD
    tm = _tile(s, FFN_TILE)

    def body(dy_ref, ot_ref, w_ref, do_ref, dl_ref, dw_ref):
        @pl.when(pl.program_id(0) == 0)
        def _():
            dw_ref[...] = jnp.zeros_like(dw_ref)

        dyb = _cast(dy_ref[...])
        o_t = ot_ref[...]
        do_ref[...] = _cast(_mm_nt(dyb, w_ref[...]))
        prod = _mm_nt(w_ref[...], dyb) * o_t
        for hh in range(n_h):
            dl_ref[hh] = jnp.sum(prod[hh * V_HEAD:(hh + 1) * V_HEAD], axis=0, keepdims=True)
        dw_ref[...] += _mm(_cast(o_t), dyb)

    return pl.pallas_call(
        body, name="mla_out_bwd", grid=(s // tm,),
        in_specs=[pl.BlockSpec((tm, d), lambda i: (i, 0)), pl.BlockSpec((hv, tm), lambda i: (0, i)),
                  _full(w_out.shape)],
        out_specs=[pl.BlockSpec((tm, hv), lambda i: (i, 0)), pl.BlockSpec((n_h, 1, tm), lambda i: (0, 0, i)),
                   _full(w_out.shape)],
        out_shape=[_sds((s, hv), MXU_DTYPE), _sds((n_h, 1, s)), _sds(w_out.shape)],
        compiler_params=_params(("arbitrary",)),
    )(dy, ot, w_out)


def _loss_and_grad(y, target):
    s, d = y.shape
    tm = _tile(s, ROW_TILE)

    def body(y_ref, t_ref, loss_ref, dy_ref):
        @pl.when(pl.program_id(0) == 0)
        def _():
            loss_ref[...] = jnp.zeros_like(loss_ref)

        err = y_ref[...] - t_ref[...]
        dy_ref[...] = err * (1.0 / d)
        loss_ref[...] += 0.5 * jnp.sum(jnp.mean(err * err, axis=-1, keepdims=True), axis=0, keepdims=True)

    tok = pl.BlockSpec((tm, d), lambda i: (i, 0))
    return pl.pallas_call(
        body, name="loss_and_grad", grid=(s // tm,),
        in_specs=[tok, tok], out_specs=[_full((1, 1)), tok],
        out_shape=[_sds((1, 1)), _sds((s, d))],
        compiler_params=_params(("arbitrary",)),
    )(y, target)


def _mesh_position():
    return lax.axis_index("x"), lax.axis_index("y"), lax.axis_index("c")


def _other_chips(x, y):
    return [(1 - x, y), (x, 1 - y), (1 - x, 1 - y)]


ANY = pl.BlockSpec(memory_space=pl.ANY)


def _gather_over_chips(arrs):
    n = len(arrs)
    halves = [a.shape[0] // 2 for a in arrs]
    assert all(a.shape[0] % 2 == 0 for a in arrs)
    own = 2 * (N_CHIPS - 1)

    def body(*refs):
        srcs, outs = refs[:n], refs[n:2 * n]
        send_sems, recv_sems = refs[2 * n:]
        x, y, c = _mesh_position()
        me, sibling = (x, y, c), (x, y, 1 - c)
        chips = _other_chips(x, y)
        my_chip = 2 * x + y

        def rows(t, chip, half):
            return outs[t].at[chip, pl.ds(half * halves[t], halves[t])]

        def copy(t, k, src, dst, to):
            return pltpu.make_async_remote_copy(src_ref=src, dst_ref=dst, send_sem=send_sems.at[t, k],
                                                recv_sem=recv_sems.at[t, k], device_id=to, device_id_type=MESH)

        started = []
        for t in range(n):
            for k, (px, py) in enumerate(chips):
                cp = copy(t, k, srcs[t].at[pl.ds(c * halves[t], halves[t])], rows(t, my_chip, c), (px, py, c))
                cp.start()
                started.append(cp)
            cp = copy(t, own, srcs[t], outs[t].at[my_chip], sibling)
            cp.start()
            started.append(cp)
        for t in range(n):
            for k, (px, py) in enumerate(chips):
                landed = rows(t, 2 * px + py, c)
                copy(t, k, landed, landed, me).wait_recv()
                cp = copy(t, N_CHIPS - 1 + k, landed, landed, sibling)
                cp.start()
                started.append(cp)
        for t in range(n):
            for k, (px, py) in enumerate(chips):
                passed = rows(t, 2 * px + py, 1 - c)
                copy(t, N_CHIPS - 1 + k, passed, passed, me).wait_recv()
            copy(t, own, srcs[t], outs[t].at[my_chip], me).wait_recv()
        for cp in started:
            cp.wait_send()

    return pl.pallas_call(
        body, name="gather_over_chips",
        in_specs=[ANY] * n, out_specs=[ANY] * n,
        out_shape=[_sds((N_CHIPS,) + a.shape, a.dtype) for a in arrs],
        scratch_shapes=[pltpu.SemaphoreType.DMA((n, own + 1)), pltpu.SemaphoreType.DMA((n, own + 1))],
    )(*arrs)


def _send_other_half_to_sibling(grads):
    n = len(grads)
    halves = [g.shape[1] // 2 for g in grads]

    def body(*refs):
        srcs, outs = refs[:n], refs[n:2 * n]
        send_sems, recv_sems = refs[2 * n:]
        x, y, c = _mesh_position()
        copies = []
        for t in range(n):
            cp = pltpu.make_async_remote_copy(
                src_ref=srcs[t].at[pl.ds(0, N_CHIPS), pl.ds((1 - c) * halves[t], halves[t])], dst_ref=outs[t],
                send_sem=send_sems.at[t], recv_sem=recv_sems.at[t], device_id=(x, y, 1 - c), device_id_type=MESH)
            cp.start()
            copies.append(cp)
        for cp in copies:
            cp.wait_recv()
        for cp in copies:
            cp.wait_send()

    return pl.pallas_call(
        body, name="send_other_half_to_sibling",
        in_specs=[ANY] * n, out_specs=[ANY] * n,
        out_shape=[_sds((N_CHIPS, h) + g.shape[2:]) for g, h in zip(grads, halves)],
        scratch_shapes=[pltpu.SemaphoreType.DMA((n,)), pltpu.SemaphoreType.DMA((n,))],
    )(*grads)


def _send_blocks_to_chips(parts):
    n = len(parts)

    def body(*refs):
        srcs, outs = refs[:n], refs[n:2 * n]
        send_sems, recv_sems = refs[2 * n:]
        x, y, c = _mesh_position()
        copies = []
        for t in range(n):
            for k, (px, py) in enumerate(_other_chips(x, y)):
                cp = pltpu.make_async_remote_copy(
                    src_ref=srcs[t].at[2 * px + py], dst_ref=outs[t].at[k], send_sem=send_sems.at[t, k],
                    recv_sem=recv_sems.at[t, k], device_id=(px, py, c), device_id_type=MESH)
                cp.start()
                copies.append(cp)
        for cp in copies:
            cp.wait_recv()
        for cp in copies:
            cp.wait_send()

    return pl.pallas_call(
        body, name="send_blocks_to_chips",
        in_specs=[ANY] * n, out_specs=[ANY] * n,
        out_shape=[_sds((N_CHIPS - 1,) + p.shape[1:], p.dtype) for p in parts],
        scratch_shapes=[pltpu.SemaphoreType.DMA((n, N_CHIPS - 1)), pltpu.SemaphoreType.DMA((n, N_CHIPS - 1))],
    )(*parts)


def _join_halves_with_sibling(sums):
    n = len(sums)

    def body(*refs):
        srcs, outs = refs[:n], refs[n:2 * n]
        send_sems, recv_sems = refs[2 * n:]
        x, y, c = _mesh_position()
        copies = []
        for t in range(n):
            h = srcs[t].shape[0] // 2
            mine = pl.ds(c * h, h)
            cp = pltpu.make_async_remote_copy(
                src_ref=srcs[t].at[mine], dst_ref=outs[t].at[mine], send_sem=send_sems.at[t],
                recv_sem=recv_sems.at[t], device_id=(x, y, 1 - c), device_id_type=MESH)
            cp.start()
            copies.append(cp)
        for t in range(n):
            h = srcs[t].shape[0] // 2
            theirs = pl.ds((1 - c) * h, h)
            pltpu.make_async_remote_copy(
                src_ref=srcs[t].at[theirs], dst_ref=outs[t].at[theirs], send_sem=send_sems.at[t],
                recv_sem=recv_sems.at[t], device_id=(x, y, 1 - c), device_id_type=MESH).wait_recv()
        for cp in copies:
            cp.wait_send()

    return pl.pallas_call(
        body, name="join_halves_with_sibling",
        in_specs=[ANY] * n, out_specs=[ANY] * n,
        out_shape=[_sds(a.shape) for a in sums],
        input_output_aliases={t: t for t in range(n)},
        scratch_shapes=[pltpu.SemaphoreType.DMA((n,)), pltpu.SemaphoreType.DMA((n,))],
    )(*sums)


def _gather_over_devices(rows):
    r = rows.shape[0]

    def body(in_ref, out_ref, send_sems, recv_sems, local_sem):
        x, y, c = _mesh_position()
        mine = pltpu.make_async_copy(in_ref, out_ref.at[4 * x + 2 * y + c], local_sem)
        mine.start()
        copies = []
        for mask in range(1, N_DEVICES):
            fx, fy, fc = (mask >> 2) & 1, (mask >> 1) & 1, mask & 1
            px, py, pc = (1 - x if fx else x), (1 - y if fy else y), (1 - c if fc else c)
            send = pltpu.make_async_remote_copy(
                src_ref=in_ref, dst_ref=out_ref.at[4 * x + 2 * y + c], send_sem=send_sems.at[mask - 1],
                recv_sem=recv_sems.at[mask - 1], device_id=(px, py, pc), device_id_type=MESH)
            send.start()
            recv = pltpu.make_async_remote_copy(
                src_ref=in_ref, dst_ref=out_ref.at[4 * px + 2 * py + pc], send_sem=send_sems.at[mask - 1],
                recv_sem=recv_sems.at[mask - 1], device_id=(px, py, pc), device_id_type=MESH)
            copies.append((send, recv))
        for _, recv in copies:
            recv.wait_recv()
        for send, _ in copies:
            send.wait_send()
        mine.wait()

    vm = pl.BlockSpec(memory_space=pltpu.VMEM)
    return pl.pallas_call(
        body, name="gather_over_devices", in_specs=[vm], out_specs=vm,
        out_shape=_sds((N_DEVICES, r, LANES)),
        scratch_shapes=[pltpu.SemaphoreType.DMA((N_DEVICES - 1,)), pltpu.SemaphoreType.DMA((N_DEVICES - 1,)),
                        pltpu.SemaphoreType.DMA],
    )(rows)


def _sum_over_all_devices(rows):
    r = rows.shape[0]

    def body(in_ref, out_ref, parts, send_sems, recv_sems, local_sem):
        x, y, c = _mesh_position()
        me = 4 * x + 2 * y + c
        mine = pltpu.make_async_copy(in_ref, parts.at[me], local_sem)
        mine.start()
        copies = []
        for mask in range(1, N_DEVICES):
            fx, fy, fc = (mask >> 2) & 1, (mask >> 1) & 1, mask & 1
            px, py, pc = (1 - x if fx else x), (1 - y if fy else y), (1 - c if fc else c)
            send = pltpu.make_async_remote_copy(
                src_ref=in_ref, dst_ref=parts.at[me], send_sem=send_sems.at[mask - 1],
                recv_sem=recv_sems.at[mask - 1], device_id=(px, py, pc), device_id_type=MESH)
            send.start()
            recv = pltpu.make_async_remote_copy(
                src_ref=in_ref, dst_ref=parts.at[4 * px + 2 * py + pc], send_sem=send_sems.at[mask - 1],
                recv_sem=recv_sems.at[mask - 1], device_id=(px, py, pc), device_id_type=MESH)
            copies.append((send, recv))
        for _, recv in copies:
            recv.wait_recv()
        for send, _ in copies:
            send.wait_send()
        mine.wait()
        acc = parts[0]
        for k in range(1, N_DEVICES):
            acc = acc + parts[k]
        out_ref[...] = acc

    vm = pl.BlockSpec(memory_space=pltpu.VMEM)
    return pl.pallas_call(
        body, name="sum_over_all_devices", in_specs=[vm], out_specs=vm, out_shape=_sds((r, LANES)),
        scratch_shapes=[pltpu.VMEM((N_DEVICES, r, LANES), F32), pltpu.SemaphoreType.DMA((N_DEVICES - 1,)),
                        pltpu.SemaphoreType.DMA((N_DEVICES - 1,)), pltpu.SemaphoreType.DMA],
    )(rows)


def _add_sibling_half(grad, received, chip, core):
    _, l, r, c = grad.shape
    half = l // 2

    def body(chip_ref, core_ref, g_ref, r_ref, wire_ref, own_ref):
        total = g_ref[...] + r_ref[...]
        wire_ref[...] = total.astype(WIRE_DTYPE)

        @pl.when(pl.program_id(1) == chip_ref[0])
        def _():
            own_ref[...] = total

    blk = lambda f: pl.BlockSpec((None, None, r, c), f)
    grid_spec = pltpu.PrefetchScalarGridSpec(
        num_scalar_prefetch=2, grid=(half, N_CHIPS),
        in_specs=[blk(lambda i, j, chip, core: (j, core[0] * half + i, 0, 0)),
                  blk(lambda i, j, chip, core: (j, i, 0, 0))],
        out_specs=[blk(lambda i, j, chip, core: (j, i, 0, 0)),
                   pl.BlockSpec((None, r, c), lambda i, j, chip, core: (i, 0, 0))])
    return pl.pallas_call(
        body, name="add_sibling_half", grid_spec=grid_spec,
        out_shape=[_sds((N_CHIPS, half, r, c), WIRE_DTYPE), _sds((half, r, c))],
        compiler_params=_params(("arbitrary", "arbitrary")),
    )(chip, core, grad, received)


def _add_chip_blocks(own, received, core):
    half, r, c = own.shape

    def body(core_ref, p_ref, r0_ref, r1_ref, r2_ref, o_ref):
        o_ref[...] = ((p_ref[...] + r0_ref[...].astype(F32)) + r1_ref[...].astype(F32)) + r2_ref[...].astype(F32)

    grid_spec = pltpu.PrefetchScalarGridSpec(
        num_scalar_prefetch=1, grid=(half,),
        in_specs=[pl.BlockSpec((None, r, c), lambda i, core: (i, 0, 0))] + [
            pl.BlockSpec((None, None, r, c), functools.partial(lambda i, core, k: (k, i, 0, 0), k=k))
            for k in range(N_CHIPS - 1)],
        out_specs=pl.BlockSpec((None, r, c), lambda i, core: (core[0] * half + i, 0, 0)))
    return pl.pallas_call(
        body, name="add_chip_blocks", grid_spec=grid_spec, out_shape=_sds((2 * half, r, c)),
        compiler_params=_params(("arbitrary",)),
    )(core, own, received, received, received)


def _adamw_math(w, g, m, v):
    m = ADAM_B1 * m + (1.0 - ADAM_B1) * g
    v = ADAM_B2 * v + (1.0 - ADAM_B2) * (g * g)
    m_hat = m / (1.0 - ADAM_B1 ** ADAM_STEP)
    v_hat = v / (1.0 - ADAM_B2 ** ADAM_STEP)
    delta = -ADAM_LR * (m_hat / (jnp.sqrt(v_hat) + ADAM_EPS) + ADAM_WD * w)
    return delta, m, v


def _adamw_stacked(w, m, v, grads, offset):
    l, r, c = w.shape
    tr = r
    while tr * c * 4 > 2**20 and tr % 16 == 0:
        tr //= 2

    def body(w_ref, m_ref, v_ref, g_ref, go_ref, d_ref, mo_ref, vo_ref):
        g = g_ref[...]
        go_ref[...] = g
        d_ref[...], mo_ref[...], vo_ref[...] = _adamw_math(w_ref[...], g, m_ref[...], v_ref[...])

    blk = pl.BlockSpec((None, tr, c), lambda i, j: (i, j, 0))
    return pl.pallas_call(
        body, name="adamw_stacked", grid=(l, r // tr),
        in_specs=[blk, blk, blk, pl.BlockSpec((None, tr, c), lambda i, j: (offset + i, j, 0))],
        out_specs=[blk] * 4, out_shape=[_sds((l, r, c))] * 4,
        compiler_params=_params(("arbitrary", "arbitrary")),
    )(w, m, v, grads)


def _adamw_small(w, m, v, g):
    def body(w_ref, m_ref, v_ref, g_ref, d_ref, mo_ref, vo_ref):
        d_ref[...], mo_ref[...], vo_ref[...] = _adamw_math(w_ref[...], g_ref[...], m_ref[...], v_ref[...])

    return pl.pallas_call(body, name="adamw_small", out_shape=[_sds(w.shape)] * 3)(w, m, v, g)


def _pack_rows(arrs):
    flat = jnp.concatenate([a.reshape(-1) for a in arrs])
    pad = (-flat.shape[0]) % (8 * LANES)
    return jnp.pad(flat, (0, pad)).reshape(-1, LANES)


def _unpack_rows(rows, shapes, lead=()):
    flat = rows.reshape(lead + (-1,))
    out, at = [], 0
    for shp in shapes:
        size = int(np.prod(shp))
        out.append(flat[..., at:at + size].reshape(lead + tuple(shp)))
        at += size
    return out


WEIGHT_NAMES = ('ffn1_norm', 'ffn1_w_gate', 'ffn1_w_up', 'ffn1_w_down', 'mix_norm', 'pool_w', 'pool_scale',
                'mla_w_in', 'mla_q_norm', 'mla_w_q_up', 'mla_kv_norm', 'mla_w_kv_up', 'mla_q_head_norm',
                'mla_k_head_norm', 'mla_w_out', 'ffn2_norm', 'ffn2_w_gate', 'ffn2_w_up', 'ffn2_w_down')


def _chips_to_columns(g):
    return jnp.transpose(g, (1, 2, 0, 3)).reshape(g.shape[1], g.shape[2], -1)


def _columns_to_chips(full):
    n, r, c4 = full.shape
    return jnp.transpose(full.reshape(n, r, N_CHIPS, c4 // N_CHIPS), (2, 0, 1, 3))


def kernel(x, positions, ffn1_norm, ffn1_w_gate, ffn1_w_up, ffn1_w_down, mix_norm, pool_w, pool_scale, mla_w_in, mla_q_norm, mla_w_q_up, mla_kv_norm, mla_w_kv_up, mla_q_head_norm, mla_k_head_norm, mla_w_out, ffn2_norm, ffn2_w_gate, ffn2_w_up, ffn2_w_down, loss_target, m_ffn1_norm, m_ffn1_w_gate, m_ffn1_w_up, m_ffn1_w_down, m_mix_norm, m_pool_w, m_pool_scale, m_mla_w_in, m_mla_q_norm, m_mla_w_q_up, m_mla_kv_norm, m_mla_w_kv_up, m_mla_q_head_norm, m_mla_k_head_norm, m_mla_w_out, m_ffn2_norm, m_ffn2_w_gate, m_ffn2_w_up, m_ffn2_w_down, v_ffn1_norm, v_ffn1_w_gate, v_ffn1_w_up, v_ffn1_w_down, v_mix_norm, v_pool_w, v_pool_scale, v_mla_w_in, v_mla_q_norm, v_mla_w_q_up, v_mla_kv_norm, v_mla_w_kv_up, v_mla_q_head_norm, v_mla_k_head_norm, v_mla_w_out, v_ffn2_norm, v_ffn2_w_gate, v_ffn2_w_up, v_ffn2_w_down):
    env = dict(locals())
    w = {n: env[n] for n in WEIGHT_NAMES}
    mom = {n: env["m_" + n] for n in WEIGHT_NAMES}
    var = {n: env["v_" + n] for n in WEIGHT_NAMES}

    s, d = x.shape[1], x.shape[2]
    depth = ffn1_norm.shape[0]
    n_mla, n_pool, n_groups = mla_w_in.shape[0], pool_w.shape[0], pool_w.shape[1]
    pool_c = pool_w.shape[3]
    q_lora = N_CHIPS * mla_q_norm.shape[1]
    kv_lora = N_CHIPS * mla_kv_norm.shape[1]
    n_heads = N_CHIPS * mla_w_q_up.shape[2] // QK_HEAD
    t_fwd, t_bwd = _tile(s, ATTN_FWD_TILE), _tile(s, ATTN_BWD_TILE)
    cx, cy, cc = _mesh_position()
    chip = 2 * cx + cy
    chip_arr = jnp.reshape(chip, (1,)).astype(jnp.int32)
    core_arr = jnp.reshape(cc, (1,)).astype(jnp.int32)

    shard_gu = _cast(jnp.swapaxes(jnp.concatenate([ffn1_w_gate, ffn1_w_up, ffn2_w_gate, ffn2_w_up], axis=0), 1, 2))
    shard_dn = _cast(jnp.concatenate([ffn1_w_down, ffn2_w_down], axis=0))
    shard_pool = _cast(pool_w.reshape((n_pool * n_groups,) + pool_w.shape[2:]))
    w_gu, w_dn, g_in, g_qup, g_kvup, g_out, g_pool = _gather_over_chips(
        [shard_gu, shard_dn, _cast(mla_w_in), _cast(mla_w_q_up), _cast(mla_w_kv_up), _cast(mla_w_out), shard_pool])
    small_shapes = [mla_q_norm.shape, mla_kv_norm.shape]
    small = _gather_over_devices(_pack_rows([mla_q_norm, mla_kv_norm]))[::2]
    qn_chips, kvn_chips = _unpack_rows(small, small_shapes, lead=(N_CHIPS,))
    q_norm_full = jnp.transpose(qn_chips, (1, 0, 2)).reshape(n_mla, 1, q_lora)
    kv_norm_full = jnp.transpose(kvn_chips, (1, 0, 2)).reshape(n_mla, 1, kv_lora)

    w_in_full = _chips_to_columns(g_in)
    w_q_heads = jnp.transpose(_chips_to_columns(g_qup).reshape(n_mla, q_lora, n_heads, QK_HEAD), (0, 2, 1, 3))
    w_kv = _chips_to_columns(g_kvup).reshape(n_mla, kv_lora, n_heads, QK_NOPE + V_HEAD)
    w_kn_heads = jnp.transpose(w_kv[..., :QK_NOPE], (0, 2, 1, 3))
    w_v_full = w_kv[..., QK_NOPE:].reshape(n_mla, kv_lora, n_heads * V_HEAD)
    w_out_full = jnp.transpose(g_out, (1, 0, 2, 3)).reshape(n_mla, n_heads * V_HEAD, d)
    pool_full = jnp.transpose(g_pool.reshape(N_CHIPS, n_pool, n_groups, pool_c // N_CHIPS, pool_c),
                              (1, 2, 0, 3, 4)).reshape(n_pool, n_groups, pool_c, pool_c)

    inv_freq = (1.0 / (ROPE_THETA ** (jnp.arange(0, QK_ROPE, 2, dtype=F32) / QK_ROPE))).reshape(1, -1)
    cos_t, sin_t = _rope_tables(positions.reshape(s, 1), inv_freq)

    row = lambda a, i: a[i].reshape(1, -1)
    i_gate1, i_up1, i_gate2, i_up2 = (lambda i: i), (lambda i: depth + i), (lambda i: 2 * depth + i), (lambda i: 3 * depth + i)
    i_dn1, i_dn2 = (lambda i: i), (lambda i: depth + i)

    h = x.reshape(s, d)
    saved = []
    for i in range(depth):
        rec = {"x_ffn1": h}
        h, *rec["ffn1"] = _ffn_fwd(h, row(ffn1_norm, i), w_gu, w_dn, i_gate1(i), i_up1(i), i_dn1(i))
        rec["x_mix"] = h
        j = i // 2
        if i % 2 == 0:
            h = _pool_fwd(h, row(mix_norm, i), pool_full[j], row(pool_scale, j))
        else:
            lat, q, k, v = _mla_qkv_fwd(h, row(mix_norm, i), w_in_full[j], q_norm_full[j], kv_norm_full[j],
                                        w_q_heads[j], w_kn_heads[j], w_v_full[j], row(mla_q_head_norm, j),
                                        row(mla_k_head_norm, j), cos_t, sin_t)
            vt = jnp.transpose(v.reshape(s // t_fwd, t_fwd, n_heads, V_HEAD), (2, 0, 3, 1))
            vt = jnp.concatenate([vt, jnp.ones((n_heads, s // t_fwd, ONES_ROWS, t_fwd), vt.dtype)], axis=2)
            ot, lse = _flash_fwd(q, k, vt)
            rec.update(lat=lat, q=q, k=k, v=v, ot=ot, lse=lse)
            h = _mla_out_fwd(h, ot, w_out_full[j])
        rec["x_ffn2"] = h
        h, *rec["ffn2"] = _ffn_fwd(h, row(ffn2_norm, i), w_gu, w_dn, i_gate2(i), i_up2(i), i_dn2(i))
        saved.append(rec)

    loss_part, dy = _loss_and_grad(h, loss_target.reshape(s, d))
    loss = lax.psum(loss_part[0, 0], ("x", "y", "c"))

    g_gu = [None] * (4 * depth)
    g_dn = [None] * (2 * depth)
    g_norm = {n: [None] * depth for n in ("ffn1_norm", "mix_norm", "ffn2_norm")}
    g_pool_w, g_pool_scale = [None] * n_pool, [None] * n_pool
    g_mla = {n: [None] * n_mla for n in ("w_in", "q_norm", "kv_norm", "w_q", "w_kv", "qhn", "khn", "w_out")}
    for i in reversed(range(depth)):
        rec = saved[i]
        hb, gate, up = rec["ffn2"]
        dy, g_norm["ffn2_norm"][i], dyb, dgt, dup, act = _ffn_bwd_dgrad(
            rec["x_ffn2"], row(ffn2_norm, i), dy, gate, up, w_gu, w_dn, i_gate2(i), i_up2(i), i_dn2(i))
        g_gu[i_gate2(i)], g_gu[i_up2(i)], g_dn[i_dn2(i)] = _ffn_wgrad(hb, dyb, dgt, dup, act)
        j = i // 2
        if i % 2 == 0:
            dy, g_norm["mix_norm"][i], g_pool_w[j], g_pool_scale[j] = _pool_bwd(
                rec["x_mix"], row(mix_norm, i), pool_full[j], row(pool_scale, j), dy)
        else:
            do, delta, g_mla["w_out"][j] = _mla_out_bwd(dy, rec["ot"], w_out_full[j])
            by_tile = lambda a: a.reshape(n_heads, s // t_bwd, 1, t_bwd)
            dqt, dk, dv = _flash_bwd(rec["q"], rec["k"], rec["v"], do,
                                     by_tile(rec["lse"]), by_tile(delta))
            dq = jnp.transpose(dqt, (0, 1, 3, 2)).reshape(n_heads, s, QK_HEAD)
            (dy, g_norm["mix_norm"][i], g_mla["w_in"][j], g_mla["q_norm"][j], g_mla["kv_norm"][j], dwq, dwkn, dwv,
             g_mla["qhn"][j], g_mla["khn"][j]) = _mla_qkv_bwd(
                rec["x_mix"], rec["lat"], dy, dq, dk, dv, row(mix_norm, i), w_in_full[j], q_norm_full[j],
                kv_norm_full[j], w_q_heads[j], w_kn_heads[j], w_v_full[j], row(mla_q_head_norm, j),
                row(mla_k_head_norm, j), cos_t, sin_t)
            g_mla["w_q"][j] = jnp.transpose(dwq, (1, 0, 2)).reshape(q_lora, n_heads * QK_HEAD)
            g_mla["w_kv"][j] = jnp.concatenate(
                [jnp.transpose(dwkn, (1, 0, 2)), dwv.reshape(kv_lora, n_heads, V_HEAD)], axis=-1
            ).reshape(kv_lora, n_heads * (QK_NOPE + V_HEAD))
        hb, gate, up = rec["ffn1"]
        dy, g_norm["ffn1_norm"][i], dyb, dgt, dup, act = _ffn_bwd_dgrad(
            rec["x_ffn1"], row(ffn1_norm, i), dy, gate, up, w_gu, w_dn, i_gate1(i), i_up1(i), i_dn1(i))
        g_gu[i_gate1(i)], g_gu[i_up1(i)], g_dn[i_dn1(i)] = _ffn_wgrad(hb, dyb, dgt, dup, act)
    grad_x = dy.reshape(x.shape)

    full_grads = [
        jnp.stack(g_gu, axis=1),
        jnp.stack(g_dn, axis=1),
        _columns_to_chips(jnp.stack(g_mla["w_in"])),
        _columns_to_chips(jnp.stack(g_mla["w_q"])),
        _columns_to_chips(jnp.stack(g_mla["w_kv"])),
        jnp.transpose(jnp.stack(g_mla["w_out"]).reshape(n_mla, N_CHIPS, -1, d), (1, 0, 2, 3)),
        jnp.transpose(jnp.stack(g_pool_w).reshape(n_pool, n_groups, N_CHIPS, pool_c // N_CHIPS, pool_c),
                      (2, 0, 1, 3, 4)).reshape(N_CHIPS, n_pool * n_groups, pool_c // N_CHIPS, pool_c),
    ]
    from_sibling = _send_other_half_to_sibling(full_grads)
    chip_sums = [_add_sibling_half(g, r, chip_arr, core_arr) for g, r in zip(full_grads, from_sibling)]
    from_chips = _send_blocks_to_chips([wire for wire, _ in chip_sums])
    half_sums = [_add_chip_blocks(own, r, core_arr) for (_, own), r in zip(chip_sums, from_chips)]
    r_gu, r_dn, r_in, r_qup, r_kvup, r_out, r_pool = _join_halves_with_sibling(half_sums)

    small_grads = [jnp.concatenate(g_norm["ffn1_norm"]), jnp.concatenate(g_norm["mix_norm"]),
                   jnp.concatenate(g_norm["ffn2_norm"]), jnp.concatenate(g_pool_scale),
                   jnp.concatenate(g_mla["qhn"]), jnp.concatenate(g_mla["khn"]),
                   jnp.concatenate(g_mla["q_norm"]), jnp.concatenate(g_mla["kv_norm"])]
    small_sum = _sum_over_all_devices(_pack_rows(small_grads))
    (s_ffn1, s_mix, s_ffn2, s_pscale, s_qhn, s_khn, s_qn, s_kvn) = _unpack_rows(small_sum, [g.shape for g in small_grads])
    qn_w, kvn_w = mla_q_norm.shape[1], mla_kv_norm.shape[1]
    s_qn = lax.dynamic_slice_in_dim(s_qn, chip * qn_w, qn_w, axis=1)
    s_kvn = lax.dynamic_slice_in_dim(s_kvn, chip * kvn_w, kvn_w, axis=1)

    grads, deltas, new_m, new_v = {}, {}, {}, {}

    def stacked(name, reduced, offset, transposed=False):
        shape = w[name].shape
        if transposed:
            as3, back = (lambda a: jnp.swapaxes(a, 1, 2)), (lambda a: jnp.swapaxes(a, 1, 2))
        else:
            as3, back = (lambda a: a.reshape((-1,) + shape[-2:])), (lambda a: a.reshape(shape))
        out = _adamw_stacked(as3(w[name]), as3(mom[name]), as3(var[name]), reduced, offset)
        grads[name], deltas[name], new_m[name], new_v[name] = [back(o) for o in out]

    def small_update(name, g):
        grads[name] = g
        deltas[name], new_m[name], new_v[name] = _adamw_small(w[name], mom[name], var[name], g)

    stacked("ffn1_w_gate", r_gu, 0, transposed=True)
    stacked("ffn1_w_up", r_gu, depth, transposed=True)
    stacked("ffn2_w_gate", r_gu, 2 * depth, transposed=True)
    stacked("ffn2_w_up", r_gu, 3 * depth, transposed=True)
    stacked("ffn1_w_down", r_dn, 0)
    stacked("ffn2_w_down", r_dn, depth)
    stacked("mla_w_in", r_in, 0)
    stacked("mla_w_q_up", r_qup, 0)
    stacked("mla_w_kv_up", r_kvup, 0)
    stacked("mla_w_out", r_out, 0)
    stacked("pool_w", r_pool, 0)
    small_update("ffn1_norm", s_ffn1)
    small_update("mix_norm", s_mix)
    small_update("ffn2_norm", s_ffn2)
    small_update("pool_scale", s_pscale)
    small_update("mla_q_head_norm", s_qhn)
    small_update("mla_k_head_norm", s_khn)
    small_update("mla_q_norm", s_qn)
    small_update("mla_kv_norm", s_kvn)

    return (loss, grad_x, *[grads[n] for n in WEIGHT_NAMES], *[deltas[n] for n in WEIGHT_NAMES],
            *[new_m[n] for n in WEIGHT_NAMES], *[new_v[n] for n in WEIGHT_NAMES])
```
